```python
import math
import jax
import jax.numpy as jnp
from jax import lax
import numpy as np

D_MODEL = 2048
BATCH = 1
SEQ = 8192
DEPTH = 2

N_EVEN = (DEPTH + 1) // 2
N_ODD = DEPTH // 2

NSA_HEADS = 8
NSA_KV_HEADS = 2
NSA_HPG = NSA_HEADS // NSA_KV_HEADS
HEAD_DIM = 128
NSA_KV_W = NSA_KV_HEADS * HEAD_DIM
CMP_LEN = 32
CMP_STRIDE = 16
SLC_LEN = 64
SLC_TOPN = 16
WINDOW = 512
Q_BLOCK = 128

LRU_WIDTH = 1024
LRU_BLOCKS = 8
LRU_BW = LRU_WIDTH // LRU_BLOCKS
CONV_W = 4
LRU_C = 8.0

EVEN_COLS = NSA_HEADS * HEAD_DIM + 6 * NSA_KV_W + 3 * NSA_HEADS + 2 * LRU_WIDTH
EVEN_MIX_OUT = NSA_HEADS * HEAD_DIM + LRU_WIDTH

HG_HEADS = 16
HG_DK = 128
HG_DV = 128
HG_CHUNK = 64
HG_KW = HG_HEADS * HG_DK
HG_VW = HG_HEADS * HG_DV
ODD_COLS = 2 * HG_KW + 2 * HG_VW

N_BUCKETS = 32
MAX_DIST = 128

N_GROUPS = 8
EXP_PER_GROUP = 8
N_EXPERTS = N_GROUPS * EXP_PER_GROUP
TOPK_IN_GROUP = 2
D_EXPERT = 512
MOE_BLOCK = 128

EPS = 1e-6

kernel_name = 'hybrid_nsa_rglru_hgrn2_hmoe'


def rmsnorm(x, g):
    x32 = x.astype(jnp.float32)
    y = x32 * lax.rsqrt(jnp.mean(x32 * x32, axis=-1, keepdims=True) + EPS)
    return (y * g.astype(jnp.float32)).astype(x.dtype)


def t5_bucket(rel):
    n = jnp.maximum(rel, 0)
    max_exact = N_BUCKETS // 2
    nf = jnp.maximum(n, 1).astype(jnp.float32)
    large = max_exact + (jnp.log(nf / max_exact) / math.log(MAX_DIST / max_exact)
                         * (N_BUCKETS - max_exact)).astype(jnp.int32)
    large = jnp.minimum(large, N_BUCKETS - 1)
    return jnp.where(n < max_exact, n, large)


def masked_softmax(s, mask):
    s = jnp.where(mask, s.astype(jnp.float32), -jnp.inf)
    m = jnp.max(s, axis=-1, keepdims=True)
    m = jnp.where(jnp.isfinite(m), m, 0.0)
    e = jnp.exp(s - m)
    d = jnp.sum(e, axis=-1, keepdims=True)
    return e / jnp.where(d > 0, d, 1.0)


def compress_blocks(k, pe, w1, b1, w2, b2):
    B, G, T, hd = k.shape
    n_cmp = (T - CMP_LEN) // CMP_STRIDE + 1
    idx = jnp.arange(n_cmp)[:, None] * CMP_STRIDE + jnp.arange(CMP_LEN)[None, :]
    blocks = (k[:, :, idx] + pe).reshape(B, G, n_cmp, CMP_LEN * hd)
    hid = jax.nn.gelu(blocks @ w1 + b1)
    return hid @ w2 + b2


def nsa_rglru_mixer(h, rel_bias, w_in, w_out, cmp_pe, cmp_w1, cmp_b1, cmp_w2, cmp_b2,
                    conv_w, conv_b, lru_wa, lru_ba, lru_wi, lru_bi, lru_lambda):
    B, T, _ = h.shape
    G, HPG, HD = NSA_KV_HEADS, NSA_HPG, HEAD_DIM
    sizes = (NSA_HEADS * HD,) + (NSA_KV_W,) * 6 + (3 * NSA_HEADS, LRU_WIDTH, LRU_WIDTH)
    cuts = []
    acc = 0
    for s in sizes[:-1]:
        acc += s
        cuts.append(acc)
    q, k_c, v_c, k_s, v_s, k_w, v_w, gate_logit, y_br, x_br = jnp.split(h @ w_in, cuts, axis=-1)

    q = q.reshape(B, T, G, HPG, HD).transpose(0, 2, 3, 1, 4) * (HD ** -0.5)

    def kv_heads(t):
        return t.reshape(B, T, G, HD).transpose(0, 2, 1, 3)

    k_c, v_c, k_s, v_s, k_w, v_w = [kv_heads(t) for t in (k_c, v_c, k_s, v_s, k_w, v_w)]
    gates = jax.nn.sigmoid(gate_logit.reshape(B, T, G, HPG, 3).transpose(0, 2, 3, 1, 4))

    k_cmp = compress_blocks(k_c, cmp_pe[0], cmp_w1[0], cmp_b1[0], cmp_w2[0], cmp_b2[0])
    v_cmp = compress_blocks(v_c, cmp_pe[1], cmp_w1[1], cmp_b1[1], cmp_w2[1], cmp_b2[1])
    n_cmp = k_cmp.shape[2]
    n_slc = T // SLC_LEN
    n_top = min(SLC_TOPN, n_slc)
    ks_blk = k_s.reshape(B, G, n_slc, SLC_LEN, HD)
    vs_blk = v_s.reshape(B, G, n_slc, SLC_LEN, HD)
    pad = ((0, 0), (0, 0), (WINDOW, 0), (0, 0))
    kw_pad = jnp.pad(k_w, pad)
    vw_pad = jnp.pad(v_w, pad)

    cmp_start = jnp.arange(n_cmp) * CMP_STRIDE
    cmp_end = cmp_start + CMP_LEN - 1
    slc_start = jnp.arange(n_slc) * SLC_LEN
    cover = ((cmp_start[:, None] < slc_start[None, :] + SLC_LEN)
             & (cmp_start[:, None] + CMP_LEN > slc_start[None, :])).astype(jnp.float32)

    tbl = rel_bias.reshape(N_BUCKETS, G, HPG)
    tbl_g = tbl.transpose(1, 0, 2)
    g_idx = jnp.arange(G)[None, :, None, None]

    def static_bias(rel):
        return tbl[t5_bucket(rel)].transpose(2, 3, 0, 1)

    gather = jax.vmap(jax.vmap(lambda blocks, ix: blocks[ix]))

    def chunk(ci):
        s0 = ci * Q_BLOCK
        qc = lax.dynamic_slice_in_dim(q, s0, Q_BLOCK, axis=3)
        gc = lax.dynamic_slice_in_dim(gates, s0, Q_BLOCK, axis=3)
        t = s0 + jnp.arange(Q_BLOCK)
        rel_c = t[:, None] - cmp_end[None, :]
        sc = jnp.einsum('bghqd,bgnd->bghqn', qc, k_cmp) + static_bias(rel_c)
        p_cmp = masked_softmax(sc, rel_c >= 0)
        o_cmp = jnp.einsum('bghqn,bgnd->bghqd', p_cmp.astype(v_cmp.dtype), v_cmp)
        imp = jnp.einsum('bghqn,nj->bgqj', p_cmp, cover)
        cur = (t // SLC_LEN)[:, None]
        blk = jnp.arange(n_slc)[None, :]
        imp = jnp.where(blk == cur, jnp.inf, jnp.where(blk > cur, -jnp.inf, imp))
        _, sel = lax.top_k(imp, n_top)
        k_sel = gather(ks_blk, sel).reshape(B, G, Q_BLOCK, n_top * SLC_LEN, HD)
        v_sel = gather(vs_blk, sel).reshape(B, G, Q_BLOCK, n_top * SLC_LEN, HD)
        kpos = (sel[..., None] * SLC_LEN + jnp.arange(SLC_LEN)).reshape(B, G, Q_BLOCK, n_top * SLC_LEN)
        rel_s = t[None, None, :, None] - kpos
        bias_s = tbl_g[g_idx, t5_bucket(rel_s)].transpose(0, 1, 4, 2, 3)
        ss = jnp.einsum('bghqd,bgqkd->bghqk', qc, k_sel) + bias_s
        p_s = masked_softmax(ss, (rel_s >= 0)[:, :, None])
        o_slc = jnp.einsum('bghqk,bgqkd->bghqd', p_s.astype(v_sel.dtype), v_sel)
        kwc = lax.dynamic_slice_in_dim(kw_pad, s0, Q_BLOCK + WINDOW, axis=2)
        vwc = lax.dynamic_slice_in_dim(vw_pad, s0, Q_BLOCK + WINDOW, axis=2)
        kpos_w = s0 - WINDOW + jnp.arange(Q_BLOCK + WINDOW)
        rel_w = t[:, None] - kpos_w[None, :]
        mask_w = (rel_w >= 0) & (rel_w < WINDOW) & (kpos_w >= 0)[None, :]
        sw = jnp.einsum('bghqd,bgkd->bghqk', qc, kwc) + static_bias(rel_w)
        p_w = masked_softmax(sw, mask_w)
        o_win = jnp.einsum('bghqk,bgkd->bghqd', p_w.astype(vwc.dtype), vwc)
        return gc[..., 0:1] * o_cmp + gc[..., 1:2] * o_slc + gc[..., 2:3] * o_win

    o = lax.map(chunk, jnp.arange(T // Q_BLOCK))
    nsa_out = o.transpose(1, 0, 4, 2, 3, 5).reshape(B, T, NSA_HEADS * HD).astype(h.dtype)

    xc = lax.conv_general_dilated(x_br, conv_w[:, None, :], window_strides=(1,),
                                  padding=[(CONV_W - 1, 0)],
                                  dimension_numbers=('NWC', 'WIO', 'NWC'),
                                  feature_group_count=LRU_WIDTH) + conv_b
    xg = xc.reshape(B, T, LRU_BLOCKS, LRU_BW)
    r_gate = jax.nn.sigmoid((jnp.einsum('btnd,nde->btne', xg, lru_wa).reshape(B, T, LRU_WIDTH)
                             + lru_ba).astype(jnp.float32))
    i_gate = jax.nn.sigmoid((jnp.einsum('btnd,nde->btne', xg, lru_wi).reshape(B, T, LRU_WIDTH)
                             + lru_bi).astype(jnp.float32))
    log_a = -LRU_C * r_gate * jax.nn.softplus(-lru_lambda.astype(jnp.float32))
    a = jnp.exp(log_a)
    b = jnp.sqrt(-jnp.expm1(2.0 * log_a)) * (i_gate * xc.astype(jnp.float32))

    def combine(left, right):
        a1, b1 = left
        a2, b2 = right
        return a1 * a2, a2 * b1 + b2

    _, hseq = lax.associative_scan(combine, (a, b), axis=1)
    lru_out = (jax.nn.gelu(y_br.astype(jnp.float32)) * hseq).astype(h.dtype)

    return jnp.concatenate([nsa_out, lru_out], axis=-1) @ w_out


def hgrn2_chunked(q, k, v, log_f):
    B, T, H, DK = q.shape
    DV = v.shape[-1]
    C = HG_CHUNK
    nc = T // C

    def chunks(t):
        return t.reshape(B, nc, C, H, t.shape[-1]).transpose(1, 0, 3, 2, 4)

    causal = jnp.tril(jnp.ones((C, C), dtype=bool))[:, :, None]

    def step(S, inp):
        qc, kc, vc, gc = inp
        bcum = jnp.cumsum(gc, axis=2)
        o_inter = jnp.einsum('bhtd,bhde->bhte', qc * jnp.exp(bcum), S)
        diff = jnp.where(causal, bcum[:, :, :, None, :] - bcum[:, :, None, :, :], -jnp.inf)
        att = jnp.einsum('bhtd,bhsd,bhtsd->bhts', qc, kc, jnp.exp(diff))
        o_intra = jnp.einsum('bhts,bhse->bhte', att, vc)
        b_last = bcum[:, :, -1:, :]
        S = (jnp.exp(b_last[:, :, 0, :])[..., None] * S
             + jnp.einsum('bhsd,bhse->bhde', kc * jnp.exp(b_last - bcum), vc))
        return S, o_inter + o_intra

    S0 = jnp.zeros((B, H, DK, DV), jnp.float32)
    _, o = lax.scan(step, S0, (chunks(q), chunks(k), chunks(v), chunks(log_f)))
    return o.transpose(1, 0, 3, 2, 4).reshape(B, T, H, DV)


def hgrn2_mixer(h, lb, w_in, w_out, norm_g):
    B, T, _ = h.shape
    proj = h @ w_in
    q, fz, iv, g = jnp.split(proj, [HG_KW, 2 * HG_KW, 2 * HG_KW + HG_VW], axis=-1)
    q = jax.nn.silu(q.astype(jnp.float32)).reshape(B, T, HG_HEADS, HG_DK)
    lb = lb.reshape(HG_HEADS, HG_DK)
    f = lb + (1.0 - lb) * jax.nn.sigmoid(fz.astype(jnp.float32).reshape(B, T, HG_HEADS, HG_DK))
    log_f = jnp.log(jnp.maximum(f, 1e-30))
    v = iv.astype(jnp.float32).reshape(B, T, HG_HEADS, HG_DV)
    o = hgrn2_chunked(q, 1.0 - f, v, log_f)
    o = o * lax.rsqrt(jnp.mean(o * o, axis=-1, keepdims=True) + EPS)
    o = o * norm_g.astype(jnp.float32).reshape(HG_HEADS, HG_DV)
    o = o.reshape(B, T, HG_VW) * jax.nn.silu(g.astype(jnp.float32))
    return o.astype(h.dtype) @ w_out


def hier_moe(h, w_grp, b_grp, w_exp, b_exp, w_gate, w_up, w_down):
    B, T, D = h.shape
    xt = h.reshape(-1, D)
    N = xt.shape[0]
    glog = (xt @ w_grp + b_grp).astype(jnp.float32)
    gprob = jax.nn.softmax(glog, axis=-1)
    grp = jnp.argmax(glog, axis=-1)
    p_grp = jnp.take_along_axis(gprob, grp[:, None], axis=-1)
    elog = (xt @ w_exp + b_exp).astype(jnp.float32).reshape(N, N_GROUPS, EXP_PER_GROUP)
    elog = jnp.take_along_axis(elog, grp[:, None, None], axis=1)[:, 0]
    top_v, top_i = lax.top_k(elog, TOPK_IN_GROUP)
    gate = p_grp * jax.nn.softmax(top_v, axis=-1)
    eid = grp[:, None] * EXP_PER_GROUP + top_i

    A = N * TOPK_IN_GROUP
    e_flat = eid.reshape(-1)
    tok = jnp.repeat(jnp.arange(N, dtype=jnp.int32), TOPK_IN_GROUP)
    w_flat = gate.reshape(-1)
    order = jnp.argsort(e_flat)
    e_s, tok_s, w_s = e_flat[order], tok[order], w_flat[order]
    counts = jnp.zeros((N_EXPERTS,), jnp.int32).at[e_flat].add(1)
    start = jnp.cumsum(counts) - counts
    padded = (counts + MOE_BLOCK - 1) // MOE_BLOCK * MOE_BLOCK
    pad_end = jnp.cumsum(padded)
    pad_start = pad_end - padded
    dest = pad_start[e_s] + jnp.arange(A, dtype=jnp.int32) - start[e_s]
    n_blk = -(-(A + N_EXPERTS * MOE_BLOCK) // MOE_BLOCK)
    n_pad = n_blk * MOE_BLOCK
    buf_tok = jnp.full((n_pad,), N, jnp.int32).at[dest].set(tok_s)
    buf_w = jnp.zeros((n_pad,), jnp.float32).at[dest].set(w_s)
    blk_e = jnp.minimum(jnp.searchsorted(pad_end, jnp.arange(n_blk, dtype=jnp.int32) * MOE_BLOCK,
                                         side='right'), N_EXPERTS - 1)
    xpad = jnp.concatenate([xt, jnp.zeros((1, D), xt.dtype)], axis=0)
    xb = xpad[buf_tok].reshape(n_blk, MOE_BLOCK, D)

    def expert_block(args):
        xblk, e = args
        hid = jax.nn.silu(xblk @ w_gate[e]) * (xblk @ w_up[e])
        return hid @ w_down[e]

    yb = lax.map(expert_block, (xb, blk_e)).reshape(n_pad, D)
    out = jnp.zeros((N + 1, D), yb.dtype).at[buf_tok].add(yb * buf_w[:, None].astype(yb.dtype))
    return out[:N].reshape(B, T, D)


def setup_inputs(seed: int = 0) -> dict:
    key = jax.random.key(seed)
    keys = iter(jax.random.split(key, 40))

    def nrm(shape, scale):
        return jax.random.normal(next(keys), shape, jnp.float32) * scale

    D = D_MODEL
    E = N_EXPERTS
    u = jax.random.uniform(next(keys), (N_EVEN, LRU_WIDTH), jnp.float32, 0.9, 0.999)
    a_base = u ** (1.0 / LRU_C)
    return {
        'x': nrm((BATCH, SEQ, D), 1.0),
        'c': nrm((BATCH, D), 1.0),
        'rel_bias': nrm((N_BUCKETS, NSA_HEADS), 0.3),
        'ada_w': nrm((DEPTH, D, 6 * D), 0.5 * D ** -0.5),
        'ada_b': nrm((DEPTH, 6 * D), 0.01),
        'norm_mix_g': 1.0 + nrm((DEPTH, D), 0.01),
        'norm_ffn_g': 1.0 + nrm((DEPTH, D), 0.01),
        'ev_w_in': nrm((N_EVEN, D, EVEN_COLS), D ** -0.5),
        'ev_w_out': nrm((N_EVEN, EVEN_MIX_OUT, D), EVEN_MIX_OUT ** -0.5),
        'cmp_pe': nrm((N_EVEN, 2, CMP_LEN, HEAD_DIM), 0.1),
        'cmp_w1': nrm((N_EVEN, 2, CMP_LEN * HEAD_DIM, HEAD_DIM), (CMP_LEN * HEAD_DIM) ** -0.5),
        'cmp_b1': nrm((N_EVEN, 2, HEAD_DIM), 0.01),
        'cmp_w2': nrm((N_EVEN, 2, HEAD_DIM, HEAD_DIM), HEAD_DIM ** -0.5),
        'cmp_b2': nrm((N_EVEN, 2, HEAD_DIM), 0.01),
        'lru_conv_w': nrm((N_EVEN, CONV_W, LRU_WIDTH), CONV_W ** -0.5),
        'lru_conv_b': nrm((N_EVEN, LRU_WIDTH), 0.01),
        'lru_wa': nrm((N_EVEN, LRU_BLOCKS, LRU_BW, LRU_BW), LRU_BW ** -0.5),
        'lru_ba': nrm((N_EVEN, LRU_WIDTH), 0.01),
        'lru_wi': nrm((N_EVEN, LRU_BLOCKS, LRU_BW, LRU_BW), LRU_BW ** -0.5),
        'lru_bi': nrm((N_EVEN, LRU_WIDTH), 0.01),
        'lru_lambda': jnp.log(a_base) - jnp.log1p(-a_base),
        'od_w_in': nrm((N_ODD, D, ODD_COLS), D ** -0.5),
        'od_w_out': nrm((N_ODD, HG_VW, D), HG_VW ** -0.5),
        'hg_lb_logits': nrm((DEPTH, HG_KW), 1.0),
        'hg_norm_g': 1.0 + nrm((N_ODD, HG_VW), 0.01),
        'moe_w_grp': nrm((DEPTH, D, N_GROUPS), D ** -0.5),
        'moe_b_grp': nrm((DEPTH, N_GROUPS), 0.01),
        'moe_w_exp': nrm((DEPTH, D, E), D ** -0.5),
        'moe_b_exp': nrm((DEPTH, E), 0.01),
        'moe_w_gate': nrm((DEPTH, E, D, D_EXPERT), D ** -0.5),
        'moe_w_up': nrm((DEPTH, E, D, D_EXPERT), D ** -0.5),
        'moe_w_down': nrm((DEPTH, E, D_EXPERT, D), D_EXPERT ** -0.5),
        'final_g': 1.0 + nrm((D,), 0.01),
    }


def reference(x, c, rel_bias, ada_w, ada_b, norm_mix_g, norm_ffn_g, ev_w_in, ev_w_out,
              cmp_pe, cmp_w1, cmp_b1, cmp_w2, cmp_b2, lru_conv_w, lru_conv_b, lru_wa, lru_ba,
              lru_wi, lru_bi, lru_lambda, od_w_in, od_w_out, hg_lb_logits, hg_norm_g,
              moe_w_grp, moe_b_grp, moe_w_exp, moe_b_exp, moe_w_gate, moe_w_up, moe_w_down,
              final_g):
    lb_all = jnp.cumsum(jax.nn.softmax(hg_lb_logits.astype(jnp.float32), axis=0), axis=0)
    lb_all = lb_all - lb_all[0]
    c_act = jax.nn.silu(c)
    for l in range(DEPTH):
        mod = c_act @ ada_w[l] + ada_b[l]
        sh1, sc1, g1, sh2, sc2, g2 = [m[:, None, :] for m in jnp.split(mod, 6, axis=-1)]
        hmix = rmsnorm(x, norm_mix_g[l]) * (1.0 + sc1) + sh1
        j = l // 2
        if l % 2 == 0:
            y = nsa_rglru_mixer(hmix, rel_bias, ev_w_in[j], ev_w_out[j], cmp_pe[j], cmp_w1[j],
                                cmp_b1[j], cmp_w2[j], cmp_b2[j], lru_conv_w[j], lru_conv_b[j],
                                lru_wa[j], lru_ba[j], lru_wi[j], lru_bi[j], lru_lambda[j])
        else:
            y = hgrn2_mixer(hmix, lb_all[l], od_w_in[j], od_w_out[j], hg_norm_g[j])
        x = x + g1 * y
        hffn = rmsnorm(x, norm_ffn_g[l]) * (1.0 + sc2) + sh2
        x = x + g2 * hier_moe(hffn, moe_w_grp[l], moe_b_grp[l], moe_w_exp[l], moe_b_exp[l],
                              moe_w_gate[l], moe_w_up[l], moe_w_down[l])
    return rmsnorm(x, final_g)
```

```python
import functools
import math

import numpy as np
import jax
import jax.numpy as jnp
from jax import lax
from jax.experimental import pallas as pl
from jax.experimental.pallas import tpu as pltpu

F32 = jnp.float32
BF16 = jnp.bfloat16

D_MODEL = 2048
DEPTH = 2
NSA_HEADS = 8
NSA_KV_HEADS = 2
NSA_HPG = NSA_HEADS // NSA_KV_HEADS
HEAD_DIM = 128
NSA_KV_W = NSA_KV_HEADS * HEAD_DIM
CMP_LEN = 32
CMP_STRIDE = 16
SLC_LEN = 64
SLC_TOPN = 16
WINDOW = 512
Q_BLOCK = 128
LRU_WIDTH = 1024
LRU_BLOCKS = 8
LRU_BW = LRU_WIDTH // LRU_BLOCKS
CONV_W = 4
LRU_C = 8.0
HG_HEADS = 16
HG_DK = 128
HG_DV = 128
HG_CHUNK = 64
HG_SUB = 16
HG_KW = HG_HEADS * HG_DK
HG_VW = HG_HEADS * HG_DV
N_BUCKETS = 32
MAX_DIST = 128
N_GROUPS = 8
EXP_PER_GROUP = 8
N_EXPERTS = N_GROUPS * EXP_PER_GROUP
TOPK_IN_GROUP = 2
D_EXPERT = 512
MOE_BLOCK = 128
EPS = 1e-6

LANES = 128
NEG = -1e30
M_FLOOR = -1e20
VMEM_LIMIT = 56 * 1024 * 1024


def _cparams(sem):
    return pltpu.CompilerParams(dimension_semantics=sem, vmem_limit_bytes=VMEM_LIMIT)


def _dot(a, b):
    return jnp.dot(a, b, preferred_element_type=F32)


def _dot_nt(a, b):
    return lax.dot_general(a, b, (((1,), (1,)), ((), ())), preferred_element_type=F32)


def _dot_tn(a, b):
    return lax.dot_general(a, b, (((0,), (0,)), ((), ())), preferred_element_type=F32)


def _bucket_starts():
    n = np.arange(0, MAX_DIST + 1, dtype=np.int32)
    max_exact = N_BUCKETS // 2
    nf = np.maximum(n, 1).astype(np.float32)
    large = max_exact + (np.log(nf / np.float32(max_exact)) / np.float32(math.log(MAX_DIST / max_exact))
                         * np.float32(N_BUCKETS - max_exact)).astype(np.int32)
    large = np.minimum(large, N_BUCKETS - 1)
    b = np.where(n < max_exact, n, large)
    starts = [int(np.argmax(b >= k)) for k in range(N_BUCKETS)]
    assert all(b[s] == k for k, s in enumerate(starts)) and b[-1] == N_BUCKETS - 1
    return starts


BUCKET_STARTS = _bucket_starts()


def _mod_kernel(c_ref, w_ref, b_ref, o_ref):
    c = c_ref[...]
    cact = c * jax.nn.sigmoid(c)
    o_ref[...] = jnp.sum(cact * w_ref[...], axis=0, keepdims=True) + b_ref[...]


def _modulation(c, ada_w, ada_b):
    D = D_MODEL
    tn = 1024
    n_out = 6 * D
    c_col = c.reshape(D, 1)
    return pl.pallas_call(
        _mod_kernel,
        out_shape=jax.ShapeDtypeStruct((DEPTH, 1, n_out), F32),
        grid=(DEPTH, n_out // tn),
        in_specs=[pl.BlockSpec((D, 1), lambda l, j: (0, 0)),
                  pl.BlockSpec((None, D, tn), lambda l, j: (l, 0, j)),
                  pl.BlockSpec((None, 1, tn), lambda l, j: (l, 0, j))],
        out_specs=pl.BlockSpec((None, 1, tn), lambda l, j: (l, 0, j)),
        compiler_params=_cparams(("arbitrary", "arbitrary")),
        name="adaln_mod",
    )(c_col, ada_w, ada_b.reshape(DEPTH, 1, n_out))


def _norm_mod(x, g, sc, sh):
    y = x * lax.rsqrt(jnp.mean(x * x, axis=-1, keepdims=True) + EPS)
    return (y * g) * (1.0 + sc) + sh


def _norm_proj_kernel(x_ref, g_ref, sc_ref, sh_ref, w_ref, *rest, has_extra):
    if has_extra:
        wx_ref, o_ref, ox_ref, h_ref = rest
    else:
        o_ref, h_ref = rest

    @pl.when(pl.program_id(1) == 0)
    def _():
        h = _norm_mod(x_ref[...], g_ref[...], sc_ref[...], sh_ref[...]).astype(BF16)
        h_ref[...] = h
        if has_extra:
            ox_ref[...] = _dot(h, wx_ref[...])

    o_ref[...] = _dot(h_ref[...], w_ref[...])


def _norm_proj(x, g, sc, sh, w, w_extra=None, tm=1024, tn=512):
    T, D = x.shape
    N = w.shape[1]
    has_extra = w_extra is not None
    vec = pl.BlockSpec((1, D), lambda i, j: (0, 0))
    in_specs = [pl.BlockSpec((tm, D), lambda i, j: (i, 0)), vec, vec, vec,
                pl.BlockSpec((D, tn), lambda i, j: (0, j))]
    out_shape = [jax.ShapeDtypeStruct((T, N), F32)]
    out_specs = [pl.BlockSpec((tm, tn), lambda i, j: (i, j))]
    args = [x, g, sc, sh, w]
    if has_extra:
        nx = w_extra.shape[1]
        in_specs.append(pl.BlockSpec((D, nx), lambda i, j: (0, 0)))
        out_shape.append(jax.ShapeDtypeStruct((T, nx), F32))
        out_specs.append(pl.BlockSpec((tm, nx), lambda i, j: (i, 0)))
        args.append(w_extra)
    return pl.pallas_call(
        functools.partial(_norm_proj_kernel, has_extra=has_extra),
        out_shape=out_shape,
        grid=(T // tm, N // tn),
        in_specs=in_specs,
        out_specs=out_specs,
        scratch_shapes=[pltpu.VMEM((tm, D), BF16)],
        compiler_params=_cparams(("arbitrary", "arbitrary")),
        name="norm_proj",
    )(*args)


def _out_proj_kernel(a1_ref, a2_ref, w1_ref, w2_ref, x_ref, g_ref, o_ref):
    y = _dot(a1_ref[...], w1_ref[...]) + _dot(a2_ref[...], w2_ref[...])
    o_ref[...] = x_ref[...] + g_ref[...] * y


def _out_proj(a1, a2, w1, w2, x, gate, tm=1024, tn=512):
    T, D = x.shape
    K1, K2 = a1.shape[1], a2.shape[1]
    return pl.pallas_call(
        _out_proj_kernel,
        out_shape=jax.ShapeDtypeStruct((T, D), F32),
        grid=(T // tm, D // tn),
        in_specs=[pl.BlockSpec((tm, K1), lambda i, j: (i, 0)),
                  pl.BlockSpec((tm, K2), lambda i, j: (i, 0)),
                  pl.BlockSpec((K1, tn), lambda i, j: (0, j)),
                  pl.BlockSpec((K2, tn), lambda i, j: (0, j)),
                  pl.BlockSpec((tm, tn), lambda i, j: (i, j)),
                  pl.BlockSpec((1, tn), lambda i, j: (0, j))],
        out_specs=pl.BlockSpec((tm, tn), lambda i, j: (i, j)),
        compiler_params=_cparams(("arbitrary", "arbitrary")),
        name="out_proj",
    )(a1, a2, w1, w2, x, gate)


def _compress_kernel(k2_ref, pe_ref, w1_ref, b1_ref, w2_ref, b2_ref, o_ref):
    half = (CMP_LEN // 2) * HEAD_DIM
    k2 = k2_ref[...]
    a = _dot((k2 + pe_ref[0:1, :]).astype(BF16), w1_ref[0:half, :].astype(BF16))
    b = _dot((k2 + pe_ref[1:2, :]).astype(BF16), w1_ref[half:2 * half, :].astype(BF16))
    nb = k2.shape[0]
    b_up = pltpu.roll(b, nb - 1, axis=0)
    hid = jax.nn.gelu(a + b_up + b1_ref[...])
    o_ref[...] = _dot(hid.astype(BF16), w2_ref[...].astype(BF16)) + b2_ref[...]


def _compress(kv, pe, w1, b1, w2, b2):
    _, G, T, HD = kv.shape
    nb = T // CMP_STRIDE
    row = CMP_STRIDE * HD
    kv2 = kv.reshape(2, G, nb, row)
    pe2 = pe.reshape(2, 2, row)
    return pl.pallas_call(
        _compress_kernel,
        out_shape=jax.ShapeDtypeStruct((2, G, nb, HD), F32),
        grid=(2, G),
        in_specs=[pl.BlockSpec((None, None, nb, row), lambda a, g: (a, g, 0, 0)),
                  pl.BlockSpec((None, 2, row), lambda a, g: (a, 0, 0)),
                  pl.BlockSpec((None, CMP_LEN * HD, HD), lambda a, g: (a, 0, 0)),
                  pl.BlockSpec((None, 1, HD), lambda a, g: (a, 0, 0)),
                  pl.BlockSpec((None, HD, HD), lambda a, g: (a, 0, 0)),
                  pl.BlockSpec((None, 1, HD), lambda a, g: (a, 0, 0))],
        out_specs=pl.BlockSpec((None, None, nb, HD), lambda a, g: (a, g, 0, 0)),
        compiler_params=_cparams(("arbitrary", "arbitrary")),
        name="nsa_compress",
    )(kv2, pe2, w1, b1.reshape(2, 1, HD), w2, b2.reshape(2, 1, HD))


def _bias_from_rel(rel, tbl_ref, head):
    val = jnp.full(rel.shape, tbl_ref[N_BUCKETS - 1, head], F32)
    for b in range(N_BUCKETS - 2, -1, -1):
        val = jnp.where(rel < BUCKET_STARTS[b + 1], tbl_ref[b, head], val)
    return val


def _lane_tile4(x):
    return jnp.concatenate([x, x, x, x], axis=1)


def _col_softmax_stats(s):
    m = jnp.maximum(jnp.max(s, axis=0, keepdims=True), M_FLOOR)
    e = jnp.exp(s - m)
    d = jnp.sum(e, axis=0, keepdims=True)
    return e, d


def _nsa_kernel(tbl_ref, q_ref, gl_ref, kc_ref, vc_ref, ks_ref, vs_ref, kw_ref, vw_ref, o_ref,
                bd_ref, bp_ref, bn_ref, imp_ref, sel_ref, *, ncp):
    g = pl.program_id(0)
    ci = pl.program_id(1)
    Q = Q_BLOCK
    HD = HEAD_DIM
    H4 = NSA_HPG

    row_i = lax.broadcasted_iota(jnp.int32, (Q, Q), 0)
    lane_i = lax.broadcasted_iota(jnp.int32, (Q, Q), 1)

    @pl.when((g == 0) & (ci == 0))
    def _build_bias_tiles():
        for h in range(NSA_HEADS):
            rel_d = lane_i - row_i
            bd_ref[h] = jnp.where(rel_d >= 0, _bias_from_rel(jnp.maximum(rel_d, 0), tbl_ref, h), NEG)
            bp_ref[h] = _bias_from_rel(lane_i - row_i + Q, tbl_ref, h)
            rel_n = lane_i - CMP_STRIDE * row_i + (CMP_STRIDE * (Q - 8) - (CMP_LEN - 1))
            bn_ref[h] = jnp.where(rel_n >= 0, _bias_from_rel(jnp.maximum(rel_n, 0), tbl_ref, h), NEG)

    def head_tiles(ref):
        return jnp.concatenate([ref[g * H4 + h] for h in range(H4)], axis=1)

    c31 = jnp.concatenate([jnp.full((1, Q), tbl_ref[N_BUCKETS - 1, g * H4 + h], F32) for h in range(H4)],
                          axis=1)

    qt = q_ref[...] * (HD ** -0.5)
    qs = jnp.concatenate([qt[:, h * HD:(h + 1) * HD] for h in range(H4)], axis=0).astype(BF16)

    near0 = 8 * ci - (Q - 8)
    kc_far = kc_ref[pl.ds(Q, ncp), :]
    vc_far = vc_ref[pl.ds(Q, ncp), :]
    near_row = pl.multiple_of(8 * ci + 8, 8)
    kc_near = kc_ref[pl.ds(near_row, Q), :]
    vc_near = vc_ref[pl.ds(near_row, Q), :]
    n_far = lax.broadcasted_iota(jnp.int32, (ncp, 1), 0)
    s_far = _dot_nt(kc_far.astype(BF16), qs) + c31
    s_far = jnp.where(n_far < near0, s_far, NEG)
    m_near = lax.broadcasted_iota(jnp.int32, (Q, 1), 0)
    s_near = _dot_nt(kc_near.astype(BF16), qs) + head_tiles(bn_ref)
    s_near = jnp.where(m_near + near0 >= 0, s_near, NEG)
    s_c = jnp.concatenate([s_far, s_near], axis=0)
    e_c, d_c = _col_softmax_stats(s_c)
    p_c = e_c * (1.0 / jnp.where(d_c > 0, d_c, 1.0))
    v_c = jnp.concatenate([vc_far, vc_near], axis=0).astype(BF16)
    o_cmp = _dot_tn(v_c, p_c.astype(BF16))

    psum = p_c[:, 0:Q] + p_c[:, Q:2 * Q] + p_c[:, 2 * Q:3 * Q] + p_c[:, 3 * Q:4 * Q]
    p_hi = psum.astype(BF16)
    p_lo = (psum - p_hi.astype(F32)).astype(BF16)
    jb = lax.broadcasted_iota(jnp.int32, (Q, ncp), 0)
    nb = lax.broadcasted_iota(jnp.int32, (Q, ncp), 1)
    ratio = SLC_LEN // CMP_STRIDE
    span = CMP_LEN // CMP_STRIDE - 1
    cover_far = ((nb >= ratio * jb - span) & (nb <= ratio * jb + ratio - 1)).astype(BF16)
    nn = lane_i + near0
    cover_near = ((nn >= ratio * row_i - span) & (nn <= ratio * row_i + ratio - 1)).astype(BF16)
    cover = jnp.concatenate([cover_far, cover_near], axis=1)
    imp = _dot(cover, p_hi) + _dot(cover, p_lo)
    cur = 2 * ci + (lane_i >= SLC_LEN).astype(jnp.int32)
    imp = jnp.where(row_i == cur, jnp.inf, jnp.where(row_i > cur, -jnp.inf, imp))
    imp_ref[...] = imp

    def rank_body(b, cnt):
        r = imp_ref[pl.ds(b, 1), :]
        ahead = (r > imp) | ((r == imp) & (b < row_i))
        return cnt + ahead.astype(F32)

    cnt = lax.fori_loop(0, 2 * ci + 2, rank_body, jnp.zeros((Q, Q), F32))
    sel_ref[...] = ((cnt < SLC_TOPN) & (row_i <= cur)).astype(F32)

    half_rows = lax.broadcasted_iota(jnp.int32, (Q, Q), 0) < SLC_LEN

    def slc_step(kt, bias, carry):
        m, l, acc = carry
        k_t = ks_ref[pl.ds(pl.multiple_of(kt * Q, Q), Q), :]
        v_t = vs_ref[pl.ds(pl.multiple_of(kt * Q, Q), Q), :]
        s = _dot_nt(k_t, qs) + bias
        sel = jnp.where(half_rows, sel_ref[pl.ds(2 * kt, 1), :], sel_ref[pl.ds(2 * kt + 1, 1), :])
        s = jnp.where(_lane_tile4(sel) > 0, s, NEG)
        m_new = jnp.maximum(m, jnp.max(s, axis=0, keepdims=True))
        alpha = jnp.exp(m - m_new)
        p = jnp.exp(s - m_new)
        l_new = alpha * l + jnp.sum(p, axis=0, keepdims=True)
        acc_new = alpha * acc + _dot_tn(v_t, p.astype(BF16))
        return m_new, l_new, acc_new

    carry0 = (jnp.full((1, H4 * Q), M_FLOOR, F32), jnp.zeros((1, H4 * Q), F32), jnp.zeros((HD, H4 * Q), F32))
    carry = lax.fori_loop(0, jnp.maximum(ci - 1, 0), lambda kt, c: slc_step(kt, c31, c), carry0)
    bp_t = head_tiles(bp_ref)
    bd_t = head_tiles(bd_ref)
    carry = lax.cond(ci >= 1, lambda c: slc_step(ci - 1, bp_t, c), lambda c: c, carry)
    _, l_s, acc_s = slc_step(ci, bd_t, carry)
    o_slc = acc_s * (1.0 / jnp.where(l_s > 0, l_s, 1.0))

    kw = kw_ref[pl.ds(pl.multiple_of(ci * Q, Q), WINDOW + Q), :]
    vw = vw_ref[pl.ds(pl.multiple_of(ci * Q, Q), WINDOW + Q), :]
    s_w = _dot_nt(kw, qs)
    n_wt = WINDOW // Q
    first = jnp.where(_lane_tile4(lane_i < row_i), c31, NEG)
    pieces = [s_w[0:Q] + first]
    for t in range(1, n_wt - 1):
        pieces.append(s_w[t * Q:(t + 1) * Q] + c31)
    pieces.append(s_w[(n_wt - 1) * Q:n_wt * Q] + bp_t)
    pieces.append(s_w[n_wt * Q:(n_wt + 1) * Q] + bd_t)
    s_w = jnp.concatenate(pieces, axis=0)
    x_w = lax.broadcasted_iota(jnp.int32, (WINDOW + Q, 1), 0)
    s_w = jnp.where(x_w + ci * Q >= WINDOW, s_w, NEG)
    e_w, d_w = _col_softmax_stats(s_w)
    p_w = e_w * (1.0 / jnp.where(d_w > 0, d_w, 1.0))
    o_win = _dot_tn(vw, p_w.astype(BF16))

    gt = jnp.transpose(jax.nn.sigmoid(gl_ref[...]))
    outs = []
    for h in range(H4):
        base = (g * H4 + h) * 3
        sl = slice(h * Q, (h + 1) * Q)
        gc = gt_row(gt, base)
        gs = gt_row(gt, base + 1)
        gw = gt_row(gt, base + 2)
        o_h = gc * o_cmp[:, sl] + gs * o_slc[:, sl] + gw * o_win[:, sl]
        outs.append(jnp.transpose(o_h))
    o_ref[...] = jnp.concatenate(outs, axis=1).astype(o_ref.dtype)


def gt_row(gt, idx):
    rows = lax.broadcasted_iota(jnp.int32, gt.shape, 0)
    return jnp.sum(jnp.where(rows == idx, gt, 0.0), axis=0, keepdims=True)


def _nsa_attention(rel_bias, proj, proj_gate, kcmp, vcmp, ks, vs, kw, vw):
    T = proj.shape[0]
    G = NSA_KV_HEADS
    Q = Q_BLOCK
    ncp = T // CMP_STRIDE
    nch = T // Q
    kernel = functools.partial(_nsa_kernel, ncp=ncp)
    full = lambda rows: pl.BlockSpec((None, rows, HEAD_DIM), lambda g, c: (g, 0, 0))
    return pl.pallas_call(
        kernel,
        out_shape=jax.ShapeDtypeStruct((T, NSA_HEADS * HEAD_DIM), BF16),
        grid=(G, nch),
        in_specs=[pl.BlockSpec(memory_space=pltpu.SMEM),
                  pl.BlockSpec((Q, NSA_HPG * HEAD_DIM), lambda g, c: (c, g)),
                  pl.BlockSpec((Q, LANES), lambda g, c: (c, 0)),
                  full(ncp + 2 * Q), full(ncp + 2 * Q),
                  full(T), full(T), full(T + WINDOW), full(T + WINDOW)],
        out_specs=pl.BlockSpec((Q, NSA_HPG * HEAD_DIM), lambda g, c: (c, g)),
        scratch_shapes=[pltpu.VMEM((NSA_HEADS, Q, Q), F32), pltpu.VMEM((NSA_HEADS, Q, Q), F32),
                        pltpu.VMEM((NSA_HEADS, Q, Q), F32), pltpu.VMEM((Q, Q), F32), pltpu.VMEM((Q, Q), F32)],
        compiler_params=_cparams(("arbitrary", "arbitrary")),
        name="nsa_attention",
    )(rel_bias, proj, proj_gate, kcmp, vcmp, ks, vs, kw, vw)


def _softplus(z):
    return jnp.maximum(z, 0.0) + jnp.log1p(jnp.exp(-jnp.abs(z)))


def _lru_kernel(x_ref, y_ref, cw_ref, cb_ref, wa_ref, ba_ref, wi_ref, bi_ref, lam_ref, o_ref,
                xbuf, hc, a_s, b_s, h_s, *, tb):
    i = pl.program_id(0)

    @pl.when(i == 0)
    def _():
        xbuf[0:8, :] = jnp.zeros((8, LRU_WIDTH), F32)
        hc[...] = jnp.zeros((1, LRU_WIDTH), F32)

    xbuf[8:8 + tb, :] = x_ref[...]
    xc = cb_ref[...]
    for w in range(CONV_W):
        xc = xc + cw_ref[w:w + 1, :] * xbuf[8 - (CONV_W - 1) + w:8 - (CONV_W - 1) + w + tb, :]
    xbuf[0:8, :] = xbuf[tb:tb + 8, :]

    xcb = xc.astype(BF16)
    ra, ia = [], []
    for n in range(LRU_BLOCKS):
        xg = xcb[:, n * LRU_BW:(n + 1) * LRU_BW]
        ra.append(_dot(xg, wa_ref[n].astype(BF16)))
        ia.append(_dot(xg, wi_ref[n].astype(BF16)))
    r_gate = jax.nn.sigmoid(jnp.concatenate(ra, axis=1) + ba_ref[...])
    i_gate = jax.nn.sigmoid(jnp.concatenate(ia, axis=1) + bi_ref[...])
    log_a = (-LRU_C * r_gate) * _softplus(-lam_ref[...])
    a = jnp.exp(log_a)
    a_s[...] = a
    b_s[...] = jnp.sqrt(-jnp.tanh(log_a) * (a * a + 1.0)) * (i_gate * xc)

    rows = lax.broadcasted_iota(jnp.int32, (8, LRU_WIDTH), 0)

    def tile(k, h):
        r0 = pl.multiple_of(k * 8, 8)
        A = a_s[pl.ds(r0, 8), :]
        B = b_s[pl.ds(r0, 8), :]
        for sh in (1, 2, 4):
            ok = rows >= sh
            A_p = pltpu.roll(A, sh, axis=0)
            B_p = pltpu.roll(B, sh, axis=0)
            B = jnp.where(ok, A * B_p + B, B)
            A = jnp.where(ok, A * A_p, A)
        H = A * h + B
        h_s[pl.ds(r0, 8), :] = H
        return H[7:8, :]

    hc[...] = lax.fori_loop(0, tb // 8, tile, hc[...])
    o_ref[...] = (jax.nn.gelu(y_ref[...]) * h_s[...]).astype(o_ref.dtype)


def _rglru(proj, y_col, x_col, conv_w, conv_b, wa, ba, wi, bi, lam, tb=256):
    T = proj.shape[0]
    W = LRU_WIDTH
    vec = pl.BlockSpec((1, W), lambda i: (0, 0))
    blk = pl.BlockSpec((LRU_BLOCKS, LRU_BW, LRU_BW), lambda i: (0, 0, 0))
    return pl.pallas_call(
        functools.partial(_lru_kernel, tb=tb),
        out_shape=jax.ShapeDtypeStruct((T, W), BF16),
        grid=(T // tb,),
        in_specs=[pl.BlockSpec((tb, W), lambda i: (i, x_col)),
                  pl.BlockSpec((tb, W), lambda i: (i, y_col)),
                  pl.BlockSpec((CONV_W, W), lambda i: (0, 0)), vec, blk, vec, blk, vec, vec],
        out_specs=pl.BlockSpec((tb, W), lambda i: (i, 0)),
        scratch_shapes=[pltpu.VMEM((tb + 8, W), F32), pltpu.VMEM((1, W), F32),
                        pltpu.VMEM((tb, W), F32), pltpu.VMEM((tb, W), F32), pltpu.VMEM((tb, W), F32)],
        compiler_params=_cparams(("arbitrary",)),
        name="rglru",
    )(proj, proj, conv_w, conv_b.reshape(1, W), wa, ba.reshape(1, W), wi, bi.reshape(1, W), lam.reshape(1, W))


def _split3(x):
    hi = x.astype(BF16)
    r = x - hi.astype(F32)
    mid = r.astype(BF16)
    lo = (r - mid.astype(F32)).astype(BF16)
    return hi, mid, lo


def _hgrn_kernel(q_ref, f_ref, v_ref, g_ref, lbl_ref, ng_ref, o_ref, st_ref, oacc, *, tbh, layer):
    t = pl.program_id(1)
    C = HG_CHUNK
    SB = HG_SUB

    @pl.when(t == 0)
    def _():
        st_ref[...] = jnp.zeros((HG_DV, HG_DK), F32)

    lg = lbl_ref[...]
    e = jnp.exp(lg - jnp.max(lg, axis=0, keepdims=True))
    sm = e / jnp.sum(e, axis=0, keepdims=True)
    cum = sm[0:1, :]
    for l in range(1, layer + 1):
        cum = cum + sm[l:l + 1, :]
    lb = cum - sm[0:1, :]

    tri = (lax.broadcasted_iota(jnp.int32, (C, C), 0) >= lax.broadcasted_iota(jnp.int32, (C, C), 1)).astype(BF16)
    sub_rows = lax.broadcasted_iota(jnp.int32, (SB, HG_DK), 0)

    def chunk(c, carry):
        r0 = pl.multiple_of(c * C, C)
        qz = q_ref[pl.ds(r0, C), :]
        qq = qz * jax.nn.sigmoid(qz)
        f = lb + (1.0 - lb) * jax.nn.sigmoid(f_ref[pl.ds(r0, C), :])
        logf = jnp.log(jnp.maximum(f, 1e-30))
        kk = 1.0 - f
        vv = v_ref[pl.ds(r0, C), :]
        l_hi, l_mid, l_lo = _split3(logf)
        bcum = _dot(tri, l_hi) + _dot(tri, l_mid) + _dot(tri, l_lo)
        st = st_ref[...]
        o_parts = []
        vb = vv.astype(BF16)
        for I in range(C // SB):
            lo_r, hi_r = I * SB, (I + 1) * SB
            qI = qq[lo_r:hi_r]
            bI = bcum[lo_r:hi_r]
            kI = kk[lo_r:hi_r]
            vI = vv[lo_r:hi_r]
            o_I = jnp.zeros((SB, HG_DV), F32)
            if I > 0:
                ref = bcum[lo_r - 1:lo_r]
                qe = (qI * jnp.exp(bI - ref)).astype(BF16)
                ke = (kk[0:lo_r] * jnp.exp(ref - bcum[0:lo_r])).astype(BF16)
                att = _dot_nt(qe, ke)
                o_I = o_I + _dot(att.astype(BF16), vb[0:lo_r])
            for s in range(SB):
                w = qI * kI[s:s + 1] * jnp.exp(jnp.minimum(bI - bI[s:s + 1], 0.0))
                w = jnp.where(sub_rows >= s, w, 0.0)
                o_I = o_I + jnp.sum(w, axis=1, keepdims=True) * vI[s:s + 1]
            o_parts.append(o_I)
        o_intra = jnp.concatenate(o_parts, axis=0)
        o_inter = _dot_nt((qq * jnp.exp(bcum)).astype(BF16), st.astype(BF16))
        oacc[pl.ds(r0, C), :] = o_inter + o_intra
        b_last = bcum[C - 1:C]
        kd = (kk * jnp.exp(b_last - bcum)).astype(BF16)
        st_ref[...] = st * jnp.exp(b_last) + _dot_tn(vb, kd)
        return carry

    lax.fori_loop(0, tbh // C, chunk, 0)
    o = oacc[...]
    o = o * lax.rsqrt(jnp.mean(o * o, axis=-1, keepdims=True) + EPS)
    gz = g_ref[...]
    o_ref[...] = ((o * ng_ref[...]) * (gz * jax.nn.sigmoid(gz))).astype(o_ref.dtype)


def _hgrn2(proj, lb_logits, norm_g, layer, tbh=512):
    T = proj.shape[0]
    H = HG_HEADS
    col = lambda off: pl.BlockSpec((tbh, HG_DK), lambda h, t: (t, off + h))
    return pl.pallas_call(
        functools.partial(_hgrn_kernel, tbh=tbh, layer=layer),
        out_shape=jax.ShapeDtypeStruct((T, HG_VW), BF16),
        grid=(H, T // tbh),
        in_specs=[col(0), col(H), col(2 * H), col(3 * H),
                  pl.BlockSpec((DEPTH, HG_DK), lambda h, t: (0, h)),
                  pl.BlockSpec((1, HG_DV), lambda h, t: (0, h))],
        out_specs=pl.BlockSpec((tbh, HG_DV), lambda h, t: (t, h)),
        scratch_shapes=[pltpu.VMEM((HG_DV, HG_DK), F32), pltpu.VMEM((tbh, HG_DV), F32)],
        compiler_params=_cparams(("arbitrary", "arbitrary")),
        name="hgrn2",
    )(proj, proj, proj, proj, lb_logits, norm_g.reshape(1, HG_VW))


def _router_kernel(x_ref, g_ref, sc_ref, sh_ref, w_ref, b_ref, h_ref, eid_ref, gate_ref):
    h = _norm_mod(x_ref[...], g_ref[...], sc_ref[...], sh_ref[...])
    h_ref[...] = h.astype(BF16)
    h1, h2, h3 = _split3(h)
    w1, w2, w3 = _split3(w_ref[...])
    logits = (_dot(h1, w1) + (_dot(h1, w2) + _dot(h2, w1))
              + (_dot(h2, w2) + _dot(h1, w3) + _dot(h3, w1))) + b_ref[...]
    lane = lax.broadcasted_iota(jnp.int32, logits.shape, 1)
    is_g = lane < N_GROUPS
    glog = jnp.where(is_g, logits, -jnp.inf)
    gmax = jnp.max(glog, axis=-1, keepdims=True)
    grp = jnp.min(jnp.where(glog == gmax, lane, LANES), axis=-1, keepdims=True)
    gsum = jnp.sum(jnp.where(is_g, jnp.exp(glog - gmax), 0.0), axis=-1, keepdims=True)
    p_grp = 1.0 / gsum
    lo = N_GROUPS + EXP_PER_GROUP * grp
    el = jnp.where((lane >= lo) & (lane < lo + EXP_PER_GROUP), logits, -jnp.inf)
    v1 = jnp.max(el, axis=-1, keepdims=True)
    i1 = jnp.min(jnp.where(el == v1, lane, LANES), axis=-1, keepdims=True)
    el2 = jnp.where(lane == i1, -jnp.inf, el)
    v2 = jnp.max(el2, axis=-1, keepdims=True)
    i2 = jnp.min(jnp.where(el2 == v2, lane, LANES), axis=-1, keepdims=True)
    e2 = jnp.exp(v2 - v1)
    den = 1.0 + e2
    g1 = p_grp * (1.0 / den)
    g2 = p_grp * (e2 / den)
    eid_ref[...] = jnp.where(lane == 0, i1 - N_GROUPS, jnp.where(lane == 1, i2 - N_GROUPS, 0))
    gate_ref[...] = jnp.where(lane == 0, g1, jnp.where(lane == 1, g2, 0.0))


def _router(x, g, sc, sh, w_r, b_r, tm=512):
    T, D = x.shape
    vec = pl.BlockSpec((1, D), lambda i: (0, 0))
    return pl.pallas_call(
        _router_kernel,
        out_shape=[jax.ShapeDtypeStruct((T, D), BF16), jax.ShapeDtypeStruct((T, LANES), jnp.int32),
                   jax.ShapeDtypeStruct((T, LANES), F32)],
        grid=(T // tm,),
        in_specs=[pl.BlockSpec((tm, D), lambda i: (i, 0)), vec, vec, vec,
                  pl.BlockSpec((D, LANES), lambda i: (0, 0)), pl.BlockSpec((1, LANES), lambda i: (0, 0))],
        out_specs=[pl.BlockSpec((tm, D), lambda i: (i, 0)), pl.BlockSpec((tm, LANES), lambda i: (i, 0)),
                   pl.BlockSpec((tm, LANES), lambda i: (i, 0))],
        compiler_params=_cparams(("arbitrary",)),
        name="moe_router",
    )(x, g, sc, sh, w_r, b_r)


def _gather_rows_kernel(idx_ref, src_ref, o_ref, sem, *, rows, n_src):
    i = pl.program_id(0)

    def copy(r):
        tok = idx_ref[i * rows + r]
        return pltpu.make_async_copy(src_ref.at[pl.ds(jnp.minimum(tok, n_src - 1), 1)], o_ref.at[pl.ds(r, 1)], sem)

    def start(r, c):
        @pl.when(idx_ref[i * rows + r] < n_src)
        def _():
            copy(r).start()
        return c

    def finish(r, c):
        @pl.when(idx_ref[i * rows + r] < n_src)
        def _():
            copy(r).wait()

        @pl.when(idx_ref[i * rows + r] >= n_src)
        def _():
            o_ref[pl.ds(r, 1), :] = jnp.zeros((1, o_ref.shape[1]), o_ref.dtype)
        return c

    lax.fori_loop(0, rows, start, 0)
    lax.fori_loop(0, rows, finish, 0)


def _gather_rows(src, idx, rows=256):
    n_src, w = src.shape
    n_out = idx.shape[0]
    return pl.pallas_call(
        functools.partial(_gather_rows_kernel, rows=rows, n_src=n_src),
        out_shape=jax.ShapeDtypeStruct((n_out, w), src.dtype),
        grid_spec=pltpu.PrefetchScalarGridSpec(
            num_scalar_prefetch=1,
            grid=(n_out // rows,),
            in_specs=[pl.BlockSpec(memory_space=pl.ANY)],
            out_specs=pl.BlockSpec((rows, w), lambda i, idx_ref: (i, 0)),
            scratch_shapes=[pltpu.SemaphoreType.DMA(())]),
        compiler_params=_cparams(("arbitrary",)),
        name="moe_gather_rows",
    )(idx, src)


def _expert_kernel(be_ref, x_ref, wg_ref, wu_ref, wd_ref, y_ref, wgb, wub, wdb):
    i = pl.program_id(0)
    prev = be_ref[jnp.maximum(i - 1, 0)]

    @pl.when((i == 0) | (be_ref[i] != prev))
    def _():
        wgb[...] = wg_ref[...].astype(BF16)
        wub[...] = wu_ref[...].astype(BF16)
        wdb[...] = wd_ref[...].astype(BF16)

    x = x_ref[...]
    hg = _dot(x, wgb[...])
    hu = _dot(x, wub[...])
    hid = (hg * jax.nn.sigmoid(hg)) * hu
    y_ref[...] = _dot(hid.astype(BF16), wdb[...])


def _experts(xb, blk_e, w_gate, w_up, w_down):
    n_pad, D = xb.shape
    n_blk = n_pad // MOE_BLOCK
    return pl.pallas_call(
        _expert_kernel,
        out_shape=jax.ShapeDtypeStruct((n_pad, D), F32),
        grid_spec=pltpu.PrefetchScalarGridSpec(
            num_scalar_prefetch=1,
            grid=(n_blk,),
            in_specs=[pl.BlockSpec((MOE_BLOCK, D), lambda i, be: (i, 0)),
                      pl.BlockSpec((None, D, D_EXPERT), lambda i, be: (be[i], 0, 0)),
                      pl.BlockSpec((None, D, D_EXPERT), lambda i, be: (be[i], 0, 0)),
                      pl.BlockSpec((None, D_EXPERT, D), lambda i, be: (be[i], 0, 0))],
            out_specs=pl.BlockSpec((MOE_BLOCK, D), lambda i, be: (i, 0)),
            scratch_shapes=[pltpu.VMEM((D, D_EXPERT), BF16), pltpu.VMEM((D, D_EXPERT), BF16),
                            pltpu.VMEM((D_EXPERT, D), BF16)]),
        compiler_params=_cparams(("arbitrary",)),
        name="moe_experts",
    )(blk_e, xb, w_gate, w_up, w_down)


def _combine_kernel(slot_ref, y_ref, x_ref, w_ref, g_ref, o_ref, buf, sem, *, rows):
    i = pl.program_id(0)

    def copy(r, k):
        return pltpu.make_async_copy(y_ref.at[pl.ds(slot_ref[(i * rows + r) * 2 + k], 1)],
                                     buf.at[k, pl.ds(r, 1)], sem)

    def start(r, c):
        copy(r, 0).start()
        copy(r, 1).start()
        return c

    def finish(r, c):
        copy(r, 0).wait()
        copy(r, 1).wait()
        return c

    lax.fori_loop(0, rows, start, 0)
    lax.fori_loop(0, rows, finish, 0)
    w = w_ref[...]
    moe = buf[0] * w[:, 0:1] + buf[1] * w[:, 1:2]
    o_ref[...] = x_ref[...] + g_ref[...] * moe


def _combine(yb, slots, x, gate_w, g2, rows=256):
    T, D = x.shape
    return pl.pallas_call(
        functools.partial(_combine_kernel, rows=rows),
        out_shape=jax.ShapeDtypeStruct((T, D), F32),
        grid_spec=pltpu.PrefetchScalarGridSpec(
            num_scalar_prefetch=1,
            grid=(T // rows,),
            in_specs=[pl.BlockSpec(memory_space=pl.ANY),
                      pl.BlockSpec((rows, D), lambda i, s: (i, 0)),
                      pl.BlockSpec((rows, LANES), lambda i, s: (i, 0)),
                      pl.BlockSpec((1, D), lambda i, s: (0, 0))],
            out_specs=pl.BlockSpec((rows, D), lambda i, s: (i, 0)),
            scratch_shapes=[pltpu.VMEM((2, rows, D), F32), pltpu.SemaphoreType.DMA(())]),
        compiler_params=_cparams(("arbitrary",)),
        name="moe_combine",
    )(slots.reshape(-1), yb, x, gate_w, g2)


def _hier_moe_residual(x, g, sc, sh, g2, w_grp, b_grp, w_exp, b_exp, w_gate, w_up, w_down):
    N, D = x.shape
    n_route = N_GROUPS + N_EXPERTS
    w_r = jnp.zeros((D, LANES), F32).at[:, :N_GROUPS].set(w_grp).at[:, N_GROUPS:n_route].set(w_exp)
    b_r = jnp.zeros((1, LANES), F32).at[0, :N_GROUPS].set(b_grp).at[0, N_GROUPS:n_route].set(b_exp)
    h, eid_l, gate_l = _router(x, g, sc, sh, w_r, b_r)
    eid = eid_l[:, :TOPK_IN_GROUP]

    A = N * TOPK_IN_GROUP
    e_flat = eid.reshape(-1)
    tok = jnp.repeat(jnp.arange(N, dtype=jnp.int32), TOPK_IN_GROUP)
    order = jnp.argsort(e_flat)
    e_s, tok_s = e_flat[order], tok[order]
    counts = jnp.zeros((N_EXPERTS,), jnp.int32).at[e_flat].add(1)
    start = jnp.cumsum(counts) - counts
    padded = (counts + MOE_BLOCK - 1) // MOE_BLOCK * MOE_BLOCK
    pad_end = jnp.cumsum(padded)
    pad_start = pad_end - padded
    dest = pad_start[e_s] + jnp.arange(A, dtype=jnp.int32) - start[e_s]
    n_blk = -(-(A + N_EXPERTS * MOE_BLOCK) // MOE_BLOCK)
    n_pad = n_blk * MOE_BLOCK
    buf_tok = jnp.full((n_pad,), N, jnp.int32).at[dest].set(tok_s)
    blk_e = jnp.minimum(jnp.searchsorted(pad_end, jnp.arange(n_blk, dtype=jnp.int32) * MOE_BLOCK, side='right'),
                        N_EXPERTS - 1).astype(jnp.int32)
    slots = jnp.zeros((A,), jnp.int32).at[order].set(dest).reshape(N, TOPK_IN_GROUP)

    h32 = lax.bitcast_convert_type(h.reshape(N, D // 2, 2), jnp.uint32)
    xb32 = _gather_rows(h32, buf_tok)
    xb = lax.bitcast_convert_type(xb32, BF16).reshape(n_pad, D)
    yb = _experts(xb, blk_e, w_gate, w_up, w_down)
    return _combine(yb, slots, x, gate_l, g2)


def _final_norm_kernel(x_ref, g_ref, o_ref):
    x = x_ref[...]
    o_ref[...] = (x * lax.rsqrt(jnp.mean(x * x, axis=-1, keepdims=True) + EPS)) * g_ref[...]


def _final_norm(x, g, tm=512):
    T, D = x.shape
    return pl.pallas_call(
        _final_norm_kernel,
        out_shape=jax.ShapeDtypeStruct((T, D), F32),
        grid=(T // tm,),
        in_specs=[pl.BlockSpec((tm, D), lambda i: (i, 0)), pl.BlockSpec((1, D), lambda i: (0, 0))],
        out_specs=pl.BlockSpec((tm, D), lambda i: (i, 0)),
        compiler_params=_cparams(("arbitrary",)),
        name="final_norm",
    )(x, g)


def _even_mixer_residual(x, g, sc, sh, g1, rel_bias, w_in, w_out, cmp_pe, cmp_w1, cmp_b1, cmp_w2, cmp_b2,
                         conv_w, conv_b, lru_wa, lru_ba, lru_wi, lru_bi, lru_lambda):
    T, D = x.shape
    G, HD = NSA_KV_HEADS, HEAD_DIM
    nq = NSA_HEADS * HD
    n_kv = 6 * NSA_KV_W
    n_gate = 3 * NSA_HEADS
    w_main = jnp.concatenate([w_in[:, :nq], w_in[:, nq + n_kv + n_gate:], w_in[:, nq:nq + n_kv]],
                             axis=1).astype(BF16)
    w_gl = jnp.zeros((D, LANES), F32).at[:, :n_gate].set(w_in[:, nq + n_kv:nq + n_kv + n_gate]).astype(BF16)
    proj, proj_gate = _norm_proj(x, g, sc, sh, w_main, w_gl)

    def kv_heads(j):
        c0 = nq + 2 * LRU_WIDTH + j * NSA_KV_W
        return proj[:, c0:c0 + NSA_KV_W].reshape(T, G, HD).transpose(1, 0, 2)

    cmp_in = jnp.stack([kv_heads(0), kv_heads(1)], axis=0)
    cmp_out = _compress(cmp_in, cmp_pe, cmp_w1, cmp_b1, cmp_w2, cmp_b2)
    ncp = T // CMP_STRIDE
    valid = (jnp.arange(ncp) < ncp - 1)[None, None, :, None]
    cmp_pad = jnp.pad(jnp.where(valid, cmp_out, 0.0), ((0, 0), (0, 0), (Q_BLOCK, Q_BLOCK), (0, 0)))
    ks = kv_heads(2).astype(BF16)
    vs = kv_heads(3).astype(BF16)
    kw = jnp.pad(kv_heads(4).astype(BF16), ((0, 0), (WINDOW, 0), (0, 0)))
    vw = jnp.pad(kv_heads(5).astype(BF16), ((0, 0), (WINDOW, 0), (0, 0)))
    nsa_out = _nsa_attention(rel_bias, proj, proj_gate, cmp_pad[0], cmp_pad[1], ks, vs, kw, vw)

    y_col = nq // LRU_WIDTH
    lru_out = _rglru(proj, y_col, y_col + 1, conv_w, conv_b, lru_wa, lru_ba, lru_wi, lru_bi, lru_lambda)
    w_o = w_out.astype(BF16)
    return _out_proj(nsa_out, lru_out, w_o[:nq], w_o[nq:], x, g1)


def _odd_mixer_residual(x, g, sc, sh, g1, lb_logits, w_in, w_out, norm_g, layer):
    proj = _norm_proj(x, g, sc, sh, w_in.astype(BF16))[0]
    o = _hgrn2(proj, lb_logits, norm_g, layer)
    w_o = w_out.astype(BF16)
    half = HG_VW // 2
    return _out_proj(o[:, :half], o[:, half:], w_o[:half], w_o[half:], x, g1)


def kernel(x, c, rel_bias, ada_w, ada_b, norm_mix_g, norm_ffn_g, ev_w_in, ev_w_out, cmp_pe, cmp_w1, cmp_b1, cmp_w2, cmp_b2, lru_conv_w, lru_conv_b, lru_wa, lru_ba, lru_wi, lru_bi, lru_lambda, od_w_in, od_w_out, hg_lb_logits, hg_norm_g, moe_w_grp, moe_b_grp, moe_w_exp, moe_b_exp, moe_w_gate, moe_w_up, moe_w_down, final_g):
    B, T, D = x.shape
    assert B == 1 and D == D_MODEL
    xt = x.reshape(T, D)
    mod = _modulation(c, ada_w, ada_b)
    for l in range(DEPTH):
        sh1, sc1, g1, sh2, sc2, g2 = [mod[l, :, k * D:(k + 1) * D] for k in range(6)]
        gm = norm_mix_g[l].reshape(1, D)
        gf = norm_ffn_g[l].reshape(1, D)
        j = l // 2
        if l % 2 == 0:
            xt = _even_mixer_residual(xt, gm, sc1, sh1, g1, rel_bias, ev_w_in[j], ev_w_out[j], cmp_pe[j],
                                      cmp_w1[j], cmp_b1[j], cmp_w2[j], cmp_b2[j], lru_conv_w[j],
                                      lru_conv_b[j], lru_wa[j], lru_ba[j], lru_wi[j], lru_bi[j], lru_lambda[j])
        else:
            xt = _odd_mixer_residual(xt, gm, sc1, sh1, g1, hg_lb_logits, od_w_in[j], od_w_out[j],
                                     hg_norm_g[j], l)
        xt = _hier_moe_residual(xt, gf, sc2, sh2, g2, moe_w_grp[l], moe_b_grp[l], moe_w_exp[l], moe_b_exp[l],
                                moe_w_gate[l], moe_w_up[l], moe_w_down[l])
    return _final_norm(xt, final_g.reshape(1, D)).reshape(B, T, D)
```

```python
import functools
import math

import numpy as np
import jax
import jax.numpy as jnp
from jax import lax
from jax.experimental import pallas as pl
from jax.experimental.pallas import tpu as pltpu

F32 = jnp.float32
BF16 = jnp.bfloat16

D_MODEL = 2048
DEPTH = 2
NSA_HEADS = 8
NSA_KV_HEADS = 2
NSA_HPG = NSA_HEADS // NSA_KV_HEADS
HEAD_DIM = 128
NSA_KV_W = NSA_KV_HEADS * HEAD_DIM
CMP_LEN = 32
CMP_STRIDE = 16
SLC_LEN = 64
SLC_TOPN = 16
WINDOW = 512
Q_BLOCK = 128
LRU_WIDTH = 1024
LRU_BLOCKS = 8
LRU_BW = LRU_WIDTH // LRU_BLOCKS
CONV_W = 4
LRU_C = 8.0
HG_HEADS = 16
HG_DK = 128
HG_DV = 128
HG_CHUNK = 64
HG_SUB = 16
HG_HEADS_PER_STEP = 2
SLC_GROUP = 4
HG_KW = HG_HEADS * HG_DK
HG_VW = HG_HEADS * HG_DV
N_BUCKETS = 32
MAX_DIST = 128
N_GROUPS = 8
EXP_PER_GROUP = 8
N_EXPERTS = N_GROUPS * EXP_PER_GROUP
TOPK_IN_GROUP = 2
D_EXPERT = 512
MOE_BLOCK = 128
EPS = 1e-6

LANES = 128
NEG = -1e30
M_FLOOR = -1e20
VMEM_LIMIT = 56 * 1024 * 1024


def _cparams(sem):
    return pltpu.CompilerParams(dimension_semantics=sem, vmem_limit_bytes=VMEM_LIMIT)


def _dot(a, b):
    return jnp.dot(a, b, preferred_element_type=F32)


def _dot_nt(a, b):
    return lax.dot_general(a, b, (((1,), (1,)), ((), ())), preferred_element_type=F32)


def _dot_tn(a, b):
    return lax.dot_general(a, b, (((0,), (0,)), ((), ())), preferred_element_type=F32)


def _bucket_starts():
    n = np.arange(0, MAX_DIST + 1, dtype=np.int32)
    max_exact = N_BUCKETS // 2
    nf = np.maximum(n, 1).astype(np.float32)
    large = max_exact + (np.log(nf / np.float32(max_exact)) / np.float32(math.log(MAX_DIST / max_exact))
                         * np.float32(N_BUCKETS - max_exact)).astype(np.int32)
    large = np.minimum(large, N_BUCKETS - 1)
    b = np.where(n < max_exact, n, large)
    starts = [int(np.argmax(b >= k)) for k in range(N_BUCKETS)]
    assert all(b[s] == k for k, s in enumerate(starts)) and b[-1] == N_BUCKETS - 1
    return starts


BUCKET_STARTS = _bucket_starts()


def _mod_kernel(c_ref, w_ref, b_ref, o_ref):
    c = c_ref[...]
    cact = c * jax.nn.sigmoid(c)
    o_ref[...] = jnp.sum(cact * w_ref[...], axis=0, keepdims=True) + b_ref[...]


def _modulation(c, ada_w, ada_b):
    D = D_MODEL
    tn = 1024
    n_out = 6 * D
    c_col = c.reshape(D, 1)
    return pl.pallas_call(
        _mod_kernel,
        out_shape=jax.ShapeDtypeStruct((DEPTH, 1, n_out), F32),
        grid=(DEPTH, n_out // tn),
        in_specs=[pl.BlockSpec((D, 1), lambda l, j: (0, 0)),
                  pl.BlockSpec((None, D, tn), lambda l, j: (l, 0, j)),
                  pl.BlockSpec((None, 1, tn), lambda l, j: (l, 0, j))],
        out_specs=pl.BlockSpec((None, 1, tn), lambda l, j: (l, 0, j)),
        compiler_params=_cparams(("arbitrary", "arbitrary")),
        name="adaln_mod",
    )(c_col, ada_w, ada_b.reshape(DEPTH, 1, n_out))


def _norm_mod(x, g, sc, sh):
    y = x * lax.rsqrt(jnp.mean(x * x, axis=-1, keepdims=True) + EPS)
    return (y * g) * (1.0 + sc) + sh


def _norm_proj_kernel(x_ref, g_ref, sc_ref, sh_ref, w_ref, *rest, has_extra):
    if has_extra:
        wx_ref, o_ref, ox_ref, h_ref = rest
    else:
        o_ref, h_ref = rest

    @pl.when(pl.program_id(1) == 0)
    def _():
        h = _norm_mod(x_ref[...], g_ref[...], sc_ref[...], sh_ref[...]).astype(BF16)
        h_ref[...] = h
        if has_extra:
            ox_ref[...] = _dot(h, wx_ref[...])

    o_ref[...] = _dot(h_ref[...], w_ref[...])


def _norm_proj(x, g, sc, sh, w, w_extra=None, tm=1024, tn=512):
    T, D = x.shape
    N = w.shape[1]
    has_extra = w_extra is not None
    vec = pl.BlockSpec((1, D), lambda i, j: (0, 0))
    in_specs = [pl.BlockSpec((tm, D), lambda i, j: (i, 0)), vec, vec, vec,
                pl.BlockSpec((D, tn), lambda i, j: (0, j))]
    out_shape = [jax.ShapeDtypeStruct((T, N), F32)]
    out_specs = [pl.BlockSpec((tm, tn), lambda i, j: (i, j))]
    args = [x, g, sc, sh, w]
    if has_extra:
        nx = w_extra.shape[1]
        in_specs.append(pl.BlockSpec((D, nx), lambda i, j: (0, 0)))
        out_shape.append(jax.ShapeDtypeStruct((T, nx), F32))
        out_specs.append(pl.BlockSpec((tm, nx), lambda i, j: (i, 0)))
        args.append(w_extra)
    return pl.pallas_call(
        functools.partial(_norm_proj_kernel, has_extra=has_extra),
        out_shape=out_shape,
        grid=(T // tm, N // tn),
        in_specs=in_specs,
        out_specs=out_specs,
        scratch_shapes=[pltpu.VMEM((tm, D), BF16)],
        compiler_params=_cparams(("arbitrary", "arbitrary")),
        name="norm_proj",
    )(*args)


def _out_proj_kernel(a1_ref, a2_ref, w1_ref, w2_ref, x_ref, g_ref, o_ref):
    y = _dot(a1_ref[...], w1_ref[...]) + _dot(a2_ref[...], w2_ref[...])
    o_ref[...] = x_ref[...] + g_ref[...] * y


def _out_proj(a1, a2, w1, w2, x, gate, tm=1024, tn=512):
    T, D = x.shape
    K1, K2 = a1.shape[1], a2.shape[1]
    return pl.pallas_call(
        _out_proj_kernel,
        out_shape=jax.ShapeDtypeStruct((T, D), F32),
        grid=(T // tm, D // tn),
        in_specs=[pl.BlockSpec((tm, K1), lambda i, j: (i, 0)),
                  pl.BlockSpec((tm, K2), lambda i, j: (i, 0)),
                  pl.BlockSpec((K1, tn), lambda i, j: (0, j)),
                  pl.BlockSpec((K2, tn), lambda i, j: (0, j)),
                  pl.BlockSpec((tm, tn), lambda i, j: (i, j)),
                  pl.BlockSpec((1, tn), lambda i, j: (0, j))],
        out_specs=pl.BlockSpec((tm, tn), lambda i, j: (i, j)),
        compiler_params=_cparams(("arbitrary", "arbitrary")),
        name="out_proj",
    )(a1, a2, w1, w2, x, gate)


def _compress_kernel(k2_ref, pe_ref, w1_ref, b1_ref, w2_ref, b2_ref, o_ref):
    half = (CMP_LEN // 2) * HEAD_DIM
    k2 = k2_ref[...]
    a = _dot((k2 + pe_ref[0:1, :]).astype(BF16), w1_ref[0:half, :].astype(BF16))
    b = _dot((k2 + pe_ref[1:2, :]).astype(BF16), w1_ref[half:2 * half, :].astype(BF16))
    nb = k2.shape[0]
    b_up = pltpu.roll(b, nb - 1, axis=0)
    hid = jax.nn.gelu(a + b_up + b1_ref[...])
    o_ref[...] = _dot(hid.astype(BF16), w2_ref[...].astype(BF16)) + b2_ref[...]


def _compress(kv, pe, w1, b1, w2, b2):
    _, G, T, HD = kv.shape
    nb = T // CMP_STRIDE
    row = CMP_STRIDE * HD
    kv2 = kv.reshape(2, G, nb, row)
    pe2 = pe.reshape(2, 2, row)
    return pl.pallas_call(
        _compress_kernel,
        out_shape=jax.ShapeDtypeStruct((2, G, nb, HD), F32),
        grid=(2, G),
        in_specs=[pl.BlockSpec((None, None, nb, row), lambda a, g: (a, g, 0, 0)),
                  pl.BlockSpec((None, 2, row), lambda a, g: (a, 0, 0)),
                  pl.BlockSpec((None, CMP_LEN * HD, HD), lambda a, g: (a, 0, 0)),
                  pl.BlockSpec((None, 1, HD), lambda a, g: (a, 0, 0)),
                  pl.BlockSpec((None, HD, HD), lambda a, g: (a, 0, 0)),
                  pl.BlockSpec((None, 1, HD), lambda a, g: (a, 0, 0))],
        out_specs=pl.BlockSpec((None, None, nb, HD), lambda a, g: (a, g, 0, 0)),
        compiler_params=_cparams(("arbitrary", "arbitrary")),
        name="nsa_compress",
    )(kv2, pe2, w1, b1.reshape(2, 1, HD), w2, b2.reshape(2, 1, HD))


def _bias_from_rel(rel, tbl_ref, head):
    val = jnp.full(rel.shape, tbl_ref[N_BUCKETS - 1, head], F32)
    for b in range(N_BUCKETS - 2, -1, -1):
        val = jnp.where(rel < BUCKET_STARTS[b + 1], tbl_ref[b, head], val)
    return val


def _lane_tile4(x):
    return jnp.concatenate([x, x, x, x], axis=1)


def _col_softmax_stats(s):
    m = jnp.maximum(jnp.max(s, axis=0, keepdims=True), M_FLOOR)
    e = jnp.exp(s - m)
    d = jnp.sum(e, axis=0, keepdims=True)
    return e, d


def _nsa_kernel(tbl_ref, q_ref, gl_ref, kc_ref, vc_ref, ks_ref, vs_ref, kw_ref, vw_ref, o_ref,
                bd_ref, bp_ref, bn_ref, imp_ref, sel_ref, *, ncp, n_tiles):
    g = pl.program_id(0)
    ci = pl.program_id(1)
    Q = Q_BLOCK
    HD = HEAD_DIM
    H4 = NSA_HPG

    row_i = lax.broadcasted_iota(jnp.int32, (Q, Q), 0)
    lane_i = lax.broadcasted_iota(jnp.int32, (Q, Q), 1)

    @pl.when((g == 0) & (ci == 0))
    def _build_bias_tiles():
        for h in range(NSA_HEADS):
            rel_d = lane_i - row_i
            bd_ref[h] = jnp.where(rel_d >= 0, _bias_from_rel(jnp.maximum(rel_d, 0), tbl_ref, h), NEG)
            bp_ref[h] = _bias_from_rel(lane_i - row_i + Q, tbl_ref, h)
            rel_n = lane_i - CMP_STRIDE * row_i + (CMP_STRIDE * (Q - 8) - (CMP_LEN - 1))
            bn_ref[h] = jnp.where(rel_n >= 0, _bias_from_rel(jnp.maximum(rel_n, 0), tbl_ref, h), NEG)

    def head_tiles(ref):
        return jnp.concatenate([ref[g * H4 + h] for h in range(H4)], axis=1)

    c31 = jnp.concatenate([jnp.full((1, Q), tbl_ref[N_BUCKETS - 1, g * H4 + h], F32) for h in range(H4)],
                          axis=1)

    qt = q_ref[...] * (HD ** -0.5)
    qs = jnp.concatenate([qt[:, h * HD:(h + 1) * HD] for h in range(H4)], axis=0).astype(BF16)

    near0 = 8 * ci - (Q - 8)
    kc_far = kc_ref[pl.ds(Q, ncp), :]
    vc_far = vc_ref[pl.ds(Q, ncp), :]
    near_row = pl.multiple_of(8 * ci + 8, 8)
    kc_near = kc_ref[pl.ds(near_row, Q), :]
    vc_near = vc_ref[pl.ds(near_row, Q), :]
    n_far = lax.broadcasted_iota(jnp.int32, (ncp, 1), 0)
    s_far = _dot_nt(kc_far.astype(BF16), qs) + c31
    s_far = jnp.where(n_far < near0, s_far, NEG)
    m_near = lax.broadcasted_iota(jnp.int32, (Q, 1), 0)
    s_near = _dot_nt(kc_near.astype(BF16), qs) + head_tiles(bn_ref)
    s_near = jnp.where(m_near + near0 >= 0, s_near, NEG)
    s_c = jnp.concatenate([s_far, s_near], axis=0)
    e_c, d_c = _col_softmax_stats(s_c)
    p_c = e_c * (1.0 / jnp.where(d_c > 0, d_c, 1.0))
    v_c = jnp.concatenate([vc_far, vc_near], axis=0).astype(BF16)
    o_cmp = _dot_tn(v_c, p_c.astype(BF16))

    psum = p_c[:, 0:Q] + p_c[:, Q:2 * Q] + p_c[:, 2 * Q:3 * Q] + p_c[:, 3 * Q:4 * Q]
    p_hi = psum.astype(BF16)
    p_lo = (psum - p_hi.astype(F32)).astype(BF16)
    jb = lax.broadcasted_iota(jnp.int32, (Q, ncp), 0)
    nb = lax.broadcasted_iota(jnp.int32, (Q, ncp), 1)
    ratio = SLC_LEN // CMP_STRIDE
    span = CMP_LEN // CMP_STRIDE - 1
    cover_far = ((nb >= ratio * jb - span) & (nb <= ratio * jb + ratio - 1)).astype(BF16)
    nn = lane_i + near0
    cover_near = ((nn >= ratio * row_i - span) & (nn <= ratio * row_i + ratio - 1)).astype(BF16)
    cover = jnp.concatenate([cover_far, cover_near], axis=1)
    imp = _dot(cover, p_hi) + _dot(cover, p_lo)
    cur = 2 * ci + (lane_i >= SLC_LEN).astype(jnp.int32)
    imp = jnp.where(row_i == cur, jnp.inf, jnp.where(row_i > cur, -jnp.inf, imp))
    imp_ref[...] = imp

    def rank_body(b, cnt):
        r = imp_ref[pl.ds(b, 1), :]
        ahead = (r > imp) | ((r == imp) & (b < row_i))
        return cnt + ahead.astype(F32)

    cnt = lax.fori_loop(0, 2 * ci + 2, rank_body, jnp.zeros((Q, Q), F32))
    sel_ref[...] = ((cnt < SLC_TOPN) & (row_i <= cur)).astype(F32)

    half_rows = lax.broadcasted_iota(jnp.int32, (Q, Q), 0) < SLC_LEN

    def slc_step(kt0, nt, kt_hi, bias, carry):
        m, l, acc = carry
        r0 = pl.multiple_of((kt0 + 1) * Q, Q)
        k_t = ks_ref[pl.ds(r0, nt * Q), :]
        v_t = vs_ref[pl.ds(r0, nt * Q), :]
        s = _dot_nt(k_t, qs) + bias
        masks = []
        for u in range(nt):
            kt = kt0 + u
            ktc = jnp.clip(kt, 0, n_tiles - 1)
            sel = jnp.where(half_rows, sel_ref[pl.ds(2 * ktc, 1), :], sel_ref[pl.ds(2 * ktc + 1, 1), :])
            masks.append(jnp.where((kt >= 0) & (kt <= kt_hi), sel, 0.0))
        mask = masks[0] if nt == 1 else jnp.concatenate(masks, axis=0)
        s = jnp.where(_lane_tile4(mask) > 0, s, NEG)
        m_new = jnp.maximum(m, jnp.max(s, axis=0, keepdims=True))
        alpha = jnp.exp(m - m_new)
        p = jnp.exp(s - m_new)
        l_new = alpha * l + jnp.sum(p, axis=0, keepdims=True)
        acc_new = alpha * acc + _dot_tn(v_t, p.astype(BF16))
        return m_new, l_new, acc_new

    carry0 = (jnp.full((1, H4 * Q), M_FLOOR, F32), jnp.zeros((1, H4 * Q), F32), jnp.zeros((HD, H4 * Q), F32))
    n_far = jnp.maximum(ci - 1, 0)
    carry = lax.fori_loop(0, (n_far + SLC_GROUP - 1) // SLC_GROUP,
                          lambda it, c: slc_step(it * SLC_GROUP, SLC_GROUP, ci - 2, c31, c), carry0)
    bp_t = head_tiles(bp_ref)
    bd_t = head_tiles(bd_ref)
    _, l_s, acc_s = slc_step(ci - 1, 2, ci, jnp.concatenate([bp_t, bd_t], axis=0), carry)
    o_slc = acc_s * (1.0 / jnp.where(l_s > 0, l_s, 1.0))

    kw = kw_ref[pl.ds(pl.multiple_of(ci * Q, Q), WINDOW + Q), :]
    vw = vw_ref[pl.ds(pl.multiple_of(ci * Q, Q), WINDOW + Q), :]
    s_w = _dot_nt(kw, qs)
    n_wt = WINDOW // Q
    first = jnp.where(_lane_tile4(lane_i < row_i), c31, NEG)
    pieces = [s_w[0:Q] + first]
    for t in range(1, n_wt - 1):
        pieces.append(s_w[t * Q:(t + 1) * Q] + c31)
    pieces.append(s_w[(n_wt - 1) * Q:n_wt * Q] + bp_t)
    pieces.append(s_w[n_wt * Q:(n_wt + 1) * Q] + bd_t)
    s_w = jnp.concatenate(pieces, axis=0)
    x_w = lax.broadcasted_iota(jnp.int32, (WINDOW + Q, 1), 0)
    s_w = jnp.where(x_w + ci * Q >= WINDOW, s_w, NEG)
    e_w, d_w = _col_softmax_stats(s_w)
    p_w = e_w * (1.0 / jnp.where(d_w > 0, d_w, 1.0))
    o_win = _dot_tn(vw, p_w.astype(BF16))

    gt = jnp.transpose(jax.nn.sigmoid(gl_ref[...]))
    outs = []
    for h in range(H4):
        base = (g * H4 + h) * 3
        sl = slice(h * Q, (h + 1) * Q)
        gc = gt_row(gt, base)
        gs = gt_row(gt, base + 1)
        gw = gt_row(gt, base + 2)
        o_h = gc * o_cmp[:, sl] + gs * o_slc[:, sl] + gw * o_win[:, sl]
        outs.append(jnp.transpose(o_h))
    o_ref[...] = jnp.concatenate(outs, axis=1).astype(o_ref.dtype)


def gt_row(gt, idx):
    rows = lax.broadcasted_iota(jnp.int32, gt.shape, 0)
    return jnp.sum(jnp.where(rows == idx, gt, 0.0), axis=0, keepdims=True)


def _nsa_attention(rel_bias, proj, proj_gate, kcmp, vcmp, ks, vs, kw, vw):
    T = proj.shape[0]
    G = NSA_KV_HEADS
    Q = Q_BLOCK
    ncp = T // CMP_STRIDE
    nch = T // Q
    assert nch % SLC_GROUP == 0
    kernel = functools.partial(_nsa_kernel, ncp=ncp, n_tiles=nch)
    full = lambda rows: pl.BlockSpec((None, rows, HEAD_DIM), lambda g, c: (g, 0, 0))
    return pl.pallas_call(
        kernel,
        out_shape=jax.ShapeDtypeStruct((T, NSA_HEADS * HEAD_DIM), BF16),
        grid=(G, nch),
        in_specs=[pl.BlockSpec(memory_space=pltpu.SMEM),
                  pl.BlockSpec((Q, NSA_HPG * HEAD_DIM), lambda g, c: (c, g)),
                  pl.BlockSpec((Q, LANES), lambda g, c: (c, 0)),
                  full(ncp + 2 * Q), full(ncp + 2 * Q),
                  full(T + Q), full(T + Q), full(T + WINDOW), full(T + WINDOW)],
        out_specs=pl.BlockSpec((Q, NSA_HPG * HEAD_DIM), lambda g, c: (c, g)),
        scratch_shapes=[pltpu.VMEM((NSA_HEADS, Q, Q), F32), pltpu.VMEM((NSA_HEADS, Q, Q), F32),
                        pltpu.VMEM((NSA_HEADS, Q, Q), F32), pltpu.VMEM((Q, Q), F32), pltpu.VMEM((Q, Q), F32)],
        compiler_params=_cparams(("arbitrary", "arbitrary")),
        name="nsa_attention",
    )(rel_bias, proj, proj_gate, kcmp, vcmp, ks, vs, kw, vw)


def _softplus(z):
    return jnp.maximum(z, 0.0) + jnp.log1p(jnp.exp(-jnp.abs(z)))


def _lru_kernel(x_ref, y_ref, cw_ref, cb_ref, wa_ref, ba_ref, wi_ref, bi_ref, lam_ref, o_ref,
                xbuf, hc, a_s, b_s, h_s, *, tb):
    i = pl.program_id(0)

    @pl.when(i == 0)
    def _():
        xbuf[0:8, :] = jnp.zeros((8, LRU_WIDTH), F32)
        hc[...] = jnp.zeros((1, LRU_WIDTH), F32)

    xbuf[8:8 + tb, :] = x_ref[...]
    xc = cb_ref[...]
    for w in range(CONV_W):
        xc = xc + cw_ref[w:w + 1, :] * xbuf[8 - (CONV_W - 1) + w:8 - (CONV_W - 1) + w + tb, :]
    xbuf[0:8, :] = xbuf[tb:tb + 8, :]

    xcb = xc.astype(BF16)
    ra, ia = [], []
    for n in range(LRU_BLOCKS):
        xg = xcb[:, n * LRU_BW:(n + 1) * LRU_BW]
        ra.append(_dot(xg, wa_ref[n].astype(BF16)))
        ia.append(_dot(xg, wi_ref[n].astype(BF16)))
    r_gate = jax.nn.sigmoid(jnp.concatenate(ra, axis=1) + ba_ref[...])
    i_gate = jax.nn.sigmoid(jnp.concatenate(ia, axis=1) + bi_ref[...])
    log_a = (-LRU_C * r_gate) * _softplus(-lam_ref[...])
    a = jnp.exp(log_a)
    a_s[...] = a
    b_s[...] = jnp.sqrt(-jnp.tanh(log_a) * (a * a + 1.0)) * (i_gate * xc)

    rows = lax.broadcasted_iota(jnp.int32, (8, LRU_WIDTH), 0)

    def tile(k, h):
        r0 = pl.multiple_of(k * 8, 8)
        A = a_s[pl.ds(r0, 8), :]
        B = b_s[pl.ds(r0, 8), :]
        for sh in (1, 2, 4):
            ok = rows >= sh
            A_p = pltpu.roll(A, sh, axis=0)
            B_p = pltpu.roll(B, sh, axis=0)
            B = jnp.where(ok, A * B_p + B, B)
            A = jnp.where(ok, A * A_p, A)
        H = A * h + B
        h_s[pl.ds(r0, 8), :] = H
        return H[7:8, :]

    hc[...] = lax.fori_loop(0, tb // 8, tile, hc[...])
    o_ref[...] = (jax.nn.gelu(y_ref[...]) * h_s[...]).astype(o_ref.dtype)


def _rglru(proj, y_col, x_col, conv_w, conv_b, wa, ba, wi, bi, lam, tb=256):
    T = proj.shape[0]
    W = LRU_WIDTH
    vec = pl.BlockSpec((1, W), lambda i: (0, 0))
    blk = pl.BlockSpec((LRU_BLOCKS, LRU_BW, LRU_BW), lambda i: (0, 0, 0))
    return pl.pallas_call(
        functools.partial(_lru_kernel, tb=tb),
        out_shape=jax.ShapeDtypeStruct((T, W), BF16),
        grid=(T // tb,),
        in_specs=[pl.BlockSpec((tb, W), lambda i: (i, x_col)),
                  pl.BlockSpec((tb, W), lambda i: (i, y_col)),
                  pl.BlockSpec((CONV_W, W), lambda i: (0, 0)), vec, blk, vec, blk, vec, vec],
        out_specs=pl.BlockSpec((tb, W), lambda i: (i, 0)),
        scratch_shapes=[pltpu.VMEM((tb + 8, W), F32), pltpu.VMEM((1, W), F32),
                        pltpu.VMEM((tb, W), F32), pltpu.VMEM((tb, W), F32), pltpu.VMEM((tb, W), F32)],
        compiler_params=_cparams(("arbitrary",)),
        name="rglru",
    )(proj, proj, conv_w, conv_b.reshape(1, W), wa, ba.reshape(1, W), wi, bi.reshape(1, W), lam.reshape(1, W))


def _split3(x):
    hi = x.astype(BF16)
    r = x - hi.astype(F32)
    mid = r.astype(BF16)
    lo = (r - mid.astype(F32)).astype(BF16)
    return hi, mid, lo


def _hgrn_kernel(q_ref, f_ref, v_ref, g_ref, lbl_ref, ng_ref, o_ref, st_ref, oacc, *, tbh, layer):
    t = pl.program_id(1)
    C = HG_CHUNK
    SB = HG_SUB

    NH = HG_HEADS_PER_STEP
    DK, DV = HG_DK, HG_DV

    @pl.when(t == 0)
    def _():
        st_ref[...] = jnp.zeros((NH, DV, DK), F32)

    lg = lbl_ref[...]
    e = jnp.exp(lg - jnp.max(lg, axis=0, keepdims=True))
    sm = e / jnp.sum(e, axis=0, keepdims=True)
    cum = sm[0:1, :]
    for l in range(1, layer + 1):
        cum = cum + sm[l:l + 1, :]
    lb_all = cum - sm[0:1, :]

    tri = (lax.broadcasted_iota(jnp.int32, (C, C), 0) >= lax.broadcasted_iota(jnp.int32, (C, C), 1)).astype(BF16)
    rows8 = lax.broadcasted_iota(jnp.int32, (8, DK), 0)

    def head_chunk(r0, hh):
        cs = slice(hh * DK, (hh + 1) * DK)
        lb = lb_all[:, cs]
        qz = q_ref[pl.ds(r0, C), cs]
        qq = qz * jax.nn.sigmoid(qz)
        f = lb + (1.0 - lb) * jax.nn.sigmoid(f_ref[pl.ds(r0, C), cs])
        logf = jnp.log(jnp.maximum(f, 1e-30))
        kk = 1.0 - f
        vv = v_ref[pl.ds(r0, C), cs]
        l_hi, l_mid, l_lo = _split3(logf)
        bcum = _dot(tri, l_hi) + _dot(tri, l_mid) + _dot(tri, l_lo)
        st = st_ref[hh]
        o_parts = []
        vb = vv.astype(BF16)
        for I in range(C // SB):
            lo_r = I * SB
            for a in range(SB // 8):
                t0 = lo_r + 8 * a
                q8 = qq[t0:t0 + 8]
                b8 = bcum[t0:t0 + 8]
                o8 = jnp.zeros((8, DV), F32)
                for s in range(lo_r, t0 + 8):
                    d = b8 - bcum[s:s + 1]
                    if s > t0:
                        d = jnp.where(rows8 >= s - t0, d, NEG)
                    w = q8 * (kk[s:s + 1] * jnp.exp(d))
                    o8 = o8 + jnp.sum(w, axis=1, keepdims=True) * vv[s:s + 1]
                o_parts.append(o8)
        for I in range(1, C // SB):
            lo_r = I * SB
            ref = bcum[lo_r - 1:lo_r]
            qe = (qq[lo_r:lo_r + SB] * jnp.exp(bcum[lo_r:lo_r + SB] - ref)).astype(BF16)
            ke = (kk[0:lo_r] * jnp.exp(ref - bcum[0:lo_r])).astype(BF16)
            att = _dot_nt(qe, ke)
            o_off = _dot(att.astype(BF16), vb[0:lo_r])
            for a in range(SB // 8):
                o_parts[lo_r // 8 + a] = o_parts[lo_r // 8 + a] + o_off[8 * a:8 * a + 8]
        o_intra = jnp.concatenate(o_parts, axis=0)
        o_inter = _dot_nt((qq * jnp.exp(bcum)).astype(BF16), st.astype(BF16))
        oacc[pl.ds(r0, C), cs] = o_inter + o_intra
        b_last = bcum[C - 1:C]
        kd = (kk * jnp.exp(b_last - bcum)).astype(BF16)
        st_ref[hh] = st * jnp.exp(b_last) + _dot_tn(vb, kd)

    def chunk(c, carry):
        r0 = pl.multiple_of(c * C, C)
        for hh in range(NH):
            head_chunk(r0, hh)
        return carry

    lax.fori_loop(0, tbh // C, chunk, 0)
    gz = g_ref[...]
    gate = (gz * jax.nn.sigmoid(gz)) * ng_ref[...]
    for hh in range(NH):
        cs = slice(hh * DV, (hh + 1) * DV)
        o = oacc[:, cs]
        o = o * lax.rsqrt(jnp.mean(o * o, axis=-1, keepdims=True) + EPS)
        o_ref[:, cs] = (o * gate[:, cs]).astype(o_ref.dtype)


def _hgrn2(proj, lb_logits, norm_g, layer, tbh=512):
    T = proj.shape[0]
    NH = HG_HEADS_PER_STEP
    HB = HG_HEADS // NH
    wk = NH * HG_DK
    col = lambda off: pl.BlockSpec((tbh, wk), lambda h, t: (t, off + h))
    return pl.pallas_call(
        functools.partial(_hgrn_kernel, tbh=tbh, layer=layer),
        out_shape=jax.ShapeDtypeStruct((T, HG_VW), BF16),
        grid=(HB, T // tbh),
        in_specs=[col(0), col(HB), col(2 * HB), col(3 * HB),
                  pl.BlockSpec((DEPTH, wk), lambda h, t: (0, h)),
                  pl.BlockSpec((1, wk), lambda h, t: (0, h))],
        out_specs=pl.BlockSpec((tbh, wk), lambda h, t: (t, h)),
        scratch_shapes=[pltpu.VMEM((NH, HG_DV, HG_DK), F32), pltpu.VMEM((tbh, wk), F32)],
        compiler_params=_cparams(("arbitrary", "arbitrary")),
        name="hgrn2",
    )(proj, proj, proj, proj, lb_logits, norm_g.reshape(1, HG_VW))


def _router_kernel(x_ref, g_ref, sc_ref, sh_ref, w_ref, b_ref, h_ref, eid_ref, gate_ref):
    h = _norm_mod(x_ref[...], g_ref[...], sc_ref[...], sh_ref[...])
    h_ref[...] = h
    h1, h2, h3 = _split3(h)
    w1, w2, w3 = _split3(w_ref[...])
    logits = (_dot(h1, w1) + (_dot(h1, w2) + _dot(h2, w1))
              + (_dot(h2, w2) + _dot(h1, w3) + _dot(h3, w1))) + b_ref[...]
    lane = lax.broadcasted_iota(jnp.int32, logits.shape, 1)
    is_g = lane < N_GROUPS
    glog = jnp.where(is_g, logits, -jnp.inf)
    gmax = jnp.max(glog, axis=-1, keepdims=True)
    grp = jnp.min(jnp.where(glog == gmax, lane, LANES), axis=-1, keepdims=True)
    gsum = jnp.sum(jnp.where(is_g, jnp.exp(glog - gmax), 0.0), axis=-1, keepdims=True)
    p_grp = 1.0 / gsum
    lo = N_GROUPS + EXP_PER_GROUP * grp
    el = jnp.where((lane >= lo) & (lane < lo + EXP_PER_GROUP), logits, -jnp.inf)
    v1 = jnp.max(el, axis=-1, keepdims=True)
    i1 = jnp.min(jnp.where(el == v1, lane, LANES), axis=-1, keepdims=True)
    el2 = jnp.where(lane == i1, -jnp.inf, el)
    v2 = jnp.max(el2, axis=-1, keepdims=True)
    i2 = jnp.min(jnp.where(el2 == v2, lane, LANES), axis=-1, keepdims=True)
    e2 = jnp.exp(v2 - v1)
    den = 1.0 + e2
    g1 = p_grp * (1.0 / den)
    g2 = p_grp * (e2 / den)
    eid_ref[...] = jnp.where(lane == 0, i1 - N_GROUPS, jnp.where(lane == 1, i2 - N_GROUPS, 0))
    gate_ref[...] = jnp.where(lane == 0, g1, jnp.where(lane == 1, g2, 0.0))


def _router(x, g, sc, sh, w_r, b_r, tm=512):
    T, D = x.shape
    vec = pl.BlockSpec((1, D), lambda i: (0, 0))
    return pl.pallas_call(
        _router_kernel,
        out_shape=[jax.ShapeDtypeStruct((T, D), F32), jax.ShapeDtypeStruct((T, LANES), jnp.int32),
                   jax.ShapeDtypeStruct((T, LANES), F32)],
        grid=(T // tm,),
        in_specs=[pl.BlockSpec((tm, D), lambda i: (i, 0)), vec, vec, vec,
                  pl.BlockSpec((D, LANES), lambda i: (0, 0)), pl.BlockSpec((1, LANES), lambda i: (0, 0))],
        out_specs=[pl.BlockSpec((tm, D), lambda i: (i, 0)), pl.BlockSpec((tm, LANES), lambda i: (i, 0)),
                   pl.BlockSpec((tm, LANES), lambda i: (i, 0))],
        compiler_params=_cparams(("arbitrary",)),
        name="moe_router",
    )(x, g, sc, sh, w_r, b_r)


def _expert_kernel(be_ref, nx_ref, ws_ref, nu_ref, tok_ref, h_ref, wg_hbm, wu_hbm, wd_hbm, y_ref,
                   wgf, wuf, wdf, wgb, wub, wdb, xbuf, wsem, xsem, *, layer):
    i = pl.program_id(0)
    n_used = nu_ref[0]
    R = MOE_BLOCK

    def w_copies(e, slot):
        return (pltpu.make_async_copy(wg_hbm.at[layer, e], wgf.at[slot], wsem.at[slot, 0]),
                pltpu.make_async_copy(wu_hbm.at[layer, e], wuf.at[slot], wsem.at[slot, 1]),
                pltpu.make_async_copy(wd_hbm.at[layer, e], wdf.at[slot], wsem.at[slot, 2]))

    def x_start(blk, slot):
        def body(r, c):
            tok = tok_ref[blk * R + r]
            pltpu.make_async_copy(h_ref.at[pl.ds(tok, 1)], xbuf.at[slot, pl.ds(r, 1)], xsem.at[slot]).start()
            return c
        lax.fori_loop(0, R, body, 0, unroll=8)

    def x_wait(slot):
        pltpu.make_async_copy(h_ref.at[pl.ds(0, R)], xbuf.at[slot], xsem.at[slot]).wait()

    @pl.when(i == 0)
    def _():
        for c in w_copies(be_ref[0], 0):
            c.start()
        x_start(0, 0)

    @pl.when(i < n_used)
    def _():
        slot = i % 2

        @pl.when(i + 1 < n_used)
        def _():
            x_start(i + 1, 1 - slot)

        @pl.when((i == 0) | (be_ref[i] != be_ref[jnp.maximum(i - 1, 0)]))
        def _():
            ws = ws_ref[i]
            for c in w_copies(be_ref[i], ws):
                c.wait()

            @pl.when(nx_ref[i] >= 0)
            def _():
                for c in w_copies(nx_ref[i], 1 - ws):
                    c.start()

            wgb[...] = wgf[ws].astype(BF16)
            wub[...] = wuf[ws].astype(BF16)
            wdb[...] = wdf[ws].astype(BF16)

        x_wait(slot)
        x = xbuf[slot].astype(BF16)
        hg = _dot(x, wgb[...])
        hu = _dot(x, wub[...])
        hid = (hg * jax.nn.sigmoid(hg)) * hu
        y_ref[...] = _dot(hid.astype(BF16), wdb[...])

    @pl.when(i >= n_used)
    def _():
        y_ref[...] = jnp.zeros(y_ref.shape, y_ref.dtype)


def _experts(h, tok, blk_e, nxt_e, w_slot, n_used, w_gate, w_up, w_down, layer):
    N, D = h.shape
    n_pad = tok.shape[0]
    n_blk = n_pad // MOE_BLOCK
    any_spec = pl.BlockSpec(memory_space=pl.ANY)
    return pl.pallas_call(
        functools.partial(_expert_kernel, layer=layer),
        out_shape=jax.ShapeDtypeStruct((n_pad, D), F32),
        grid_spec=pltpu.PrefetchScalarGridSpec(
            num_scalar_prefetch=5,
            grid=(n_blk,),
            in_specs=[any_spec, any_spec, any_spec, any_spec],
            out_specs=pl.BlockSpec((MOE_BLOCK, D), lambda i, *_: (i, 0)),
            scratch_shapes=[pltpu.VMEM((2, D, D_EXPERT), F32), pltpu.VMEM((2, D, D_EXPERT), F32),
                            pltpu.VMEM((2, D_EXPERT, D), F32),
                            pltpu.VMEM((D, D_EXPERT), BF16), pltpu.VMEM((D, D_EXPERT), BF16),
                            pltpu.VMEM((D_EXPERT, D), BF16),
                            pltpu.VMEM((2, MOE_BLOCK, D), F32),
                            pltpu.SemaphoreType.DMA((2, 3)), pltpu.SemaphoreType.DMA((2,))]),
        compiler_params=_cparams(("arbitrary",)),
        name="moe_experts",
    )(blk_e, nxt_e, w_slot, n_used, tok, h, w_gate, w_up, w_down)


def _combine_kernel(slot_ref, y_ref, x_ref, w_ref, g_ref, o_ref, buf, sem, *, rows):
    i = pl.program_id(0)

    def copy(r, k):
        return pltpu.make_async_copy(y_ref.at[pl.ds(slot_ref[(i * rows + r) * 2 + k], 1)],
                                     buf.at[k, pl.ds(r, 1)], sem)

    def start(r, c):
        copy(r, 0).start()
        copy(r, 1).start()
        return c

    lax.fori_loop(0, rows, start, 0, unroll=8)
    for k in range(TOPK_IN_GROUP):
        pltpu.make_async_copy(y_ref.at[pl.ds(0, rows)], buf.at[k], sem).wait()
    w = w_ref[...]
    moe = buf[0] * w[:, 0:1] + buf[1] * w[:, 1:2]
    o_ref[...] = x_ref[...] + g_ref[...] * moe


def _combine(yb, slots, x, gate_w, g2, rows=256):
    T, D = x.shape
    return pl.pallas_call(
        functools.partial(_combine_kernel, rows=rows),
        out_shape=jax.ShapeDtypeStruct((T, D), F32),
        grid_spec=pltpu.PrefetchScalarGridSpec(
            num_scalar_prefetch=1,
            grid=(T // rows,),
            in_specs=[pl.BlockSpec(memory_space=pl.ANY),
                      pl.BlockSpec((rows, D), lambda i, s: (i, 0)),
                      pl.BlockSpec((rows, LANES), lambda i, s: (i, 0)),
                      pl.BlockSpec((1, D), lambda i, s: (0, 0))],
            out_specs=pl.BlockSpec((rows, D), lambda i, s: (i, 0)),
            scratch_shapes=[pltpu.VMEM((2, rows, D), F32), pltpu.SemaphoreType.DMA(())]),
        compiler_params=_cparams(("arbitrary",)),
        name="moe_combine",
    )(slots.reshape(-1), yb, x, gate_w, g2)


def _hier_moe_residual(x, g, sc, sh, g2, w_grp, b_grp, w_exp, b_exp, w_gate, w_up, w_down, layer):
    N, D = x.shape
    n_route = N_GROUPS + N_EXPERTS
    w_r = jnp.zeros((D, LANES), F32).at[:, :N_GROUPS].set(w_grp).at[:, N_GROUPS:n_route].set(w_exp)
    b_r = jnp.zeros((1, LANES), F32).at[0, :N_GROUPS].set(b_grp).at[0, N_GROUPS:n_route].set(b_exp)
    h, eid_l, gate_l = _router(x, g, sc, sh, w_r, b_r)
    eid = eid_l[:, :TOPK_IN_GROUP]

    A = N * TOPK_IN_GROUP
    e_flat = eid.reshape(-1)
    tok = jnp.repeat(jnp.arange(N, dtype=jnp.int32), TOPK_IN_GROUP)
    order = jnp.argsort(e_flat)
    e_s, tok_s = e_flat[order], tok[order]
    counts = jnp.zeros((N_EXPERTS,), jnp.int32).at[e_flat].add(1)
    start = jnp.cumsum(counts) - counts
    padded = (counts + MOE_BLOCK - 1) // MOE_BLOCK * MOE_BLOCK
    pad_end = jnp.cumsum(padded)
    pad_start = pad_end - padded
    dest = pad_start[e_s] + jnp.arange(A, dtype=jnp.int32) - start[e_s]
    n_blk = -(-(A + N_EXPERTS * MOE_BLOCK) // MOE_BLOCK)
    n_pad = n_blk * MOE_BLOCK
    buf_tok = jnp.zeros((n_pad,), jnp.int32).at[dest].set(tok_s)
    blk_idx = jnp.arange(n_blk, dtype=jnp.int32)
    n_used = pad_end[-1] // MOE_BLOCK
    blk_e = jnp.minimum(jnp.searchsorted(pad_end, blk_idx * MOE_BLOCK, side='right'), N_EXPERTS - 1).astype(jnp.int32)
    blk_e = jnp.where(blk_idx < n_used, blk_e, blk_e[n_used - 1])
    change = jnp.concatenate([jnp.ones((1,), bool), blk_e[1:] != blk_e[:-1]])
    w_slot = ((jnp.cumsum(change.astype(jnp.int32)) - 1) % 2).astype(jnp.int32)
    nxt_idx = lax.cummin(jnp.where(change, blk_idx, n_blk), axis=0, reverse=True)
    nxt_idx = jnp.concatenate([nxt_idx[1:], jnp.full((1,), n_blk, jnp.int32)])
    nxt_e = jnp.where(nxt_idx < n_blk, blk_e[jnp.minimum(nxt_idx, n_blk - 1)], -1).astype(jnp.int32)
    slots = jnp.zeros((A,), jnp.int32).at[order].set(dest).reshape(N, TOPK_IN_GROUP)

    yb = _experts(h, buf_tok, blk_e, nxt_e, w_slot, n_used.reshape(1).astype(jnp.int32), w_gate, w_up, w_down, layer)
    return _combine(yb, slots, x, gate_l, g2)


def _final_norm_kernel(x_ref, g_ref, o_ref):
    x = x_ref[...]
    o_ref[...] = (x * lax.rsqrt(jnp.mean(x * x, axis=-1, keepdims=True) + EPS)) * g_ref[...]


def _final_norm(x, g, tm=512):
    T, D = x.shape
    return pl.pallas_call(
        _final_norm_kernel,
        out_shape=jax.ShapeDtypeStruct((T, D), F32),
        grid=(T // tm,),
        in_specs=[pl.BlockSpec((tm, D), lambda i: (i, 0)), pl.BlockSpec((1, D), lambda i: (0, 0))],
        out_specs=pl.BlockSpec((tm, D), lambda i: (i, 0)),
        compiler_params=_cparams(("arbitrary",)),
        name="final_norm",
    )(x, g)


def _even_mixer_residual(x, g, sc, sh, g1, rel_bias, w_in, w_out, cmp_pe, cmp_w1, cmp_b1, cmp_w2, cmp_b2,
                         conv_w, conv_b, lru_wa, lru_ba, lru_wi, lru_bi, lru_lambda):
    T, D = x.shape
    G, HD = NSA_KV_HEADS, HEAD_DIM
    nq = NSA_HEADS * HD
    n_kv = 6 * NSA_KV_W
    n_gate = 3 * NSA_HEADS
    w_main = jnp.concatenate([w_in[:, :nq], w_in[:, nq + n_kv + n_gate:], w_in[:, nq:nq + n_kv]],
                             axis=1).astype(BF16)
    w_gl = jnp.zeros((D, LANES), F32).at[:, :n_gate].set(w_in[:, nq + n_kv:nq + n_kv + n_gate]).astype(BF16)
    proj, proj_gate = _norm_proj(x, g, sc, sh, w_main, w_gl)

    def kv_heads(j):
        c0 = nq + 2 * LRU_WIDTH + j * NSA_KV_W
        return proj[:, c0:c0 + NSA_KV_W].reshape(T, G, HD).transpose(1, 0, 2)

    cmp_in = jnp.stack([kv_heads(0), kv_heads(1)], axis=0)
    cmp_out = _compress(cmp_in, cmp_pe, cmp_w1, cmp_b1, cmp_w2, cmp_b2)
    ncp = T // CMP_STRIDE
    valid = (jnp.arange(ncp) < ncp - 1)[None, None, :, None]
    cmp_pad = jnp.pad(jnp.where(valid, cmp_out, 0.0), ((0, 0), (0, 0), (Q_BLOCK, Q_BLOCK), (0, 0)))
    ks = jnp.pad(kv_heads(2).astype(BF16), ((0, 0), (Q_BLOCK, 0), (0, 0)))
    vs = jnp.pad(kv_heads(3).astype(BF16), ((0, 0), (Q_BLOCK, 0), (0, 0)))
    kw = jnp.pad(kv_heads(4).astype(BF16), ((0, 0), (WINDOW, 0), (0, 0)))
    vw = jnp.pad(kv_heads(5).astype(BF16), ((0, 0), (WINDOW, 0), (0, 0)))
    nsa_out = _nsa_attention(rel_bias, proj, proj_gate, cmp_pad[0], cmp_pad[1], ks, vs, kw, vw)

    y_col = nq // LRU_WIDTH
    lru_out = _rglru(proj, y_col, y_col + 1, conv_w, conv_b, lru_wa, lru_ba, lru_wi, lru_bi, lru_lambda)
    w_o = w_out.astype(BF16)
    return _out_proj(nsa_out, lru_out, w_o[:nq], w_o[nq:], x, g1)


def _odd_mixer_residual(x, g, sc, sh, g1, lb_logits, w_in, w_out, norm_g, layer):
    proj = _norm_proj(x, g, sc, sh, w_in.astype(BF16))[0]
    o = _hgrn2(proj, lb_logits, norm_g, layer)
    w_o = w_out.astype(BF16)
    half = HG_VW // 2
    return _out_proj(o[:, :half], o[:, half:], w_o[:half], w_o[half:], x, g1)


def kernel(x, c, rel_bias, ada_w, ada_b, norm_mix_g, norm_ffn_g, ev_w_in, ev_w_out, cmp_pe, cmp_w1, cmp_b1, cmp_w2, cmp_b2, lru_conv_w, lru_conv_b, lru_wa, lru_ba, lru_wi, lru_bi, lru_lambda, od_w_in, od_w_out, hg_lb_logits, hg_norm_g, moe_w_grp, moe_b_grp, moe_w_exp, moe_b_exp, moe_w_gate, moe_w_up, moe_w_down, final_g):
    B, T, D = x.shape
    assert B == 1 and D == D_MODEL
    xt = x.reshape(T, D)
    mod = _modulation(c, ada_w, ada_b)
    for l in range(DEPTH):
        sh1, sc1, g1, sh2, sc2, g2 = [mod[l, :, k * D:(k + 1) * D] for k in range(6)]
        gm = norm_mix_g[l].reshape(1, D)
        gf = norm_ffn_g[l].reshape(1, D)
        j = l // 2
        if l % 2 == 0:
            xt = _even_mixer_residual(xt, gm, sc1, sh1, g1, rel_bias, ev_w_in[j], ev_w_out[j], cmp_pe[j],
                                      cmp_w1[j], cmp_b1[j], cmp_w2[j], cmp_b2[j], lru_conv_w[j],
                                      lru_conv_b[j], lru_wa[j], lru_ba[j], lru_wi[j], lru_bi[j], lru_lambda[j])
        else:
            xt = _odd_mixer_residual(xt, gm, sc1, sh1, g1, hg_lb_logits, od_w_in[j], od_w_out[j],
                                     hg_norm_g[j], l)
        xt = _hier_moe_residual(xt, gf, sc2, sh2, g2, moe_w_grp[l], moe_b_grp[l], moe_w_exp[l], moe_b_exp[l],
                                moe_w_gate, moe_w_up, moe_w_down, l)
    return _final_norm(xt, final_g.reshape(1, D)).reshape(B, T, D)
```

```python
import functools
import math

import numpy as np
import jax
import jax.numpy as jnp
from jax import lax
from jax.experimental import pallas as pl
from jax.experimental.pallas import tpu as pltpu

F32 = jnp.float32
BF16 = jnp.bfloat16

D_MODEL = 2048
DEPTH = 2
NSA_HEADS = 8
NSA_KV_HEADS = 2
NSA_HPG = NSA_HEADS // NSA_KV_HEADS
HEAD_DIM = 128
NSA_KV_W = NSA_KV_HEADS * HEAD_DIM
CMP_LEN = 32
CMP_STRIDE = 16
SLC_LEN = 64
SLC_TOPN = 16
WINDOW = 512
Q_BLOCK = 128
LRU_WIDTH = 1024
LRU_BLOCKS = 8
LRU_BW = LRU_WIDTH // LRU_BLOCKS
CONV_W = 4
LRU_C = 8.0
HG_HEADS = 16
HG_DK = 128
HG_DV = 128
HG_CHUNK = 64
HG_SUB = 16
HG_HEADS_PER_STEP = 4
SLC_GROUP = 4
RANK_UNROLL = 4
SLC_TAIL = 2 * SLC_GROUP * Q_BLOCK
HG_KW = HG_HEADS * HG_DK
HG_VW = HG_HEADS * HG_DV
N_BUCKETS = 32
MAX_DIST = 128
N_GROUPS = 8
EXP_PER_GROUP = 8
N_EXPERTS = N_GROUPS * EXP_PER_GROUP
TOPK_IN_GROUP = 2
D_EXPERT = 512
MOE_BLOCK = 128
EPS = 1e-6

LANES = 128
NEG = -1e30
M_FLOOR = -1e20
VMEM_LIMIT = 56 * 1024 * 1024


def _cparams(sem):
    return pltpu.CompilerParams(dimension_semantics=sem, vmem_limit_bytes=VMEM_LIMIT)


def _dot(a, b):
    return jnp.dot(a, b, preferred_element_type=F32)


def _dot_nt(a, b):
    return lax.dot_general(a, b, (((1,), (1,)), ((), ())), preferred_element_type=F32)


def _dot_tn(a, b):
    return lax.dot_general(a, b, (((0,), (0,)), ((), ())), preferred_element_type=F32)


def _bucket_starts():
    n = np.arange(0, MAX_DIST + 1, dtype=np.int32)
    max_exact = N_BUCKETS // 2
    nf = np.maximum(n, 1).astype(np.float32)
    large = max_exact + (np.log(nf / np.float32(max_exact)) / np.float32(math.log(MAX_DIST / max_exact))
                         * np.float32(N_BUCKETS - max_exact)).astype(np.int32)
    large = np.minimum(large, N_BUCKETS - 1)
    b = np.where(n < max_exact, n, large)
    starts = [int(np.argmax(b >= k)) for k in range(N_BUCKETS)]
    assert all(b[s] == k for k, s in enumerate(starts)) and b[-1] == N_BUCKETS - 1
    return starts


BUCKET_STARTS = _bucket_starts()


def _mod_kernel(c_ref, w_ref, b_ref, o_ref):
    c = c_ref[...]
    cact = c * jax.nn.sigmoid(c)
    o_ref[...] = jnp.sum(cact * w_ref[...], axis=0, keepdims=True) + b_ref[...]


def _modulation(c, ada_w, ada_b):
    D = D_MODEL
    tn = 1024
    n_out = 6 * D
    c_col = c.reshape(D, 1)
    return pl.pallas_call(
        _mod_kernel,
        out_shape=jax.ShapeDtypeStruct((DEPTH, 1, n_out), F32),
        grid=(DEPTH, n_out // tn),
        in_specs=[pl.BlockSpec((D, 1), lambda l, j: (0, 0)),
                  pl.BlockSpec((None, D, tn), lambda l, j: (l, 0, j)),
                  pl.BlockSpec((None, 1, tn), lambda l, j: (l, 0, j))],
        out_specs=pl.BlockSpec((None, 1, tn), lambda l, j: (l, 0, j)),
        compiler_params=_cparams(("arbitrary", "arbitrary")),
        name="adaln_mod",
    )(c_col, ada_w, ada_b.reshape(DEPTH, 1, n_out))


def _norm_mod(x, g, sc, sh):
    y = x * lax.rsqrt(jnp.mean(x * x, axis=-1, keepdims=True) + EPS)
    return (y * g) * (1.0 + sc) + sh


def _norm_proj_kernel(x_ref, g_ref, sc_ref, sh_ref, w_ref, *rest, has_extra, bf16_from):
    if has_extra:
        wx_ref, o_ref, ox_ref, ob_ref, h_ref = rest
    else:
        o_ref, h_ref = rest
    j = pl.program_id(1)

    @pl.when(j == 0)
    def _():
        h = _norm_mod(x_ref[...], g_ref[...], sc_ref[...], sh_ref[...]).astype(BF16)
        h_ref[...] = h
        if has_extra:
            ox_ref[...] = _dot(h, wx_ref[...])

    y = _dot(h_ref[...], w_ref[...])
    o_ref[...] = y
    if has_extra:
        @pl.when(j >= bf16_from)
        def _():
            ob_ref[...] = y.astype(BF16)


def _norm_proj(x, g, sc, sh, w, w_extra=None, bf16_from=None, tm=1024, tn=512):
    T, D = x.shape
    N = w.shape[1]
    has_extra = w_extra is not None
    vec = pl.BlockSpec((1, D), lambda i, j: (0, 0))
    in_specs = [pl.BlockSpec((tm, D), lambda i, j: (i, 0)), vec, vec, vec,
                pl.BlockSpec((D, tn), lambda i, j: (0, j))]
    out_shape = [jax.ShapeDtypeStruct((T, N), F32)]
    out_specs = [pl.BlockSpec((tm, tn), lambda i, j: (i, j))]
    args = [x, g, sc, sh, w]
    if has_extra:
        nx = w_extra.shape[1]
        in_specs.append(pl.BlockSpec((D, nx), lambda i, j: (0, 0)))
        out_shape.append(jax.ShapeDtypeStruct((T, nx), F32))
        out_specs.append(pl.BlockSpec((tm, nx), lambda i, j: (i, 0)))
        args.append(w_extra)
        out_shape.append(jax.ShapeDtypeStruct((T, N - bf16_from * tn), BF16))
        out_specs.append(pl.BlockSpec((tm, tn), lambda i, j: (i, jnp.maximum(j - bf16_from, 0))))
    return pl.pallas_call(
        functools.partial(_norm_proj_kernel, has_extra=has_extra, bf16_from=bf16_from),
        out_shape=out_shape,
        grid=(T // tm, N // tn),
        in_specs=in_specs,
        out_specs=out_specs,
        scratch_shapes=[pltpu.VMEM((tm, D), BF16)],
        compiler_params=_cparams(("arbitrary", "arbitrary")),
        name="norm_proj",
    )(*args)


def _out_proj_kernel(a1_ref, a2_ref, w1_ref, w2_ref, x_ref, g_ref, o_ref):
    y = _dot(a1_ref[...], w1_ref[...]) + _dot(a2_ref[...], w2_ref[...])
    o_ref[...] = x_ref[...] + g_ref[...] * y


def _out_proj(a1, a2, w1, w2, x, gate, tm=1024, tn=512):
    T, D = x.shape
    K1, K2 = a1.shape[1], a2.shape[1]
    return pl.pallas_call(
        _out_proj_kernel,
        out_shape=jax.ShapeDtypeStruct((T, D), F32),
        grid=(T // tm, D // tn),
        in_specs=[pl.BlockSpec((tm, K1), lambda i, j: (i, 0)),
                  pl.BlockSpec((tm, K2), lambda i, j: (i, 0)),
                  pl.BlockSpec((K1, tn), lambda i, j: (0, j)),
                  pl.BlockSpec((K2, tn), lambda i, j: (0, j)),
                  pl.BlockSpec((tm, tn), lambda i, j: (i, j)),
                  pl.BlockSpec((1, tn), lambda i, j: (0, j))],
        out_specs=pl.BlockSpec((tm, tn), lambda i, j: (i, j)),
        compiler_params=_cparams(("arbitrary", "arbitrary")),
        name="out_proj",
    )(a1, a2, w1, w2, x, gate)


def _compress_kernel(k2_ref, pe_ref, w1_ref, b1_ref, w2_ref, b2_ref, o_ref):
    half = (CMP_LEN // 2) * HEAD_DIM
    k2 = k2_ref[...]
    a = _dot((k2 + pe_ref[0:1, :]).astype(BF16), w1_ref[0:half, :].astype(BF16))
    b = _dot((k2 + pe_ref[1:2, :]).astype(BF16), w1_ref[half:2 * half, :].astype(BF16))
    nb = k2.shape[0]
    b_up = pltpu.roll(b, nb - 1, axis=0)
    hid = jax.nn.gelu(a + b_up + b1_ref[...])
    o_ref[...] = _dot(hid.astype(BF16), w2_ref[...].astype(BF16)) + b2_ref[...]


def _compress(kv, pe, w1, b1, w2, b2):
    _, G, T, HD = kv.shape
    nb = T // CMP_STRIDE
    row = CMP_STRIDE * HD
    kv2 = kv.reshape(2, G, nb, row)
    pe2 = pe.reshape(2, 2, row)
    return pl.pallas_call(
        _compress_kernel,
        out_shape=jax.ShapeDtypeStruct((2, G, nb, HD), F32),
        grid=(2, G),
        in_specs=[pl.BlockSpec((None, None, nb, row), lambda a, g: (a, g, 0, 0)),
                  pl.BlockSpec((None, 2, row), lambda a, g: (a, 0, 0)),
                  pl.BlockSpec((None, CMP_LEN * HD, HD), lambda a, g: (a, 0, 0)),
                  pl.BlockSpec((None, 1, HD), lambda a, g: (a, 0, 0)),
                  pl.BlockSpec((None, HD, HD), lambda a, g: (a, 0, 0)),
                  pl.BlockSpec((None, 1, HD), lambda a, g: (a, 0, 0))],
        out_specs=pl.BlockSpec((None, None, nb, HD), lambda a, g: (a, g, 0, 0)),
        compiler_params=_cparams(("arbitrary", "arbitrary")),
        name="nsa_compress",
    )(kv2, pe2, w1, b1.reshape(2, 1, HD), w2, b2.reshape(2, 1, HD))


def _bias_from_rel(rel, tbl_ref, head):
    far = tbl_ref[N_BUCKETS - 1, head]
    val = jnp.zeros(rel.shape, F32)
    for b in range(N_BUCKETS - 2, -1, -1):
        val = jnp.where(rel < BUCKET_STARTS[b + 1], tbl_ref[b, head] - far, val)
    return val


def _lane_tile4(x):
    return jnp.concatenate([x, x, x, x], axis=1)


def _col_softmax_stats(s):
    m = jnp.maximum(jnp.max(s, axis=0, keepdims=True), M_FLOOR)
    e = jnp.exp(s - m)
    d = jnp.sum(e, axis=0, keepdims=True)
    return e, d


def _nsa_kernel(tbl_ref, q_ref, gl_ref, kc_ref, vc_ref, ks_ref, vs_ref, kw_ref, vw_ref, o_ref,
                bd_ref, bp_ref, bn_ref, imp_ref, sel_ref, *, ncp, n_tiles):
    g = pl.program_id(0)
    ci = pl.program_id(1)
    Q = Q_BLOCK
    HD = HEAD_DIM
    H4 = NSA_HPG

    row_i = lax.broadcasted_iota(jnp.int32, (Q, Q), 0)
    lane_i = lax.broadcasted_iota(jnp.int32, (Q, Q), 1)

    @pl.when((g == 0) & (ci == 0))
    def _build_bias_tiles():
        for h in range(NSA_HEADS):
            rel_d = lane_i - row_i
            bd_ref[h] = jnp.where(rel_d >= 0, _bias_from_rel(jnp.maximum(rel_d, 0), tbl_ref, h), NEG)
            bp_ref[h] = _bias_from_rel(lane_i - row_i + Q, tbl_ref, h)
            rel_n = lane_i - CMP_STRIDE * row_i + (CMP_STRIDE * (Q - 8) - (CMP_LEN - 1))
            bn_ref[h] = jnp.where(rel_n >= 0, _bias_from_rel(jnp.maximum(rel_n, 0), tbl_ref, h), NEG)

    def head_tiles(ref):
        return jnp.concatenate([ref[g * H4 + h] for h in range(H4)], axis=1)

    qt =q_ref[...] * (HD ** -0.5)
    qs = jnp.concatenate([qt[:, h * HD:(h + 1) * HD] for h in range(H4)], axis=0).astype(BF16)

    near0 = 8 * ci - (Q - 8)
    kc_far = kc_ref[pl.ds(Q, ncp), :]
    vc_far = vc_ref[pl.ds(Q, ncp), :]
    near_row = pl.multiple_of(8 * ci + 8, 8)
    kc_near = kc_ref[pl.ds(near_row, Q), :]
    vc_near = vc_ref[pl.ds(near_row, Q), :]
    n_far = lax.broadcasted_iota(jnp.int32, (ncp, 1), 0)
    s_far = jnp.where(n_far < near0, _dot_nt(kc_far.astype(BF16), qs), NEG)
    m_near = lax.broadcasted_iota(jnp.int32, (Q, 1), 0)
    s_near = _dot_nt(kc_near.astype(BF16), qs) + head_tiles(bn_ref)
    s_near = jnp.where(m_near + near0 >= 0, s_near, NEG)
    s_c = jnp.concatenate([s_far, s_near], axis=0)
    e_c, d_c = _col_softmax_stats(s_c)
    p_c = e_c * (1.0 / jnp.where(d_c > 0, d_c, 1.0))
    v_c = jnp.concatenate([vc_far, vc_near], axis=0).astype(BF16)
    o_cmp = _dot_tn(v_c, p_c.astype(BF16))

    psum = p_c[:, 0:Q] + p_c[:, Q:2 * Q] + p_c[:, 2 * Q:3 * Q] + p_c[:, 3 * Q:4 * Q]
    p_hi = psum.astype(BF16)
    p_lo = (psum - p_hi.astype(F32)).astype(BF16)
    jb = lax.broadcasted_iota(jnp.int32, (Q, ncp), 0)
    nb = lax.broadcasted_iota(jnp.int32, (Q, ncp), 1)
    ratio = SLC_LEN // CMP_STRIDE
    span = CMP_LEN // CMP_STRIDE - 1
    cover_far = ((nb >= ratio * jb - span) & (nb <= ratio * jb + ratio - 1)).astype(BF16)
    nn = lane_i + near0
    cover_near = ((nn >= ratio * row_i - span) & (nn <= ratio * row_i + ratio - 1)).astype(BF16)
    cover = jnp.concatenate([cover_far, cover_near], axis=1)
    imp = _dot(cover, p_hi) + _dot(cover, p_lo)
    cur = 2 * ci + (lane_i >= SLC_LEN).astype(jnp.int32)
    imp = jnp.where(row_i == cur, jnp.inf, jnp.where(row_i > cur, -jnp.inf, imp))
    imp_ref[...] = imp

    def rank_body(it, cnt):
        for u in range(RANK_UNROLL):
            b = it * RANK_UNROLL + u
            r = imp_ref[pl.ds(b, 1), :]
            ahead = (r > imp) | ((r == imp) & (b < row_i))
            cnt = cnt + ahead.astype(F32)
        return cnt

    cnt = lax.fori_loop(0, (2 * ci + 2 + RANK_UNROLL - 1) // RANK_UNROLL, rank_body, jnp.zeros((Q, Q), F32))
    sel_ref[...] = ((cnt < SLC_TOPN) & (row_i <= cur)).astype(F32)

    half_rows = lax.broadcasted_iota(jnp.int32, (Q, Q), 0) < SLC_LEN

    def slc_step(kt0, nt, kt_hi, bias, carry):
        m, l, acc = carry
        r0 = pl.multiple_of((kt0 + 1) * Q, Q)
        k_t = ks_ref[pl.ds(r0, nt * Q), :]
        v_t = vs_ref[pl.ds(r0, nt * Q), :]
        s = _dot_nt(k_t, qs)
        if bias is not None:
            s = s + bias
        masks = []
        for u in range(nt):
            kt = kt0 + u
            ktc = jnp.clip(kt, 0, n_tiles - 1)
            sel = jnp.where(half_rows, sel_ref[pl.ds(2 * ktc, 1), :], sel_ref[pl.ds(2 * ktc + 1, 1), :])
            masks.append(jnp.where((kt >= 0) & (kt <= kt_hi), sel, 0.0))
        keep = (masks[0] if nt == 1 else jnp.concatenate(masks, axis=0)) > 0
        s = jnp.concatenate([jnp.where(keep, s[:, h * Q:(h + 1) * Q], NEG) for h in range(H4)], axis=1)
        m_new = jnp.maximum(m, jnp.max(s, axis=0, keepdims=True))
        alpha = jnp.exp(m - m_new)
        p = jnp.exp(s - m_new)
        l_new = alpha * l + jnp.sum(p, axis=0, keepdims=True)
        acc_new = alpha * acc + _dot_tn(v_t, p.astype(BF16))
        return m_new, l_new, acc_new

    carry0 = (jnp.full((1, H4 * Q), M_FLOOR, F32), jnp.zeros((1, H4 * Q), F32), jnp.zeros((HD, H4 * Q), F32))
    n_far = jnp.maximum(ci - 1, 0)

    def far_body(it, carries):
        ca, cb = carries
        kt0 = it * (2 * SLC_GROUP)
        return (slc_step(kt0, SLC_GROUP, ci - 2, None, ca), slc_step(kt0 + SLC_GROUP, SLC_GROUP, ci - 2, None, cb))

    (m_a, l_a, acc_a), (m_b, l_b, acc_b) = lax.fori_loop(
        0, (n_far + 2 * SLC_GROUP - 1) // (2 * SLC_GROUP), far_body, (carry0, carry0))
    m_ab = jnp.maximum(m_a, m_b)
    w_a = jnp.exp(m_a - m_ab)
    w_b = jnp.exp(m_b - m_ab)
    carry = (m_ab, w_a * l_a + w_b * l_b, w_a * acc_a + w_b * acc_b)
    bp_t = head_tiles(bp_ref)
    bd_t = head_tiles(bd_ref)
    _, l_s, acc_s = slc_step(ci - 1, 2, ci, jnp.concatenate([bp_t, bd_t], axis=0), carry)
    o_slc = acc_s * (1.0 / jnp.where(l_s > 0, l_s, 1.0))

    kw = kw_ref[pl.ds(pl.multiple_of(ci * Q, Q), WINDOW + Q), :]
    vw = vw_ref[pl.ds(pl.multiple_of(ci * Q, Q), WINDOW + Q), :]
    s_w = _dot_nt(kw, qs)
    n_wt = WINDOW // Q
    in_window = lane_i < row_i
    pieces = [jnp.concatenate([jnp.where(in_window, s_w[0:Q, h * Q:(h + 1) * Q], NEG) for h in range(H4)], axis=1)]
    for t in range(1, n_wt - 1):
        pieces.append(s_w[t * Q:(t + 1) * Q])
    pieces.append(s_w[(n_wt - 1) * Q:n_wt * Q] + bp_t)
    pieces.append(s_w[n_wt * Q:(n_wt + 1) * Q] + bd_t)
    s_w = jnp.concatenate(pieces, axis=0)
    x_w = lax.broadcasted_iota(jnp.int32, (WINDOW + Q, 1), 0)
    s_w = jnp.where(x_w + ci * Q >= WINDOW, s_w, NEG)
    e_w, d_w = _col_softmax_stats(s_w)
    p_w = e_w * (1.0 / jnp.where(d_w > 0, d_w, 1.0))
    o_win = _dot_tn(vw, p_w.astype(BF16))

    gt = jnp.transpose(jax.nn.sigmoid(gl_ref[...]))
    outs = []
    for h in range(H4):
        base = (g * H4 + h) * 3
        sl = slice(h * Q, (h + 1) * Q)
        gc = gt_row(gt, base)
        gs = gt_row(gt, base + 1)
        gw = gt_row(gt, base + 2)
        o_h = gc * o_cmp[:, sl] + gs * o_slc[:, sl] + gw * o_win[:, sl]
        outs.append(jnp.transpose(o_h))
    o_ref[...] = jnp.concatenate(outs, axis=1).astype(o_ref.dtype)


def gt_row(gt, idx):
    rows = lax.broadcasted_iota(jnp.int32, gt.shape, 0)
    return jnp.sum(jnp.where(rows == idx, gt, 0.0), axis=0, keepdims=True)


def _nsa_attention(rel_bias, proj, proj_gate, kcmp, vcmp, ks, vs, kw, vw):
    T = proj.shape[0]
    G = NSA_KV_HEADS
    Q = Q_BLOCK
    ncp = T // CMP_STRIDE
    nch = T // Q
    assert nch % SLC_GROUP == 0 and T // SLC_LEN <= Q
    kernel = functools.partial(_nsa_kernel, ncp=ncp, n_tiles=nch)
    full = lambda rows: pl.BlockSpec((None, rows, HEAD_DIM), lambda g, c: (g, 0, 0))
    return pl.pallas_call(
        kernel,
        out_shape=jax.ShapeDtypeStruct((T, NSA_HEADS * HEAD_DIM), BF16),
        grid=(G, nch),
        in_specs=[pl.BlockSpec(memory_space=pltpu.SMEM),
                  pl.BlockSpec((Q, NSA_HPG * HEAD_DIM), lambda g, c: (c, g)),
                  pl.BlockSpec((Q, LANES), lambda g, c: (c, 0)),
                  full(ncp + 2 * Q), full(ncp + 2 * Q),
                  full(T + Q + SLC_TAIL), full(T + Q + SLC_TAIL), full(T + WINDOW), full(T + WINDOW)],
        out_specs=pl.BlockSpec((Q, NSA_HPG * HEAD_DIM), lambda g, c: (c, g)),
        scratch_shapes=[pltpu.VMEM((NSA_HEADS, Q, Q), F32), pltpu.VMEM((NSA_HEADS, Q, Q), F32),
                        pltpu.VMEM((NSA_HEADS, Q, Q), F32), pltpu.VMEM((Q, Q), F32), pltpu.VMEM((Q, Q), F32)],
        compiler_params=_cparams(("arbitrary", "arbitrary")),
        name="nsa_attention",
    )(rel_bias, proj, proj_gate, kcmp, vcmp, ks, vs, kw, vw)


def _softplus(z):
    return jnp.maximum(z, 0.0) + jnp.log1p(jnp.exp(-jnp.abs(z)))


def _lru_kernel(x_ref, y_ref, cw_ref, cb_ref, wa_ref, ba_ref, wi_ref, bi_ref, lam_ref, o_ref,
                xbuf, hc, a_s, b_s, h_s, *, tb):
    i = pl.program_id(0)

    @pl.when(i == 0)
    def _():
        xbuf[0:8, :] = jnp.zeros((8, LRU_WIDTH), F32)
        hc[...] = jnp.zeros((1, LRU_WIDTH), F32)

    xbuf[8:8 + tb, :] = x_ref[...]
    xc = cb_ref[...]
    for w in range(CONV_W):
        xc = xc + cw_ref[w:w + 1, :] * xbuf[8 - (CONV_W - 1) + w:8 - (CONV_W - 1) + w + tb, :]
    xbuf[0:8, :] = xbuf[tb:tb + 8, :]

    xcb = xc.astype(BF16)
    ra, ia = [], []
    for n in range(LRU_BLOCKS):
        xg = xcb[:, n * LRU_BW:(n + 1) * LRU_BW]
        ra.append(_dot(xg, wa_ref[n].astype(BF16)))
        ia.append(_dot(xg, wi_ref[n].astype(BF16)))
    r_gate = jax.nn.sigmoid(jnp.concatenate(ra, axis=1) + ba_ref[...])
    i_gate = jax.nn.sigmoid(jnp.concatenate(ia, axis=1) + bi_ref[...])
    log_a = (-LRU_C * r_gate) * _softplus(-lam_ref[...])
    a = jnp.exp(log_a)
    a_s[...] = a
    b_s[...] = jnp.sqrt(-jnp.tanh(log_a) * (a * a + 1.0)) * (i_gate * xc)

    rows = lax.broadcasted_iota(jnp.int32, (8, LRU_WIDTH), 0)

    def tile(k, h):
        r0 = pl.multiple_of(k * 8, 8)
        A = a_s[pl.ds(r0, 8), :]
        B = b_s[pl.ds(r0, 8), :]
        for sh in (1, 2, 4):
            ok = rows >= sh
            A_p = pltpu.roll(A, sh, axis=0)
            B_p = pltpu.roll(B, sh, axis=0)
            B = jnp.where(ok, A * B_p + B, B)
            A = jnp.where(ok, A * A_p, A)
        H = A * h + B
        h_s[pl.ds(r0, 8), :] = H
        return H[7:8, :]

    hc[...] = lax.fori_loop(0, tb // 8, tile, hc[...])
    o_ref[...] = (jax.nn.gelu(y_ref[...]) * h_s[...]).astype(o_ref.dtype)


def _rglru(proj, y_col, x_col, conv_w, conv_b, wa, ba, wi, bi, lam, tb=256):
    T = proj.shape[0]
    W = LRU_WIDTH
    vec = pl.BlockSpec((1, W), lambda i: (0, 0))
    blk = pl.BlockSpec((LRU_BLOCKS, LRU_BW, LRU_BW), lambda i: (0, 0, 0))
    return pl.pallas_call(
        functools.partial(_lru_kernel, tb=tb),
        out_shape=jax.ShapeDtypeStruct((T, W), BF16),
        grid=(T // tb,),
        in_specs=[pl.BlockSpec((tb, W), lambda i: (i, x_col)),
                  pl.BlockSpec((tb, W), lambda i: (i, y_col)),
                  pl.BlockSpec((CONV_W, W), lambda i: (0, 0)), vec, blk, vec, blk, vec, vec],
        out_specs=pl.BlockSpec((tb, W), lambda i: (i, 0)),
        scratch_shapes=[pltpu.VMEM((tb + 8, W), F32), pltpu.VMEM((1, W), F32),
                        pltpu.VMEM((tb, W), F32), pltpu.VMEM((tb, W), F32), pltpu.VMEM((tb, W), F32)],
        compiler_params=_cparams(("arbitrary",)),
        name="rglru",
    )(proj, proj, conv_w, conv_b.reshape(1, W), wa, ba.reshape(1, W), wi, bi.reshape(1, W), lam.reshape(1, W))


def _split3(x):
    hi = x.astype(BF16)
    r = x - hi.astype(F32)
    mid = r.astype(BF16)
    lo = (r - mid.astype(F32)).astype(BF16)
    return hi, mid, lo


def _hgrn_kernel(q_ref, f_ref, v_ref, g_ref, lbl_ref, ng_ref, o_ref, st_ref, oacc, *, tbh, layer):
    t = pl.program_id(1)
    C = HG_CHUNK
    SB = HG_SUB

    NH = HG_HEADS_PER_STEP
    DK, DV = HG_DK, HG_DV

    @pl.when(t == 0)
    def _():
        st_ref[...] = jnp.zeros((NH, DV, DK), F32)

    lg = lbl_ref[...]
    e = jnp.exp(lg - jnp.max(lg, axis=0, keepdims=True))
    sm = e / jnp.sum(e, axis=0, keepdims=True)
    cum = sm[0:1, :]
    for l in range(1, layer + 1):
        cum = cum + sm[l:l + 1, :]
    lb_all = cum - sm[0:1, :]

    tri = (lax.broadcasted_iota(jnp.int32, (C, C), 0) >= lax.broadcasted_iota(jnp.int32, (C, C), 1)).astype(BF16)
    rows8 = lax.broadcasted_iota(jnp.int32, (8, DK), 0)

    def head_chunk(r0, hh):
        cs = slice(hh * DK, (hh + 1) * DK)
        lb = lb_all[:, cs]
        qz = q_ref[pl.ds(r0, C), cs]
        qq = qz * jax.nn.sigmoid(qz)
        f = lb + (1.0 - lb) * jax.nn.sigmoid(f_ref[pl.ds(r0, C), cs])
        logf = jnp.log(jnp.maximum(f, 1e-30))
        kk = 1.0 - f
        vv = v_ref[pl.ds(r0, C), cs]
        l_hi, l_mid, l_lo = _split3(logf)
        bcum = _dot(tri, l_hi) + _dot(tri, l_mid) + _dot(tri, l_lo)
        st = st_ref[hh]
        o_parts = []
        vb = vv.astype(BF16)
        for I in range(C // SB):
            lo_r = I * SB
            for a in range(SB // 8):
                t0 = lo_r + 8 * a
                q8 = qq[t0:t0 + 8]
                b8 = bcum[t0:t0 + 8]
                o8 = jnp.zeros((8, DV), F32)
                for s in range(lo_r, t0 + 8):
                    d = b8 - bcum[s:s + 1]
                    if s > t0:
                        d = jnp.where(rows8 >= s - t0, d, NEG)
                    w = q8 * (kk[s:s + 1] * jnp.exp(d))
                    o8 = o8 + jnp.sum(w, axis=1, keepdims=True) * vv[s:s + 1]
                o_parts.append(o8)
        for I in range(1, C // SB):
            lo_r = I * SB
            ref = bcum[lo_r - 1:lo_r]
            qe = (qq[lo_r:lo_r + SB] * jnp.exp(bcum[lo_r:lo_r + SB] - ref)).astype(BF16)
            ke = (kk[0:lo_r] * jnp.exp(ref - bcum[0:lo_r])).astype(BF16)
            att = _dot_nt(qe, ke)
            o_off = _dot(att.astype(BF16), vb[0:lo_r])
            for a in range(SB // 8):
                o_parts[lo_r // 8 + a] = o_parts[lo_r // 8 + a] + o_off[8 * a:8 * a + 8]
        o_intra = jnp.concatenate(o_parts, axis=0)
        o_inter = _dot_nt((qq * jnp.exp(bcum)).astype(BF16), st.astype(BF16))
        oacc[pl.ds(r0, C), cs] = o_inter + o_intra
        b_last = bcum[C - 1:C]
        kd = (kk * jnp.exp(b_last - bcum)).astype(BF16)
        st_ref[hh] = st * jnp.exp(b_last) + _dot_tn(vb, kd)

    def chunk(c, carry):
        r0 = pl.multiple_of(c * C, C)
        for hh in range(NH):
            head_chunk(r0, hh)
        return carry

    lax.fori_loop(0, tbh // C, chunk, 0)
    gz = g_ref[...]
    gate = (gz * jax.nn.sigmoid(gz)) * ng_ref[...]
    for hh in range(NH):
        cs = slice(hh * DV, (hh + 1) * DV)
        o = oacc[:, cs]
        o = o * lax.rsqrt(jnp.mean(o * o, axis=-1, keepdims=True) + EPS)
        o_ref[:, cs] = (o * gate[:, cs]).astype(o_ref.dtype)


def _hgrn2(proj, lb_logits, norm_g, layer, tbh=512):
    T = proj.shape[0]
    NH = HG_HEADS_PER_STEP
    HB = HG_HEADS // NH
    wk = NH * HG_DK
    col = lambda off: pl.BlockSpec((tbh, wk), lambda h, t: (t, off + h))
    return pl.pallas_call(
        functools.partial(_hgrn_kernel, tbh=tbh, layer=layer),
        out_shape=jax.ShapeDtypeStruct((T, HG_VW), BF16),
        grid=(HB, T // tbh),
        in_specs=[col(0), col(HB), col(2 * HB), col(3 * HB),
                  pl.BlockSpec((DEPTH, wk), lambda h, t: (0, h)),
                  pl.BlockSpec((1, wk), lambda h, t: (0, h))],
        out_specs=pl.BlockSpec((tbh, wk), lambda h, t: (t, h)),
        scratch_shapes=[pltpu.VMEM((NH, HG_DV, HG_DK), F32), pltpu.VMEM((tbh, wk), F32)],
        compiler_params=_cparams(("arbitrary", "arbitrary")),
        name="hgrn2",
    )(proj, proj, proj, proj, lb_logits, norm_g.reshape(1, HG_VW))


def _router_kernel(x_ref, g_ref, sc_ref, sh_ref, w_ref, b_ref, h_ref, eid_ref, gate_ref):
    h = _norm_mod(x_ref[...], g_ref[...], sc_ref[...], sh_ref[...])
    h_ref[...] = h
    h1, h2, h3 = _split3(h)
    w1, w2, w3 = _split3(w_ref[...])
    logits = (_dot(h1, w1) + (_dot(h1, w2) + _dot(h2, w1))
              + (_dot(h2, w2) + _dot(h1, w3) + _dot(h3, w1))) + b_ref[...]
    lane = lax.broadcasted_iota(jnp.int32, logits.shape, 1)
    is_g = lane < N_GROUPS
    glog = jnp.where(is_g, logits, -jnp.inf)
    gmax = jnp.max(glog, axis=-1, keepdims=True)
    grp = jnp.min(jnp.where(glog == gmax, lane, LANES), axis=-1, keepdims=True)
    gsum = jnp.sum(jnp.where(is_g, jnp.exp(glog - gmax), 0.0), axis=-1, keepdims=True)
    p_grp = 1.0 / gsum
    lo = N_GROUPS + EXP_PER_GROUP * grp
    el = jnp.where((lane >= lo) & (lane < lo + EXP_PER_GROUP), logits, -jnp.inf)
    v1 = jnp.max(el, axis=-1, keepdims=True)
    i1 = jnp.min(jnp.where(el == v1, lane, LANES), axis=-1, keepdims=True)
    el2 = jnp.where(lane == i1, -jnp.inf, el)
    v2 = jnp.max(el2, axis=-1, keepdims=True)
    i2 = jnp.min(jnp.where(el2 == v2, lane, LANES), axis=-1, keepdims=True)
    e2 = jnp.exp(v2 - v1)
    den = 1.0 + e2
    g1 = p_grp * (1.0 / den)
    g2 = p_grp * (e2 / den)
    eid_ref[...] = jnp.where(lane == 0, i1 - N_GROUPS, jnp.where(lane == 1, i2 - N_GROUPS, 0))
    gate_ref[...] = jnp.where(lane == 0, g1, jnp.where(lane == 1, g2, 0.0))


def _router(x, g, sc, sh, w_r, b_r, tm=512):
    T, D = x.shape
    vec = pl.BlockSpec((1, D), lambda i: (0, 0))
    return pl.pallas_call(
        _router_kernel,
        out_shape=[jax.ShapeDtypeStruct((T, D), F32), jax.ShapeDtypeStruct((T, LANES), jnp.int32),
                   jax.ShapeDtypeStruct((T, LANES), F32)],
        grid=(T // tm,),
        in_specs=[pl.BlockSpec((tm, D), lambda i: (i, 0)), vec, vec, vec,
                  pl.BlockSpec((D, LANES), lambda i: (0, 0)), pl.BlockSpec((1, LANES), lambda i: (0, 0))],
        out_specs=[pl.BlockSpec((tm, D), lambda i: (i, 0)), pl.BlockSpec((tm, LANES), lambda i: (i, 0)),
                   pl.BlockSpec((tm, LANES), lambda i: (i, 0))],
        compiler_params=_cparams(("arbitrary",)),
        name="moe_router",
    )(x, g, sc, sh, w_r, b_r)


def _cast_rows(src_ref, slot, dst_ref, rows):
    def body(c, carry):
        r = pl.multiple_of(c * rows, rows)
        dst_ref[pl.ds(r, rows), :] = src_ref[slot, pl.ds(r, rows), :].astype(BF16)
        return carry
    lax.fori_loop(0, dst_ref.shape[0] // rows, body, 0, unroll=4)


def _expert_kernel(rid_ref, re_ref, nu_ref, tok_ref, h_ref, wg_hbm, wu_hbm, wd_hbm, y_ref,
                   wgf, wuf, wdf, wgb, wub, wdb, xbuf, wsem, xsem, *, layer):
    i = pl.program_id(0)
    n_used = nu_ref[0]
    R = MOE_BLOCK

    half = D_EXPERT // 2

    def w_copies(e, slot):
        return ((pltpu.make_async_copy(wg_hbm.at[layer, e], wgf.at[slot], wsem.at[slot, 0]), 0),
                (pltpu.make_async_copy(wu_hbm.at[layer, e], wuf.at[slot], wsem.at[slot, 1]), 1),
                (pltpu.make_async_copy(wd_hbm.at[layer, e, pl.ds(0, half)], wdf.at[slot, pl.ds(0, half)],
                                       wsem.at[slot, 2]), 0),
                (pltpu.make_async_copy(wd_hbm.at[layer, e, pl.ds(half, half)], wdf.at[slot, pl.ds(half, half)],
                                       wsem.at[slot, 3]), 1))

    def x_start(blk, slot):
        def body(r, c):
            tok = tok_ref[blk * R + r]
            pltpu.make_async_copy(h_ref.at[pl.ds(tok, 1)], xbuf.at[slot, pl.ds(r, 1)], xsem.at[slot]).start()
            return c
        lax.fori_loop(0, R, body, 0, unroll=8)

    def x_wait(slot):
        pltpu.make_async_copy(h_ref.at[pl.ds(0, R)], xbuf.at[slot], xsem.at[slot]).wait()

    def w_request(run):
        @pl.when(re_ref[run] >= 0)
        def _():
            for c, pri in w_copies(re_ref[run], run % 2):
                c.start(priority=pri)

    @pl.when(i == 0)
    def _():
        w_request(0)
        w_request(1)
        x_start(0, 0)

    @pl.when(i < n_used)
    def _():
        slot = i % 2

        @pl.when(i + 1 < n_used)
        def _():
            x_start(i + 1, 1 - slot)

        run = rid_ref[i]

        @pl.when((i == 0) | (run != rid_ref[jnp.maximum(i - 1, 0)]))
        def _():
            ws = run % 2
            for c, _ in w_copies(re_ref[run], ws):
                c.wait()
            _cast_rows(wgf, ws, wgb, 64)
            _cast_rows(wuf, ws, wub, 64)
            _cast_rows(wdf, ws, wdb, 16)
            w_request(run + 2)

        x_wait(slot)
        x = xbuf[slot].astype(BF16)
        hg = _dot(x, wgb[...])
        hu = _dot(x, wub[...])
        hid = (hg * jax.nn.sigmoid(hg)) * hu
        y_ref[...] = _dot(hid.astype(BF16), wdb[...])

    @pl.when(i >= n_used)
    def _():
        y_ref[...] = jnp.zeros(y_ref.shape, y_ref.dtype)


def _experts(h, tok, run_id, run_e, n_used, w_gate, w_up, w_down, layer):
    N, D = h.shape
    n_pad = tok.shape[0]
    n_blk = n_pad // MOE_BLOCK
    any_spec = pl.BlockSpec(memory_space=pl.ANY)
    return pl.pallas_call(
        functools.partial(_expert_kernel, layer=layer),
        out_shape=jax.ShapeDtypeStruct((n_pad, D), F32),
        grid_spec=pltpu.PrefetchScalarGridSpec(
            num_scalar_prefetch=4,
            grid=(n_blk,),
            in_specs=[any_spec, any_spec, any_spec, any_spec],
            out_specs=pl.BlockSpec((MOE_BLOCK, D), lambda i, *_: (i, 0)),
            scratch_shapes=[pltpu.VMEM((2, D, D_EXPERT), F32), pltpu.VMEM((2, D, D_EXPERT), F32),
                            pltpu.VMEM((2, D_EXPERT, D), F32),
                            pltpu.VMEM((D, D_EXPERT), BF16), pltpu.VMEM((D, D_EXPERT), BF16),
                            pltpu.VMEM((D_EXPERT, D), BF16),
                            pltpu.VMEM((2, MOE_BLOCK, D), F32),
                            pltpu.SemaphoreType.DMA((2, 4)), pltpu.SemaphoreType.DMA((2,))]),
        compiler_params=_cparams(("arbitrary",)),
        name="moe_experts",
    )(run_id, run_e, n_used, tok, h, w_gate, w_up, w_down)


def _combine_kernel(slot_ref, y_ref, x_ref, w_ref, g_ref, o_ref, buf, sem, *, rows):
    i = pl.program_id(0)

    def copy(r, k):
        return pltpu.make_async_copy(y_ref.at[pl.ds(slot_ref[(i * rows + r) * 2 + k], 1)],
                                     buf.at[k, pl.ds(r, 1)], sem)

    def start(r, c):
        copy(r, 0).start()
        copy(r, 1).start()
        return c

    lax.fori_loop(0, rows, start, 0, unroll=8)
    for k in range(TOPK_IN_GROUP):
        pltpu.make_async_copy(y_ref.at[pl.ds(0, rows)], buf.at[k], sem).wait()
    w = w_ref[...]
    moe = buf[0] * w[:, 0:1] + buf[1] * w[:, 1:2]
    o_ref[...] = x_ref[...] + g_ref[...] * moe


def _combine(yb, slots, x, gate_w, g2, rows=256):
    T, D = x.shape
    return pl.pallas_call(
        functools.partial(_combine_kernel, rows=rows),
        out_shape=jax.ShapeDtypeStruct((T, D), F32),
        grid_spec=pltpu.PrefetchScalarGridSpec(
            num_scalar_prefetch=1,
            grid=(T // rows,),
            in_specs=[pl.BlockSpec(memory_space=pl.ANY),
                      pl.BlockSpec((rows, D), lambda i, s: (i, 0)),
                      pl.BlockSpec((rows, LANES), lambda i, s: (i, 0)),
                      pl.BlockSpec((1, D), lambda i, s: (0, 0))],
            out_specs=pl.BlockSpec((rows, D), lambda i, s: (i, 0)),
            scratch_shapes=[pltpu.VMEM((2, rows, D), F32), pltpu.SemaphoreType.DMA(())]),
        compiler_params=_cparams(("arbitrary",)),
        name="moe_combine",
    )(slots.reshape(-1), yb, x, gate_w, g2)


def _hier_moe_residual(x, g, sc, sh, g2, w_grp, b_grp, w_exp, b_exp, w_gate, w_up, w_down, layer):
    N, D = x.shape
    n_route = N_GROUPS + N_EXPERTS
    w_r = jnp.zeros((D, LANES), F32).at[:, :N_GROUPS].set(w_grp).at[:, N_GROUPS:n_route].set(w_exp)
    b_r = jnp.zeros((1, LANES), F32).at[0, :N_GROUPS].set(b_grp).at[0, N_GROUPS:n_route].set(b_exp)
    h, eid_l, gate_l = _router(x, g, sc, sh, w_r, b_r)
    eid = eid_l[:, :TOPK_IN_GROUP]

    A = N * TOPK_IN_GROUP
    e_flat = eid.reshape(-1)
    tok = jnp.repeat(jnp.arange(N, dtype=jnp.int32), TOPK_IN_GROUP)
    order = jnp.argsort(e_flat)
    e_s, tok_s = e_flat[order], tok[order]
    counts = jnp.zeros((N_EXPERTS,), jnp.int32).at[e_flat].add(1)
    start = jnp.cumsum(counts) - counts
    padded = (counts + MOE_BLOCK - 1) // MOE_BLOCK * MOE_BLOCK
    pad_end = jnp.cumsum(padded)
    pad_start = pad_end - padded
    dest = pad_start[e_s] + jnp.arange(A, dtype=jnp.int32) - start[e_s]
    n_blk = -(-(A + N_EXPERTS * MOE_BLOCK) // MOE_BLOCK)
    n_pad = n_blk * MOE_BLOCK
    buf_tok = jnp.zeros((n_pad,), jnp.int32).at[dest].set(tok_s)
    blk_idx = jnp.arange(n_blk, dtype=jnp.int32)
    n_used = pad_end[-1] // MOE_BLOCK
    blk_e = jnp.minimum(jnp.searchsorted(pad_end, blk_idx * MOE_BLOCK, side='right'), N_EXPERTS - 1).astype(jnp.int32)
    blk_e = jnp.where(blk_idx < n_used, blk_e, blk_e[n_used - 1])
    change = jnp.concatenate([jnp.ones((1,), bool), blk_e[1:] != blk_e[:-1]])
    run_id = (jnp.cumsum(change.astype(jnp.int32)) - 1).astype(jnp.int32)
    runs = jnp.arange(n_blk + 2, dtype=jnp.int32)
    run_e = jnp.max(jnp.where(run_id[None, :] == runs[:, None], blk_e[None, :], -1), axis=1).astype(jnp.int32)
    slots = jnp.zeros((A,), jnp.int32).at[order].set(dest).reshape(N, TOPK_IN_GROUP)

    yb = _experts(h, buf_tok, run_id, run_e, n_used.reshape(1).astype(jnp.int32), w_gate, w_up, w_down, layer)
    return _combine(yb, slots, x, gate_l, g2)


def _final_norm_kernel(x_ref, g_ref, o_ref):
    x = x_ref[...]
    o_ref[...] = (x * lax.rsqrt(jnp.mean(x * x, axis=-1, keepdims=True) + EPS)) * g_ref[...]


def _final_norm(x, g, tm=512):
    T, D = x.shape
    return pl.pallas_call(
        _final_norm_kernel,
        out_shape=jax.ShapeDtypeStruct((T, D), F32),
        grid=(T // tm,),
        in_specs=[pl.BlockSpec((tm, D), lambda i: (i, 0)), pl.BlockSpec((1, D), lambda i: (0, 0))],
        out_specs=pl.BlockSpec((tm, D), lambda i: (i, 0)),
        compiler_params=_cparams(("arbitrary",)),
        name="final_norm",
    )(x, g)


def _even_mixer_residual(x, g, sc, sh, g1, rel_bias, w_in, w_out, cmp_pe, cmp_w1, cmp_b1, cmp_w2, cmp_b2,
                         conv_w, conv_b, lru_wa, lru_ba, lru_wi, lru_bi, lru_lambda):
    T, D = x.shape
    G, HD = NSA_KV_HEADS, HEAD_DIM
    nq = NSA_HEADS * HD
    n_kv = 6 * NSA_KV_W
    n_gate = 3 * NSA_HEADS
    w_main = jnp.concatenate([w_in[:, :nq], w_in[:, nq + n_kv + n_gate:], w_in[:, nq:nq + n_kv]],
                             axis=1).astype(BF16)
    w_gl = jnp.zeros((D, LANES), F32).at[:, :n_gate].set(w_in[:, nq + n_kv:nq + n_kv + n_gate]).astype(BF16)
    kv0 = nq + 2 * LRU_WIDTH
    tn = 512
    proj, proj_gate, kv_b = _norm_proj(x, g, sc, sh, w_main, w_gl, bf16_from=kv0 // tn, tn=tn)

    def kv_heads(j, src=kv_b, c_base=0):
        c0 = c_base + j * NSA_KV_W
        return src[:, c0:c0 + NSA_KV_W].reshape(T, G, HD).transpose(1, 0, 2)

    cmp_in = jnp.stack([kv_heads(0, proj, kv0), kv_heads(1, proj, kv0)], axis=0)
    cmp_out = _compress(cmp_in, cmp_pe, cmp_w1, cmp_b1, cmp_w2, cmp_b2)
    ncp = T // CMP_STRIDE
    valid = (jnp.arange(ncp) < ncp - 1)[None, None, :, None]
    cmp_pad = jnp.pad(jnp.where(valid, cmp_out, 0.0), ((0, 0), (0, 0), (Q_BLOCK, Q_BLOCK), (0, 0)))
    slc_pad = ((0, 0), (Q_BLOCK, SLC_TAIL), (0, 0))
    ks = jnp.pad(kv_heads(2), slc_pad)
    vs = jnp.pad(kv_heads(3), slc_pad)
    kw = jnp.pad(kv_heads(4), ((0, 0), (WINDOW, 0), (0, 0)))
    vw = jnp.pad(kv_heads(5), ((0, 0), (WINDOW, 0), (0, 0)))
    nsa_out = _nsa_attention(rel_bias, proj, proj_gate, cmp_pad[0], cmp_pad[1], ks, vs, kw, vw)

    y_col = nq // LRU_WIDTH
    lru_out = _rglru(proj, y_col, y_col + 1, conv_w, conv_b, lru_wa, lru_ba, lru_wi, lru_bi, lru_lambda)
    w_o = w_out.astype(BF16)
    return _out_proj(nsa_out, lru_out, w_o[:nq], w_o[nq:], x, g1)


def _odd_mixer_residual(x, g, sc, sh, g1, lb_logits, w_in, w_out, norm_g, layer):
    proj = _norm_proj(x, g, sc, sh, w_in.astype(BF16))[0]
    o = _hgrn2(proj, lb_logits, norm_g, layer)
    w_o = w_out.astype(BF16)
    half = HG_VW // 2
    return _out_proj(o[:, :half], o[:, half:], w_o[:half], w_o[half:], x, g1)


def kernel(x, c, rel_bias, ada_w, ada_b, norm_mix_g, norm_ffn_g, ev_w_in, ev_w_out, cmp_pe, cmp_w1, cmp_b1, cmp_w2, cmp_b2, lru_conv_w, lru_conv_b, lru_wa, lru_ba, lru_wi, lru_bi, lru_lambda, od_w_in, od_w_out, hg_lb_logits, hg_norm_g, moe_w_grp, moe_b_grp, moe_w_exp, moe_b_exp, moe_w_gate, moe_w_up, moe_w_down, final_g):
    B, T, D = x.shape
    assert B == 1 and D == D_MODEL
    xt = x.reshape(T, D)
    mod = _modulation(c, ada_w, ada_b)
    for l in range(DEPTH):
        sh1, sc1, g1, sh2, sc2, g2 = [mod[l, :, k * D:(k + 1) * D] for k in range(6)]
        gm = norm_mix_g[l].reshape(1, D)
        gf = norm_ffn_g[l].reshape(1, D)
        j = l // 2
        if l % 2 == 0:
            xt = _even_mixer_residual(xt, gm, sc1, sh1, g1, rel_bias, ev_w_in[j], ev_w_out[j], cmp_pe[j],
                                      cmp_w1[j], cmp_b1[j], cmp_w2[j], cmp_b2[j], lru_conv_w[j],
                                      lru_conv_b[j], lru_wa[j], lru_ba[j], lru_wi[j], lru_bi[j], lru_lambda[j])
        else:
            xt = _odd_mixer_residual(xt, gm, sc1, sh1, g1, hg_lb_logits, od_w_in[j], od_w_out[j],
                                     hg_norm_g[j], l)
        xt = _hier_moe_residual(xt, gf, sc2, sh2, g2, moe_w_grp[l], moe_b_grp[l], moe_w_exp[l], moe_b_exp[l],
                                moe_w_gate, moe_w_up, moe_w_down, l)
    return _final_norm(xt, final_g.reshape(1, D)).reshape(B, T, D)
```

```python
import functools
import math

import numpy as np
import jax
import jax.numpy as jnp
from jax import lax
from jax.experimental import pallas as pl
from jax.experimental.pallas import tpu as pltpu

F32 = jnp.float32
BF16 = jnp.bfloat16

D_MODEL = 2048
DEPTH = 2
NSA_HEADS = 8
NSA_KV_HEADS = 2
NSA_HPG = NSA_HEADS // NSA_KV_HEADS
HEAD_DIM = 128
NSA_KV_W = NSA_KV_HEADS * HEAD_DIM
CMP_LEN = 32
CMP_STRIDE = 16
SLC_LEN = 64
SLC_TOPN = 16
WINDOW = 512
Q_BLOCK = 128
LRU_WIDTH = 1024
LRU_BLOCKS = 8
LRU_BW = LRU_WIDTH // LRU_BLOCKS
CONV_W = 4
LRU_C = 8.0
HG_HEADS = 16
HG_DK = 128
HG_DV = 128
HG_CHUNK = 64
HG_SUB = 16
HG_HEADS_PER_STEP = 4
SLC_GROUP = 4
RANK_UNROLL = 4
SLC_TAIL = 2 * SLC_GROUP * Q_BLOCK
HG_KW = HG_HEADS * HG_DK
HG_VW = HG_HEADS * HG_DV
N_BUCKETS = 32
MAX_DIST = 128
N_GROUPS = 8
EXP_PER_GROUP = 8
N_EXPERTS = N_GROUPS * EXP_PER_GROUP
TOPK_IN_GROUP = 2
D_EXPERT = 512
MOE_BLOCK = 128
EPS = 1e-6

LANES = 128
NEG = -1e30
M_FLOOR = -1e20
VMEM_LIMIT = 56 * 1024 * 1024


def _cparams(sem):
    return pltpu.CompilerParams(dimension_semantics=sem, vmem_limit_bytes=VMEM_LIMIT)


def _dot(a, b):
    return jnp.dot(a, b, preferred_element_type=F32)


def _dot_nt(a, b):
    return lax.dot_general(a, b, (((1,), (1,)), ((), ())), preferred_element_type=F32)


def _dot_tn(a, b):
    return lax.dot_general(a, b, (((0,), (0,)), ((), ())), preferred_element_type=F32)


def _bucket_starts():
    n = np.arange(0, MAX_DIST + 1, dtype=np.int32)
    max_exact = N_BUCKETS // 2
    nf = np.maximum(n, 1).astype(np.float32)
    large = max_exact + (np.log(nf / np.float32(max_exact)) / np.float32(math.log(MAX_DIST / max_exact))
                         * np.float32(N_BUCKETS - max_exact)).astype(np.int32)
    large = np.minimum(large, N_BUCKETS - 1)
    b = np.where(n < max_exact, n, large)
    starts = [int(np.argmax(b >= k)) for k in range(N_BUCKETS)]
    assert all(b[s] == k for k, s in enumerate(starts)) and b[-1] == N_BUCKETS - 1
    return starts


BUCKET_STARTS = _bucket_starts()


def _mod_kernel(c_ref, w_ref, b_ref, o_ref):
    c = c_ref[...]
    cact = c * jax.nn.sigmoid(c)
    o_ref[...] = jnp.sum(cact * w_ref[...], axis=0, keepdims=True) + b_ref[...]


def _modulation(c, ada_w, ada_b):
    D = D_MODEL
    tn = 1024
    n_out = 6 * D
    c_col = c.reshape(D, 1)
    return pl.pallas_call(
        _mod_kernel,
        out_shape=jax.ShapeDtypeStruct((DEPTH, 1, n_out), F32),
        grid=(DEPTH, n_out // tn),
        in_specs=[pl.BlockSpec((D, 1), lambda l, j: (0, 0)),
                  pl.BlockSpec((None, D, tn), lambda l, j: (l, 0, j)),
                  pl.BlockSpec((None, 1, tn), lambda l, j: (l, 0, j))],
        out_specs=pl.BlockSpec((None, 1, tn), lambda l, j: (l, 0, j)),
        compiler_params=_cparams(("arbitrary", "arbitrary")),
        name="adaln_mod",
    )(c_col, ada_w, ada_b.reshape(DEPTH, 1, n_out))


def _norm_mod(x, g, sc, sh):
    y = x * lax.rsqrt(jnp.mean(x * x, axis=-1, keepdims=True) + EPS)
    return (y * g) * (1.0 + sc) + sh


def _norm_proj_kernel(x_ref, g_ref, sc_ref, sh_ref, w_ref, *rest, has_extra, bf16_from):
    if has_extra:
        wx_ref, o_ref, ox_ref, ob_ref, h_ref = rest
    else:
        o_ref, h_ref = rest
    j = pl.program_id(1)

    @pl.when(j == 0)
    def _():
        h = _norm_mod(x_ref[...], g_ref[...], sc_ref[...], sh_ref[...]).astype(BF16)
        h_ref[...] = h
        if has_extra:
            ox_ref[...] = _dot(h, wx_ref[...])

    y = _dot(h_ref[...], w_ref[...])
    o_ref[...] = y
    if has_extra:
        @pl.when(j >= bf16_from)
        def _():
            ob_ref[...] = y.astype(BF16)


def _norm_proj(x, g, sc, sh, w, w_extra=None, bf16_from=None, tm=1024, tn=512):
    T, D = x.shape
    N = w.shape[1]
    has_extra = w_extra is not None
    vec = pl.BlockSpec((1, D), lambda i, j: (0, 0))
    in_specs = [pl.BlockSpec((tm, D), lambda i, j: (i, 0)), vec, vec, vec,
                pl.BlockSpec((D, tn), lambda i, j: (0, j))]
    out_shape = [jax.ShapeDtypeStruct((T, N), F32)]
    out_specs = [pl.BlockSpec((tm, tn), lambda i, j: (i, j))]
    args = [x, g, sc, sh, w]
    if has_extra:
        nx = w_extra.shape[1]
        in_specs.append(pl.BlockSpec((D, nx), lambda i, j: (0, 0)))
        out_shape.append(jax.ShapeDtypeStruct((T, nx), F32))
        out_specs.append(pl.BlockSpec((tm, nx), lambda i, j: (i, 0)))
        args.append(w_extra)
        out_shape.append(jax.ShapeDtypeStruct((T, N - bf16_from * tn), BF16))
        out_specs.append(pl.BlockSpec((tm, tn), lambda i, j: (i, jnp.maximum(j - bf16_from, 0))))
    return pl.pallas_call(
        functools.partial(_norm_proj_kernel, has_extra=has_extra, bf16_from=bf16_from),
        out_shape=out_shape,
        grid=(T // tm, N // tn),
        in_specs=in_specs,
        out_specs=out_specs,
        scratch_shapes=[pltpu.VMEM((tm, D), BF16)],
        compiler_params=_cparams(("arbitrary", "arbitrary")),
        name="norm_proj",
    )(*args)


def _out_proj_kernel(a1_ref, a2_ref, w1_ref, w2_ref, x_ref, g_ref, o_ref):
    y = _dot(a1_ref[...], w1_ref[...]) + _dot(a2_ref[...], w2_ref[...])
    o_ref[...] = x_ref[...] + g_ref[...] * y


def _out_proj(a1, a2, w1, w2, x, gate, tm=1024, tn=512):
    T, D = x.shape
    K1, K2 = a1.shape[1], a2.shape[1]
    return pl.pallas_call(
        _out_proj_kernel,
        out_shape=jax.ShapeDtypeStruct((T, D), F32),
        grid=(T // tm, D // tn),
        in_specs=[pl.BlockSpec((tm, K1), lambda i, j: (i, 0)),
                  pl.BlockSpec((tm, K2), lambda i, j: (i, 0)),
                  pl.BlockSpec((K1, tn), lambda i, j: (0, j)),
                  pl.BlockSpec((K2, tn), lambda i, j: (0, j)),
                  pl.BlockSpec((tm, tn), lambda i, j: (i, j)),
                  pl.BlockSpec((1, tn), lambda i, j: (0, j))],
        out_specs=pl.BlockSpec((tm, tn), lambda i, j: (i, j)),
        compiler_params=_cparams(("arbitrary", "arbitrary")),
        name="out_proj",
    )(a1, a2, w1, w2, x, gate)


def _compress_kernel(k2_ref, pe_ref, w1_ref, b1_ref, w2_ref, b2_ref, o_ref):
    half = (CMP_LEN // 2) * HEAD_DIM
    k2 = k2_ref[...]
    a = _dot((k2 + pe_ref[0:1, :]).astype(BF16), w1_ref[0:half, :].astype(BF16))
    b = _dot((k2 + pe_ref[1:2, :]).astype(BF16), w1_ref[half:2 * half, :].astype(BF16))
    nb = k2.shape[0]
    b_up = pltpu.roll(b, nb - 1, axis=0)
    hid = jax.nn.gelu(a + b_up + b1_ref[...])
    o_ref[...] = _dot(hid.astype(BF16), w2_ref[...].astype(BF16)) + b2_ref[...]


def _compress(kv, pe, w1, b1, w2, b2):
    _, G, T, HD = kv.shape
    nb = T // CMP_STRIDE
    row = CMP_STRIDE * HD
    kv2 = kv.reshape(2, G, nb, row)
    pe2 = pe.reshape(2, 2, row)
    return pl.pallas_call(
        _compress_kernel,
        out_shape=jax.ShapeDtypeStruct((2, G, nb, HD), F32),
        grid=(2, G),
        in_specs=[pl.BlockSpec((None, None, nb, row), lambda a, g: (a, g, 0, 0)),
                  pl.BlockSpec((None, 2, row), lambda a, g: (a, 0, 0)),
                  pl.BlockSpec((None, CMP_LEN * HD, HD), lambda a, g: (a, 0, 0)),
                  pl.BlockSpec((None, 1, HD), lambda a, g: (a, 0, 0)),
                  pl.BlockSpec((None, HD, HD), lambda a, g: (a, 0, 0)),
                  pl.BlockSpec((None, 1, HD), lambda a, g: (a, 0, 0))],
        out_specs=pl.BlockSpec((None, None, nb, HD), lambda a, g: (a, g, 0, 0)),
        compiler_params=_cparams(("arbitrary", "arbitrary")),
        name="nsa_compress",
    )(kv2, pe2, w1, b1.reshape(2, 1, HD), w2, b2.reshape(2, 1, HD))


def _bias_from_rel(rel, tbl_ref, head):
    far = tbl_ref[N_BUCKETS - 1, head]
    val = jnp.zeros(rel.shape, F32)
    for b in range(N_BUCKETS - 2, -1, -1):
        val = jnp.where(rel < BUCKET_STARTS[b + 1], tbl_ref[b, head] - far, val)
    return val


def _lane_tile4(x):
    return jnp.concatenate([x, x, x, x], axis=1)


def _col_softmax_stats(s):
    m = jnp.maximum(jnp.max(s, axis=0, keepdims=True), M_FLOOR)
    e = jnp.exp(s - m)
    d = jnp.sum(e, axis=0, keepdims=True)
    return e, d


def _nsa_kernel(tbl_ref, q_ref, gl_ref, kc_ref, vc_ref, ks_ref, vs_ref, kw_ref, vw_ref, o_ref,
                bd_ref, bp_ref, bn_ref, imp_ref, *, ncp):
    g = pl.program_id(0)
    ci = pl.program_id(1)
    Q = Q_BLOCK
    HD = HEAD_DIM
    H4 = NSA_HPG

    row_i = lax.broadcasted_iota(jnp.int32, (Q, Q), 0)
    lane_i = lax.broadcasted_iota(jnp.int32, (Q, Q), 1)

    @pl.when((g == 0) & (ci == 0))
    def _build_bias_tiles():
        for h in range(NSA_HEADS):
            rel_d = lane_i - row_i
            bd_ref[h] = jnp.where(rel_d >= 0, _bias_from_rel(jnp.maximum(rel_d, 0), tbl_ref, h), NEG)
            bp_ref[h] = _bias_from_rel(lane_i - row_i + Q, tbl_ref, h)
            rel_n = lane_i - CMP_STRIDE * row_i + (CMP_STRIDE * (Q - 8) - (CMP_LEN - 1))
            bn_ref[h] = jnp.where(rel_n >= 0, _bias_from_rel(jnp.maximum(rel_n, 0), tbl_ref, h), NEG)

    def head_tiles(ref):
        return jnp.concatenate([ref[g * H4 + h] for h in range(H4)], axis=1)

    qt =q_ref[...] * (HD ** -0.5)
    qs = jnp.concatenate([qt[:, h * HD:(h + 1) * HD] for h in range(H4)], axis=0).astype(BF16)

    near0 = 8 * ci - (Q - 8)
    kc_far = kc_ref[pl.ds(Q, ncp), :]
    vc_far = vc_ref[pl.ds(Q, ncp), :]
    near_row = pl.multiple_of(8 * ci + 8, 8)
    kc_near = kc_ref[pl.ds(near_row, Q), :]
    vc_near = vc_ref[pl.ds(near_row, Q), :]
    n_far = lax.broadcasted_iota(jnp.int32, (ncp, 1), 0)
    s_far = jnp.where(n_far < near0, _dot_nt(kc_far.astype(BF16), qs), NEG)
    m_near = lax.broadcasted_iota(jnp.int32, (Q, 1), 0)
    s_near = _dot_nt(kc_near.astype(BF16), qs) + head_tiles(bn_ref)
    s_near = jnp.where(m_near + near0 >= 0, s_near, NEG)
    s_c = jnp.concatenate([s_far, s_near], axis=0)
    e_c, d_c = _col_softmax_stats(s_c)
    p_c = e_c * (1.0 / jnp.where(d_c > 0, d_c, 1.0))
    v_c = jnp.concatenate([vc_far, vc_near], axis=0).astype(BF16)
    o_cmp = _dot_tn(v_c, p_c.astype(BF16))

    psum = p_c[:, 0:Q] + p_c[:, Q:2 * Q] + p_c[:, 2 * Q:3 * Q] + p_c[:, 3 * Q:4 * Q]
    p_hi = psum.astype(BF16)
    p_lo = (psum - p_hi.astype(F32)).astype(BF16)
    jb = lax.broadcasted_iota(jnp.int32, (Q, ncp), 0)
    nb = lax.broadcasted_iota(jnp.int32, (Q, ncp), 1)
    ratio = SLC_LEN // CMP_STRIDE
    span = CMP_LEN // CMP_STRIDE - 1
    cover_far = ((nb >= ratio * jb - span) & (nb <= ratio * jb + ratio - 1)).astype(BF16)
    nn = lane_i + near0
    cover_near = ((nn >= ratio * row_i - span) & (nn <= ratio * row_i + ratio - 1)).astype(BF16)
    cover = jnp.concatenate([cover_far, cover_near], axis=1)
    imp = _dot(cover, p_hi) + _dot(cover, p_lo)
    cur = 2 * ci + (lane_i >= SLC_LEN).astype(jnp.int32)
    imp = jnp.where(row_i == cur, jnp.inf, jnp.where(row_i > cur, -jnp.inf, imp))
    imp_ref[...] = imp

    def rank_body(it, cnt):
        for u in range(RANK_UNROLL):
            b = it * RANK_UNROLL + u
            r = imp_ref[pl.ds(b, 1), :]
            ahead = (r > imp) | ((r == imp) & (b < row_i))
            cnt = cnt + ahead.astype(F32)
        return cnt

    cnt = lax.fori_loop(0, (2 * ci + 2 + RANK_UNROLL - 1) // RANK_UNROLL, rank_body, jnp.zeros((Q, Q), F32))
    sel = ((cnt < SLC_TOPN) & (row_i <= cur)).astype(F32)

    sel_q = jnp.transpose(sel)
    near_blk = lane_i >= 2 * ci - 2
    m_far = jnp.where((sel_q > 0) & jnp.logical_not(near_blk), 0.0, NEG).astype(BF16)
    m_near = jnp.where((sel_q > 0) & near_blk, 0.0, NEG).astype(BF16)
    qa_far = jnp.concatenate([qs, jnp.concatenate([m_far] * H4, axis=0)], axis=1)
    qa_near = jnp.concatenate([qs, jnp.concatenate([m_near] * H4, axis=0)], axis=1)

    def slc_rows(kt0, nt):
        r0 = pl.multiple_of((kt0 + 1) * Q, Q)
        return ks_ref[pl.ds(r0, nt * Q), :], vs_ref[pl.ds(r0, nt * Q), :]

    def online_update(streams):
        m_new = [jnp.maximum(c[0], jnp.max(s, axis=0, keepdims=True)) for s, _, c in streams]
        alpha = [jnp.exp(c[0] - mn) for (_, _, c), mn in zip(streams, m_new)]
        p = [jnp.exp(s - mn) for (s, _, _), mn in zip(streams, m_new)]
        l_new = [a * c[1] + jnp.sum(pp, axis=0, keepdims=True) for a, pp, (_, _, c) in zip(alpha, p, streams)]
        pv = [_dot_tn(v, pp.astype(BF16)) for pp, (_, v, _) in zip(p, streams)]
        return [(mn, ln, a * c[2] + x) for mn, ln, a, x, (_, _, c) in zip(m_new, l_new, alpha, pv, streams)]

    carry0 = (jnp.full((1, H4 * Q), M_FLOOR, F32), jnp.zeros((1, H4 * Q), F32), jnp.zeros((HD, H4 * Q), F32))
    n_far = jnp.maximum(ci - 1, 0)

    def far_body(it, carries):
        kt0 = it * (2 * SLC_GROUP)
        k_a, v_a = slc_rows(kt0, SLC_GROUP)
        k_b, v_b = slc_rows(kt0 + SLC_GROUP, SLC_GROUP)
        s_a = _dot_nt(k_a, qa_far)
        s_b = _dot_nt(k_b, qa_far)
        return tuple(online_update([(s_a, v_a, carries[0]), (s_b, v_b, carries[1])]))

    (m_a, l_a, acc_a), (m_b, l_b, acc_b) = lax.fori_loop(
        0, (n_far + 2 * SLC_GROUP - 1) // (2 * SLC_GROUP), far_body, (carry0, carry0))
    m_ab = jnp.maximum(m_a, m_b)
    w_a = jnp.exp(m_a - m_ab)
    w_b = jnp.exp(m_b - m_ab)
    carry = (m_ab, w_a * l_a + w_b * l_b, w_a * acc_a + w_b * acc_b)
    bp_t = head_tiles(bp_ref)
    bd_t = head_tiles(bd_ref)
    k_n, v_n = slc_rows(ci - 1, 2)
    s_n = _dot_nt(k_n, qa_near) + jnp.concatenate([bp_t, bd_t], axis=0)
    (_, l_s, acc_s), = online_update([(s_n, v_n, carry)])
    o_slc = acc_s * (1.0 / jnp.where(l_s > 0, l_s, 1.0))

    kw = kw_ref[pl.ds(pl.multiple_of(ci * Q, Q), WINDOW + Q), :]
    vw = vw_ref[pl.ds(pl.multiple_of(ci * Q, Q), WINDOW + Q), :]
    s_w = _dot_nt(kw, qs)
    n_wt = WINDOW // Q
    in_window = lane_i < row_i
    pieces = [jnp.concatenate([jnp.where(in_window, s_w[0:Q, h * Q:(h + 1) * Q], NEG) for h in range(H4)], axis=1)]
    for t in range(1, n_wt - 1):
        pieces.append(s_w[t * Q:(t + 1) * Q])
    pieces.append(s_w[(n_wt - 1) * Q:n_wt * Q] + bp_t)
    pieces.append(s_w[n_wt * Q:(n_wt + 1) * Q] + bd_t)
    s_w = jnp.concatenate(pieces, axis=0)
    x_w = lax.broadcasted_iota(jnp.int32, (WINDOW + Q, 1), 0)
    s_w = jnp.where(x_w + ci * Q >= WINDOW, s_w, NEG)
    e_w, d_w = _col_softmax_stats(s_w)
    p_w = e_w * (1.0 / jnp.where(d_w > 0, d_w, 1.0))
    o_win = _dot_tn(vw, p_w.astype(BF16))

    gt = jnp.transpose(jax.nn.sigmoid(gl_ref[...]))
    outs = []
    for h in range(H4):
        base = (g * H4 + h) * 3
        sl = slice(h * Q, (h + 1) * Q)
        gc = gt_row(gt, base)
        gs = gt_row(gt, base + 1)
        gw = gt_row(gt, base + 2)
        o_h = gc * o_cmp[:, sl] + gs * o_slc[:, sl] + gw * o_win[:, sl]
        outs.append(jnp.transpose(o_h))
    o_ref[...] = jnp.concatenate(outs, axis=1).astype(o_ref.dtype)


def gt_row(gt, idx):
    rows = lax.broadcasted_iota(jnp.int32, gt.shape, 0)
    return jnp.sum(jnp.where(rows == idx, gt, 0.0), axis=0, keepdims=True)


def _nsa_attention(rel_bias, proj, proj_gate, kcmp, vcmp, ks, vs, kw, vw):
    T = proj.shape[0]
    G = NSA_KV_HEADS
    Q = Q_BLOCK
    ncp = T // CMP_STRIDE
    nch = T // Q
    assert nch % SLC_GROUP == 0 and T // SLC_LEN <= Q
    kernel = functools.partial(_nsa_kernel, ncp=ncp)
    full = lambda rows, w=HEAD_DIM: pl.BlockSpec((None, rows, w), lambda g, c: (g, 0, 0))
    return pl.pallas_call(
        kernel,
        out_shape=jax.ShapeDtypeStruct((T, NSA_HEADS * HEAD_DIM), BF16),
        grid=(G, nch),
        in_specs=[pl.BlockSpec(memory_space=pltpu.SMEM),
                  pl.BlockSpec((Q, NSA_HPG * HEAD_DIM), lambda g, c: (c, g)),
                  pl.BlockSpec((Q, LANES), lambda g, c: (c, 0)),
                  full(ncp + 2 * Q), full(ncp + 2 * Q),
                  full(T + Q + SLC_TAIL, HEAD_DIM + Q), full(T + Q + SLC_TAIL), full(T + WINDOW), full(T + WINDOW)],
        out_specs=pl.BlockSpec((Q, NSA_HPG * HEAD_DIM), lambda g, c: (c, g)),
        scratch_shapes=[pltpu.VMEM((NSA_HEADS, Q, Q), F32), pltpu.VMEM((NSA_HEADS, Q, Q), F32),
                        pltpu.VMEM((NSA_HEADS, Q, Q), F32), pltpu.VMEM((Q, Q), F32)],
        compiler_params=_cparams(("arbitrary", "arbitrary")),
        name="nsa_attention",
    )(rel_bias, proj, proj_gate, kcmp, vcmp, ks, vs, kw, vw)


def _softplus(z):
    return jnp.maximum(z, 0.0) + jnp.log1p(jnp.exp(-jnp.abs(z)))


def _lru_kernel(x_ref, y_ref, cw_ref, cb_ref, wa_ref, ba_ref, wi_ref, bi_ref, lam_ref, o_ref,
                xbuf, hc, a_s, b_s, h_s, *, tb):
    i = pl.program_id(0)

    @pl.when(i == 0)
    def _():
        xbuf[0:8, :] = jnp.zeros((8, LRU_WIDTH), F32)
        hc[...] = jnp.zeros((1, LRU_WIDTH), F32)

    xbuf[8:8 + tb, :] = x_ref[...]
    xc = cb_ref[...]
    for w in range(CONV_W):
        xc = xc + cw_ref[w:w + 1, :] * xbuf[8 - (CONV_W - 1) + w:8 - (CONV_W - 1) + w + tb, :]
    xbuf[0:8, :] = xbuf[tb:tb + 8, :]

    xcb = xc.astype(BF16)
    ra, ia = [], []
    for n in range(LRU_BLOCKS):
        xg = xcb[:, n * LRU_BW:(n + 1) * LRU_BW]
        ra.append(_dot(xg, wa_ref[n].astype(BF16)))
        ia.append(_dot(xg, wi_ref[n].astype(BF16)))
    r_gate = jax.nn.sigmoid(jnp.concatenate(ra, axis=1) + ba_ref[...])
    i_gate = jax.nn.sigmoid(jnp.concatenate(ia, axis=1) + bi_ref[...])
    log_a = (-LRU_C * r_gate) * _softplus(-lam_ref[...])
    a = jnp.exp(log_a)
    a_s[...] = a
    b_s[...] = jnp.sqrt(-jnp.tanh(log_a) * (a * a + 1.0)) * (i_gate * xc)

    rows = lax.broadcasted_iota(jnp.int32, (8, LRU_WIDTH), 0)

    def tile(k, h):
        r0 = pl.multiple_of(k * 8, 8)
        A = a_s[pl.ds(r0, 8), :]
        B = b_s[pl.ds(r0, 8), :]
        for sh in (1, 2, 4):
            ok = rows >= sh
            A_p = pltpu.roll(A, sh, axis=0)
            B_p = pltpu.roll(B, sh, axis=0)
            B = jnp.where(ok, A * B_p + B, B)
            A = jnp.where(ok, A * A_p, A)
        H = A * h + B
        h_s[pl.ds(r0, 8), :] = H
        return H[7:8, :]

    hc[...] = lax.fori_loop(0, tb // 8, tile, hc[...])
    o_ref[...] = (jax.nn.gelu(y_ref[...]) * h_s[...]).astype(o_ref.dtype)


def _rglru(proj, y_col, x_col, conv_w, conv_b, wa, ba, wi, bi, lam, tb=256):
    T = proj.shape[0]
    W = LRU_WIDTH
    vec = pl.BlockSpec((1, W), lambda i: (0, 0))
    blk = pl.BlockSpec((LRU_BLOCKS, LRU_BW, LRU_BW), lambda i: (0, 0, 0))
    return pl.pallas_call(
        functools.partial(_lru_kernel, tb=tb),
        out_shape=jax.ShapeDtypeStruct((T, W), BF16),
        grid=(T // tb,),
        in_specs=[pl.BlockSpec((tb, W), lambda i: (i, x_col)),
                  pl.BlockSpec((tb, W), lambda i: (i, y_col)),
                  pl.BlockSpec((CONV_W, W), lambda i: (0, 0)), vec, blk, vec, blk, vec, vec],
        out_specs=pl.BlockSpec((tb, W), lambda i: (i, 0)),
        scratch_shapes=[pltpu.VMEM((tb + 8, W), F32), pltpu.VMEM((1, W), F32),
                        pltpu.VMEM((tb, W), F32), pltpu.VMEM((tb, W), F32), pltpu.VMEM((tb, W), F32)],
        compiler_params=_cparams(("arbitrary",)),
        name="rglru",
    )(proj, proj, conv_w, conv_b.reshape(1, W), wa, ba.reshape(1, W), wi, bi.reshape(1, W), lam.reshape(1, W))


def _split3(x):
    hi = x.astype(BF16)
    r = x - hi.astype(F32)
    mid = r.astype(BF16)
    lo = (r - mid.astype(F32)).astype(BF16)
    return hi, mid, lo


def _hgrn_kernel(q_ref, f_ref, v_ref, g_ref, lbl_ref, ng_ref, o_ref, st_ref, oacc, *, tbh, layer):
    t = pl.program_id(1)
    C = HG_CHUNK
    SB = HG_SUB

    NH = HG_HEADS_PER_STEP
    DK, DV = HG_DK, HG_DV

    @pl.when(t == 0)
    def _():
        st_ref[...] = jnp.zeros((NH, DV, DK), F32)

    lg = lbl_ref[...]
    e = jnp.exp(lg - jnp.max(lg, axis=0, keepdims=True))
    sm = e / jnp.sum(e, axis=0, keepdims=True)
    cum = sm[0:1, :]
    for l in range(1, layer + 1):
        cum = cum + sm[l:l + 1, :]
    lb_all = cum - sm[0:1, :]

    tri = (lax.broadcasted_iota(jnp.int32, (C, C), 0) >= lax.broadcasted_iota(jnp.int32, (C, C), 1)).astype(BF16)
    rows8 = lax.broadcasted_iota(jnp.int32, (8, DK), 0)

    def head_chunk(r0, hh):
        cs = slice(hh * DK, (hh + 1) * DK)
        lb = lb_all[:, cs]
        qz = q_ref[pl.ds(r0, C), cs]
        qq = qz * jax.nn.sigmoid(qz)
        f = lb + (1.0 - lb) * jax.nn.sigmoid(f_ref[pl.ds(r0, C), cs])
        logf = jnp.log(jnp.maximum(f, 1e-30))
        kk = 1.0 - f
        vv = v_ref[pl.ds(r0, C), cs]
        l_hi, l_mid, l_lo = _split3(logf)
        bcum = _dot(tri, l_hi) + _dot(tri, l_mid) + _dot(tri, l_lo)
        st = st_ref[hh]
        o_parts = []
        vb = vv.astype(BF16)
        for I in range(C // SB):
            lo_r = I * SB
            for a in range(SB // 8):
                t0 = lo_r + 8 * a
                q8 = qq[t0:t0 + 8]
                b8 = bcum[t0:t0 + 8]
                o8 = jnp.zeros((8, DV), F32)
                for s in range(lo_r, t0 + 8):
                    d = b8 - bcum[s:s + 1]
                    if s > t0:
                        d = jnp.where(rows8 >= s - t0, d, NEG)
                    w = q8 * (kk[s:s + 1] * jnp.exp(d))
                    o8 = o8 + jnp.sum(w, axis=1, keepdims=True) * vv[s:s + 1]
                o_parts.append(o8)
        for I in range(1, C // SB):
            lo_r = I * SB
            ref = bcum[lo_r - 1:lo_r]
            qe = (qq[lo_r:lo_r + SB] * jnp.exp(bcum[lo_r:lo_r + SB] - ref)).astype(BF16)
            ke = (kk[0:lo_r] * jnp.exp(ref - bcum[0:lo_r])).astype(BF16)
            att = _dot_nt(qe, ke)
            o_off = _dot(att.astype(BF16), vb[0:lo_r])
            for a in range(SB // 8):
                o_parts[lo_r // 8 + a] = o_parts[lo_r // 8 + a] + o_off[8 * a:8 * a + 8]
        o_intra = jnp.concatenate(o_parts, axis=0)
        o_inter = _dot_nt((qq * jnp.exp(bcum)).astype(BF16), st.astype(BF16))
        oacc[pl.ds(r0, C), cs] = o_inter + o_intra
        b_last = bcum[C - 1:C]
        kd = (kk * jnp.exp(b_last - bcum)).astype(BF16)
        st_ref[hh] = st * jnp.exp(b_last) + _dot_tn(vb, kd)

    def chunk(c, carry):
        r0 = pl.multiple_of(c * C, C)
        for hh in range(NH):
            head_chunk(r0, hh)
        return carry

    lax.fori_loop(0, tbh // C, chunk, 0)
    gz = g_ref[...]
    gate = (gz * jax.nn.sigmoid(gz)) * ng_ref[...]
    for hh in range(NH):
        cs = slice(hh * DV, (hh + 1) * DV)
        o = oacc[:, cs]
        o = o * lax.rsqrt(jnp.mean(o * o, axis=-1, keepdims=True) + EPS)
        o_ref[:, cs] = (o * gate[:, cs]).astype(o_ref.dtype)


def _hgrn2(proj, lb_logits, norm_g, layer, tbh=512):
    T = proj.shape[0]
    NH = HG_HEADS_PER_STEP
    HB = HG_HEADS // NH
    wk = NH * HG_DK
    col = lambda off: pl.BlockSpec((tbh, wk), lambda h, t: (t, off + h))
    return pl.pallas_call(
        functools.partial(_hgrn_kernel, tbh=tbh, layer=layer),
        out_shape=jax.ShapeDtypeStruct((T, HG_VW), BF16),
        grid=(HB, T // tbh),
        in_specs=[col(0), col(HB), col(2 * HB), col(3 * HB),
                  pl.BlockSpec((DEPTH, wk), lambda h, t: (0, h)),
                  pl.BlockSpec((1, wk), lambda h, t: (0, h))],
        out_specs=pl.BlockSpec((tbh, wk), lambda h, t: (t, h)),
        scratch_shapes=[pltpu.VMEM((NH, HG_DV, HG_DK), F32), pltpu.VMEM((tbh, wk), F32)],
        compiler_params=_cparams(("arbitrary", "arbitrary")),
        name="hgrn2",
    )(proj, proj, proj, proj, lb_logits, norm_g.reshape(1, HG_VW))


def _router_kernel(x_ref, g_ref, sc_ref, sh_ref, w_ref, b_ref, h_ref, eid_ref, gate_ref):
    h = _norm_mod(x_ref[...], g_ref[...], sc_ref[...], sh_ref[...])
    h_ref[...] = h
    h1, h2, h3 = _split3(h)
    w1, w2, w3 = _split3(w_ref[...])
    logits = (_dot(h1, w1) + (_dot(h1, w2) + _dot(h2, w1))
              + (_dot(h2, w2) + _dot(h1, w3) + _dot(h3, w1))) + b_ref[...]
    lane = lax.broadcasted_iota(jnp.int32, logits.shape, 1)
    is_g = lane < N_GROUPS
    glog = jnp.where(is_g, logits, -jnp.inf)
    gmax = jnp.max(glog, axis=-1, keepdims=True)
    grp = jnp.min(jnp.where(glog == gmax, lane, LANES), axis=-1, keepdims=True)
    gsum = jnp.sum(jnp.where(is_g, jnp.exp(glog - gmax), 0.0), axis=-1, keepdims=True)
    p_grp = 1.0 / gsum
    lo = N_GROUPS + EXP_PER_GROUP * grp
    el = jnp.where((lane >= lo) & (lane < lo + EXP_PER_GROUP), logits, -jnp.inf)
    v1 = jnp.max(el, axis=-1, keepdims=True)
    i1 = jnp.min(jnp.where(el == v1, lane, LANES), axis=-1, keepdims=True)
    el2 = jnp.where(lane == i1, -jnp.inf, el)
    v2 = jnp.max(el2, axis=-1, keepdims=True)
    i2 = jnp.min(jnp.where(el2 == v2, lane, LANES), axis=-1, keepdims=True)
    e2 = jnp.exp(v2 - v1)
    den = 1.0 + e2
    g1 = p_grp * (1.0 / den)
    g2 = p_grp * (e2 / den)
    eid_ref[...] = jnp.where(lane == 0, i1 - N_GROUPS, jnp.where(lane == 1, i2 - N_GROUPS, 0))
    gate_ref[...] = jnp.where(lane == 0, g1, jnp.where(lane == 1, g2, 0.0))


def _router(x, g, sc, sh, w_r, b_r, tm=512):
    T, D = x.shape
    vec = pl.BlockSpec((1, D), lambda i: (0, 0))
    return pl.pallas_call(
        _router_kernel,
        out_shape=[jax.ShapeDtypeStruct((T, D), F32), jax.ShapeDtypeStruct((T, LANES), jnp.int32),
                   jax.ShapeDtypeStruct((T, LANES), F32)],
        grid=(T // tm,),
        in_specs=[pl.BlockSpec((tm, D), lambda i: (i, 0)), vec, vec, vec,
                  pl.BlockSpec((D, LANES), lambda i: (0, 0)), pl.BlockSpec((1, LANES), lambda i: (0, 0))],
        out_specs=[pl.BlockSpec((tm, D), lambda i: (i, 0)), pl.BlockSpec((tm, LANES), lambda i: (i, 0)),
                   pl.BlockSpec((tm, LANES), lambda i: (i, 0))],
        compiler_params=_cparams(("arbitrary",)),
        name="moe_router",
    )(x, g, sc, sh, w_r, b_r)


def _cast_rows(src_ref, slot, dst_ref, rows):
    def body(c, carry):
        r = pl.multiple_of(c * rows, rows)
        dst_ref[pl.ds(r, rows), :] = src_ref[slot, pl.ds(r, rows), :].astype(BF16)
        return carry
    lax.fori_loop(0, dst_ref.shape[0] // rows, body, 0, unroll=4)


def _expert_kernel(rid_ref, re_ref, nu_ref, tok_ref, h_ref, wg_hbm, wu_hbm, wd_hbm, y_ref,
                   wgf, wuf, wdf, wgb, wub, wdb, xbuf, wsem, xsem, *, layer):
    i = pl.program_id(0)
    n_used = nu_ref[0]
    R = MOE_BLOCK

    def w_copies(e, slot):
        return ((pltpu.make_async_copy(wg_hbm.at[layer, e], wgf.at[slot], wsem.at[slot, 0]), 1),
                (pltpu.make_async_copy(wu_hbm.at[layer, e], wuf.at[slot], wsem.at[slot, 1]), 1),
                (pltpu.make_async_copy(wd_hbm.at[layer, e], wdf.at[slot], wsem.at[slot, 2]), 1))

    def x_start(blk, slot):
        def body(r, c):
            tok = tok_ref[blk * R + r]
            pltpu.make_async_copy(h_ref.at[pl.ds(tok, 1)], xbuf.at[slot, pl.ds(r, 1)], xsem.at[slot]).start()
            return c
        lax.fori_loop(0, R, body, 0, unroll=8)

    def x_wait(slot):
        pltpu.make_async_copy(h_ref.at[pl.ds(0, R)], xbuf.at[slot], xsem.at[slot]).wait()

    def w_request(run):
        @pl.when(re_ref[run] >= 0)
        def _():
            for c, pri in w_copies(re_ref[run], run % 2):
                c.start(priority=pri)

    @pl.when(i == 0)
    def _():
        w_request(0)
        w_request(1)
        x_start(0, 0)

    @pl.when(i < n_used)
    def _():
        slot = i % 2

        @pl.when(i + 1 < n_used)
        def _():
            x_start(i + 1, 1 - slot)

        run = rid_ref[i]

        @pl.when((i == 0) | (run != rid_ref[jnp.maximum(i - 1, 0)]))
        def _():
            ws = run % 2
            for c, _ in w_copies(re_ref[run], ws):
                c.wait()
            _cast_rows(wgf, ws, wgb, 64)
            _cast_rows(wuf, ws, wub, 64)
            _cast_rows(wdf, ws, wdb, 16)
            w_request(run + 2)

        x_wait(slot)
        x = xbuf[slot].astype(BF16)
        hg = _dot(x, wgb[...])
        hu = _dot(x, wub[...])
        hid = (hg * jax.nn.sigmoid(hg)) * hu
        y_ref[...] = _dot(hid.astype(BF16), wdb[...])

    @pl.when(i >= n_used)
    def _():
        y_ref[...] = jnp.zeros(y_ref.shape, y_ref.dtype)


def _experts(h, tok, run_id, run_e, n_used, w_gate, w_up, w_down, layer):
    N, D = h.shape
    n_pad = tok.shape[0]
    n_blk = n_pad // MOE_BLOCK
    any_spec = pl.BlockSpec(memory_space=pl.ANY)
    return pl.pallas_call(
        functools.partial(_expert_kernel, layer=layer),
        out_shape=jax.ShapeDtypeStruct((n_pad, D), F32),
        grid_spec=pltpu.PrefetchScalarGridSpec(
            num_scalar_prefetch=4,
            grid=(n_blk,),
            in_specs=[any_spec, any_spec, any_spec, any_spec],
            out_specs=pl.BlockSpec((MOE_BLOCK, D), lambda i, *_: (i, 0)),
            scratch_shapes=[pltpu.VMEM((2, D, D_EXPERT), F32), pltpu.VMEM((2, D, D_EXPERT), F32),
                            pltpu.VMEM((2, D_EXPERT, D), F32),
                            pltpu.VMEM((D, D_EXPERT), BF16), pltpu.VMEM((D, D_EXPERT), BF16),
                            pltpu.VMEM((D_EXPERT, D), BF16),
                            pltpu.VMEM((2, MOE_BLOCK, D), F32),
                            pltpu.SemaphoreType.DMA((2, 4)), pltpu.SemaphoreType.DMA((2,))]),
        compiler_params=_cparams(("arbitrary",)),
        name="moe_experts",
    )(run_id, run_e, n_used, tok, h, w_gate, w_up, w_down)


def _combine_kernel(slot_ref, y_ref, x_ref, w_ref, g_ref, fg_ref, o_ref, buf, sem, *, rows, final_norm):
    i = pl.program_id(0)

    def copy(r, k):
        return pltpu.make_async_copy(y_ref.at[pl.ds(slot_ref[(i * rows + r) * 2 + k], 1)],
                                     buf.at[k, pl.ds(r, 1)], sem)

    def start(r, c):
        copy(r, 0).start()
        copy(r, 1).start()
        return c

    lax.fori_loop(0, rows, start, 0, unroll=8)
    for k in range(TOPK_IN_GROUP):
        pltpu.make_async_copy(y_ref.at[pl.ds(0, rows)], buf.at[k], sem).wait()
    w = w_ref[...]
    moe = buf[0] * w[:, 0:1] + buf[1] * w[:, 1:2]
    y = x_ref[...] + g_ref[...] * moe
    if final_norm:
        y = (y * lax.rsqrt(jnp.mean(y * y, axis=-1, keepdims=True) + EPS)) * fg_ref[...]
    o_ref[...] = y


def _combine(yb, slots, x, gate_w, g2, final_g, final_norm, rows=256):
    T, D = x.shape
    return pl.pallas_call(
        functools.partial(_combine_kernel, rows=rows, final_norm=final_norm),
        out_shape=jax.ShapeDtypeStruct((T, D), F32),
        grid_spec=pltpu.PrefetchScalarGridSpec(
            num_scalar_prefetch=1,
            grid=(T // rows,),
            in_specs=[pl.BlockSpec(memory_space=pl.ANY),
                      pl.BlockSpec((rows, D), lambda i, s: (i, 0)),
                      pl.BlockSpec((rows, LANES), lambda i, s: (i, 0)),
                      pl.BlockSpec((1, D), lambda i, s: (0, 0)),
                      pl.BlockSpec((1, D), lambda i, s: (0, 0))],
            out_specs=pl.BlockSpec((rows, D), lambda i, s: (i, 0)),
            scratch_shapes=[pltpu.VMEM((2, rows, D), F32), pltpu.SemaphoreType.DMA(())]),
        compiler_params=_cparams(("arbitrary",)),
        name="moe_combine",
    )(slots.reshape(-1), yb, x, gate_w, g2, final_g)


def _hier_moe_residual(x, g, sc, sh, g2, w_grp, b_grp, w_exp, b_exp, w_gate, w_up, w_down, layer, final_g,
                       final_norm):
    N, D = x.shape
    n_route = N_GROUPS + N_EXPERTS
    w_r = jnp.zeros((D, LANES), F32).at[:, :N_GROUPS].set(w_grp).at[:, N_GROUPS:n_route].set(w_exp)
    b_r = jnp.zeros((1, LANES), F32).at[0, :N_GROUPS].set(b_grp).at[0, N_GROUPS:n_route].set(b_exp)
    h, eid_l, gate_l = _router(x, g, sc, sh, w_r, b_r)
    eid = eid_l[:, :TOPK_IN_GROUP]

    A = N * TOPK_IN_GROUP
    e_flat = eid.reshape(-1)
    tok = jnp.repeat(jnp.arange(N, dtype=jnp.int32), TOPK_IN_GROUP)
    order = jnp.argsort(e_flat)
    e_s, tok_s = e_flat[order], tok[order]
    counts = jnp.zeros((N_EXPERTS,), jnp.int32).at[e_flat].add(1)
    start = jnp.cumsum(counts) - counts
    padded = (counts + MOE_BLOCK - 1) // MOE_BLOCK * MOE_BLOCK
    pad_end = jnp.cumsum(padded)
    pad_start = pad_end - padded
    dest = pad_start[e_s] + jnp.arange(A, dtype=jnp.int32) - start[e_s]
    n_blk = -(-(A + N_EXPERTS * MOE_BLOCK) // MOE_BLOCK)
    n_pad = n_blk * MOE_BLOCK
    buf_tok = jnp.zeros((n_pad,), jnp.int32).at[dest].set(tok_s)
    blk_idx = jnp.arange(n_blk, dtype=jnp.int32)
    n_used = pad_end[-1] // MOE_BLOCK
    blk_e = jnp.minimum(jnp.searchsorted(pad_end, blk_idx * MOE_BLOCK, side='right'), N_EXPERTS - 1).astype(jnp.int32)
    blk_e = jnp.where(blk_idx < n_used, blk_e, blk_e[n_used - 1])
    change = jnp.concatenate([jnp.ones((1,), bool), blk_e[1:] != blk_e[:-1]])
    run_id = (jnp.cumsum(change.astype(jnp.int32)) - 1).astype(jnp.int32)
    runs = jnp.arange(n_blk + 2, dtype=jnp.int32)
    run_e = jnp.max(jnp.where(run_id[None, :] == runs[:, None], blk_e[None, :], -1), axis=1).astype(jnp.int32)
    slots = jnp.zeros((A,), jnp.int32).at[order].set(dest).reshape(N, TOPK_IN_GROUP)

    yb = _experts(h, buf_tok, run_id, run_e, n_used.reshape(1).astype(jnp.int32), w_gate, w_up, w_down, layer)
    return _combine(yb, slots, x, gate_l, g2, final_g, final_norm)


def _even_mixer_residual(x, g, sc, sh, g1, rel_bias, w_in, w_out, cmp_pe, cmp_w1, cmp_b1, cmp_w2, cmp_b2,
                         conv_w, conv_b, lru_wa, lru_ba, lru_wi, lru_bi, lru_lambda):
    T, D = x.shape
    G, HD = NSA_KV_HEADS, HEAD_DIM
    nq = NSA_HEADS * HD
    n_kv = 6 * NSA_KV_W
    n_gate = 3 * NSA_HEADS
    w_main = jnp.concatenate([w_in[:, :nq], w_in[:, nq + n_kv + n_gate:], w_in[:, nq:nq + n_kv]],
                             axis=1).astype(BF16)
    w_gl = jnp.zeros((D, LANES), F32).at[:, :n_gate].set(w_in[:, nq + n_kv:nq + n_kv + n_gate]).astype(BF16)
    kv0 = nq + 2 * LRU_WIDTH
    tn = 512
    proj, proj_gate, kv_b = _norm_proj(x, g, sc, sh, w_main, w_gl, bf16_from=kv0 // tn, tn=tn)

    def kv_heads(j, src=kv_b, c_base=0):
        c0 = c_base + j * NSA_KV_W
        return src[:, c0:c0 + NSA_KV_W].reshape(T, G, HD).transpose(1, 0, 2)

    cmp_in = jnp.stack([kv_heads(0, proj, kv0), kv_heads(1, proj, kv0)], axis=0)
    cmp_out = _compress(cmp_in, cmp_pe, cmp_w1, cmp_b1, cmp_w2, cmp_b2)
    ncp = T // CMP_STRIDE
    valid = (jnp.arange(ncp) < ncp - 1)[None, None, :, None]
    cmp_pad = jnp.pad(jnp.where(valid, cmp_out, 0.0), ((0, 0), (0, 0), (Q_BLOCK, Q_BLOCK), (0, 0)))
    slc_pad = ((0, 0), (Q_BLOCK, SLC_TAIL), (0, 0))
    blk_of_row = jnp.pad(jnp.arange(T, dtype=jnp.int32) // SLC_LEN, (Q_BLOCK, SLC_TAIL), constant_values=Q_BLOCK - 1)
    blk_onehot = (blk_of_row[:, None] == jnp.arange(Q_BLOCK, dtype=jnp.int32)[None, :]).astype(BF16)
    ks = jnp.concatenate([jnp.pad(kv_heads(2), slc_pad), jnp.broadcast_to(blk_onehot, (G,) + blk_onehot.shape)], axis=2)
    vs = jnp.pad(kv_heads(3), slc_pad)
    kw = jnp.pad(kv_heads(4), ((0, 0), (WINDOW, 0), (0, 0)))
    vw = jnp.pad(kv_heads(5), ((0, 0), (WINDOW, 0), (0, 0)))
    nsa_out = _nsa_attention(rel_bias, proj, proj_gate, cmp_pad[0], cmp_pad[1], ks, vs, kw, vw)

    y_col = nq // LRU_WIDTH
    lru_out = _rglru(proj, y_col, y_col + 1, conv_w, conv_b, lru_wa, lru_ba, lru_wi, lru_bi, lru_lambda)
    w_o = w_out.astype(BF16)
    return _out_proj(nsa_out, lru_out, w_o[:nq], w_o[nq:], x, g1)


def _odd_mixer_residual(x, g, sc, sh, g1, lb_logits, w_in, w_out, norm_g, layer):
    proj = _norm_proj(x, g, sc, sh, w_in.astype(BF16))[0]
    o = _hgrn2(proj, lb_logits, norm_g, layer)
    w_o = w_out.astype(BF16)
    half = HG_VW // 2
    return _out_proj(o[:, :half], o[:, half:], w_o[:half], w_o[half:], x, g1)


def kernel(x, c, rel_bias, ada_w, ada_b, norm_mix_g, norm_ffn_g, ev_w_in, ev_w_out, cmp_pe, cmp_w1, cmp_b1, cmp_w2, cmp_b2, lru_conv_w, lru_conv_b, lru_wa, lru_ba, lru_wi, lru_bi, lru_lambda, od_w_in, od_w_out, hg_lb_logits, hg_norm_g, moe_w_grp, moe_b_grp, moe_w_exp, moe_b_exp, moe_w_gate, moe_w_up, moe_w_down, final_g):
    B, T, D = x.shape
    assert B == 1 and D == D_MODEL
    xt = x.reshape(T, D)
    mod = _modulation(c, ada_w, ada_b)
    for l in range(DEPTH):
        sh1, sc1, g1, sh2, sc2, g2 = [mod[l, :, k * D:(k + 1) * D] for k in range(6)]
        gm = norm_mix_g[l].reshape(1, D)
        gf = norm_ffn_g[l].reshape(1, D)
        j = l // 2
        if l % 2 == 0:
            xt = _even_mixer_residual(xt, gm, sc1, sh1, g1, rel_bias, ev_w_in[j], ev_w_out[j], cmp_pe[j],
                                      cmp_w1[j], cmp_b1[j], cmp_w2[j], cmp_b2[j], lru_conv_w[j],
                                      lru_conv_b[j], lru_wa[j], lru_ba[j], lru_wi[j], lru_bi[j], lru_lambda[j])
        else:
            xt = _odd_mixer_residual(xt, gm, sc1, sh1, g1, hg_lb_logits, od_w_in[j], od_w_out[j],
                                     hg_norm_g[j], l)
        xt = _hier_moe_residual(xt, gf, sc2, sh2, g2, moe_w_grp[l], moe_b_grp[l], moe_w_exp[l], moe_b_exp[l],
                                moe_w_gate, moe_w_up, moe_w_down, l, final_g.reshape(1, D), l == DEPTH - 1)
    return xt.reshape(B, T, D)
```

```python
import functools
import math

import numpy as np
import jax
import jax.numpy as jnp
from jax import lax
from jax.experimental import pallas as pl
from jax.experimental.pallas import tpu as pltpu

F32 = jnp.float32
BF16 = jnp.bfloat16

D_MODEL = 2048
DEPTH = 2
NSA_HEADS = 8
NSA_KV_HEADS = 2
NSA_HPG = NSA_HEADS // NSA_KV_HEADS
HEAD_DIM = 128
NSA_KV_W = NSA_KV_HEADS * HEAD_DIM
CMP_LEN = 32
CMP_STRIDE = 16
SLC_LEN = 64
SLC_TOPN = 16
WINDOW = 512
Q_BLOCK = 128
LRU_WIDTH = 1024
LRU_BLOCKS = 8
LRU_BW = LRU_WIDTH // LRU_BLOCKS
CONV_W = 4
LRU_C = 8.0
HG_HEADS = 16
HG_DK = 128
HG_DV = 128
HG_CHUNK = 64
HG_SUB = 16
HG_HEADS_PER_STEP = 4
SLC_GROUP = 4
RANK_UNROLL = 4
SLC_TAIL = 2 * SLC_GROUP * Q_BLOCK
HG_KW = HG_HEADS * HG_DK
HG_VW = HG_HEADS * HG_DV
N_BUCKETS = 32
MAX_DIST = 128
N_GROUPS = 8
EXP_PER_GROUP = 8
N_EXPERTS = N_GROUPS * EXP_PER_GROUP
TOPK_IN_GROUP = 2
D_EXPERT = 512
MOE_BLOCK = 128
EPS = 1e-6

LANES = 128
PACKED_ROWS = D_MODEL // 2 // LANES
NEG = -1e30
M_FLOOR = -1e20
VMEM_LIMIT = 56 * 1024 * 1024


def _cparams(sem):
    return pltpu.CompilerParams(dimension_semantics=sem, vmem_limit_bytes=VMEM_LIMIT)


def _dot(a, b):
    return jnp.dot(a, b, preferred_element_type=F32)


def _dot_nt(a, b):
    return lax.dot_general(a, b, (((1,), (1,)), ((), ())), preferred_element_type=F32)


def _dot_tn(a, b):
    return lax.dot_general(a, b, (((0,), (0,)), ((), ())), preferred_element_type=F32)


def _bucket_starts():
    n = np.arange(0, MAX_DIST + 1, dtype=np.int32)
    max_exact = N_BUCKETS // 2
    nf = np.maximum(n, 1).astype(np.float32)
    large = max_exact + (np.log(nf / np.float32(max_exact)) / np.float32(math.log(MAX_DIST / max_exact))
                         * np.float32(N_BUCKETS - max_exact)).astype(np.int32)
    large = np.minimum(large, N_BUCKETS - 1)
    b = np.where(n < max_exact, n, large)
    starts = [int(np.argmax(b >= k)) for k in range(N_BUCKETS)]
    assert all(b[s] == k for k, s in enumerate(starts)) and b[-1] == N_BUCKETS - 1
    return starts


BUCKET_STARTS = _bucket_starts()


def _mod_kernel(c_ref, w_ref, b_ref, o_ref):
    c = c_ref[...]
    cact = c * jax.nn.sigmoid(c)
    o_ref[...] = jnp.sum(cact * w_ref[...], axis=0, keepdims=True) + b_ref[...]


def _modulation(c, ada_w, ada_b):
    D = D_MODEL
    tn = 1024
    n_out = 6 * D
    c_col = c.reshape(D, 1)
    return pl.pallas_call(
        _mod_kernel,
        out_shape=jax.ShapeDtypeStruct((DEPTH, 1, n_out), F32),
        grid=(DEPTH, n_out // tn),
        in_specs=[pl.BlockSpec((D, 1), lambda l, j: (0, 0)),
                  pl.BlockSpec((None, D, tn), lambda l, j: (l, 0, j)),
                  pl.BlockSpec((None, 1, tn), lambda l, j: (l, 0, j))],
        out_specs=pl.BlockSpec((None, 1, tn), lambda l, j: (l, 0, j)),
        compiler_params=_cparams(("arbitrary", "arbitrary")),
        name="adaln_mod",
    )(c_col, ada_w, ada_b.reshape(DEPTH, 1, n_out))


def _norm_mod(x, g, sc, sh):
    y = x * lax.rsqrt(jnp.mean(x * x, axis=-1, keepdims=True) + EPS)
    return (y * g) * (1.0 + sc) + sh


def _norm_proj_kernel(x_ref, g_ref, sc_ref, sh_ref, w_ref, *rest, has_extra, bf16_from):
    if has_extra:
        wx_ref, o_ref, ox_ref, ob_ref, h_ref = rest
    else:
        o_ref, h_ref = rest
    j = pl.program_id(1)

    @pl.when(j == 0)
    def _():
        h = _norm_mod(x_ref[...], g_ref[...], sc_ref[...], sh_ref[...]).astype(BF16)
        h_ref[...] = h
        if has_extra:
            ox_ref[...] = _dot(h, wx_ref[...])

    y = _dot(h_ref[...], w_ref[...])
    o_ref[...] = y
    if has_extra:
        @pl.when(j >= bf16_from)
        def _():
            ob_ref[...] = y.astype(BF16)


def _norm_proj(x, g, sc, sh, w, w_extra=None, bf16_from=None, tm=1024, tn=512):
    T, D = x.shape
    N = w.shape[1]
    has_extra = w_extra is not None
    vec = pl.BlockSpec((1, D), lambda i, j: (0, 0))
    in_specs = [pl.BlockSpec((tm, D), lambda i, j: (i, 0)), vec, vec, vec,
                pl.BlockSpec((D, tn), lambda i, j: (0, j))]
    out_shape = [jax.ShapeDtypeStruct((T, N), F32)]
    out_specs = [pl.BlockSpec((tm, tn), lambda i, j: (i, j))]
    args = [x, g, sc, sh, w]
    if has_extra:
        nx = w_extra.shape[1]
        in_specs.append(pl.BlockSpec((D, nx), lambda i, j: (0, 0)))
        out_shape.append(jax.ShapeDtypeStruct((T, nx), F32))
        out_specs.append(pl.BlockSpec((tm, nx), lambda i, j: (i, 0)))
        args.append(w_extra)
        out_shape.append(jax.ShapeDtypeStruct((T, N - bf16_from * tn), BF16))
        out_specs.append(pl.BlockSpec((tm, tn), lambda i, j: (i, jnp.maximum(j - bf16_from, 0))))
    return pl.pallas_call(
        functools.partial(_norm_proj_kernel, has_extra=has_extra, bf16_from=bf16_from),
        out_shape=out_shape,
        grid=(T // tm, N // tn),
        in_specs=in_specs,
        out_specs=out_specs,
        scratch_shapes=[pltpu.VMEM((tm, D), BF16)],
        compiler_params=_cparams(("arbitrary", "arbitrary")),
        name="norm_proj",
    )(*args)


def _out_proj_kernel(a1_ref, a2_ref, w1_ref, w2_ref, x_ref, g_ref, o_ref):
    y = _dot(a1_ref[...], w1_ref[...]) + _dot(a2_ref[...], w2_ref[...])
    o_ref[...] = x_ref[...] + g_ref[...] * y


def _out_proj(a1, a2, w1, w2, x, gate, tm=1024, tn=512):
    T, D = x.shape
    K1, K2 = a1.shape[1], a2.shape[1]
    return pl.pallas_call(
        _out_proj_kernel,
        out_shape=jax.ShapeDtypeStruct((T, D), F32),
        grid=(T // tm, D // tn),
        in_specs=[pl.BlockSpec((tm, K1), lambda i, j: (i, 0)),
                  pl.BlockSpec((tm, K2), lambda i, j: (i, 0)),
                  pl.BlockSpec((K1, tn), lambda i, j: (0, j)),
                  pl.BlockSpec((K2, tn), lambda i, j: (0, j)),
                  pl.BlockSpec((tm, tn), lambda i, j: (i, j)),
                  pl.BlockSpec((1, tn), lambda i, j: (0, j))],
        out_specs=pl.BlockSpec((tm, tn), lambda i, j: (i, j)),
        compiler_params=_cparams(("arbitrary", "arbitrary")),
        name="out_proj",
    )(a1, a2, w1, w2, x, gate)


def _compress_kernel(k2_ref, pe_ref, w1_ref, b1_ref, w2_ref, b2_ref, o_ref):
    half = (CMP_LEN // 2) * HEAD_DIM
    k2 = k2_ref[...]
    a = _dot((k2 + pe_ref[0:1, :]).astype(BF16), w1_ref[0:half, :].astype(BF16))
    b = _dot((k2 + pe_ref[1:2, :]).astype(BF16), w1_ref[half:2 * half, :].astype(BF16))
    nb = k2.shape[0]
    b_up = pltpu.roll(b, nb - 1, axis=0)
    hid = jax.nn.gelu(a + b_up + b1_ref[...])
    o_ref[...] = _dot(hid.astype(BF16), w2_ref[...].astype(BF16)) + b2_ref[...]


def _compress(kv, pe, w1, b1, w2, b2):
    _, G, T, HD = kv.shape
    nb = T // CMP_STRIDE
    row = CMP_STRIDE * HD
    kv2 = kv.reshape(2, G, nb, row)
    pe2 = pe.reshape(2, 2, row)
    return pl.pallas_call(
        _compress_kernel,
        out_shape=jax.ShapeDtypeStruct((2, G, nb, HD), F32),
        grid=(2, G),
        in_specs=[pl.BlockSpec((None, None, nb, row), lambda a, g: (a, g, 0, 0)),
                  pl.BlockSpec((None, 2, row), lambda a, g: (a, 0, 0)),
                  pl.BlockSpec((None, CMP_LEN * HD, HD), lambda a, g: (a, 0, 0)),
                  pl.BlockSpec((None, 1, HD), lambda a, g: (a, 0, 0)),
                  pl.BlockSpec((None, HD, HD), lambda a, g: (a, 0, 0)),
                  pl.BlockSpec((None, 1, HD), lambda a, g: (a, 0, 0))],
        out_specs=pl.BlockSpec((None, None, nb, HD), lambda a, g: (a, g, 0, 0)),
        compiler_params=_cparams(("arbitrary", "arbitrary")),
        name="nsa_compress",
    )(kv2, pe2, w1, b1.reshape(2, 1, HD), w2, b2.reshape(2, 1, HD))


def _bias_from_rel(rel, tbl_ref, head):
    far = tbl_ref[N_BUCKETS - 1, head]
    val = jnp.zeros(rel.shape, F32)
    for b in range(N_BUCKETS - 2, -1, -1):
        val = jnp.where(rel < BUCKET_STARTS[b + 1], tbl_ref[b, head] - far, val)
    return val


def _lane_tile4(x):
    return jnp.concatenate([x, x, x, x], axis=1)


def _col_softmax_stats(s):
    m = jnp.maximum(jnp.max(s, axis=0, keepdims=True), M_FLOOR)
    e = jnp.exp(s - m)
    d = jnp.sum(e, axis=0, keepdims=True)
    return e, d


def _nsa_kernel(tbl_ref, q_ref, gl_ref, kc_ref, vc_ref, ks_ref, vs_ref, kw_ref, vw_ref, o_ref,
                bd_ref, bp_ref, bn_ref, imp_ref, *, ncp):
    g = pl.program_id(0)
    ci = pl.program_id(1)
    Q = Q_BLOCK
    HD = HEAD_DIM
    H4 = NSA_HPG

    row_i = lax.broadcasted_iota(jnp.int32, (Q, Q), 0)
    lane_i = lax.broadcasted_iota(jnp.int32, (Q, Q), 1)

    @pl.when((g == 0) & (ci == 0))
    def _build_bias_tiles():
        for h in range(NSA_HEADS):
            rel_d = lane_i - row_i
            bd_ref[h] = jnp.where(rel_d >= 0, _bias_from_rel(jnp.maximum(rel_d, 0), tbl_ref, h), NEG)
            bp_ref[h] = _bias_from_rel(lane_i - row_i + Q, tbl_ref, h)
            rel_n = lane_i - CMP_STRIDE * row_i + (CMP_STRIDE * (Q - 8) - (CMP_LEN - 1))
            bn_ref[h] = jnp.where(rel_n >= 0, _bias_from_rel(jnp.maximum(rel_n, 0), tbl_ref, h), NEG)

    def head_tiles(ref):
        return jnp.concatenate([ref[g * H4 + h] for h in range(H4)], axis=1)

    qt =q_ref[...] * (HD ** -0.5)
    qs = jnp.concatenate([qt[:, h * HD:(h + 1) * HD] for h in range(H4)], axis=0).astype(BF16)

    near0 = 8 * ci - (Q - 8)
    kc_far = kc_ref[pl.ds(Q, ncp), :]
    vc_far = vc_ref[pl.ds(Q, ncp), :]
    near_row = pl.multiple_of(8 * ci + 8, 8)
    kc_near = kc_ref[pl.ds(near_row, Q), :]
    vc_near = vc_ref[pl.ds(near_row, Q), :]
    n_far = lax.broadcasted_iota(jnp.int32, (ncp, 1), 0)
    s_far = jnp.where(n_far < near0, _dot_nt(kc_far.astype(BF16), qs), NEG)
    m_near = lax.broadcasted_iota(jnp.int32, (Q, 1), 0)
    s_near = _dot_nt(kc_near.astype(BF16), qs) + head_tiles(bn_ref)
    s_near = jnp.where(m_near + near0 >= 0, s_near, NEG)
    s_c = jnp.concatenate([s_far, s_near], axis=0)
    e_c, d_c = _col_softmax_stats(s_c)
    p_c = e_c * (1.0 / jnp.where(d_c > 0, d_c, 1.0))
    v_c = jnp.concatenate([vc_far, vc_near], axis=0).astype(BF16)
    o_cmp = _dot_tn(v_c, p_c.astype(BF16))

    psum = p_c[:, 0:Q] + p_c[:, Q:2 * Q] + p_c[:, 2 * Q:3 * Q] + p_c[:, 3 * Q:4 * Q]
    p_hi = psum.astype(BF16)
    p_lo = (psum - p_hi.astype(F32)).astype(BF16)
    jb = lax.broadcasted_iota(jnp.int32, (Q, ncp), 0)
    nb = lax.broadcasted_iota(jnp.int32, (Q, ncp), 1)
    ratio = SLC_LEN // CMP_STRIDE
    span = CMP_LEN // CMP_STRIDE - 1
    cover_far = ((nb >= ratio * jb - span) & (nb <= ratio * jb + ratio - 1)).astype(BF16)
    nn = lane_i + near0
    cover_near = ((nn >= ratio * row_i - span) & (nn <= ratio * row_i + ratio - 1)).astype(BF16)
    cover = jnp.concatenate([cover_far, cover_near], axis=1)
    imp = _dot(cover, p_hi) + _dot(cover, p_lo)
    cur = 2 * ci + (lane_i >= SLC_LEN).astype(jnp.int32)
    imp = jnp.where(row_i == cur, jnp.inf, jnp.where(row_i > cur, -jnp.inf, imp))
    imp_ref[...] = imp

    def rank_body(it, cnt):
        for u in range(RANK_UNROLL):
            b = it * RANK_UNROLL + u
            r = imp_ref[pl.ds(b, 1), :]
            ahead = (r > imp) | ((r == imp) & (b < row_i))
            cnt = cnt + ahead.astype(F32)
        return cnt

    cnt = lax.fori_loop(0, (2 * ci + 2 + RANK_UNROLL - 1) // RANK_UNROLL, rank_body, jnp.zeros((Q, Q), F32))
    sel = ((cnt < SLC_TOPN) & (row_i <= cur)).astype(F32)

    sel_q = jnp.transpose(sel)
    near_blk = lane_i >= 2 * ci - 2
    m_far = jnp.where((sel_q > 0) & jnp.logical_not(near_blk), 0.0, NEG).astype(BF16)
    m_near = jnp.where((sel_q > 0) & near_blk, 0.0, NEG).astype(BF16)
    qa_far = jnp.concatenate([qs, jnp.concatenate([m_far] * H4, axis=0)], axis=1)
    qa_near = jnp.concatenate([qs, jnp.concatenate([m_near] * H4, axis=0)], axis=1)

    def slc_rows(kt0, nt):
        r0 = pl.multiple_of((kt0 + 1) * Q, Q)
        return ks_ref[pl.ds(r0, nt * Q), :], vs_ref[pl.ds(r0, nt * Q), :]

    def online_update(streams):
        m_new = [jnp.maximum(c[0], jnp.max(s, axis=0, keepdims=True)) for s, _, c in streams]
        alpha = [jnp.exp(c[0] - mn) for (_, _, c), mn in zip(streams, m_new)]
        p = [jnp.exp(s - mn) for (s, _, _), mn in zip(streams, m_new)]
        l_new = [a * c[1] + jnp.sum(pp, axis=0, keepdims=True) for a, pp, (_, _, c) in zip(alpha, p, streams)]
        pv = [_dot_tn(v, pp.astype(BF16)) for pp, (_, v, _) in zip(p, streams)]
        return [(mn, ln, a * c[2] + x) for mn, ln, a, x, (_, _, c) in zip(m_new, l_new, alpha, pv, streams)]

    carry0 = (jnp.full((1, H4 * Q), M_FLOOR, F32), jnp.zeros((1, H4 * Q), F32), jnp.zeros((HD, H4 * Q), F32))
    n_far = jnp.maximum(ci - 1, 0)

    def far_body(it, carries):
        kt0 = it * (2 * SLC_GROUP)
        k_a, v_a = slc_rows(kt0, SLC_GROUP)
        k_b, v_b = slc_rows(kt0 + SLC_GROUP, SLC_GROUP)
        s_a = _dot_nt(k_a, qa_far)
        s_b = _dot_nt(k_b, qa_far)
        return tuple(online_update([(s_a, v_a, carries[0]), (s_b, v_b, carries[1])]))

    (m_a, l_a, acc_a), (m_b, l_b, acc_b) = lax.fori_loop(
        0, (n_far + 2 * SLC_GROUP - 1) // (2 * SLC_GROUP), far_body, (carry0, carry0))
    m_ab = jnp.maximum(m_a, m_b)
    w_a = jnp.exp(m_a - m_ab)
    w_b = jnp.exp(m_b - m_ab)
    carry = (m_ab, w_a * l_a + w_b * l_b, w_a * acc_a + w_b * acc_b)
    bp_t = head_tiles(bp_ref)
    bd_t = head_tiles(bd_ref)
    k_n, v_n = slc_rows(ci - 1, 2)
    s_n = _dot_nt(k_n, qa_near) + jnp.concatenate([bp_t, bd_t], axis=0)
    (_, l_s, acc_s), = online_update([(s_n, v_n, carry)])
    o_slc = acc_s * (1.0 / jnp.where(l_s > 0, l_s, 1.0))

    kw = kw_ref[pl.ds(pl.multiple_of(ci * Q, Q), WINDOW + Q), :]
    vw = vw_ref[pl.ds(pl.multiple_of(ci * Q, Q), WINDOW + Q), :]
    s_w = _dot_nt(kw, qs)
    n_wt = WINDOW // Q
    in_window = lane_i < row_i
    pieces = [jnp.concatenate([jnp.where(in_window, s_w[0:Q, h * Q:(h + 1) * Q], NEG) for h in range(H4)], axis=1)]
    for t in range(1, n_wt - 1):
        pieces.append(s_w[t * Q:(t + 1) * Q])
    pieces.append(s_w[(n_wt - 1) * Q:n_wt * Q] + bp_t)
    pieces.append(s_w[n_wt * Q:(n_wt + 1) * Q] + bd_t)
    s_w = jnp.concatenate(pieces, axis=0)
    x_w = lax.broadcasted_iota(jnp.int32, (WINDOW + Q, 1), 0)
    s_w = jnp.where(x_w + ci * Q >= WINDOW, s_w, NEG)
    e_w, d_w = _col_softmax_stats(s_w)
    p_w = e_w * (1.0 / jnp.where(d_w > 0, d_w, 1.0))
    o_win = _dot_tn(vw, p_w.astype(BF16))

    gt = jnp.transpose(jax.nn.sigmoid(gl_ref[...]))
    outs = []
    for h in range(H4):
        base = (g * H4 + h) * 3
        sl = slice(h * Q, (h + 1) * Q)
        gc = gt_row(gt, base)
        gs = gt_row(gt, base + 1)
        gw = gt_row(gt, base + 2)
        o_h = gc * o_cmp[:, sl] + gs * o_slc[:, sl] + gw * o_win[:, sl]
        outs.append(jnp.transpose(o_h))
    o_ref[...] = jnp.concatenate(outs, axis=1).astype(o_ref.dtype)


def gt_row(gt, idx):
    rows = lax.broadcasted_iota(jnp.int32, gt.shape, 0)
    return jnp.sum(jnp.where(rows == idx, gt, 0.0), axis=0, keepdims=True)


def _nsa_attention(rel_bias, proj, proj_gate, kcmp, vcmp, ks, vs, kw, vw):
    T = proj.shape[0]
    G = NSA_KV_HEADS
    Q = Q_BLOCK
    ncp = T // CMP_STRIDE
    nch = T // Q
    assert nch % SLC_GROUP == 0 and T // SLC_LEN <= Q
    kernel = functools.partial(_nsa_kernel, ncp=ncp)
    full = lambda rows, w=HEAD_DIM: pl.BlockSpec((None, rows, w), lambda g, c: (g, 0, 0))
    return pl.pallas_call(
        kernel,
        out_shape=jax.ShapeDtypeStruct((T, NSA_HEADS * HEAD_DIM), BF16),
        grid=(G, nch),
        in_specs=[pl.BlockSpec(memory_space=pltpu.SMEM),
                  pl.BlockSpec((Q, NSA_HPG * HEAD_DIM), lambda g, c: (c, g)),
                  pl.BlockSpec((Q, LANES), lambda g, c: (c, 0)),
                  full(ncp + 2 * Q), full(ncp + 2 * Q),
                  full(T + Q + SLC_TAIL, HEAD_DIM + Q), full(T + Q + SLC_TAIL), full(T + WINDOW), full(T + WINDOW)],
        out_specs=pl.BlockSpec((Q, NSA_HPG * HEAD_DIM), lambda g, c: (c, g)),
        scratch_shapes=[pltpu.VMEM((NSA_HEADS, Q, Q), F32), pltpu.VMEM((NSA_HEADS, Q, Q), F32),
                        pltpu.VMEM((NSA_HEADS, Q, Q), F32), pltpu.VMEM((Q, Q), F32)],
        compiler_params=_cparams(("arbitrary", "arbitrary")),
        name="nsa_attention",
    )(rel_bias, proj, proj_gate, kcmp, vcmp, ks, vs, kw, vw)


def _softplus(z):
    return jnp.maximum(z, 0.0) + jnp.log1p(jnp.exp(-jnp.abs(z)))


def _lru_kernel(x_ref, y_ref, cw_ref, cb_ref, wa_ref, ba_ref, wi_ref, bi_ref, lam_ref, o_ref,
                xbuf, hc, a_s, b_s, h_s, *, tb):
    i = pl.program_id(0)

    @pl.when(i == 0)
    def _():
        xbuf[0:8, :] = jnp.zeros((8, LRU_WIDTH), F32)
        hc[...] = jnp.zeros((1, LRU_WIDTH), F32)

    xbuf[8:8 + tb, :] = x_ref[...]
    xc = cb_ref[...]
    for w in range(CONV_W):
        xc = xc + cw_ref[w:w + 1, :] * xbuf[8 - (CONV_W - 1) + w:8 - (CONV_W - 1) + w + tb, :]
    xbuf[0:8, :] = xbuf[tb:tb + 8, :]

    xcb = xc.astype(BF16)
    ra, ia = [], []
    for n in range(LRU_BLOCKS):
        xg = xcb[:, n * LRU_BW:(n + 1) * LRU_BW]
        ra.append(_dot(xg, wa_ref[n].astype(BF16)))
        ia.append(_dot(xg, wi_ref[n].astype(BF16)))
    r_gate = jax.nn.sigmoid(jnp.concatenate(ra, axis=1) + ba_ref[...])
    i_gate = jax.nn.sigmoid(jnp.concatenate(ia, axis=1) + bi_ref[...])
    log_a = (-LRU_C * r_gate) * _softplus(-lam_ref[...])
    a = jnp.exp(log_a)
    a_s[...] = a
    b_s[...] = jnp.sqrt(-jnp.tanh(log_a) * (a * a + 1.0)) * (i_gate * xc)

    rows = lax.broadcasted_iota(jnp.int32, (8, LRU_WIDTH), 0)

    def tile(k, h):
        r0 = pl.multiple_of(k * 8, 8)
        A = a_s[pl.ds(r0, 8), :]
        B = b_s[pl.ds(r0, 8), :]
        for sh in (1, 2, 4):
            ok = rows >= sh
            A_p = pltpu.roll(A, sh, axis=0)
            B_p = pltpu.roll(B, sh, axis=0)
            B = jnp.where(ok, A * B_p + B, B)
            A = jnp.where(ok, A * A_p, A)
        H = A * h + B
        h_s[pl.ds(r0, 8), :] = H
        return H[7:8, :]

    hc[...] = lax.fori_loop(0, tb // 8, tile, hc[...])
    o_ref[...] = (jax.nn.gelu(y_ref[...]) * h_s[...]).astype(o_ref.dtype)


def _rglru(proj, y_col, x_col, conv_w, conv_b, wa, ba, wi, bi, lam, tb=256):
    T = proj.shape[0]
    W = LRU_WIDTH
    vec = pl.BlockSpec((1, W), lambda i: (0, 0))
    blk = pl.BlockSpec((LRU_BLOCKS, LRU_BW, LRU_BW), lambda i: (0, 0, 0))
    return pl.pallas_call(
        functools.partial(_lru_kernel, tb=tb),
        out_shape=jax.ShapeDtypeStruct((T, W), BF16),
        grid=(T // tb,),
        in_specs=[pl.BlockSpec((tb, W), lambda i: (i, x_col)),
                  pl.BlockSpec((tb, W), lambda i: (i, y_col)),
                  pl.BlockSpec((CONV_W, W), lambda i: (0, 0)), vec, blk, vec, blk, vec, vec],
        out_specs=pl.BlockSpec((tb, W), lambda i: (i, 0)),
        scratch_shapes=[pltpu.VMEM((tb + 8, W), F32), pltpu.VMEM((1, W), F32),
                        pltpu.VMEM((tb, W), F32), pltpu.VMEM((tb, W), F32), pltpu.VMEM((tb, W), F32)],
        compiler_params=_cparams(("arbitrary",)),
        name="rglru",
    )(proj, proj, conv_w, conv_b.reshape(1, W), wa, ba.reshape(1, W), wi, bi.reshape(1, W), lam.reshape(1, W))


def _split3(x):
    hi = x.astype(BF16)
    r = x - hi.astype(F32)
    mid = r.astype(BF16)
    lo = (r - mid.astype(F32)).astype(BF16)
    return hi, mid, lo


def _hgrn_kernel(q_ref, f_ref, v_ref, g_ref, lbl_ref, ng_ref, o_ref, st_ref, oacc, *, tbh, layer):
    t = pl.program_id(1)
    C = HG_CHUNK
    SB = HG_SUB

    NH = HG_HEADS_PER_STEP
    DK, DV = HG_DK, HG_DV

    @pl.when(t == 0)
    def _():
        st_ref[...] = jnp.zeros((NH, DV, DK), F32)

    lg = lbl_ref[...]
    e = jnp.exp(lg - jnp.max(lg, axis=0, keepdims=True))
    sm = e / jnp.sum(e, axis=0, keepdims=True)
    cum = sm[0:1, :]
    for l in range(1, layer + 1):
        cum = cum + sm[l:l + 1, :]
    lb_all = cum - sm[0:1, :]

    tri = (lax.broadcasted_iota(jnp.int32, (C, C), 0) >= lax.broadcasted_iota(jnp.int32, (C, C), 1)).astype(BF16)
    rows8 = lax.broadcasted_iota(jnp.int32, (8, DK), 0)

    def head_chunk(r0, hh):
        cs = slice(hh * DK, (hh + 1) * DK)
        lb = lb_all[:, cs]
        qz = q_ref[pl.ds(r0, C), cs]
        qq = qz * jax.nn.sigmoid(qz)
        f = lb + (1.0 - lb) * jax.nn.sigmoid(f_ref[pl.ds(r0, C), cs])
        logf = jnp.log(jnp.maximum(f, 1e-30))
        kk = 1.0 - f
        vv = v_ref[pl.ds(r0, C), cs]
        l_hi, l_mid, l_lo = _split3(logf)
        bcum = _dot(tri, l_hi) + _dot(tri, l_mid) + _dot(tri, l_lo)
        st = st_ref[hh]
        o_parts = []
        vb = vv.astype(BF16)
        for I in range(C // SB):
            lo_r = I * SB
            for a in range(SB // 8):
                t0 = lo_r + 8 * a
                q8 = qq[t0:t0 + 8]
                b8 = bcum[t0:t0 + 8]
                o8 = jnp.zeros((8, DV), F32)
                for s in range(lo_r, t0 + 8):
                    d = b8 - bcum[s:s + 1]
                    if s > t0:
                        d = jnp.where(rows8 >= s - t0, d, NEG)
                    w = q8 * (kk[s:s + 1] * jnp.exp(d))
                    o8 = o8 + jnp.sum(w, axis=1, keepdims=True) * vv[s:s + 1]
                o_parts.append(o8)
        for I in range(1, C // SB):
            lo_r = I * SB
            ref = bcum[lo_r - 1:lo_r]
            qe = (qq[lo_r:lo_r + SB] * jnp.exp(bcum[lo_r:lo_r + SB] - ref)).astype(BF16)
            ke = (kk[0:lo_r] * jnp.exp(ref - bcum[0:lo_r])).astype(BF16)
            att = _dot_nt(qe, ke)
            o_off = _dot(att.astype(BF16), vb[0:lo_r])
            for a in range(SB // 8):
                o_parts[lo_r // 8 + a] = o_parts[lo_r // 8 + a] + o_off[8 * a:8 * a + 8]
        o_intra = jnp.concatenate(o_parts, axis=0)
        o_inter = _dot_nt((qq * jnp.exp(bcum)).astype(BF16), st.astype(BF16))
        oacc[pl.ds(r0, C), cs] = o_inter + o_intra
        b_last = bcum[C - 1:C]
        kd = (kk * jnp.exp(b_last - bcum)).astype(BF16)
        st_ref[hh] = st * jnp.exp(b_last) + _dot_tn(vb, kd)

    def chunk(c, carry):
        r0 = pl.multiple_of(c * C, C)
        for hh in range(NH):
            head_chunk(r0, hh)
        return carry

    lax.fori_loop(0, tbh // C, chunk, 0)
    gz = g_ref[...]
    gate = (gz * jax.nn.sigmoid(gz)) * ng_ref[...]
    for hh in range(NH):
        cs = slice(hh * DV, (hh + 1) * DV)
        o = oacc[:, cs]
        o = o * lax.rsqrt(jnp.mean(o * o, axis=-1, keepdims=True) + EPS)
        o_ref[:, cs] = (o * gate[:, cs]).astype(o_ref.dtype)


def _hgrn2(proj, lb_logits, norm_g, layer, tbh=512):
    T = proj.shape[0]
    NH = HG_HEADS_PER_STEP
    HB = HG_HEADS // NH
    wk = NH * HG_DK
    col = lambda off: pl.BlockSpec((tbh, wk), lambda h, t: (t, off + h))
    return pl.pallas_call(
        functools.partial(_hgrn_kernel, tbh=tbh, layer=layer),
        out_shape=jax.ShapeDtypeStruct((T, HG_VW), BF16),
        grid=(HB, T // tbh),
        in_specs=[col(0), col(HB), col(2 * HB), col(3 * HB),
                  pl.BlockSpec((DEPTH, wk), lambda h, t: (0, h)),
                  pl.BlockSpec((1, wk), lambda h, t: (0, h))],
        out_specs=pl.BlockSpec((tbh, wk), lambda h, t: (t, h)),
        scratch_shapes=[pltpu.VMEM((NH, HG_DV, HG_DK), F32), pltpu.VMEM((tbh, wk), F32)],
        compiler_params=_cparams(("arbitrary", "arbitrary")),
        name="hgrn2",
    )(proj, proj, proj, proj, lb_logits, norm_g.reshape(1, HG_VW))


def _router_kernel(x_ref, g_ref, sc_ref, sh_ref, w_ref, b_ref, h_ref, eid_ref, gate_ref, cnt_ref):
    h = _norm_mod(x_ref[...], g_ref[...], sc_ref[...], sh_ref[...])
    tm, d = h.shape
    hb = h.astype(BF16)
    lo = lax.bitcast_convert_type(hb[:, :d // 2].astype(F32), jnp.uint32) >> 16
    hi = lax.bitcast_convert_type(hb[:, d // 2:].astype(F32), jnp.uint32)
    packed = hi | lo
    pt = d // 2 // LANES
    for c in range(pt):
        h_ref[pl.ds(c, tm, stride=pt), :] = packed[:, c * LANES:(c + 1) * LANES]
    h1, h2, h3 = _split3(h)
    w1, w2, w3 = _split3(w_ref[...])
    logits = (_dot(h1, w1) + (_dot(h1, w2) + _dot(h2, w1))
              + (_dot(h2, w2) + _dot(h1, w3) + _dot(h3, w1))) + b_ref[...]
    lane = lax.broadcasted_iota(jnp.int32, logits.shape, 1)
    is_g = lane < N_GROUPS
    glog = jnp.where(is_g, logits, -jnp.inf)
    gmax = jnp.max(glog, axis=-1, keepdims=True)
    grp = jnp.min(jnp.where(glog == gmax, lane, LANES), axis=-1, keepdims=True)
    gsum = jnp.sum(jnp.where(is_g, jnp.exp(glog - gmax), 0.0), axis=-1, keepdims=True)
    p_grp = 1.0 / gsum
    lo = N_GROUPS + EXP_PER_GROUP * grp
    el = jnp.where((lane >= lo) & (lane < lo + EXP_PER_GROUP), logits, -jnp.inf)
    v1 = jnp.max(el, axis=-1, keepdims=True)
    i1 = jnp.min(jnp.where(el == v1, lane, LANES), axis=-1, keepdims=True)
    el2 = jnp.where(lane == i1, -jnp.inf, el)
    v2 = jnp.max(el2, axis=-1, keepdims=True)
    i2 = jnp.min(jnp.where(el2 == v2, lane, LANES), axis=-1, keepdims=True)
    e2 = jnp.exp(v2 - v1)
    den = 1.0 + e2
    g1 = p_grp * (1.0 / den)
    g2 = p_grp * (e2 / den)
    eid_ref[...] = jnp.where(lane == 0, i1 - N_GROUPS, jnp.where(lane == 1, i2 - N_GROUPS, 0))
    gate_ref[...] = jnp.where(lane == 0, g1, jnp.where(lane == 1, g2, 0.0))

    @pl.when(pl.program_id(0) == 0)
    def _():
        cnt_ref[...] = jnp.zeros(cnt_ref.shape, F32)

    picked = ((lane == i1 - N_GROUPS) | (lane == i2 - N_GROUPS)).astype(F32)
    cnt_ref[...] += jnp.sum(picked, axis=0, keepdims=True)


def _dispatch_kernel(eid_ref, base_ref, dest_ref, tok_ref, carry, dstage, dsm, zbuf, sem, *, tb):
    i = pl.program_id(0)

    @pl.when(i == 0)
    def _():
        carry[...] = jnp.zeros(carry.shape, F32)
        zbuf[...] = jnp.zeros(zbuf.shape, jnp.int32)
        z = pltpu.make_async_copy(zbuf, tok_ref, sem)
        z.start()
        z.wait()

    e = eid_ref[...]
    lane = lax.broadcasted_iota(jnp.int32, e.shape, 1)
    oh0 = (lane == e[:, 0:1]).astype(F32)
    oh1 = (lane == e[:, 1:2]).astype(F32)
    both = oh0 + oh1
    before = (lax.broadcasted_iota(jnp.int32, (tb, tb), 1) < lax.broadcasted_iota(jnp.int32, (tb, tb), 0)).astype(BF16)
    prior = _dot(before, both.astype(BF16)) + (carry[...] + base_ref[...])
    d0 = jnp.sum(oh0 * prior, axis=1, keepdims=True)
    d1 = jnp.sum(oh1 * (prior + oh0), axis=1, keepdims=True)
    carry[...] += jnp.sum(both, axis=0, keepdims=True)
    dest = jnp.where(lane == 0, d0, jnp.where(lane == 1, d1, 0.0)).astype(jnp.int32)
    dest_ref[...] = dest

    dstage[...] = jnp.transpose(dest)[0:8, :]
    cp = pltpu.make_async_copy(dstage, dsm, sem)
    cp.start()
    cp.wait()

    def body(n, c):
        for k in range(TOPK_IN_GROUP):
            p = dsm[k, n]
            tok_ref[p >> 7, p & (LANES - 1)] = i * tb + n
        return c

    lax.fori_loop(0, tb, body, 0, unroll=4)


def _dispatch(eid_l, base, n_blk, tb=512):
    N = eid_l.shape[0]
    return pl.pallas_call(
        functools.partial(_dispatch_kernel, tb=tb),
        out_shape=[jax.ShapeDtypeStruct((N, LANES), jnp.int32), jax.ShapeDtypeStruct((n_blk, LANES), jnp.int32)],
        grid=(N // tb,),
        in_specs=[pl.BlockSpec((tb, LANES), lambda i: (i, 0)), pl.BlockSpec((1, LANES), lambda i: (0, 0))],
        out_specs=[pl.BlockSpec((tb, LANES), lambda i: (i, 0)), pl.BlockSpec(memory_space=pltpu.SMEM)],
        scratch_shapes=[pltpu.VMEM((1, LANES), F32), pltpu.VMEM((8, tb), jnp.int32), pltpu.SMEM((8, tb), jnp.int32),
                        pltpu.VMEM((n_blk, LANES), jnp.int32), pltpu.SemaphoreType.DMA(())],
        compiler_params=_cparams(("arbitrary",)),
        name="moe_dispatch",
    )(eid_l, base)


def _router(x, g, sc, sh, w_r, b_r, tm=512):
    T, D = x.shape
    vec = pl.BlockSpec((1, D), lambda i: (0, 0))
    return pl.pallas_call(
        _router_kernel,
        out_shape=[jax.ShapeDtypeStruct((T * PACKED_ROWS, LANES), jnp.uint32), jax.ShapeDtypeStruct((T, LANES), jnp.int32),
                   jax.ShapeDtypeStruct((T, LANES), F32), jax.ShapeDtypeStruct((1, LANES), F32)],
        grid=(T // tm,),
        in_specs=[pl.BlockSpec((tm, D), lambda i: (i, 0)), vec, vec, vec,
                  pl.BlockSpec((D, LANES), lambda i: (0, 0)), pl.BlockSpec((1, LANES), lambda i: (0, 0))],
        out_specs=[pl.BlockSpec((tm * PACKED_ROWS, LANES), lambda i: (i, 0)),
                   pl.BlockSpec((tm, LANES), lambda i: (i, 0)),
                   pl.BlockSpec((tm, LANES), lambda i: (i, 0)),
                   pl.BlockSpec((1, LANES), lambda i: (0, 0))],
        compiler_params=_cparams(("arbitrary",)),
        name="moe_router",
    )(x, g, sc, sh, w_r, b_r)


def _cast_rows(src_ref, slot, dst_ref, rows):
    def body(c, carry):
        r = pl.multiple_of(c * rows, rows)
        dst_ref[pl.ds(r, rows), :] = src_ref[slot, pl.ds(r, rows), :].astype(BF16)
        return carry
    lax.fori_loop(0, dst_ref.shape[0] // rows, body, 0, unroll=4)


def _expert_kernel(rid_ref, re_ref, nu_ref, tok_ref, h_ref, wg_hbm, wu_hbm, wd_hbm, y_ref,
                   wgf, wuf, wdf, wgb, wub, wdb, xbuf, wsem, xsem, *, layer):
    i = pl.program_id(0)
    n_used = nu_ref[0]
    R = MOE_BLOCK

    def w_copies(e, slot):
        return ((pltpu.make_async_copy(wg_hbm.at[layer, e], wgf.at[slot], wsem.at[slot, 0]), 1),
                (pltpu.make_async_copy(wu_hbm.at[layer, e], wuf.at[slot], wsem.at[slot, 1]), 1),
                (pltpu.make_async_copy(wd_hbm.at[layer, e], wdf.at[slot], wsem.at[slot, 2]), 1))

    HT = PACKED_ROWS

    def x_start(blk, slot):
        def body(r, c):
            tok = tok_ref[blk * R + r]
            pltpu.make_async_copy(h_ref.at[pl.ds(pl.multiple_of(tok * HT, HT), HT)],
                                  xbuf.at[slot, pl.ds(pl.multiple_of(r * HT, HT), HT)], xsem.at[slot]).start()
            return c
        lax.fori_loop(0, R, body, 0, unroll=8)

    def x_wait(slot):
        pltpu.make_async_copy(h_ref.at[pl.ds(0, R * HT)], xbuf.at[slot], xsem.at[slot]).wait()

    def w_request(run):
        @pl.when(re_ref[run] >= 0)
        def _():
            for c, pri in w_copies(re_ref[run], run % 2):
                c.start(priority=pri)

    @pl.when(i == 0)
    def _():
        w_request(0)
        w_request(1)
        x_start(0, 0)

    @pl.when(i < n_used)
    def _():
        slot = i % 2

        @pl.when(i + 1 < n_used)
        def _():
            x_start(i + 1, 1 - slot)

        run = rid_ref[i]

        @pl.when((i == 0) | (run != rid_ref[jnp.maximum(i - 1, 0)]))
        def _():
            ws = run % 2
            for c, _ in w_copies(re_ref[run], ws):
                c.wait()
            _cast_rows(wgf, ws, wgb, 64)
            _cast_rows(wuf, ws, wub, 64)
            _cast_rows(wdf, ws, wdb, 16)
            w_request(run + 2)

        x_wait(slot)
        words = [xbuf[slot, pl.ds(c, R, stride=HT), :] for c in range(HT)]
        lo = [lax.bitcast_convert_type(w << 16, F32) for w in words]
        hi = [lax.bitcast_convert_type(w & jnp.uint32(0xFFFF0000), F32) for w in words]
        x = jnp.concatenate(lo + hi, axis=1).astype(BF16)
        hg = _dot(x, wgb[...])
        hu = _dot(x, wub[...])
        hid = (hg * jax.nn.sigmoid(hg)) * hu
        y_ref[...] = _dot(hid.astype(BF16), wdb[...])

    @pl.when(i >= n_used)
    def _():
        y_ref[...] = jnp.zeros(y_ref.shape, y_ref.dtype)


def _experts(h, tok, run_id, run_e, n_used, w_gate, w_up, w_down, layer):
    D = D_MODEL
    n_pad = tok.shape[0]
    n_blk = n_pad // MOE_BLOCK
    any_spec = pl.BlockSpec(memory_space=pl.ANY)
    return pl.pallas_call(
        functools.partial(_expert_kernel, layer=layer),
        out_shape=jax.ShapeDtypeStruct((n_pad, D), F32),
        grid_spec=pltpu.PrefetchScalarGridSpec(
            num_scalar_prefetch=4,
            grid=(n_blk,),
            in_specs=[any_spec, any_spec, any_spec, any_spec],
            out_specs=pl.BlockSpec((MOE_BLOCK, D), lambda i, *_: (i, 0)),
            scratch_shapes=[pltpu.VMEM((2, D, D_EXPERT), F32), pltpu.VMEM((2, D, D_EXPERT), F32),
                            pltpu.VMEM((2, D_EXPERT, D), F32),
                            pltpu.VMEM((D, D_EXPERT), BF16), pltpu.VMEM((D, D_EXPERT), BF16),
                            pltpu.VMEM((D_EXPERT, D), BF16),
                            pltpu.VMEM((2, MOE_BLOCK * PACKED_ROWS, LANES), jnp.uint32),
                            pltpu.SemaphoreType.DMA((2, 4)), pltpu.SemaphoreType.DMA((2,))]),
        compiler_params=_cparams(("arbitrary",)),
        name="moe_experts",
    )(run_id, run_e, n_used, tok, h, w_gate, w_up, w_down)


def _combine_kernel(slot_ref, y_ref, x_ref, w_ref, g_ref, fg_ref, o_ref, buf, sem, *, rows, final_norm):
    i = pl.program_id(0)

    def copy(r, k):
        return pltpu.make_async_copy(y_ref.at[pl.ds(slot_ref[(i * rows + r) * 2 + k], 1)],
                                     buf.at[k, pl.ds(r, 1)], sem)

    def start(r, c):
        copy(r, 0).start()
        copy(r, 1).start()
        return c

    lax.fori_loop(0, rows, start, 0, unroll=8)
    for k in range(TOPK_IN_GROUP):
        pltpu.make_async_copy(y_ref.at[pl.ds(0, rows)], buf.at[k], sem).wait()
    w = w_ref[...]
    moe = buf[0] * w[:, 0:1] + buf[1] * w[:, 1:2]
    y = x_ref[...] + g_ref[...] * moe
    if final_norm:
        y = (y * lax.rsqrt(jnp.mean(y * y, axis=-1, keepdims=True) + EPS)) * fg_ref[...]
    o_ref[...] = y


def _combine(yb, slots, x, gate_w, g2, final_g, final_norm, rows=256):
    T, D = x.shape
    return pl.pallas_call(
        functools.partial(_combine_kernel, rows=rows, final_norm=final_norm),
        out_shape=jax.ShapeDtypeStruct((T, D), F32),
        grid_spec=pltpu.PrefetchScalarGridSpec(
            num_scalar_prefetch=1,
            grid=(T // rows,),
            in_specs=[pl.BlockSpec(memory_space=pl.ANY),
                      pl.BlockSpec((rows, D), lambda i, s: (i, 0)),
                      pl.BlockSpec((rows, LANES), lambda i, s: (i, 0)),
                      pl.BlockSpec((1, D), lambda i, s: (0, 0)),
                      pl.BlockSpec((1, D), lambda i, s: (0, 0))],
            out_specs=pl.BlockSpec((rows, D), lambda i, s: (i, 0)),
            scratch_shapes=[pltpu.VMEM((2, rows, D), F32), pltpu.SemaphoreType.DMA(())]),
        compiler_params=_cparams(("arbitrary",)),
        name="moe_combine",
    )(slots.reshape(-1), yb, x, gate_w, g2, final_g)


def _hier_moe_residual(x, g, sc, sh, g2, w_grp, b_grp, w_exp, b_exp, w_gate, w_up, w_down, layer, final_g,
                       final_norm):
    N, D = x.shape
    n_route = N_GROUPS + N_EXPERTS
    w_r = jnp.zeros((D, LANES), F32).at[:, :N_GROUPS].set(w_grp).at[:, N_GROUPS:n_route].set(w_exp)
    b_r = jnp.zeros((1, LANES), F32).at[0, :N_GROUPS].set(b_grp).at[0, N_GROUPS:n_route].set(b_exp)
    h, eid_l, gate_l, cnt = _router(x, g, sc, sh, w_r, b_r)

    A = N * TOPK_IN_GROUP
    counts = cnt[0, :N_EXPERTS].astype(jnp.int32)
    padded = (counts + MOE_BLOCK - 1) // MOE_BLOCK * MOE_BLOCK
    pad_end = jnp.cumsum(padded)
    pad_start = pad_end - padded
    n_blk = -(-(A + N_EXPERTS * MOE_BLOCK) // MOE_BLOCK)
    base = jnp.zeros((1, LANES), F32).at[0, :N_EXPERTS].set(pad_start.astype(F32))
    dest_l, tok_tbl = _dispatch(eid_l, base, n_blk)
    buf_tok = tok_tbl.reshape(-1)
    blk_idx = jnp.arange(n_blk, dtype=jnp.int32)
    n_used = pad_end[-1] // MOE_BLOCK
    blk_e = jnp.minimum(jnp.searchsorted(pad_end, blk_idx * MOE_BLOCK, side='right'), N_EXPERTS - 1).astype(jnp.int32)
    blk_e = jnp.where(blk_idx < n_used, blk_e, blk_e[n_used - 1])
    change = jnp.concatenate([jnp.ones((1,), bool), blk_e[1:] != blk_e[:-1]])
    run_id = (jnp.cumsum(change.astype(jnp.int32)) - 1).astype(jnp.int32)
    runs = jnp.arange(n_blk + 2, dtype=jnp.int32)
    run_e = jnp.max(jnp.where(run_id[None, :] == runs[:, None], blk_e[None, :], -1), axis=1).astype(jnp.int32)
    slots = dest_l[:, :TOPK_IN_GROUP]

    yb = _experts(h, buf_tok, run_id, run_e, n_used.reshape(1).astype(jnp.int32), w_gate, w_up, w_down, layer)
    return _combine(yb, slots, x, gate_l, g2, final_g, final_norm)


def _even_mixer_residual(x, g, sc, sh, g1, rel_bias, w_in, w_out, cmp_pe, cmp_w1, cmp_b1, cmp_w2, cmp_b2,
                         conv_w, conv_b, lru_wa, lru_ba, lru_wi, lru_bi, lru_lambda):
    T, D = x.shape
    G, HD = NSA_KV_HEADS, HEAD_DIM
    nq = NSA_HEADS * HD
    n_kv = 6 * NSA_KV_W
    n_gate = 3 * NSA_HEADS
    w_main = jnp.concatenate([w_in[:, :nq], w_in[:, nq + n_kv + n_gate:], w_in[:, nq:nq + n_kv]],
                             axis=1).astype(BF16)
    w_gl = jnp.zeros((D, LANES), F32).at[:, :n_gate].set(w_in[:, nq + n_kv:nq + n_kv + n_gate]).astype(BF16)
    kv0 = nq + 2 * LRU_WIDTH
    tn = 512
    proj, proj_gate, kv_b = _norm_proj(x, g, sc, sh, w_main, w_gl, bf16_from=kv0 // tn, tn=tn)

    def kv_heads(j, src=kv_b, c_base=0):
        c0 = c_base + j * NSA_KV_W
        return src[:, c0:c0 + NSA_KV_W].reshape(T, G, HD).transpose(1, 0, 2)

    cmp_in = jnp.stack([kv_heads(0, proj, kv0), kv_heads(1, proj, kv0)], axis=0)
    cmp_out = _compress(cmp_in, cmp_pe, cmp_w1, cmp_b1, cmp_w2, cmp_b2)
    ncp = T // CMP_STRIDE
    valid = (jnp.arange(ncp) < ncp - 1)[None, None, :, None]
    cmp_pad = jnp.pad(jnp.where(valid, cmp_out, 0.0), ((0, 0), (0, 0), (Q_BLOCK, Q_BLOCK), (0, 0)))
    slc_pad = ((0, 0), (Q_BLOCK, SLC_TAIL), (0, 0))
    blk_of_row = jnp.pad(jnp.arange(T, dtype=jnp.int32) // SLC_LEN, (Q_BLOCK, SLC_TAIL), constant_values=Q_BLOCK - 1)
    blk_onehot = (blk_of_row[:, None] == jnp.arange(Q_BLOCK, dtype=jnp.int32)[None, :]).astype(BF16)
    ks = jnp.concatenate([jnp.pad(kv_heads(2), slc_pad), jnp.broadcast_to(blk_onehot, (G,) + blk_onehot.shape)], axis=2)
    vs = jnp.pad(kv_heads(3), slc_pad)
    kw = jnp.pad(kv_heads(4), ((0, 0), (WINDOW, 0), (0, 0)))
    vw = jnp.pad(kv_heads(5), ((0, 0), (WINDOW, 0), (0, 0)))
    nsa_out = _nsa_attention(rel_bias, proj, proj_gate, cmp_pad[0], cmp_pad[1], ks, vs, kw, vw)

    y_col = nq // LRU_WIDTH
    lru_out = _rglru(proj, y_col, y_col + 1, conv_w, conv_b, lru_wa, lru_ba, lru_wi, lru_bi, lru_lambda)
    w_o = w_out.astype(BF16)
    return _out_proj(nsa_out, lru_out, w_o[:nq], w_o[nq:], x, g1)


def _odd_mixer_residual(x, g, sc, sh, g1, lb_logits, w_in, w_out, norm_g, layer):
    proj = _norm_proj(x, g, sc, sh, w_in.astype(BF16))[0]
    o = _hgrn2(proj, lb_logits, norm_g, layer)
    w_o = w_out.astype(BF16)
    half = HG_VW // 2
    return _out_proj(o[:, :half], o[:, half:], w_o[:half], w_o[half:], x, g1)


def kernel(x, c, rel_bias, ada_w, ada_b, norm_mix_g, norm_ffn_g, ev_w_in, ev_w_out, cmp_pe, cmp_w1, cmp_b1, cmp_w2, cmp_b2, lru_conv_w, lru_conv_b, lru_wa, lru_ba, lru_wi, lru_bi, lru_lambda, od_w_in, od_w_out, hg_lb_logits, hg_norm_g, moe_w_grp, moe_b_grp, moe_w_exp, moe_b_exp, moe_w_gate, moe_w_up, moe_w_down, final_g):
    B, T, D = x.shape
    assert B == 1 and D == D_MODEL
    xt = x.reshape(T, D)
    mod = _modulation(c, ada_w, ada_b)
    for l in range(DEPTH):
        sh1, sc1, g1, sh2, sc2, g2 = [mod[l, :, k * D:(k + 1) * D] for k in range(6)]
        gm = norm_mix_g[l].reshape(1, D)
        gf = norm_ffn_g[l].reshape(1, D)
        j = l // 2
        if l % 2 == 0:
            xt = _even_mixer_residual(xt, gm, sc1, sh1, g1, rel_bias, ev_w_in[j], ev_w_out[j], cmp_pe[j],
                                      cmp_w1[j], cmp_b1[j], cmp_w2[j], cmp_b2[j], lru_conv_w[j],
                                      lru_conv_b[j], lru_wa[j], lru_ba[j], lru_wi[j], lru_bi[j], lru_lambda[j])
        else:
            xt = _odd_mixer_residual(xt, gm, sc1, sh1, g1, hg_lb_logits, od_w_in[j], od_w_out[j],
                                     hg_norm_g[j], l)
        xt = _hier_moe_residual(xt, gf, sc2, sh2, g2, moe_w_grp[l], moe_b_grp[l], moe_w_exp[l], moe_b_exp[l],
                                moe_w_gate, moe_w_up, moe_w_down, l, final_g.reshape(1, D), l == DEPTH - 1)
    return xt.reshape(B, T, D)
```

```python
import functools
import math

import numpy as np
import jax
import jax.numpy as jnp
from jax import lax
from jax.experimental import pallas as pl
from jax.experimental.pallas import tpu as pltpu

F32 = jnp.float32
BF16 = jnp.bfloat16

D_MODEL = 2048
DEPTH = 2
NSA_HEADS = 8
NSA_KV_HEADS = 2
NSA_HPG = NSA_HEADS // NSA_KV_HEADS
HEAD_DIM = 128
NSA_KV_W = NSA_KV_HEADS * HEAD_DIM
CMP_LEN = 32
CMP_STRIDE = 16
SLC_LEN = 64
SLC_TOPN = 16
WINDOW = 512
Q_BLOCK = 128
LRU_WIDTH = 1024
LRU_BLOCKS = 8
LRU_BW = LRU_WIDTH // LRU_BLOCKS
CONV_W = 4
LRU_C = 8.0
HG_HEADS = 16
HG_DK = 128
HG_DV = 128
HG_CHUNK = 64
HG_SUB = 16
HG_HEADS_PER_STEP = 8
SLC_GROUP = 4
RANK_UNROLL = 4
RANK_BAND = 32
SLC_TAIL = 2 * SLC_GROUP * Q_BLOCK
HG_KW = HG_HEADS * HG_DK
HG_VW = HG_HEADS * HG_DV
N_BUCKETS = 32
MAX_DIST = 128
N_GROUPS = 8
EXP_PER_GROUP = 8
N_EXPERTS = N_GROUPS * EXP_PER_GROUP
TOPK_IN_GROUP = 2
D_EXPERT = 512
MOE_BLOCK = 128
EPS = 1e-6

LANES = 128
PACKED_ROWS = D_MODEL // 2 // LANES
NEG = -1e30
M_FLOOR = -1e20
LOG2E = math.log2(math.e)
VMEM_LIMIT = 56 * 1024 * 1024


def _cparams(sem):
    return pltpu.CompilerParams(dimension_semantics=sem, vmem_limit_bytes=VMEM_LIMIT)


def _dot(a, b):
    return jnp.dot(a, b, preferred_element_type=F32)


def _dot_nt(a, b):
    return lax.dot_general(a, b, (((1,), (1,)), ((), ())), preferred_element_type=F32)


def _dot_tn(a, b):
    return lax.dot_general(a, b, (((0,), (0,)), ((), ())), preferred_element_type=F32)


def _bucket_starts():
    n = np.arange(0, MAX_DIST + 1, dtype=np.int32)
    max_exact = N_BUCKETS // 2
    nf = np.maximum(n, 1).astype(np.float32)
    large = max_exact + (np.log(nf / np.float32(max_exact)) / np.float32(math.log(MAX_DIST / max_exact))
                         * np.float32(N_BUCKETS - max_exact)).astype(np.int32)
    large = np.minimum(large, N_BUCKETS - 1)
    b = np.where(n < max_exact, n, large)
    starts = [int(np.argmax(b >= k)) for k in range(N_BUCKETS)]
    assert all(b[s] == k for k, s in enumerate(starts)) and b[-1] == N_BUCKETS - 1
    return starts


BUCKET_STARTS = _bucket_starts()


def _mod_kernel(c_ref, w_ref, b_ref, o_ref):
    c = c_ref[...]
    cact = c * jax.nn.sigmoid(c)
    o_ref[...] = jnp.sum(cact * w_ref[...], axis=0, keepdims=True) + b_ref[...]


def _modulation(c, ada_w, ada_b):
    D = D_MODEL
    tn = 1024
    n_out = 6 * D
    c_col = c.reshape(D, 1)
    return pl.pallas_call(
        _mod_kernel,
        out_shape=jax.ShapeDtypeStruct((DEPTH, 1, n_out), F32),
        grid=(DEPTH, n_out // tn),
        in_specs=[pl.BlockSpec((D, 1), lambda l, j: (0, 0)),
                  pl.BlockSpec((None, D, tn), lambda l, j: (l, 0, j)),
                  pl.BlockSpec((None, 1, tn), lambda l, j: (l, 0, j))],
        out_specs=pl.BlockSpec((None, 1, tn), lambda l, j: (l, 0, j)),
        compiler_params=_cparams(("arbitrary", "arbitrary")),
        name="adaln_mod",
    )(c_col, ada_w, ada_b.reshape(DEPTH, 1, n_out))


def _norm_mod(x, g, sc, sh):
    y = x * lax.rsqrt(jnp.mean(x * x, axis=-1, keepdims=True) + EPS)
    return (y * g) * (1.0 + sc) + sh


def _norm_proj_kernel(x_ref, g_ref, sc_ref, sh_ref, w_ref, *rest, has_extra, bf16_from):
    if has_extra:
        wx_ref, o_ref, ox_ref, ob_ref, h_ref = rest
    else:
        o_ref, h_ref = rest
    j = pl.program_id(1)

    @pl.when(j == 0)
    def _():
        h = _norm_mod(x_ref[...], g_ref[...], sc_ref[...], sh_ref[...]).astype(BF16)
        h_ref[...] = h
        if has_extra:
            ox_ref[...] = _dot(h, wx_ref[...])

    y = _dot(h_ref[...], w_ref[...])
    o_ref[...] = y
    if has_extra:
        @pl.when(j >= bf16_from)
        def _():
            ob_ref[...] = y.astype(BF16)


def _norm_proj(x, g, sc, sh, w, w_extra=None, bf16_from=None, tm=1024, tn=512):
    T, D = x.shape
    N = w.shape[1]
    has_extra = w_extra is not None
    vec = pl.BlockSpec((1, D), lambda i, j: (0, 0))
    in_specs = [pl.BlockSpec((tm, D), lambda i, j: (i, 0)), vec, vec, vec,
                pl.BlockSpec((D, tn), lambda i, j: (0, j))]
    out_shape = [jax.ShapeDtypeStruct((T, N), F32)]
    out_specs = [pl.BlockSpec((tm, tn), lambda i, j: (i, j))]
    args = [x, g, sc, sh, w]
    if has_extra:
        nx = w_extra.shape[1]
        in_specs.append(pl.BlockSpec((D, nx), lambda i, j: (0, 0)))
        out_shape.append(jax.ShapeDtypeStruct((T, nx), F32))
        out_specs.append(pl.BlockSpec((tm, nx), lambda i, j: (i, 0)))
        args.append(w_extra)
        out_shape.append(jax.ShapeDtypeStruct((T, N - bf16_from * tn), BF16))
        out_specs.append(pl.BlockSpec((tm, tn), lambda i, j: (i, jnp.maximum(j - bf16_from, 0))))
    return pl.pallas_call(
        functools.partial(_norm_proj_kernel, has_extra=has_extra, bf16_from=bf16_from),
        out_shape=out_shape,
        grid=(T // tm, N // tn),
        in_specs=in_specs,
        out_specs=out_specs,
        scratch_shapes=[pltpu.VMEM((tm, D), BF16)],
        compiler_params=_cparams(("arbitrary", "arbitrary")),
        name="norm_proj",
    )(*args)


def _out_proj_kernel(a1_ref, a2_ref, w1_ref, w2_ref, x_ref, g_ref, o_ref):
    y = _dot(a1_ref[...], w1_ref[...]) + _dot(a2_ref[...], w2_ref[...])
    o_ref[...] = x_ref[...] + g_ref[...] * y


def _out_proj(a1, a2, w, x, gate, a2_col=0, tm=1024, tn=512):
    T, D = x.shape
    K1 = K2 = w.shape[0] // 2
    return pl.pallas_call(
        _out_proj_kernel,
        out_shape=jax.ShapeDtypeStruct((T, D), F32),
        grid=(T // tm, D // tn),
        in_specs=[pl.BlockSpec((tm, K1), lambda i, j: (i, 0)),
                  pl.BlockSpec((tm, K2), lambda i, j: (i, a2_col)),
                  pl.BlockSpec((K1, tn), lambda i, j: (0, j)),
                  pl.BlockSpec((K2, tn), lambda i, j: (1, j)),
                  pl.BlockSpec((tm, tn), lambda i, j: (i, j)),
                  pl.BlockSpec((1, tn), lambda i, j: (0, j))],
        out_specs=pl.BlockSpec((tm, tn), lambda i, j: (i, j)),
        compiler_params=_cparams(("arbitrary", "arbitrary")),
        name="out_proj",
    )(a1, a2, w, w, x, gate)


def _compress_kernel(k2_ref, pe_ref, w1_ref, b1_ref, w2_ref, b2_ref, o_ref):
    half = (CMP_LEN // 2) * HEAD_DIM
    k2 = k2_ref[...]
    a = _dot((k2 + pe_ref[0:1, :]).astype(BF16), w1_ref[0:half, :].astype(BF16))
    b = _dot((k2 + pe_ref[1:2, :]).astype(BF16), w1_ref[half:2 * half, :].astype(BF16))
    nb = k2.shape[0]
    b_up = pltpu.roll(b, nb - 1, axis=0)
    hid = jax.nn.gelu(a + b_up + b1_ref[...])
    o_ref[...] = _dot(hid.astype(BF16), w2_ref[...].astype(BF16)) + b2_ref[...]


def _compress(kv, pe, w1, b1, w2, b2):
    _, G, T, HD = kv.shape
    nb = T // CMP_STRIDE
    row = CMP_STRIDE * HD
    kv2 = kv.reshape(2, G, nb, row)
    pe2 = pe.reshape(2, 2, row)
    return pl.pallas_call(
        _compress_kernel,
        out_shape=jax.ShapeDtypeStruct((2, G, nb, HD), F32),
        grid=(2, G),
        in_specs=[pl.BlockSpec((None, None, nb, row), lambda a, g: (a, g, 0, 0)),
                  pl.BlockSpec((None, 2, row), lambda a, g: (a, 0, 0)),
                  pl.BlockSpec((None, CMP_LEN * HD, HD), lambda a, g: (a, 0, 0)),
                  pl.BlockSpec((None, 1, HD), lambda a, g: (a, 0, 0)),
                  pl.BlockSpec((None, HD, HD), lambda a, g: (a, 0, 0)),
                  pl.BlockSpec((None, 1, HD), lambda a, g: (a, 0, 0))],
        out_specs=pl.BlockSpec((None, None, nb, HD), lambda a, g: (a, g, 0, 0)),
        compiler_params=_cparams(("arbitrary", "arbitrary")),
        name="nsa_compress",
    )(kv2, pe2, w1, b1.reshape(2, 1, HD), w2, b2.reshape(2, 1, HD))


def _bias_from_rel(rel, tbl_ref, head):
    far = tbl_ref[N_BUCKETS - 1, head]
    val = jnp.zeros(rel.shape, F32)
    for b in range(N_BUCKETS - 2, -1, -1):
        val = jnp.where(rel < BUCKET_STARTS[b + 1], (tbl_ref[b, head] - far) * LOG2E, val)
    return val


def _lane_tile4(x):
    return jnp.concatenate([x, x, x, x], axis=1)


def _col_softmax_stats(s):
    m = jnp.maximum(jnp.max(s, axis=0, keepdims=True), M_FLOOR)
    e = jnp.exp2(s - m)
    d = jnp.sum(e, axis=0, keepdims=True)
    return e, d


def _nsa_kernel(tbl_ref, q_ref, gl_ref, kc_ref, vc_ref, ks_ref, vs_ref, kw_ref, vw_ref, o_ref,
                bd_ref, bp_ref, bn_ref, imp_ref, cnt_ref, *, ncp):
    g = pl.program_id(0)
    ci = pl.program_id(1)
    Q = Q_BLOCK
    HD = HEAD_DIM
    H4 = NSA_HPG

    row_i = lax.broadcasted_iota(jnp.int32, (Q, Q), 0)
    lane_i = lax.broadcasted_iota(jnp.int32, (Q, Q), 1)

    @pl.when((g == 0) & (ci == 0))
    def _build_bias_tiles():
        for h in range(NSA_HEADS):
            rel_d = lane_i - row_i
            bd_ref[h] = jnp.where(rel_d >= 0, _bias_from_rel(jnp.maximum(rel_d, 0), tbl_ref, h), NEG)
            bp_ref[h] = _bias_from_rel(lane_i - row_i + Q, tbl_ref, h)
            rel_n = lane_i - CMP_STRIDE * row_i + (CMP_STRIDE * (Q - 8) - (CMP_LEN - 1))
            bn_ref[h] = jnp.where(rel_n >= 0, _bias_from_rel(jnp.maximum(rel_n, 0), tbl_ref, h), NEG)

    def head_tiles(ref):
        return jnp.concatenate([ref[g * H4 + h] for h in range(H4)], axis=1)

    qt = q_ref[...] * (HD ** -0.5 * LOG2E)
    qs = jnp.concatenate([qt[:, h * HD:(h + 1) * HD] for h in range(H4)], axis=0).astype(BF16)

    near0 = 8 * ci - (Q - 8)
    kc_far = kc_ref[pl.ds(Q, ncp), :]
    vc_far = vc_ref[pl.ds(Q, ncp), :]
    near_row = pl.multiple_of(8 * ci + 8, 8)
    kc_near = kc_ref[pl.ds(near_row, Q), :]
    vc_near = vc_ref[pl.ds(near_row, Q), :]
    n_far = lax.broadcasted_iota(jnp.int32, (ncp, 1), 0)
    s_far = jnp.where(n_far < near0, _dot_nt(kc_far.astype(BF16), qs), NEG)
    m_near = lax.broadcasted_iota(jnp.int32, (Q, 1), 0)
    s_near = _dot_nt(kc_near.astype(BF16), qs) + head_tiles(bn_ref)
    s_near = jnp.where(m_near + near0 >= 0, s_near, NEG)
    s_c = jnp.concatenate([s_far, s_near], axis=0)
    e_c, d_c = _col_softmax_stats(s_c)
    p_c = e_c * (1.0 / jnp.where(d_c > 0, d_c, 1.0))
    v_c = jnp.concatenate([vc_far, vc_near], axis=0).astype(BF16)
    o_cmp = _dot_tn(v_c, p_c.astype(BF16))

    psum = p_c[:, 0:Q] + p_c[:, Q:2 * Q] + p_c[:, 2 * Q:3 * Q] + p_c[:, 3 * Q:4 * Q]
    p_hi = psum.astype(BF16)
    p_lo = (psum - p_hi.astype(F32)).astype(BF16)
    jb = lax.broadcasted_iota(jnp.int32, (Q, ncp), 0)
    nb = lax.broadcasted_iota(jnp.int32, (Q, ncp), 1)
    ratio = SLC_LEN // CMP_STRIDE
    span = CMP_LEN // CMP_STRIDE - 1
    cover_far = ((nb >= ratio * jb - span) & (nb <= ratio * jb + ratio - 1)).astype(BF16)
    nn = lane_i + near0
    cover_near = ((nn >= ratio * row_i - span) & (nn <= ratio * row_i + ratio - 1)).astype(BF16)
    cover = jnp.concatenate([cover_far, cover_near], axis=1)
    imp = _dot(cover, p_hi) + _dot(cover, p_lo)
    cur = 2 * ci + (lane_i >= SLC_LEN).astype(jnp.int32)
    imp = jnp.where(row_i == cur, jnp.inf, jnp.where(row_i > cur, -jnp.inf, imp))
    imp_ref[...] = imp

    n_cand = (2 * ci + 2 + RANK_UNROLL - 1) // RANK_UNROLL
    cnt_ref[...] = jnp.zeros((Q, Q), F32)
    for band in range(Q // RANK_BAND):
        lo_row = band * RANK_BAND

        @pl.when(lo_row <= 2 * ci + 1)
        def _(lo_row=lo_row):
            imp_b = imp[lo_row:lo_row + RANK_BAND]
            row_b = row_i[lo_row:lo_row + RANK_BAND]

            def rank_body(it, cnt):
                for u in range(RANK_UNROLL):
                    b = it * RANK_UNROLL + u
                    r = imp_ref[pl.ds(b, 1), :]
                    ahead = (r > imp_b) | ((r == imp_b) & (b < row_b))
                    cnt = cnt + ahead.astype(F32)
                return cnt

            cnt_ref[lo_row:lo_row + RANK_BAND, :] = lax.fori_loop(0, n_cand, rank_body,
                                                                   jnp.zeros((RANK_BAND, Q), F32))

    sel = ((cnt_ref[...] < SLC_TOPN) & (row_i <= cur)).astype(F32)

    sel_q = jnp.transpose(sel)
    near_blk = lane_i >= 2 * ci - 2
    m_far = jnp.where((sel_q > 0) & jnp.logical_not(near_blk), 0.0, NEG).astype(BF16)
    m_near = jnp.where((sel_q > 0) & near_blk, 0.0, NEG).astype(BF16)
    qa_far = jnp.concatenate([qs, jnp.concatenate([m_far] * H4, axis=0)], axis=1)
    qa_near = jnp.concatenate([qs, jnp.concatenate([m_near] * H4, axis=0)], axis=1)

    def slc_rows(kt0, nt):
        r0 = pl.multiple_of((kt0 + 1) * Q, Q)
        return ks_ref[pl.ds(r0, nt * Q), :], vs_ref[pl.ds(r0, nt * Q), :]

    def online_update(streams):
        m_new = [jnp.maximum(c[0], jnp.max(s, axis=0, keepdims=True)) for s, _, c in streams]
        alpha = [jnp.exp2(c[0] - mn) for (_, _, c), mn in zip(streams, m_new)]
        p = [jnp.exp2(s - mn) for (s, _, _), mn in zip(streams, m_new)]
        l_new = [a * c[1] + jnp.sum(pp, axis=0, keepdims=True) for a, pp, (_, _, c) in zip(alpha, p, streams)]
        pv = [_dot_tn(v, pp.astype(BF16)) for pp, (_, v, _) in zip(p, streams)]
        return [(mn, ln, a * c[2] + x) for mn, ln, a, x, (_, _, c) in zip(m_new, l_new, alpha, pv, streams)]

    carry0 = (jnp.full((1, H4 * Q), M_FLOOR, F32), jnp.zeros((1, H4 * Q), F32), jnp.zeros((HD, H4 * Q), F32))
    n_far = jnp.maximum(ci - 1, 0)

    def far_body(it, carries):
        kt0 = it * (2 * SLC_GROUP)
        k_a, v_a = slc_rows(kt0, SLC_GROUP)
        k_b, v_b = slc_rows(kt0 + SLC_GROUP, SLC_GROUP)
        s_a = _dot_nt(k_a, qa_far)
        s_b = _dot_nt(k_b, qa_far)
        return tuple(online_update([(s_a, v_a, carries[0]), (s_b, v_b, carries[1])]))

    (m_a, l_a, acc_a), (m_b, l_b, acc_b) = lax.fori_loop(
        0, (n_far + 2 * SLC_GROUP - 1) // (2 * SLC_GROUP), far_body, (carry0, carry0))
    m_ab = jnp.maximum(m_a, m_b)
    w_a = jnp.exp2(m_a - m_ab)
    w_b = jnp.exp2(m_b - m_ab)
    carry = (m_ab, w_a * l_a + w_b * l_b, w_a * acc_a + w_b * acc_b)
    bp_t = head_tiles(bp_ref)
    bd_t = head_tiles(bd_ref)
    k_n, v_n = slc_rows(ci - 1, 2)
    s_n = _dot_nt(k_n, qa_near) + jnp.concatenate([bp_t, bd_t], axis=0)
    (_, l_s, acc_s), = online_update([(s_n, v_n, carry)])
    o_slc = acc_s * (1.0 / jnp.where(l_s > 0, l_s, 1.0))

    kw = kw_ref[pl.ds(pl.multiple_of(ci * Q, Q), WINDOW + Q), :]
    vw = vw_ref[pl.ds(pl.multiple_of(ci * Q, Q), WINDOW + Q), :]
    s_w = _dot_nt(kw, qs)
    n_wt = WINDOW // Q
    in_window = lane_i < row_i
    pieces = [jnp.concatenate([jnp.where(in_window, s_w[0:Q, h * Q:(h + 1) * Q], NEG) for h in range(H4)], axis=1)]
    for t in range(1, n_wt - 1):
        pieces.append(s_w[t * Q:(t + 1) * Q])
    pieces.append(s_w[(n_wt - 1) * Q:n_wt * Q] + bp_t)
    pieces.append(s_w[n_wt * Q:(n_wt + 1) * Q] + bd_t)
    s_w = jnp.concatenate(pieces, axis=0)
    x_w = lax.broadcasted_iota(jnp.int32, (WINDOW + Q, 1), 0)
    s_w = jnp.where(x_w + ci * Q >= WINDOW, s_w, NEG)
    e_w, d_w = _col_softmax_stats(s_w)
    p_w = e_w * (1.0 / jnp.where(d_w > 0, d_w, 1.0))
    o_win = _dot_tn(vw, p_w.astype(BF16))

    gt = jnp.transpose(jax.nn.sigmoid(gl_ref[...]))
    outs = []
    for h in range(H4):
        base = (g * H4 + h) * 3
        sl = slice(h * Q, (h + 1) * Q)
        gc = gt_row(gt, base)
        gs = gt_row(gt, base + 1)
        gw = gt_row(gt, base + 2)
        o_h = gc * o_cmp[:, sl] + gs * o_slc[:, sl] + gw * o_win[:, sl]
        outs.append(jnp.transpose(o_h))
    o_ref[...] = jnp.concatenate(outs, axis=1).astype(o_ref.dtype)


def gt_row(gt, idx):
    rows = lax.broadcasted_iota(jnp.int32, gt.shape, 0)
    return jnp.sum(jnp.where(rows == idx, gt, 0.0), axis=0, keepdims=True)


def _nsa_attention(rel_bias, proj, proj_gate, kcmp, vcmp, ks, vs, kw, vw):
    T = proj.shape[0]
    G = NSA_KV_HEADS
    Q = Q_BLOCK
    ncp = T // CMP_STRIDE
    nch = T // Q
    assert nch % SLC_GROUP == 0 and T // SLC_LEN <= Q
    kernel = functools.partial(_nsa_kernel, ncp=ncp)
    full = lambda rows, w=HEAD_DIM: pl.BlockSpec((None, rows, w), lambda g, c: (g, 0, 0))
    return pl.pallas_call(
        kernel,
        out_shape=jax.ShapeDtypeStruct((T, NSA_HEADS * HEAD_DIM), BF16),
        grid=(G, nch),
        in_specs=[pl.BlockSpec(memory_space=pltpu.SMEM),
                  pl.BlockSpec((Q, NSA_HPG * HEAD_DIM), lambda g, c: (c, g)),
                  pl.BlockSpec((Q, LANES), lambda g, c: (c, 0)),
                  full(ncp + 2 * Q), full(ncp + 2 * Q),
                  full(T + Q + SLC_TAIL, HEAD_DIM + Q), full(T + Q + SLC_TAIL), full(T + WINDOW), full(T + WINDOW)],
        out_specs=pl.BlockSpec((Q, NSA_HPG * HEAD_DIM), lambda g, c: (c, g)),
        scratch_shapes=[pltpu.VMEM((NSA_HEADS, Q, Q), F32), pltpu.VMEM((NSA_HEADS, Q, Q), F32),
                        pltpu.VMEM((NSA_HEADS, Q, Q), F32), pltpu.VMEM((Q, Q), F32), pltpu.VMEM((Q, Q), F32)],
        compiler_params=_cparams(("arbitrary", "arbitrary")),
        name="nsa_attention",
    )(rel_bias, proj, proj_gate, kcmp, vcmp, ks, vs, kw, vw)


def _softplus(z):
    return jnp.maximum(z, 0.0) + jnp.log1p(jnp.exp(-jnp.abs(z)))


def _lru_kernel(x_ref, y_ref, cw_ref, cb_ref, wa_ref, ba_ref, wi_ref, bi_ref, lam_ref, o_ref,
                xbuf, hc, a_s, b_s, h_s, *, tb):
    i = pl.program_id(0)

    @pl.when(i == 0)
    def _():
        xbuf[0:8, :] = jnp.zeros((8, LRU_WIDTH), F32)
        hc[...] = jnp.zeros((1, LRU_WIDTH), F32)

    xbuf[8:8 + tb, :] = x_ref[...]
    xc = cb_ref[...]
    for w in range(CONV_W):
        xc = xc + cw_ref[w:w + 1, :] * xbuf[8 - (CONV_W - 1) + w:8 - (CONV_W - 1) + w + tb, :]
    xbuf[0:8, :] = xbuf[tb:tb + 8, :]

    xcb = xc.astype(BF16)
    ra, ia = [], []
    for n in range(LRU_BLOCKS):
        xg = xcb[:, n * LRU_BW:(n + 1) * LRU_BW]
        ra.append(_dot(xg, wa_ref[n].astype(BF16)))
        ia.append(_dot(xg, wi_ref[n].astype(BF16)))
    r_gate = jax.nn.sigmoid(jnp.concatenate(ra, axis=1) + ba_ref[...])
    i_gate = jax.nn.sigmoid(jnp.concatenate(ia, axis=1) + bi_ref[...])
    log_a = (-LRU_C * r_gate) * _softplus(-lam_ref[...])
    a = jnp.exp(log_a)
    a_s[...] = a
    b_s[...] = jnp.sqrt(-jnp.tanh(log_a) * (a * a + 1.0)) * (i_gate * xc)

    rows = lax.broadcasted_iota(jnp.int32, (8, LRU_WIDTH), 0)

    def tile(k, h):
        r0 = pl.multiple_of(k * 8, 8)
        A = a_s[pl.ds(r0, 8), :]
        B = b_s[pl.ds(r0, 8), :]
        for sh in (1, 2, 4):
            ok = rows >= sh
            A_p = pltpu.roll(A, sh, axis=0)
            B_p = pltpu.roll(B, sh, axis=0)
            B = jnp.where(ok, A * B_p + B, B)
            A = jnp.where(ok, A * A_p, A)
        H = A * h + B
        h_s[pl.ds(r0, 8), :] = H
        return H[7:8, :]

    hc[...] = lax.fori_loop(0, tb // 8, tile, hc[...])
    o_ref[...] = (jax.nn.gelu(y_ref[...]) * h_s[...]).astype(o_ref.dtype)


def _rglru(proj, y_col, x_col, conv_w, conv_b, wa, ba, wi, bi, lam, tb=256):
    T = proj.shape[0]
    W = LRU_WIDTH
    vec = pl.BlockSpec((1, W), lambda i: (0, 0))
    blk = pl.BlockSpec((LRU_BLOCKS, LRU_BW, LRU_BW), lambda i: (0, 0, 0))
    return pl.pallas_call(
        functools.partial(_lru_kernel, tb=tb),
        out_shape=jax.ShapeDtypeStruct((T, W), BF16),
        grid=(T // tb,),
        in_specs=[pl.BlockSpec((tb, W), lambda i: (i, x_col)),
                  pl.BlockSpec((tb, W), lambda i: (i, y_col)),
                  pl.BlockSpec((CONV_W, W), lambda i: (0, 0)), vec, blk, vec, blk, vec, vec],
        out_specs=pl.BlockSpec((tb, W), lambda i: (i, 0)),
        scratch_shapes=[pltpu.VMEM((tb + 8, W), F32), pltpu.VMEM((1, W), F32),
                        pltpu.VMEM((tb, W), F32), pltpu.VMEM((tb, W), F32), pltpu.VMEM((tb, W), F32)],
        compiler_params=_cparams(("arbitrary",)),
        name="rglru",
    )(proj, proj, conv_w, conv_b.reshape(1, W), wa, ba.reshape(1, W), wi, bi.reshape(1, W), lam.reshape(1, W))


def _split3(x):
    hi = x.astype(BF16)
    r = x - hi.astype(F32)
    mid = r.astype(BF16)
    lo = (r - mid.astype(F32)).astype(BF16)
    return hi, mid, lo


def _hgrn_kernel(q_ref, f_ref, v_ref, g_ref, lbl_ref, ng_ref, o_ref, st_ref, oacc, *, tbh, layer):
    t = pl.program_id(1)
    C = HG_CHUNK
    SB = HG_SUB

    NH = HG_HEADS_PER_STEP
    DK, DV = HG_DK, HG_DV

    @pl.when(t == 0)
    def _():
        st_ref[...] = jnp.zeros((NH, DV, DK), F32)

    lg = lbl_ref[...]
    e = jnp.exp(lg - jnp.max(lg, axis=0, keepdims=True))
    sm = e / jnp.sum(e, axis=0, keepdims=True)
    cum = sm[0:1, :]
    for l in range(1, layer + 1):
        cum = cum + sm[l:l + 1, :]
    lb_all = cum - sm[0:1, :]

    tri = (lax.broadcasted_iota(jnp.int32, (C, C), 0) >= lax.broadcasted_iota(jnp.int32, (C, C), 1)).astype(BF16)
    rows8 = lax.broadcasted_iota(jnp.int32, (8, DK), 0)

    def head_chunk(r0, hh):
        cs = slice(hh * DK, (hh + 1) * DK)
        lb = lb_all[:, cs]
        qz = q_ref[pl.ds(r0, C), cs]
        qq = qz * jax.nn.sigmoid(qz)
        f = lb + (1.0 - lb) * jax.nn.sigmoid(f_ref[pl.ds(r0, C), cs])
        logf = jnp.log2(jnp.maximum(f, 1e-30))
        kk = 1.0 - f
        vv = v_ref[pl.ds(r0, C), cs]
        l_hi, l_mid, l_lo = _split3(logf)
        bcum = _dot(tri, l_hi) + _dot(tri, l_mid) + _dot(tri, l_lo)
        st = st_ref[hh]
        o_parts = []
        vb = vv.astype(BF16)
        for I in range(C // SB):
            lo_r = I * SB
            for a in range(SB // 8):
                t0 = lo_r + 8 * a
                q8 = qq[t0:t0 + 8]
                b8 = bcum[t0:t0 + 8]
                o8 = jnp.zeros((8, DV), F32)
                for s in range(lo_r, t0 + 8):
                    d = b8 - bcum[s:s + 1]
                    if s > t0:
                        d = jnp.where(rows8 >= s - t0, d, NEG)
                    w = q8 * (kk[s:s + 1] * jnp.exp2(d))
                    o8 = o8 + jnp.sum(w, axis=1, keepdims=True) * vv[s:s + 1]
                o_parts.append(o8)
        for I in range(1, C // SB):
            lo_r = I * SB
            ref = bcum[lo_r - 1:lo_r]
            qe = (qq[lo_r:lo_r + SB] * jnp.exp2(bcum[lo_r:lo_r + SB] - ref)).astype(BF16)
            ke = (kk[0:lo_r] * jnp.exp2(ref - bcum[0:lo_r])).astype(BF16)
            att = _dot_nt(qe, ke)
            o_off = _dot(att.astype(BF16), vb[0:lo_r])
            for a in range(SB // 8):
                o_parts[lo_r // 8 + a] = o_parts[lo_r // 8 + a] + o_off[8 * a:8 * a + 8]
        o_intra = jnp.concatenate(o_parts, axis=0)
        o_inter = _dot_nt((qq * jnp.exp2(bcum)).astype(BF16), st.astype(BF16))
        oacc[pl.ds(r0, C), cs] = o_inter + o_intra
        b_last = bcum[C - 1:C]
        kd = (kk * jnp.exp2(b_last - bcum)).astype(BF16)
        st_ref[hh] = st * jnp.exp2(b_last) + _dot_tn(vb, kd)

    def chunk(c, carry):
        r0 = pl.multiple_of(c * C, C)
        for hh in range(NH):
            head_chunk(r0, hh)
        return carry

    lax.fori_loop(0, tbh // C, chunk, 0)
    gz = g_ref[...]
    gate = (gz * jax.nn.sigmoid(gz)) * ng_ref[...]
    for hh in range(NH):
        cs = slice(hh * DV, (hh + 1) * DV)
        o = oacc[:, cs]
        o = o * lax.rsqrt(jnp.mean(o * o, axis=-1, keepdims=True) + EPS)
        o_ref[:, cs] = (o * gate[:, cs]).astype(o_ref.dtype)


def _hgrn2(proj, lb_logits, norm_g, layer, tbh=512):
    T = proj.shape[0]
    NH = HG_HEADS_PER_STEP
    HB = HG_HEADS // NH
    wk = NH * HG_DK
    col = lambda off: pl.BlockSpec((tbh, wk), lambda h, t: (t, off + h))
    return pl.pallas_call(
        functools.partial(_hgrn_kernel, tbh=tbh, layer=layer),
        out_shape=jax.ShapeDtypeStruct((T, HG_VW), BF16),
        grid=(HB, T // tbh),
        in_specs=[col(0), col(HB), col(2 * HB), col(3 * HB),
                  pl.BlockSpec((DEPTH, wk), lambda h, t: (0, h)),
                  pl.BlockSpec((1, wk), lambda h, t: (0, h))],
        out_specs=pl.BlockSpec((tbh, wk), lambda h, t: (t, h)),
        scratch_shapes=[pltpu.VMEM((NH, HG_DV, HG_DK), F32), pltpu.VMEM((tbh, wk), F32)],
        compiler_params=_cparams(("arbitrary", "arbitrary")),
        name="hgrn2",
    )(proj, proj, proj, proj, lb_logits, norm_g.reshape(1, HG_VW))


def _router_kernel(x_ref, g_ref, sc_ref, sh_ref, w_ref, b_ref, h_ref, eid_ref, gate_ref, cnt_ref):
    h = _norm_mod(x_ref[...], g_ref[...], sc_ref[...], sh_ref[...])
    tm, d = h.shape
    hb = h.astype(BF16)
    lo = lax.bitcast_convert_type(hb[:, :d // 2].astype(F32), jnp.uint32) >> 16
    hi = lax.bitcast_convert_type(hb[:, d // 2:].astype(F32), jnp.uint32)
    packed = hi | lo
    pt = d // 2 // LANES
    for c in range(pt):
        h_ref[pl.ds(c, tm, stride=pt), :] = packed[:, c * LANES:(c + 1) * LANES]
    h1, h2, h3 = _split3(h)
    w1, w2, w3 = _split3(w_ref[...])
    logits = (_dot(h1, w1) + (_dot(h1, w2) + _dot(h2, w1))
              + (_dot(h2, w2) + _dot(h1, w3) + _dot(h3, w1))) + b_ref[...]
    lane = lax.broadcasted_iota(jnp.int32, logits.shape, 1)
    is_g = lane < N_GROUPS
    glog = jnp.where(is_g, logits, -jnp.inf)
    gmax = jnp.max(glog, axis=-1, keepdims=True)
    grp = jnp.min(jnp.where(glog == gmax, lane, LANES), axis=-1, keepdims=True)
    gsum = jnp.sum(jnp.where(is_g, jnp.exp(glog - gmax), 0.0), axis=-1, keepdims=True)
    p_grp = 1.0 / gsum
    lo = N_GROUPS + EXP_PER_GROUP * grp
    el = jnp.where((lane >= lo) & (lane < lo + EXP_PER_GROUP), logits, -jnp.inf)
    v1 = jnp.max(el, axis=-1, keepdims=True)
    i1 = jnp.min(jnp.where(el == v1, lane, LANES), axis=-1, keepdims=True)
    el2 = jnp.where(lane == i1, -jnp.inf, el)
    v2 = jnp.max(el2, axis=-1, keepdims=True)
    i2 = jnp.min(jnp.where(el2 == v2, lane, LANES), axis=-1, keepdims=True)
    e2 = jnp.exp(v2 - v1)
    den = 1.0 + e2
    g1 = p_grp * (1.0 / den)
    g2 = p_grp * (e2 / den)
    eid_ref[...] = jnp.where(lane == 0, i1 - N_GROUPS, jnp.where(lane == 1, i2 - N_GROUPS, 0))
    gate_ref[...] = jnp.where(lane == 0, g1, jnp.where(lane == 1, g2, 0.0))

    @pl.when(pl.program_id(0) == 0)
    def _():
        cnt_ref[...] = jnp.zeros(cnt_ref.shape, F32)

    picked = ((lane == i1 - N_GROUPS) | (lane == i2 - N_GROUPS)).astype(F32)
    cnt_ref[...] += jnp.sum(picked, axis=0, keepdims=True)


def _dispatch_kernel(eid_ref, base_ref, dest_ref, tok_ref, carry, dstage, dsm, zbuf, sem, *, tb):
    i = pl.program_id(0)

    @pl.when(i == 0)
    def _():
        carry[...] = jnp.zeros(carry.shape, F32)
        zbuf[...] = jnp.zeros(zbuf.shape, jnp.int32)
        z = pltpu.make_async_copy(zbuf, tok_ref, sem)
        z.start()
        z.wait()

    e = eid_ref[...]
    lane = lax.broadcasted_iota(jnp.int32, e.shape, 1)
    oh0 = (lane == e[:, 0:1]).astype(F32)
    oh1 = (lane == e[:, 1:2]).astype(F32)
    both = oh0 + oh1
    before = (lax.broadcasted_iota(jnp.int32, (tb, tb), 1) < lax.broadcasted_iota(jnp.int32, (tb, tb), 0)).astype(BF16)
    prior = _dot(before, both.astype(BF16)) + (carry[...] + base_ref[...])
    d0 = jnp.sum(oh0 * prior, axis=1, keepdims=True)
    d1 = jnp.sum(oh1 * (prior + oh0), axis=1, keepdims=True)
    carry[...] += jnp.sum(both, axis=0, keepdims=True)
    dest = jnp.where(lane == 0, d0, jnp.where(lane == 1, d1, 0.0)).astype(jnp.int32)
    dest_ref[...] = dest

    dstage[...] = jnp.transpose(dest)[0:8, :]
    cp = pltpu.make_async_copy(dstage, dsm, sem)
    cp.start()
    cp.wait()

    def body(n, c):
        for k in range(TOPK_IN_GROUP):
            p = dsm[k, n]
            tok_ref[p >> 7, p & (LANES - 1)] = i * tb + n
        return c

    lax.fori_loop(0, tb, body, 0, unroll=4)


def _dispatch(eid_l, base, n_blk, tb=512):
    N = eid_l.shape[0]
    return pl.pallas_call(
        functools.partial(_dispatch_kernel, tb=tb),
        out_shape=[jax.ShapeDtypeStruct((N, LANES), jnp.int32), jax.ShapeDtypeStruct((n_blk, LANES), jnp.int32)],
        grid=(N // tb,),
        in_specs=[pl.BlockSpec((tb, LANES), lambda i: (i, 0)), pl.BlockSpec((1, LANES), lambda i: (0, 0))],
        out_specs=[pl.BlockSpec((tb, LANES), lambda i: (i, 0)), pl.BlockSpec(memory_space=pltpu.SMEM)],
        scratch_shapes=[pltpu.VMEM((1, LANES), F32), pltpu.VMEM((8, tb), jnp.int32), pltpu.SMEM((8, tb), jnp.int32),
                        pltpu.VMEM((n_blk, LANES), jnp.int32), pltpu.SemaphoreType.DMA(())],
        compiler_params=_cparams(("arbitrary",)),
        name="moe_dispatch",
    )(eid_l, base)


def _router(x, g, sc, sh, w_r, b_r, tm=512):
    T, D = x.shape
    vec = pl.BlockSpec((1, D), lambda i: (0, 0))
    return pl.pallas_call(
        _router_kernel,
        out_shape=[jax.ShapeDtypeStruct((T * PACKED_ROWS, LANES), jnp.uint32), jax.ShapeDtypeStruct((T, LANES), jnp.int32),
                   jax.ShapeDtypeStruct((T, LANES), F32), jax.ShapeDtypeStruct((1, LANES), F32)],
        grid=(T // tm,),
        in_specs=[pl.BlockSpec((tm, D), lambda i: (i, 0)), vec, vec, vec,
                  pl.BlockSpec((D, LANES), lambda i: (0, 0)), pl.BlockSpec((1, LANES), lambda i: (0, 0))],
        out_specs=[pl.BlockSpec((tm * PACKED_ROWS, LANES), lambda i: (i, 0)),
                   pl.BlockSpec((tm, LANES), lambda i: (i, 0)),
                   pl.BlockSpec((tm, LANES), lambda i: (i, 0)),
                   pl.BlockSpec((1, LANES), lambda i: (0, 0))],
        compiler_params=_cparams(("arbitrary",)),
        name="moe_router",
    )(x, g, sc, sh, w_r, b_r)


def _cast_rows(src_ref, slot, dst_ref, rows):
    def body(c, carry):
        r = pl.multiple_of(c * rows, rows)
        dst_ref[pl.ds(r, rows), :] = src_ref[slot, pl.ds(r, rows), :].astype(BF16)
        return carry
    lax.fori_loop(0, dst_ref.shape[0] // rows, body, 0, unroll=4)


def _expert_kernel(rid_ref, re_ref, nu_ref, tok_ref, h_ref, wg_hbm, wu_hbm, wd_hbm, y_ref,
                   wgf, wuf, wdf, wgb, wub, wdb, xbuf, wsem, xsem, *, layer):
    i = pl.program_id(0)
    n_used = nu_ref[0]
    R = MOE_BLOCK

    def w_copies(e, slot):
        return ((pltpu.make_async_copy(wg_hbm.at[layer, e], wgf.at[slot], wsem.at[slot, 0]), 1),
                (pltpu.make_async_copy(wu_hbm.at[layer, e], wuf.at[slot], wsem.at[slot, 1]), 1),
                (pltpu.make_async_copy(wd_hbm.at[layer, e], wdf.at[slot], wsem.at[slot, 2]), 1))

    HT = PACKED_ROWS

    def x_start(blk, slot):
        def body(r, c):
            tok = tok_ref[blk * R + r]
            pltpu.make_async_copy(h_ref.at[pl.ds(pl.multiple_of(tok * HT, HT), HT)],
                                  xbuf.at[slot, pl.ds(pl.multiple_of(r * HT, HT), HT)], xsem.at[slot]).start()
            return c
        lax.fori_loop(0, R, body, 0, unroll=8)

    def x_wait(slot):
        pltpu.make_async_copy(h_ref.at[pl.ds(0, R * HT)], xbuf.at[slot], xsem.at[slot]).wait()

    def w_request(run):
        @pl.when(re_ref[run] >= 0)
        def _():
            for c, pri in w_copies(re_ref[run], run % 2):
                c.start(priority=pri)

    @pl.when(i == 0)
    def _():
        w_request(0)
        w_request(1)
        x_start(0, 0)

    @pl.when(i < n_used)
    def _():
        slot = i % 2

        @pl.when(i + 1 < n_used)
        def _():
            x_start(i + 1, 1 - slot)

        run = rid_ref[i]

        @pl.when((i == 0) | (run != rid_ref[jnp.maximum(i - 1, 0)]))
        def _():
            ws = run % 2
            for c, _ in w_copies(re_ref[run], ws):
                c.wait()
            _cast_rows(wgf, ws, wgb, 64)
            _cast_rows(wuf, ws, wub, 64)
            _cast_rows(wdf, ws, wdb, 16)
            w_request(run + 2)

        x_wait(slot)
        words = [xbuf[slot, pl.ds(c, R, stride=HT), :] for c in range(HT)]
        lo = [lax.bitcast_convert_type(w << 16, F32) for w in words]
        hi = [lax.bitcast_convert_type(w & jnp.uint32(0xFFFF0000), F32) for w in words]
        x = jnp.concatenate(lo + hi, axis=1).astype(BF16)
        hg = _dot(x, wgb[...])
        hu = _dot(x, wub[...])
        hid = (hg * jax.nn.sigmoid(hg)) * hu
        y_ref[...] = _dot(hid.astype(BF16), wdb[...])

    @pl.when(i >= n_used)
    def _():
        y_ref[...] = jnp.zeros(y_ref.shape, y_ref.dtype)


def _experts(h, tok, run_id, run_e, n_used, w_gate, w_up, w_down, layer):
    D = D_MODEL
    n_pad = tok.shape[0]
    n_blk = n_pad // MOE_BLOCK
    any_spec = pl.BlockSpec(memory_space=pl.ANY)
    return pl.pallas_call(
        functools.partial(_expert_kernel, layer=layer),
        out_shape=jax.ShapeDtypeStruct((n_pad, D), F32),
        grid_spec=pltpu.PrefetchScalarGridSpec(
            num_scalar_prefetch=4,
            grid=(n_blk,),
            in_specs=[any_spec, any_spec, any_spec, any_spec],
            out_specs=pl.BlockSpec((MOE_BLOCK, D), lambda i, *_: (i, 0)),
            scratch_shapes=[pltpu.VMEM((2, D, D_EXPERT), F32), pltpu.VMEM((2, D, D_EXPERT), F32),
                            pltpu.VMEM((2, D_EXPERT, D), F32),
                            pltpu.VMEM((D, D_EXPERT), BF16), pltpu.VMEM((D, D_EXPERT), BF16),
                            pltpu.VMEM((D_EXPERT, D), BF16),
                            pltpu.VMEM((2, MOE_BLOCK * PACKED_ROWS, LANES), jnp.uint32),
                            pltpu.SemaphoreType.DMA((2, 4)), pltpu.SemaphoreType.DMA((2,))]),
        compiler_params=_cparams(("arbitrary",)),
        name="moe_experts",
    )(run_id, run_e, n_used, tok, h, w_gate, w_up, w_down)


def _combine_kernel(slot_ref, y_ref, x_ref, w_ref, g_ref, fg_ref, o_ref, buf, sem, *, rows, final_norm):
    i = pl.program_id(0)

    def copy(r, k):
        return pltpu.make_async_copy(y_ref.at[pl.ds(slot_ref[(i * rows + r) * 2 + k], 1)],
                                     buf.at[k, pl.ds(r, 1)], sem)

    def start(r, c):
        copy(r, 0).start()
        copy(r, 1).start()
        return c

    lax.fori_loop(0, rows, start, 0, unroll=8)
    for k in range(TOPK_IN_GROUP):
        pltpu.make_async_copy(y_ref.at[pl.ds(0, rows)], buf.at[k], sem).wait()
    w = w_ref[...]
    moe = buf[0] * w[:, 0:1] + buf[1] * w[:, 1:2]
    y = x_ref[...] + g_ref[...] * moe
    if final_norm:
        y = (y * lax.rsqrt(jnp.mean(y * y, axis=-1, keepdims=True) + EPS)) * fg_ref[...]
    o_ref[...] = y


def _combine(yb, slots, x, gate_w, g2, final_g, final_norm, rows=256):
    T, D = x.shape
    return pl.pallas_call(
        functools.partial(_combine_kernel, rows=rows, final_norm=final_norm),
        out_shape=jax.ShapeDtypeStruct((T, D), F32),
        grid_spec=pltpu.PrefetchScalarGridSpec(
            num_scalar_prefetch=1,
            grid=(T // rows,),
            in_specs=[pl.BlockSpec(memory_space=pl.ANY),
                      pl.BlockSpec((rows, D), lambda i, s: (i, 0)),
                      pl.BlockSpec((rows, LANES), lambda i, s: (i, 0)),
                      pl.BlockSpec((1, D), lambda i, s: (0, 0)),
                      pl.BlockSpec((1, D), lambda i, s: (0, 0))],
            out_specs=pl.BlockSpec((rows, D), lambda i, s: (i, 0)),
            scratch_shapes=[pltpu.VMEM((2, rows, D), F32), pltpu.SemaphoreType.DMA(())]),
        compiler_params=_cparams(("arbitrary",)),
        name="moe_combine",
    )(slots.reshape(-1), yb, x, gate_w, g2, final_g)


def _hier_moe_residual(x, g, sc, sh, g2, w_grp, b_grp, w_exp, b_exp, w_gate, w_up, w_down, layer, final_g,
                       final_norm):
    N, D = x.shape
    n_route = N_GROUPS + N_EXPERTS
    w_r = jnp.zeros((D, LANES), F32).at[:, :N_GROUPS].set(w_grp).at[:, N_GROUPS:n_route].set(w_exp)
    b_r = jnp.zeros((1, LANES), F32).at[0, :N_GROUPS].set(b_grp).at[0, N_GROUPS:n_route].set(b_exp)
    h, eid_l, gate_l, cnt = _router(x, g, sc, sh, w_r, b_r)

    A = N * TOPK_IN_GROUP
    counts = cnt[0, :N_EXPERTS].astype(jnp.int32)
    padded = (counts + MOE_BLOCK - 1) // MOE_BLOCK * MOE_BLOCK
    pad_end = jnp.cumsum(padded)
    pad_start = pad_end - padded
    n_blk = -(-(A + N_EXPERTS * MOE_BLOCK) // MOE_BLOCK)
    base = jnp.zeros((1, LANES), F32).at[0, :N_EXPERTS].set(pad_start.astype(F32))
    dest_l, tok_tbl = _dispatch(eid_l, base, n_blk)
    buf_tok = tok_tbl.reshape(-1)
    blk_idx = jnp.arange(n_blk, dtype=jnp.int32)
    n_used = pad_end[-1] // MOE_BLOCK
    blk_e = jnp.minimum(jnp.sum((pad_end[None, :] <= (blk_idx * MOE_BLOCK)[:, None]).astype(jnp.int32), axis=1),
                        N_EXPERTS - 1)
    blk_e = jnp.where(blk_idx < n_used, blk_e, blk_e[n_used - 1])
    change = jnp.concatenate([jnp.ones((1,), bool), blk_e[1:] != blk_e[:-1]])
    run_id = (jnp.cumsum(change.astype(jnp.int32)) - 1).astype(jnp.int32)
    runs = jnp.arange(n_blk + 2, dtype=jnp.int32)
    run_e = jnp.max(jnp.where(run_id[None, :] == runs[:, None], blk_e[None, :], -1), axis=1).astype(jnp.int32)
    slots = dest_l[:, :TOPK_IN_GROUP]

    yb = _experts(h, buf_tok, run_id, run_e, n_used.reshape(1).astype(jnp.int32), w_gate, w_up, w_down, layer)
    return _combine(yb, slots, x, gate_l, g2, final_g, final_norm)


def _even_mixer_residual(x, g, sc, sh, g1, rel_bias, w_in, w_out, cmp_pe, cmp_w1, cmp_b1, cmp_w2, cmp_b2,
                         conv_w, conv_b, lru_wa, lru_ba, lru_wi, lru_bi, lru_lambda):
    T, D = x.shape
    G, HD = NSA_KV_HEADS, HEAD_DIM
    nq = NSA_HEADS * HD
    n_kv = 6 * NSA_KV_W
    n_gate = 3 * NSA_HEADS
    w_main = jnp.concatenate([w_in[:, :nq], w_in[:, nq + n_kv + n_gate:], w_in[:, nq:nq + n_kv]],
                             axis=1).astype(BF16)
    w_gl = jnp.zeros((D, LANES), F32).at[:, :n_gate].set(w_in[:, nq + n_kv:nq + n_kv + n_gate]).astype(BF16)
    kv0 = nq + 2 * LRU_WIDTH
    tn = 512
    proj, proj_gate, kv_b = _norm_proj(x, g, sc, sh, w_main, w_gl, bf16_from=kv0 // tn, tn=tn)

    def kv_heads(j, src=kv_b, c_base=0):
        c0 = c_base + j * NSA_KV_W
        return src[:, c0:c0 + NSA_KV_W].reshape(T, G, HD).transpose(1, 0, 2)

    cmp_in = jnp.stack([kv_heads(0, proj, kv0), kv_heads(1, proj, kv0)], axis=0)
    cmp_out = _compress(cmp_in, cmp_pe, cmp_w1, cmp_b1, cmp_w2, cmp_b2)
    ncp = T // CMP_STRIDE
    valid = (jnp.arange(ncp) < ncp - 1)[None, None, :, None]
    cmp_pad = jnp.pad(jnp.where(valid, cmp_out, 0.0), ((0, 0), (0, 0), (Q_BLOCK, Q_BLOCK), (0, 0)))
    slc_pad = ((0, 0), (Q_BLOCK, SLC_TAIL), (0, 0))
    blk_of_row = jnp.pad(jnp.arange(T, dtype=jnp.int32) // SLC_LEN, (Q_BLOCK, SLC_TAIL), constant_values=Q_BLOCK - 1)
    blk_onehot = (blk_of_row[:, None] == jnp.arange(Q_BLOCK, dtype=jnp.int32)[None, :]).astype(BF16)
    ks = jnp.concatenate([jnp.pad(kv_heads(2), slc_pad), jnp.broadcast_to(blk_onehot, (G,) + blk_onehot.shape)], axis=2)
    vs = jnp.pad(kv_heads(3), slc_pad)
    kw = jnp.pad(kv_heads(4), ((0, 0), (WINDOW, 0), (0, 0)))
    vw = jnp.pad(kv_heads(5), ((0, 0), (WINDOW, 0), (0, 0)))
    nsa_out = _nsa_attention(rel_bias, proj, proj_gate, cmp_pad[0], cmp_pad[1], ks, vs, kw, vw)

    y_col = nq // LRU_WIDTH
    lru_out = _rglru(proj, y_col, y_col + 1, conv_w, conv_b, lru_wa, lru_ba, lru_wi, lru_bi, lru_lambda)
    w_o = w_out.astype(BF16)
    return _out_proj(nsa_out, lru_out, w_o, x, g1)


def _odd_mixer_residual(x, g, sc, sh, g1, lb_logits, w_in, w_out, norm_g, layer):
    proj = _norm_proj(x, g, sc, sh, w_in.astype(BF16))[0]
    o = _hgrn2(proj, lb_logits, norm_g, layer)
    w_o = w_out.astype(BF16)
    return _out_proj(o, o, w_o, x, g1, a2_col=1)


def kernel(x, c, rel_bias, ada_w, ada_b, norm_mix_g, norm_ffn_g, ev_w_in, ev_w_out, cmp_pe, cmp_w1, cmp_b1, cmp_w2, cmp_b2, lru_conv_w, lru_conv_b, lru_wa, lru_ba, lru_wi, lru_bi, lru_lambda, od_w_in, od_w_out, hg_lb_logits, hg_norm_g, moe_w_grp, moe_b_grp, moe_w_exp, moe_b_exp, moe_w_gate, moe_w_up, moe_w_down, final_g):
    B, T, D = x.shape
    assert B == 1 and D == D_MODEL
    xt = x.reshape(T, D)
    mod = _modulation(c, ada_w, ada_b)
    for l in range(DEPTH):
        sh1, sc1, g1, sh2, sc2, g2 = [mod[l, :, k * D:(k + 1) * D] for k in range(6)]
        gm = norm_mix_g[l].reshape(1, D)
        gf = norm_ffn_g[l].reshape(1, D)
        j = l // 2
        if l % 2 == 0:
            xt = _even_mixer_residual(xt, gm, sc1, sh1, g1, rel_bias, ev_w_in[j], ev_w_out[j], cmp_pe[j],
                                      cmp_w1[j], cmp_b1[j], cmp_w2[j], cmp_b2[j], lru_conv_w[j],
                                      lru_conv_b[j], lru_wa[j], lru_ba[j], lru_wi[j], lru_bi[j], lru_lambda[j])
        else:
            xt = _odd_mixer_residual(xt, gm, sc1, sh1, g1, hg_lb_logits, od_w_in[j], od_w_out[j],
                                     hg_norm_g[j], l)
        xt = _hier_moe_residual(xt, gf, sc2, sh2, g2, moe_w_grp[l], moe_b_grp[l], moe_w_exp[l], moe_b_exp[l],
                                moe_w_gate, moe_w_up, moe_w_down, l, final_g.reshape(1, D), l == DEPTH - 1)
    return xt.reshape(B, T, D)
```

```python
import functools
import math

import numpy as np
import jax
import jax.numpy as jnp
from jax import lax
from jax.experimental import pallas as pl
from jax.experimental.pallas import tpu as pltpu

F32 = jnp.float32
BF16 = jnp.bfloat16

D_MODEL = 2048
DEPTH = 2
NSA_HEADS = 8
NSA_KV_HEADS = 2
NSA_HPG = NSA_HEADS // NSA_KV_HEADS
HEAD_DIM = 128
NSA_KV_W = NSA_KV_HEADS * HEAD_DIM
CMP_LEN = 32
CMP_STRIDE = 16
SLC_LEN = 64
SLC_TOPN = 16
WINDOW = 512
Q_BLOCK = 128
LRU_WIDTH = 1024
LRU_BLOCKS = 8
LRU_BW = LRU_WIDTH // LRU_BLOCKS
CONV_W = 4
LRU_C = 8.0
HG_HEADS = 16
HG_DK = 128
HG_DV = 128
HG_CHUNK = 64
HG_SUB = 16
HG_HEADS_PER_STEP = 8
SLC_GROUP = 4
RANK_UNROLL = 4
RANK_BAND = 32
SLC_TAIL = 2 * SLC_GROUP * Q_BLOCK
HG_KW = HG_HEADS * HG_DK
HG_VW = HG_HEADS * HG_DV
N_BUCKETS = 32
MAX_DIST = 128
N_GROUPS = 8
EXP_PER_GROUP = 8
N_EXPERTS = N_GROUPS * EXP_PER_GROUP
TOPK_IN_GROUP = 2
D_EXPERT = 512
MOE_BLOCK = 128
EPS = 1e-6

LANES = 128
PACKED_ROWS = D_MODEL // 2 // LANES
NEG = -1e30
M_FLOOR = -1e20
LOG2E = math.log2(math.e)
VMEM_LIMIT = 56 * 1024 * 1024


def _cparams(sem):
    return pltpu.CompilerParams(dimension_semantics=sem, vmem_limit_bytes=VMEM_LIMIT)


def _dot(a, b):
    return jnp.dot(a, b, preferred_element_type=F32)


def _dot_nt(a, b):
    return lax.dot_general(a, b, (((1,), (1,)), ((), ())), preferred_element_type=F32)


def _dot_tn(a, b):
    return lax.dot_general(a, b, (((0,), (0,)), ((), ())), preferred_element_type=F32)


def _bucket_starts():
    n = np.arange(0, MAX_DIST + 1, dtype=np.int32)
    max_exact = N_BUCKETS // 2
    nf = np.maximum(n, 1).astype(np.float32)
    large = max_exact + (np.log(nf / np.float32(max_exact)) / np.float32(math.log(MAX_DIST / max_exact))
                         * np.float32(N_BUCKETS - max_exact)).astype(np.int32)
    large = np.minimum(large, N_BUCKETS - 1)
    b = np.where(n < max_exact, n, large)
    starts = [int(np.argmax(b >= k)) for k in range(N_BUCKETS)]
    assert all(b[s] == k for k, s in enumerate(starts)) and b[-1] == N_BUCKETS - 1
    return starts


BUCKET_STARTS = _bucket_starts()


def _mod_kernel(c_ref, w_ref, b_ref, o_ref):
    c = c_ref[...]
    cact = c * jax.nn.sigmoid(c)
    o_ref[...] = jnp.sum(cact * w_ref[...], axis=0, keepdims=True) + b_ref[...]


def _modulation(c, ada_w, ada_b):
    D = D_MODEL
    tn = 1024
    n_out = 6 * D
    c_col = c.reshape(D, 1)
    return pl.pallas_call(
        _mod_kernel,
        out_shape=jax.ShapeDtypeStruct((DEPTH, 1, n_out), F32),
        grid=(DEPTH, n_out // tn),
        in_specs=[pl.BlockSpec((D, 1), lambda l, j: (0, 0)),
                  pl.BlockSpec((None, D, tn), lambda l, j: (l, 0, j)),
                  pl.BlockSpec((None, 1, tn), lambda l, j: (l, 0, j))],
        out_specs=pl.BlockSpec((None, 1, tn), lambda l, j: (l, 0, j)),
        compiler_params=_cparams(("arbitrary", "arbitrary")),
        name="adaln_mod",
    )(c_col, ada_w, ada_b.reshape(DEPTH, 1, n_out))


def _norm_mod(x, g, sc, sh):
    y = x * lax.rsqrt(jnp.mean(x * x, axis=-1, keepdims=True) + EPS)
    return (y * g) * (1.0 + sc) + sh


def _norm_proj_kernel(x_ref, g_ref, sc_ref, sh_ref, w_ref, *rest, has_extra, bf16_from):
    if has_extra:
        wx_ref, o_ref, ox_ref, ob_ref, h_ref = rest
    else:
        o_ref, h_ref = rest
    j = pl.program_id(1)

    @pl.when(j == 0)
    def _():
        h = _norm_mod(x_ref[...], g_ref[...], sc_ref[...], sh_ref[...]).astype(BF16)
        h_ref[...] = h
        if has_extra:
            ox_ref[...] = _dot(h, wx_ref[...])

    y = _dot(h_ref[...], w_ref[...])
    o_ref[...] = y
    if has_extra:
        @pl.when(j >= bf16_from)
        def _():
            ob_ref[...] = y.astype(BF16)


def _norm_proj(x, g, sc, sh, w, w_extra=None, bf16_from=None, tm=1024, tn=512):
    T, D = x.shape
    N = w.shape[1]
    has_extra = w_extra is not None
    vec = pl.BlockSpec((1, D), lambda i, j: (0, 0))
    in_specs = [pl.BlockSpec((tm, D), lambda i, j: (i, 0)), vec, vec, vec,
                pl.BlockSpec((D, tn), lambda i, j: (0, j))]
    out_shape = [jax.ShapeDtypeStruct((T, N), F32)]
    out_specs = [pl.BlockSpec((tm, tn), lambda i, j: (i, j))]
    args = [x, g, sc, sh, w]
    if has_extra:
        nx = w_extra.shape[1]
        in_specs.append(pl.BlockSpec((D, nx), lambda i, j: (0, 0)))
        out_shape.append(jax.ShapeDtypeStruct((T, nx), F32))
        out_specs.append(pl.BlockSpec((tm, nx), lambda i, j: (i, 0)))
        args.append(w_extra)
        out_shape.append(jax.ShapeDtypeStruct((T, N - bf16_from * tn), BF16))
        out_specs.append(pl.BlockSpec((tm, tn), lambda i, j: (i, jnp.maximum(j - bf16_from, 0))))
    return pl.pallas_call(
        functools.partial(_norm_proj_kernel, has_extra=has_extra, bf16_from=bf16_from),
        out_shape=out_shape,
        grid=(T // tm, N // tn),
        in_specs=in_specs,
        out_specs=out_specs,
        scratch_shapes=[pltpu.VMEM((tm, D), BF16)],
        compiler_params=_cparams(("arbitrary", "arbitrary")),
        name="norm_proj",
    )(*args)


def _out_proj_kernel(a1_ref, a2_ref, w1_ref, w2_ref, x_ref, g_ref, o_ref):
    y = _dot(a1_ref[...], w1_ref[...]) + _dot(a2_ref[...], w2_ref[...])
    o_ref[...] = x_ref[...] + g_ref[...] * y


def _out_proj(a1, a2, w, x, gate, a2_col=0, tm=1024, tn=1024):
    T, D = x.shape
    K1 = K2 = w.shape[0] // 2
    return pl.pallas_call(
        _out_proj_kernel,
        out_shape=jax.ShapeDtypeStruct((T, D), F32),
        grid=(T // tm, D // tn),
        in_specs=[pl.BlockSpec((tm, K1), lambda i, j: (i, 0)),
                  pl.BlockSpec((tm, K2), lambda i, j: (i, a2_col)),
                  pl.BlockSpec((K1, tn), lambda i, j: (0, j)),
                  pl.BlockSpec((K2, tn), lambda i, j: (1, j)),
                  pl.BlockSpec((tm, tn), lambda i, j: (i, j)),
                  pl.BlockSpec((1, tn), lambda i, j: (0, j))],
        out_specs=pl.BlockSpec((tm, tn), lambda i, j: (i, j)),
        compiler_params=_cparams(("arbitrary", "arbitrary")),
        name="out_proj",
    )(a1, a2, w, w, x, gate)


def _compress_kernel(k2_ref, pe_ref, w1_ref, b1_ref, w2_ref, b2_ref, o_ref):
    half = (CMP_LEN // 2) * HEAD_DIM
    k2 = k2_ref[...]
    a = _dot((k2 + pe_ref[0:1, :]).astype(BF16), w1_ref[0:half, :].astype(BF16))
    b = _dot((k2 + pe_ref[1:2, :]).astype(BF16), w1_ref[half:2 * half, :].astype(BF16))
    nb = k2.shape[0]
    b_up = pltpu.roll(b, nb - 1, axis=0)
    hid = jax.nn.gelu(a + b_up + b1_ref[...])
    o_ref[...] = _dot(hid.astype(BF16), w2_ref[...].astype(BF16)) + b2_ref[...]


def _compress(kv, pe, w1, b1, w2, b2):
    _, G, T, HD = kv.shape
    nb = T // CMP_STRIDE
    row = CMP_STRIDE * HD
    kv2 = kv.reshape(2, G, nb, row)
    pe2 = pe.reshape(2, 2, row)
    return pl.pallas_call(
        _compress_kernel,
        out_shape=jax.ShapeDtypeStruct((2, G, nb, HD), F32),
        grid=(2, G),
        in_specs=[pl.BlockSpec((None, None, nb, row), lambda a, g: (a, g, 0, 0)),
                  pl.BlockSpec((None, 2, row), lambda a, g: (a, 0, 0)),
                  pl.BlockSpec((None, CMP_LEN * HD, HD), lambda a, g: (a, 0, 0)),
                  pl.BlockSpec((None, 1, HD), lambda a, g: (a, 0, 0)),
                  pl.BlockSpec((None, HD, HD), lambda a, g: (a, 0, 0)),
                  pl.BlockSpec((None, 1, HD), lambda a, g: (a, 0, 0))],
        out_specs=pl.BlockSpec((None, None, nb, HD), lambda a, g: (a, g, 0, 0)),
        compiler_params=_cparams(("arbitrary", "arbitrary")),
        name="nsa_compress",
    )(kv2, pe2, w1, b1.reshape(2, 1, HD), w2, b2.reshape(2, 1, HD))


def _bias_from_rel(rel, tbl_ref, head):
    far = tbl_ref[N_BUCKETS - 1, head]
    val = jnp.zeros(rel.shape, F32)
    for b in range(N_BUCKETS - 2, -1, -1):
        val = jnp.where(rel < BUCKET_STARTS[b + 1], (tbl_ref[b, head] - far) * LOG2E, val)
    return val


def _lane_tile4(x):
    return jnp.concatenate([x, x, x, x], axis=1)


def _col_softmax_stats(s):
    m = jnp.maximum(jnp.max(s, axis=0, keepdims=True), M_FLOOR)
    e = jnp.exp2(s - m)
    d = jnp.sum(e, axis=0, keepdims=True)
    return e, d


def _nsa_kernel(tbl_ref, q_ref, gl_ref, kc_ref, vc_ref, ks_ref, vs_ref, kw_ref, vw_ref, o_ref,
                bd_ref, bp_ref, bn_ref, imp_ref, cnt_ref, *, ncp):
    g = pl.program_id(0)
    ci = pl.program_id(1)
    Q = Q_BLOCK
    HD = HEAD_DIM
    H4 = NSA_HPG

    row_i = lax.broadcasted_iota(jnp.int32, (Q, Q), 0)
    lane_i = lax.broadcasted_iota(jnp.int32, (Q, Q), 1)

    @pl.when((g == 0) & (ci == 0))
    def _build_bias_tiles():
        for h in range(NSA_HEADS):
            rel_d = lane_i - row_i
            bd_ref[h] = jnp.where(rel_d >= 0, _bias_from_rel(jnp.maximum(rel_d, 0), tbl_ref, h), NEG)
            bp_ref[h] = _bias_from_rel(lane_i - row_i + Q, tbl_ref, h)
            rel_n = lane_i - CMP_STRIDE * row_i + (CMP_STRIDE * (Q - 8) - (CMP_LEN - 1))
            bn_ref[h] = jnp.where(rel_n >= 0, _bias_from_rel(jnp.maximum(rel_n, 0), tbl_ref, h), NEG)

    def head_tiles(ref):
        return jnp.concatenate([ref[g * H4 + h] for h in range(H4)], axis=1)

    qt = q_ref[...] * (HD ** -0.5 * LOG2E)
    qs = jnp.concatenate([qt[:, h * HD:(h + 1) * HD] for h in range(H4)], axis=0).astype(BF16)

    near0 = 8 * ci - (Q - 8)
    kc_far = kc_ref[pl.ds(Q, ncp), :]
    vc_far = vc_ref[pl.ds(Q, ncp), :]
    near_row = pl.multiple_of(8 * ci + 8, 8)
    kc_near = kc_ref[pl.ds(near_row, Q), :]
    vc_near = vc_ref[pl.ds(near_row, Q), :]
    n_far = lax.broadcasted_iota(jnp.int32, (ncp, 1), 0)
    s_far = jnp.where(n_far < near0, _dot_nt(kc_far.astype(BF16), qs), NEG)
    m_near = lax.broadcasted_iota(jnp.int32, (Q, 1), 0)
    s_near = _dot_nt(kc_near.astype(BF16), qs) + head_tiles(bn_ref)
    s_near = jnp.where(m_near + near0 >= 0, s_near, NEG)
    s_c = jnp.concatenate([s_far, s_near], axis=0)
    e_c, d_c = _col_softmax_stats(s_c)
    p_c = e_c * (1.0 / jnp.where(d_c > 0, d_c, 1.0))
    v_c = jnp.concatenate([vc_far, vc_near], axis=0).astype(BF16)
    o_cmp = _dot_tn(v_c, p_c.astype(BF16))

    psum = p_c[:, 0:Q] + p_c[:, Q:2 * Q] + p_c[:, 2 * Q:3 * Q] + p_c[:, 3 * Q:4 * Q]
    p_hi = psum.astype(BF16)
    p_lo = (psum - p_hi.astype(F32)).astype(BF16)
    jb = lax.broadcasted_iota(jnp.int32, (Q, ncp), 0)
    nb = lax.broadcasted_iota(jnp.int32, (Q, ncp), 1)
    ratio = SLC_LEN // CMP_STRIDE
    span = CMP_LEN // CMP_STRIDE - 1
    cover_far = ((nb >= ratio * jb - span) & (nb <= ratio * jb + ratio - 1)).astype(BF16)
    nn = lane_i + near0
    cover_near = ((nn >= ratio * row_i - span) & (nn <= ratio * row_i + ratio - 1)).astype(BF16)
    cover = jnp.concatenate([cover_far, cover_near], axis=1)
    imp = _dot(cover, p_hi) + _dot(cover, p_lo)
    cur = 2 * ci + (lane_i >= SLC_LEN).astype(jnp.int32)
    imp = jnp.where(row_i == cur, jnp.inf, jnp.where(row_i > cur, -jnp.inf, imp))
    imp_ref[...] = imp

    n_cand = (2 * ci + 2 + RANK_UNROLL - 1) // RANK_UNROLL
    cnt_ref[...] = jnp.zeros((Q, Q), F32)
    for band in range(Q // RANK_BAND):
        lo_row = band * RANK_BAND

        @pl.when(lo_row <= 2 * ci + 1)
        def _(lo_row=lo_row):
            imp_b = imp[lo_row:lo_row + RANK_BAND]
            row_b = row_i[lo_row:lo_row + RANK_BAND]

            def rank_body(it, cnt):
                for u in range(RANK_UNROLL):
                    b = it * RANK_UNROLL + u
                    r = imp_ref[pl.ds(b, 1), :]
                    ahead = (r > imp_b) | ((r == imp_b) & (b < row_b))
                    cnt = cnt + ahead.astype(F32)
                return cnt

            cnt_ref[lo_row:lo_row + RANK_BAND, :] = lax.fori_loop(0, n_cand, rank_body,
                                                                   jnp.zeros((RANK_BAND, Q), F32))

    sel = ((cnt_ref[...] < SLC_TOPN) & (row_i <= cur)).astype(F32)

    sel_q = jnp.transpose(sel)
    near_blk = lane_i >= 2 * ci - 2
    m_far = jnp.where((sel_q > 0) & jnp.logical_not(near_blk), 0.0, NEG).astype(BF16)
    m_near = jnp.where((sel_q > 0) & near_blk, 0.0, NEG).astype(BF16)
    qa_far = jnp.concatenate([qs, jnp.concatenate([m_far] * H4, axis=0)], axis=1)
    qa_near = jnp.concatenate([qs, jnp.concatenate([m_near] * H4, axis=0)], axis=1)

    def slc_rows(kt0, nt):
        r0 = pl.multiple_of((kt0 + 1) * Q, Q)
        return ks_ref[pl.ds(r0, nt * Q), :], vs_ref[pl.ds(r0, nt * Q), :]

    def online_update(streams):
        m_new = [jnp.maximum(c[0], jnp.max(s, axis=0, keepdims=True)) for s, _, c in streams]
        alpha = [jnp.exp2(c[0] - mn) for (_, _, c), mn in zip(streams, m_new)]
        p = [jnp.exp2(s - mn) for (s, _, _), mn in zip(streams, m_new)]
        l_new = [a * c[1] + jnp.sum(pp, axis=0, keepdims=True) for a, pp, (_, _, c) in zip(alpha, p, streams)]
        pv = [_dot_tn(v, pp.astype(BF16)) for pp, (_, v, _) in zip(p, streams)]
        return [(mn, ln, a * c[2] + x) for mn, ln, a, x, (_, _, c) in zip(m_new, l_new, alpha, pv, streams)]

    carry0 = (jnp.full((1, H4 * Q), M_FLOOR, F32), jnp.zeros((1, H4 * Q), F32), jnp.zeros((HD, H4 * Q), F32))
    n_far = jnp.maximum(ci - 1, 0)

    def far_body(it, carries):
        kt0 = it * (2 * SLC_GROUP)
        k_a, v_a = slc_rows(kt0, SLC_GROUP)
        k_b, v_b = slc_rows(kt0 + SLC_GROUP, SLC_GROUP)
        s_a = _dot_nt(k_a, qa_far)
        s_b = _dot_nt(k_b, qa_far)
        return tuple(online_update([(s_a, v_a, carries[0]), (s_b, v_b, carries[1])]))

    (m_a, l_a, acc_a), (m_b, l_b, acc_b) = lax.fori_loop(
        0, (n_far + 2 * SLC_GROUP - 1) // (2 * SLC_GROUP), far_body, (carry0, carry0))
    m_ab = jnp.maximum(m_a, m_b)
    w_a = jnp.exp2(m_a - m_ab)
    w_b = jnp.exp2(m_b - m_ab)
    carry = (m_ab, w_a * l_a + w_b * l_b, w_a * acc_a + w_b * acc_b)
    bp_t = head_tiles(bp_ref)
    bd_t = head_tiles(bd_ref)
    k_n, v_n = slc_rows(ci - 1, 2)
    s_n = _dot_nt(k_n, qa_near) + jnp.concatenate([bp_t, bd_t], axis=0)
    (_, l_s, acc_s), = online_update([(s_n, v_n, carry)])
    o_slc = acc_s * (1.0 / jnp.where(l_s > 0, l_s, 1.0))

    kw = kw_ref[pl.ds(pl.multiple_of(ci * Q, Q), WINDOW + Q), :]
    vw = vw_ref[pl.ds(pl.multiple_of(ci * Q, Q), WINDOW + Q), :]
    s_w = _dot_nt(kw, qs)
    n_wt = WINDOW // Q
    in_window = lane_i < row_i
    pieces = [jnp.concatenate([jnp.where(in_window, s_w[0:Q, h * Q:(h + 1) * Q], NEG) for h in range(H4)], axis=1)]
    for t in range(1, n_wt - 1):
        pieces.append(s_w[t * Q:(t + 1) * Q])
    pieces.append(s_w[(n_wt - 1) * Q:n_wt * Q] + bp_t)
    pieces.append(s_w[n_wt * Q:(n_wt + 1) * Q] + bd_t)
    s_w = jnp.concatenate(pieces, axis=0)
    x_w = lax.broadcasted_iota(jnp.int32, (WINDOW + Q, 1), 0)
    s_w = jnp.where(x_w + ci * Q >= WINDOW, s_w, NEG)
    e_w, d_w = _col_softmax_stats(s_w)
    p_w = e_w * (1.0 / jnp.where(d_w > 0, d_w, 1.0))
    o_win = _dot_tn(vw, p_w.astype(BF16))

    gt = jnp.transpose(jax.nn.sigmoid(gl_ref[...]))
    outs = []
    for h in range(H4):
        base = (g * H4 + h) * 3
        sl = slice(h * Q, (h + 1) * Q)
        gc = gt_row(gt, base)
        gs = gt_row(gt, base + 1)
        gw = gt_row(gt, base + 2)
        o_h = gc * o_cmp[:, sl] + gs * o_slc[:, sl] + gw * o_win[:, sl]
        outs.append(jnp.transpose(o_h))
    o_ref[...] = jnp.concatenate(outs, axis=1).astype(o_ref.dtype)


def gt_row(gt, idx):
    rows = lax.broadcasted_iota(jnp.int32, gt.shape, 0)
    return jnp.sum(jnp.where(rows == idx, gt, 0.0), axis=0, keepdims=True)


def _nsa_attention(rel_bias, proj, proj_gate, kcmp, vcmp, ks, vs, kw, vw):
    T = proj.shape[0]
    G = NSA_KV_HEADS
    Q = Q_BLOCK
    ncp = T // CMP_STRIDE
    nch = T // Q
    assert nch % SLC_GROUP == 0 and T // SLC_LEN <= Q
    kernel = functools.partial(_nsa_kernel, ncp=ncp)
    full = lambda rows, w=HEAD_DIM: pl.BlockSpec((None, rows, w), lambda g, c: (g, 0, 0))
    return pl.pallas_call(
        kernel,
        out_shape=jax.ShapeDtypeStruct((T, NSA_HEADS * HEAD_DIM), BF16),
        grid=(G, nch),
        in_specs=[pl.BlockSpec(memory_space=pltpu.SMEM),
                  pl.BlockSpec((Q, NSA_HPG * HEAD_DIM), lambda g, c: (c, g)),
                  pl.BlockSpec((Q, LANES), lambda g, c: (c, 0)),
                  full(ncp + 2 * Q), full(ncp + 2 * Q),
                  full(T + Q + SLC_TAIL, HEAD_DIM + Q), full(T + Q + SLC_TAIL), full(T + WINDOW), full(T + WINDOW)],
        out_specs=pl.BlockSpec((Q, NSA_HPG * HEAD_DIM), lambda g, c: (c, g)),
        scratch_shapes=[pltpu.VMEM((NSA_HEADS, Q, Q), F32), pltpu.VMEM((NSA_HEADS, Q, Q), F32),
                        pltpu.VMEM((NSA_HEADS, Q, Q), F32), pltpu.VMEM((Q, Q), F32), pltpu.VMEM((Q, Q), F32)],
        compiler_params=_cparams(("arbitrary", "arbitrary")),
        name="nsa_attention",
    )(rel_bias, proj, proj_gate, kcmp, vcmp, ks, vs, kw, vw)


def _softplus(z):
    return jnp.maximum(z, 0.0) + jnp.log1p(jnp.exp(-jnp.abs(z)))


def _lru_kernel(x_ref, y_ref, cw_ref, cb_ref, wa_ref, ba_ref, wi_ref, bi_ref, lam_ref, o_ref,
                xbuf, hc, a_s, b_s, h_s, *, tb):
    i = pl.program_id(0)

    @pl.when(i == 0)
    def _():
        xbuf[0:8, :] = jnp.zeros((8, LRU_WIDTH), F32)
        hc[...] = jnp.zeros((1, LRU_WIDTH), F32)

    xbuf[8:8 + tb, :] = x_ref[...]
    xc = cb_ref[...]
    for w in range(CONV_W):
        xc = xc + cw_ref[w:w + 1, :] * xbuf[8 - (CONV_W - 1) + w:8 - (CONV_W - 1) + w + tb, :]
    xbuf[0:8, :] = xbuf[tb:tb + 8, :]

    xcb = xc.astype(BF16)
    ra, ia = [], []
    for n in range(LRU_BLOCKS):
        xg = xcb[:, n * LRU_BW:(n + 1) * LRU_BW]
        ra.append(_dot(xg, wa_ref[n].astype(BF16)))
        ia.append(_dot(xg, wi_ref[n].astype(BF16)))
    r_gate = jax.nn.sigmoid(jnp.concatenate(ra, axis=1) + ba_ref[...])
    i_gate = jax.nn.sigmoid(jnp.concatenate(ia, axis=1) + bi_ref[...])
    log_a = (-LRU_C * r_gate) * _softplus(-lam_ref[...])
    a = jnp.exp(log_a)
    a_s[...] = a
    b_s[...] = jnp.sqrt(-jnp.tanh(log_a) * (a * a + 1.0)) * (i_gate * xc)

    rows = lax.broadcasted_iota(jnp.int32, (8, LRU_WIDTH), 0)

    def tile(k, h):
        r0 = pl.multiple_of(k * 8, 8)
        A = a_s[pl.ds(r0, 8), :]
        B = b_s[pl.ds(r0, 8), :]
        for sh in (1, 2, 4):
            ok = rows >= sh
            A_p = pltpu.roll(A, sh, axis=0)
            B_p = pltpu.roll(B, sh, axis=0)
            B = jnp.where(ok, A * B_p + B, B)
            A = jnp.where(ok, A * A_p, A)
        H = A * h + B
        h_s[pl.ds(r0, 8), :] = H
        return H[7:8, :]

    hc[...] = lax.fori_loop(0, tb // 8, tile, hc[...])
    o_ref[...] = (jax.nn.gelu(y_ref[...]) * h_s[...]).astype(o_ref.dtype)


def _rglru(proj, y_col, x_col, conv_w, conv_b, wa, ba, wi, bi, lam, tb=256):
    T = proj.shape[0]
    W = LRU_WIDTH
    vec = pl.BlockSpec((1, W), lambda i: (0, 0))
    blk = pl.BlockSpec((LRU_BLOCKS, LRU_BW, LRU_BW), lambda i: (0, 0, 0))
    return pl.pallas_call(
        functools.partial(_lru_kernel, tb=tb),
        out_shape=jax.ShapeDtypeStruct((T, W), BF16),
        grid=(T // tb,),
        in_specs=[pl.BlockSpec((tb, W), lambda i: (i, x_col)),
                  pl.BlockSpec((tb, W), lambda i: (i, y_col)),
                  pl.BlockSpec((CONV_W, W), lambda i: (0, 0)), vec, blk, vec, blk, vec, vec],
        out_specs=pl.BlockSpec((tb, W), lambda i: (i, 0)),
        scratch_shapes=[pltpu.VMEM((tb + 8, W), F32), pltpu.VMEM((1, W), F32),
                        pltpu.VMEM((tb, W), F32), pltpu.VMEM((tb, W), F32), pltpu.VMEM((tb, W), F32)],
        compiler_params=_cparams(("arbitrary",)),
        name="rglru",
    )(proj, proj, conv_w, conv_b.reshape(1, W), wa, ba.reshape(1, W), wi, bi.reshape(1, W), lam.reshape(1, W))


def _split3(x):
    hi = x.astype(BF16)
    r = x - hi.astype(F32)
    mid = r.astype(BF16)
    lo = (r - mid.astype(F32)).astype(BF16)
    return hi, mid, lo


def _hgrn_kernel(q_ref, f_ref, v_ref, g_ref, lbl_ref, ng_ref, o_ref, st_ref, oacc, *, tbh, layer):
    t = pl.program_id(1)
    C = HG_CHUNK
    SB = HG_SUB

    NH = HG_HEADS_PER_STEP
    DK, DV = HG_DK, HG_DV

    @pl.when(t == 0)
    def _():
        st_ref[...] = jnp.zeros((NH, DV, DK), F32)

    lg = lbl_ref[...]
    e = jnp.exp(lg - jnp.max(lg, axis=0, keepdims=True))
    sm = e / jnp.sum(e, axis=0, keepdims=True)
    cum = sm[0:1, :]
    for l in range(1, layer + 1):
        cum = cum + sm[l:l + 1, :]
    lb_all = cum - sm[0:1, :]

    tri = (lax.broadcasted_iota(jnp.int32, (C, C), 0) >= lax.broadcasted_iota(jnp.int32, (C, C), 1)).astype(BF16)
    rows8 = lax.broadcasted_iota(jnp.int32, (8, DK), 0)

    def head_chunk(r0, hh):
        cs = slice(hh * DK, (hh + 1) * DK)
        lb = lb_all[:, cs]
        qz = q_ref[pl.ds(r0, C), cs]
        qq = qz * jax.nn.sigmoid(qz)
        f = lb + (1.0 - lb) * jax.nn.sigmoid(f_ref[pl.ds(r0, C), cs])
        logf = jnp.log2(jnp.maximum(f, 1e-30))
        kk = 1.0 - f
        vv = v_ref[pl.ds(r0, C), cs]
        l_hi, l_mid, l_lo = _split3(logf)
        bcum = _dot(tri, l_hi) + _dot(tri, l_mid) + _dot(tri, l_lo)
        st = st_ref[hh]
        o_parts = []
        vb = vv.astype(BF16)
        for I in range(C // SB):
            lo_r = I * SB
            for a in range(SB // 8):
                t0 = lo_r + 8 * a
                q8 = qq[t0:t0 + 8]
                b8 = bcum[t0:t0 + 8]
                o8 = jnp.zeros((8, DV), F32)
                for s in range(lo_r, t0 + 8):
                    d = b8 - bcum[s:s + 1]
                    if s > t0:
                        d = jnp.where(rows8 >= s - t0, d, NEG)
                    w = q8 * (kk[s:s + 1] * jnp.exp2(d))
                    o8 = o8 + jnp.sum(w, axis=1, keepdims=True) * vv[s:s + 1]
                o_parts.append(o8)
        for I in range(1, C // SB):
            lo_r = I * SB
            ref = bcum[lo_r - 1:lo_r]
            qe = (qq[lo_r:lo_r + SB] * jnp.exp2(bcum[lo_r:lo_r + SB] - ref)).astype(BF16)
            ke = (kk[0:lo_r] * jnp.exp2(ref - bcum[0:lo_r])).astype(BF16)
            att = _dot_nt(qe, ke)
            o_off = _dot(att.astype(BF16), vb[0:lo_r])
            for a in range(SB // 8):
                o_parts[lo_r // 8 + a] = o_parts[lo_r // 8 + a] + o_off[8 * a:8 * a + 8]
        o_intra = jnp.concatenate(o_parts, axis=0)
        o_inter = _dot_nt((qq * jnp.exp2(bcum)).astype(BF16), st.astype(BF16))
        oacc[pl.ds(r0, C), cs] = o_inter + o_intra
        b_last = bcum[C - 1:C]
        kd = (kk * jnp.exp2(b_last - bcum)).astype(BF16)
        st_ref[hh] = st * jnp.exp2(b_last) + _dot_tn(vb, kd)

    def chunk(c, carry):
        r0 = pl.multiple_of(c * C, C)
        for hh in range(NH):
            head_chunk(r0, hh)
        return carry

    lax.fori_loop(0, tbh // C, chunk, 0)
    gz = g_ref[...]
    gate = (gz * jax.nn.sigmoid(gz)) * ng_ref[...]
    for hh in range(NH):
        cs = slice(hh * DV, (hh + 1) * DV)
        o = oacc[:, cs]
        o = o * lax.rsqrt(jnp.mean(o * o, axis=-1, keepdims=True) + EPS)
        o_ref[:, cs] = (o * gate[:, cs]).astype(o_ref.dtype)


def _hgrn2(proj, lb_logits, norm_g, layer, tbh=1024):
    T = proj.shape[0]
    NH = HG_HEADS_PER_STEP
    HB = HG_HEADS // NH
    wk = NH * HG_DK
    col = lambda off: pl.BlockSpec((tbh, wk), lambda h, t: (t, off + h))
    return pl.pallas_call(
        functools.partial(_hgrn_kernel, tbh=tbh, layer=layer),
        out_shape=jax.ShapeDtypeStruct((T, HG_VW), BF16),
        grid=(HB, T // tbh),
        in_specs=[col(0), col(HB), col(2 * HB), col(3 * HB),
                  pl.BlockSpec((DEPTH, wk), lambda h, t: (0, h)),
                  pl.BlockSpec((1, wk), lambda h, t: (0, h))],
        out_specs=pl.BlockSpec((tbh, wk), lambda h, t: (t, h)),
        scratch_shapes=[pltpu.VMEM((NH, HG_DV, HG_DK), F32), pltpu.VMEM((tbh, wk), F32)],
        compiler_params=_cparams(("arbitrary", "arbitrary")),
        name="hgrn2",
    )(proj, proj, proj, proj, lb_logits, norm_g.reshape(1, HG_VW))


def _router_kernel(x_ref, g_ref, sc_ref, sh_ref, w_ref, b_ref, h_ref, eid_ref, gate_ref, cnt_ref):
    h = _norm_mod(x_ref[...], g_ref[...], sc_ref[...], sh_ref[...])
    tm, d = h.shape
    hb = h.astype(BF16)
    lo = lax.bitcast_convert_type(hb[:, :d // 2].astype(F32), jnp.uint32) >> 16
    hi = lax.bitcast_convert_type(hb[:, d // 2:].astype(F32), jnp.uint32)
    packed = hi | lo
    pt = d // 2 // LANES
    for c in range(pt):
        h_ref[pl.ds(c, tm, stride=pt), :] = packed[:, c * LANES:(c + 1) * LANES]
    h1, h2, h3 = _split3(h)
    w1, w2, w3 = _split3(w_ref[...])
    logits = (_dot(h1, w1) + (_dot(h1, w2) + _dot(h2, w1))
              + (_dot(h2, w2) + _dot(h1, w3) + _dot(h3, w1))) + b_ref[...]
    lane = lax.broadcasted_iota(jnp.int32, logits.shape, 1)
    is_g = lane < N_GROUPS
    glog = jnp.where(is_g, logits, -jnp.inf)
    gmax = jnp.max(glog, axis=-1, keepdims=True)
    grp = jnp.min(jnp.where(glog == gmax, lane, LANES), axis=-1, keepdims=True)
    gsum = jnp.sum(jnp.where(is_g, jnp.exp(glog - gmax), 0.0), axis=-1, keepdims=True)
    p_grp = 1.0 / gsum
    lo = N_GROUPS + EXP_PER_GROUP * grp
    el = jnp.where((lane >= lo) & (lane < lo + EXP_PER_GROUP), logits, -jnp.inf)
    v1 = jnp.max(el, axis=-1, keepdims=True)
    i1 = jnp.min(jnp.where(el == v1, lane, LANES), axis=-1, keepdims=True)
    el2 = jnp.where(lane == i1, -jnp.inf, el)
    v2 = jnp.max(el2, axis=-1, keepdims=True)
    i2 = jnp.min(jnp.where(el2 == v2, lane, LANES), axis=-1, keepdims=True)
    e2 = jnp.exp(v2 - v1)
    den = 1.0 + e2
    g1 = p_grp * (1.0 / den)
    g2 = p_grp * (e2 / den)
    eid_ref[...] = jnp.where(lane == 0, i1 - N_GROUPS, jnp.where(lane == 1, i2 - N_GROUPS, 0))
    gate_ref[...] = jnp.where(lane == 0, g1, jnp.where(lane == 1, g2, 0.0))

    @pl.when(pl.program_id(0) == 0)
    def _():
        cnt_ref[...] = jnp.zeros(cnt_ref.shape, F32)

    picked = ((lane == i1 - N_GROUPS) | (lane == i2 - N_GROUPS)).astype(F32)
    cnt_ref[...] += jnp.sum(picked, axis=0, keepdims=True)


def _dispatch_kernel(eid_ref, base_ref, dest_ref, tok_ref, carry, dstage, dsm, zbuf, sem, *, tb):
    i = pl.program_id(0)

    @pl.when(i == 0)
    def _():
        carry[...] = jnp.zeros(carry.shape, F32)
        zbuf[...] = jnp.zeros(zbuf.shape, jnp.int32)
        z = pltpu.make_async_copy(zbuf, tok_ref, sem)
        z.start()
        z.wait()

    e = eid_ref[...]
    lane = lax.broadcasted_iota(jnp.int32, e.shape, 1)
    oh0 = (lane == e[:, 0:1]).astype(F32)
    oh1 = (lane == e[:, 1:2]).astype(F32)
    both = oh0 + oh1
    before = (lax.broadcasted_iota(jnp.int32, (tb, tb), 1) < lax.broadcasted_iota(jnp.int32, (tb, tb), 0)).astype(BF16)
    prior = _dot(before, both.astype(BF16)) + (carry[...] + base_ref[...])
    d0 = jnp.sum(oh0 * prior, axis=1, keepdims=True)
    d1 = jnp.sum(oh1 * (prior + oh0), axis=1, keepdims=True)
    carry[...] += jnp.sum(both, axis=0, keepdims=True)
    dest = jnp.where(lane == 0, d0, jnp.where(lane == 1, d1, 0.0)).astype(jnp.int32)
    dest_ref[...] = dest

    dstage[...] = jnp.transpose(dest)[0:8, :]
    cp = pltpu.make_async_copy(dstage, dsm, sem)
    cp.start()
    cp.wait()

    def body(n, c):
        for k in range(TOPK_IN_GROUP):
            p = dsm[k, n]
            tok_ref[p >> 7, p & (LANES - 1)] = i * tb + n
        return c

    lax.fori_loop(0, tb, body, 0, unroll=8)


def _dispatch(eid_l, base, n_blk, tb=512):
    N = eid_l.shape[0]
    return pl.pallas_call(
        functools.partial(_dispatch_kernel, tb=tb),
        out_shape=[jax.ShapeDtypeStruct((N, LANES), jnp.int32), jax.ShapeDtypeStruct((n_blk, LANES), jnp.int32)],
        grid=(N // tb,),
        in_specs=[pl.BlockSpec((tb, LANES), lambda i: (i, 0)), pl.BlockSpec((1, LANES), lambda i: (0, 0))],
        out_specs=[pl.BlockSpec((tb, LANES), lambda i: (i, 0)), pl.BlockSpec(memory_space=pltpu.SMEM)],
        scratch_shapes=[pltpu.VMEM((1, LANES), F32), pltpu.VMEM((8, tb), jnp.int32), pltpu.SMEM((8, tb), jnp.int32),
                        pltpu.VMEM((n_blk, LANES), jnp.int32), pltpu.SemaphoreType.DMA(())],
        compiler_params=_cparams(("arbitrary",)),
        name="moe_dispatch",
    )(eid_l, base)


def _router(x, g, sc, sh, w_r, b_r, tm=512):
    T, D = x.shape
    vec = pl.BlockSpec((1, D), lambda i: (0, 0))
    return pl.pallas_call(
        _router_kernel,
        out_shape=[jax.ShapeDtypeStruct((T * PACKED_ROWS, LANES), jnp.uint32), jax.ShapeDtypeStruct((T, LANES), jnp.int32),
                   jax.ShapeDtypeStruct((T, LANES), F32), jax.ShapeDtypeStruct((1, LANES), F32)],
        grid=(T // tm,),
        in_specs=[pl.BlockSpec((tm, D), lambda i: (i, 0)), vec, vec, vec,
                  pl.BlockSpec((D, LANES), lambda i: (0, 0)), pl.BlockSpec((1, LANES), lambda i: (0, 0))],
        out_specs=[pl.BlockSpec((tm * PACKED_ROWS, LANES), lambda i: (i, 0)),
                   pl.BlockSpec((tm, LANES), lambda i: (i, 0)),
                   pl.BlockSpec((tm, LANES), lambda i: (i, 0)),
                   pl.BlockSpec((1, LANES), lambda i: (0, 0))],
        compiler_params=_cparams(("arbitrary",)),
        name="moe_router",
    )(x, g, sc, sh, w_r, b_r)


def _cast_rows(src_ref, slot, dst_ref, rows):
    def body(c, carry):
        r = pl.multiple_of(c * rows, rows)
        dst_ref[pl.ds(r, rows), :] = src_ref[slot, pl.ds(r, rows), :].astype(BF16)
        return carry
    lax.fori_loop(0, dst_ref.shape[0] // rows, body, 0, unroll=4)


def _expert_kernel(rid_ref, re_ref, nu_ref, nv_ref, tok_ref, h_ref, wg_hbm, wu_hbm, wd_hbm, y_ref,
                   wgf, wuf, wdf, wgb, wub, wdb, xbuf, wsem, xsem, *, layer):
    i = pl.program_id(0)
    n_used = nu_ref[0]
    R = MOE_BLOCK

    def w_copies(e, slot):
        return ((pltpu.make_async_copy(wg_hbm.at[layer, e], wgf.at[slot], wsem.at[slot, 0]), 1),
                (pltpu.make_async_copy(wu_hbm.at[layer, e], wuf.at[slot], wsem.at[slot, 1]), 1),
                (pltpu.make_async_copy(wd_hbm.at[layer, e], wdf.at[slot], wsem.at[slot, 2]), 1))

    HT = PACKED_ROWS

    GR = 8

    def groups(blk):
        return (nv_ref[blk] + GR - 1) // GR

    def x_start(blk, slot):
        def body(gi, c):
            for u in range(GR):
                r = gi * GR + u
                tok = tok_ref[blk * R + r]
                pltpu.make_async_copy(h_ref.at[pl.ds(pl.multiple_of(tok * HT, HT), HT)],
                                      xbuf.at[slot, pl.ds(pl.multiple_of(r * HT, HT), HT)], xsem.at[slot]).start()
            return c
        lax.fori_loop(0, groups(blk), body, 0)

    def x_wait(blk, slot):
        def body(gi, c):
            pltpu.make_async_copy(h_ref.at[pl.ds(0, GR * HT)], xbuf.at[slot, pl.ds(0, GR * HT)], xsem.at[slot]).wait()
            return c
        lax.fori_loop(0, groups(blk), body, 0)

    def w_request(run):
        @pl.when(re_ref[run] >= 0)
        def _():
            for c, pri in w_copies(re_ref[run], run % 2):
                c.start(priority=pri)

    @pl.when(i == 0)
    def _():
        w_request(0)
        w_request(1)
        xbuf[...] = jnp.zeros(xbuf.shape, xbuf.dtype)
        x_start(0, 0)

    @pl.when(i < n_used)
    def _():
        slot = i % 2

        @pl.when(i + 1 < n_used)
        def _():
            x_start(i + 1, 1 - slot)

        run = rid_ref[i]

        @pl.when((i == 0) | (run != rid_ref[jnp.maximum(i - 1, 0)]))
        def _():
            ws = run % 2
            for c, _ in w_copies(re_ref[run], ws):
                c.wait()
            _cast_rows(wgf, ws, wgb, 64)
            _cast_rows(wuf, ws, wub, 64)
            _cast_rows(wdf, ws, wdb, 16)
            w_request(run + 2)

        x_wait(i, slot)
        words = [xbuf[slot, pl.ds(c, R, stride=HT), :] for c in range(HT)]
        lo = [lax.bitcast_convert_type(w << 16, F32) for w in words]
        hi = [lax.bitcast_convert_type(w & jnp.uint32(0xFFFF0000), F32) for w in words]
        x = jnp.concatenate(lo + hi, axis=1).astype(BF16)
        hg = _dot(x, wgb[...])
        hu = _dot(x, wub[...])
        hid = (hg * jax.nn.sigmoid(hg)) * hu
        y_ref[...] = _dot(hid.astype(BF16), wdb[...])

    @pl.when(i >= n_used)
    def _():
        y_ref[...] = jnp.zeros(y_ref.shape, y_ref.dtype)


def _experts(h, tok, run_id, run_e, n_used, n_valid, w_gate, w_up, w_down, layer):
    D = D_MODEL
    n_pad = tok.shape[0]
    n_blk = n_pad // MOE_BLOCK
    any_spec = pl.BlockSpec(memory_space=pl.ANY)
    return pl.pallas_call(
        functools.partial(_expert_kernel, layer=layer),
        out_shape=jax.ShapeDtypeStruct((n_pad, D), F32),
        grid_spec=pltpu.PrefetchScalarGridSpec(
            num_scalar_prefetch=5,
            grid=(n_blk,),
            in_specs=[any_spec, any_spec, any_spec, any_spec],
            out_specs=pl.BlockSpec((MOE_BLOCK, D), lambda i, *_: (i, 0)),
            scratch_shapes=[pltpu.VMEM((2, D, D_EXPERT), F32), pltpu.VMEM((2, D, D_EXPERT), F32),
                            pltpu.VMEM((2, D_EXPERT, D), F32),
                            pltpu.VMEM((D, D_EXPERT), BF16), pltpu.VMEM((D, D_EXPERT), BF16),
                            pltpu.VMEM((D_EXPERT, D), BF16),
                            pltpu.VMEM((2, MOE_BLOCK * PACKED_ROWS, LANES), jnp.uint32),
                            pltpu.SemaphoreType.DMA((2, 4)), pltpu.SemaphoreType.DMA((2,))]),
        compiler_params=_cparams(("arbitrary",)),
        name="moe_experts",
    )(run_id, run_e, n_used, n_valid, tok, h, w_gate, w_up, w_down)


def _combine_kernel(slot_ref, y_ref, x_ref, w_ref, g_ref, fg_ref, o_ref, buf, sem, *, rows, final_norm):
    i = pl.program_id(0)

    def copy(r, k):
        return pltpu.make_async_copy(y_ref.at[pl.ds(slot_ref[(i * rows + r) * 2 + k], 1)],
                                     buf.at[k, pl.ds(r, 1)], sem)

    def start(r, c):
        copy(r, 0).start()
        copy(r, 1).start()
        return c

    lax.fori_loop(0, rows, start, 0, unroll=8)
    for k in range(TOPK_IN_GROUP):
        pltpu.make_async_copy(y_ref.at[pl.ds(0, rows)], buf.at[k], sem).wait()
    w = w_ref[...]
    moe = buf[0] * w[:, 0:1] + buf[1] * w[:, 1:2]
    y = x_ref[...] + g_ref[...] * moe
    if final_norm:
        y = (y * lax.rsqrt(jnp.mean(y * y, axis=-1, keepdims=True) + EPS)) * fg_ref[...]
    o_ref[...] = y


def _combine(yb, slots, x, gate_w, g2, final_g, final_norm, rows=256):
    T, D = x.shape
    return pl.pallas_call(
        functools.partial(_combine_kernel, rows=rows, final_norm=final_norm),
        out_shape=jax.ShapeDtypeStruct((T, D), F32),
        grid_spec=pltpu.PrefetchScalarGridSpec(
            num_scalar_prefetch=1,
            grid=(T // rows,),
            in_specs=[pl.BlockSpec(memory_space=pl.ANY),
                      pl.BlockSpec((rows, D), lambda i, s: (i, 0)),
                      pl.BlockSpec((rows, LANES), lambda i, s: (i, 0)),
                      pl.BlockSpec((1, D), lambda i, s: (0, 0)),
                      pl.BlockSpec((1, D), lambda i, s: (0, 0))],
            out_specs=pl.BlockSpec((rows, D), lambda i, s: (i, 0)),
            scratch_shapes=[pltpu.VMEM((2, rows, D), F32), pltpu.SemaphoreType.DMA(())]),
        compiler_params=_cparams(("arbitrary",)),
        name="moe_combine",
    )(slots.reshape(-1), yb, x, gate_w, g2, final_g)


def _hier_moe_residual(x, g, sc, sh, g2, w_grp, b_grp, w_exp, b_exp, w_gate, w_up, w_down, layer, final_g,
                       final_norm):
    N, D = x.shape
    n_route = N_GROUPS + N_EXPERTS
    w_r = jnp.zeros((D, LANES), F32).at[:, :N_GROUPS].set(w_grp).at[:, N_GROUPS:n_route].set(w_exp)
    b_r = jnp.zeros((1, LANES), F32).at[0, :N_GROUPS].set(b_grp).at[0, N_GROUPS:n_route].set(b_exp)
    h, eid_l, gate_l, cnt = _router(x, g, sc, sh, w_r, b_r)

    A = N * TOPK_IN_GROUP
    counts = cnt[0, :N_EXPERTS].astype(jnp.int32)
    padded = (counts + MOE_BLOCK - 1) // MOE_BLOCK * MOE_BLOCK
    pad_end = jnp.cumsum(padded)
    pad_start = pad_end - padded
    n_blk = -(-(A + N_EXPERTS * MOE_BLOCK) // MOE_BLOCK)
    base = jnp.zeros((1, LANES), F32).at[0, :N_EXPERTS].set(pad_start.astype(F32))
    dest_l, tok_tbl = _dispatch(eid_l, base, n_blk)
    buf_tok = tok_tbl.reshape(-1)
    blk_idx = jnp.arange(n_blk, dtype=jnp.int32)
    n_used = pad_end[-1] // MOE_BLOCK
    blk_e = jnp.minimum(jnp.sum((pad_end[None, :] <= (blk_idx * MOE_BLOCK)[:, None]).astype(jnp.int32), axis=1),
                        N_EXPERTS - 1)
    blk_e = jnp.where(blk_idx < n_used, blk_e, blk_e[n_used - 1])
    change = jnp.concatenate([jnp.ones((1,), bool), blk_e[1:] != blk_e[:-1]])
    run_id = (jnp.cumsum(change.astype(jnp.int32)) - 1).astype(jnp.int32)
    runs = jnp.arange(n_blk + 2, dtype=jnp.int32)
    run_e = jnp.max(jnp.where(run_id[None, :] == runs[:, None], blk_e[None, :], -1), axis=1).astype(jnp.int32)
    slots = dest_l[:, :TOPK_IN_GROUP]

    n_valid = jnp.clip(counts[blk_e] - (blk_idx * MOE_BLOCK - pad_start[blk_e]), 0, MOE_BLOCK).astype(jnp.int32)
    yb = _experts(h, buf_tok, run_id, run_e, n_used.reshape(1).astype(jnp.int32), n_valid, w_gate, w_up, w_down, layer)
    return _combine(yb, slots, x, gate_l, g2, final_g, final_norm)


def _even_mixer_residual(x, g, sc, sh, g1, rel_bias, w_in, w_out, cmp_pe, cmp_w1, cmp_b1, cmp_w2, cmp_b2,
                         conv_w, conv_b, lru_wa, lru_ba, lru_wi, lru_bi, lru_lambda):
    T, D = x.shape
    G, HD = NSA_KV_HEADS, HEAD_DIM
    nq = NSA_HEADS * HD
    n_kv = 6 * NSA_KV_W
    n_gate = 3 * NSA_HEADS
    w_main = jnp.concatenate([w_in[:, :nq], w_in[:, nq + n_kv + n_gate:], w_in[:, nq:nq + n_kv]],
                             axis=1).astype(BF16)
    w_gl = jnp.zeros((D, LANES), F32).at[:, :n_gate].set(w_in[:, nq + n_kv:nq + n_kv + n_gate]).astype(BF16)
    kv0 = nq + 2 * LRU_WIDTH
    tn = 768
    proj, proj_gate, kv_b = _norm_proj(x, g, sc, sh, w_main, w_gl, bf16_from=kv0 // tn, tn=tn)

    def kv_heads(j, src=kv_b, c_base=0):
        c0 = c_base + j * NSA_KV_W
        return src[:, c0:c0 + NSA_KV_W].reshape(T, G, HD).transpose(1, 0, 2)

    cmp_in = jnp.stack([kv_heads(0, proj, kv0), kv_heads(1, proj, kv0)], axis=0)
    cmp_out = _compress(cmp_in, cmp_pe, cmp_w1, cmp_b1, cmp_w2, cmp_b2)
    ncp = T // CMP_STRIDE
    valid = (jnp.arange(ncp) < ncp - 1)[None, None, :, None]
    cmp_pad = jnp.pad(jnp.where(valid, cmp_out, 0.0), ((0, 0), (0, 0), (Q_BLOCK, Q_BLOCK), (0, 0)))
    slc_pad = ((0, 0), (Q_BLOCK, SLC_TAIL), (0, 0))
    blk_of_row = jnp.pad(jnp.arange(T, dtype=jnp.int32) // SLC_LEN, (Q_BLOCK, SLC_TAIL), constant_values=Q_BLOCK - 1)
    blk_onehot = (blk_of_row[:, None] == jnp.arange(Q_BLOCK, dtype=jnp.int32)[None, :]).astype(BF16)
    ks = jnp.concatenate([jnp.pad(kv_heads(2), slc_pad), jnp.broadcast_to(blk_onehot, (G,) + blk_onehot.shape)], axis=2)
    vs = jnp.pad(kv_heads(3), slc_pad)
    kw = jnp.pad(kv_heads(4), ((0, 0), (WINDOW, 0), (0, 0)))
    vw = jnp.pad(kv_heads(5), ((0, 0), (WINDOW, 0), (0, 0)))
    nsa_out = _nsa_attention(rel_bias, proj, proj_gate, cmp_pad[0], cmp_pad[1], ks, vs, kw, vw)

    y_col = nq // LRU_WIDTH
    lru_out = _rglru(proj, y_col, y_col + 1, conv_w, conv_b, lru_wa, lru_ba, lru_wi, lru_bi, lru_lambda)
    w_o = w_out.astype(BF16)
    return _out_proj(nsa_out, lru_out, w_o, x, g1)


def _odd_mixer_residual(x, g, sc, sh, g1, lb_logits, w_in, w_out, norm_g, layer):
    proj = _norm_proj(x, g, sc, sh, w_in.astype(BF16), tn=1024)[0]
    o = _hgrn2(proj, lb_logits, norm_g, layer)
    w_o = w_out.astype(BF16)
    return _out_proj(o, o, w_o, x, g1, a2_col=1)


def kernel(x, c, rel_bias, ada_w, ada_b, norm_mix_g, norm_ffn_g, ev_w_in, ev_w_out, cmp_pe, cmp_w1, cmp_b1, cmp_w2, cmp_b2, lru_conv_w, lru_conv_b, lru_wa, lru_ba, lru_wi, lru_bi, lru_lambda, od_w_in, od_w_out, hg_lb_logits, hg_norm_g, moe_w_grp, moe_b_grp, moe_w_exp, moe_b_exp, moe_w_gate, moe_w_up, moe_w_down, final_g):
    B, T, D = x.shape
    assert B == 1 and D == D_MODEL
    xt = x.reshape(T, D)
    mod = _modulation(c, ada_w, ada_b)
    for l in range(DEPTH):
        sh1, sc1, g1, sh2, sc2, g2 = [mod[l, :, k * D:(k + 1) * D] for k in range(6)]
        gm = norm_mix_g[l].reshape(1, D)
        gf = norm_ffn_g[l].reshape(1, D)
        j = l // 2
        if l % 2 == 0:
            xt = _even_mixer_residual(xt, gm, sc1, sh1, g1, rel_bias, ev_w_in[j], ev_w_out[j], cmp_pe[j],
                                      cmp_w1[j], cmp_b1[j], cmp_w2[j], cmp_b2[j], lru_conv_w[j],
                                      lru_conv_b[j], lru_wa[j], lru_ba[j], lru_wi[j], lru_bi[j], lru_lambda[j])
        else:
            xt = _odd_mixer_residual(xt, gm, sc1, sh1, g1, hg_lb_logits, od_w_in[j], od_w_out[j],
                                     hg_norm_g[j], l)
        xt = _hier_moe_residual(xt, gf, sc2, sh2, g2, moe_w_grp[l], moe_b_grp[l], moe_w_exp[l], moe_b_exp[l],
                                moe_w_gate, moe_w_up, moe_w_down, l, final_g.reshape(1, D), l == DEPTH - 1)
    return xt.reshape(B, T, D)
```

```python
import functools
import math

import numpy as np
import jax
import jax.numpy as jnp
from jax import lax
from jax.experimental import pallas as pl
from jax.experimental.pallas import tpu as pltpu

F32 = jnp.float32
BF16 = jnp.bfloat16

D_MODEL = 2048
DEPTH = 2
NSA_HEADS = 8
NSA_KV_HEADS = 2
NSA_HPG = NSA_HEADS // NSA_KV_HEADS
HEAD_DIM = 128
NSA_KV_W = NSA_KV_HEADS * HEAD_DIM
CMP_LEN = 32
CMP_STRIDE = 16
SLC_LEN = 64
SLC_TOPN = 16
WINDOW = 512
Q_BLOCK = 128
LRU_WIDTH = 1024
LRU_BLOCKS = 8
LRU_BW = LRU_WIDTH // LRU_BLOCKS
CONV_W = 4
LRU_C = 8.0
HG_HEADS = 16
HG_DK = 128
HG_DV = 128
HG_CHUNK = 64
HG_SUB = 16
HG_HEADS_PER_STEP = 16
SLC_GROUP = 4
RANK_UNROLL = 4
RANK_BAND = 32
SLC_TAIL = 2 * SLC_GROUP * Q_BLOCK
HG_KW = HG_HEADS * HG_DK
HG_VW = HG_HEADS * HG_DV
N_BUCKETS = 32
MAX_DIST = 128
N_GROUPS = 8
EXP_PER_GROUP = 8
N_EXPERTS = N_GROUPS * EXP_PER_GROUP
TOPK_IN_GROUP = 2
D_EXPERT = 512
MOE_BLOCK = 128
EPS = 1e-6

LANES = 128
PACKED_ROWS = D_MODEL // 2 // LANES
NEG = -1e30
M_FLOOR = -1e20
LOG2E = math.log2(math.e)
VMEM_LIMIT = 56 * 1024 * 1024


def _cparams(sem):
    return pltpu.CompilerParams(dimension_semantics=sem, vmem_limit_bytes=VMEM_LIMIT)


def _dot(a, b):
    return jnp.dot(a, b, preferred_element_type=F32)


def _dot_nt(a, b):
    return lax.dot_general(a, b, (((1,), (1,)), ((), ())), preferred_element_type=F32)


def _dot_tn(a, b):
    return lax.dot_general(a, b, (((0,), (0,)), ((), ())), preferred_element_type=F32)


def _bucket_starts():
    n = np.arange(0, MAX_DIST + 1, dtype=np.int32)
    max_exact = N_BUCKETS // 2
    nf = np.maximum(n, 1).astype(np.float32)
    large = max_exact + (np.log(nf / np.float32(max_exact)) / np.float32(math.log(MAX_DIST / max_exact))
                         * np.float32(N_BUCKETS - max_exact)).astype(np.int32)
    large = np.minimum(large, N_BUCKETS - 1)
    b = np.where(n < max_exact, n, large)
    starts = [int(np.argmax(b >= k)) for k in range(N_BUCKETS)]
    assert all(b[s] == k for k, s in enumerate(starts)) and b[-1] == N_BUCKETS - 1
    return starts


BUCKET_STARTS = _bucket_starts()


def _mod_kernel(c_ref, w_ref, b_ref, o_ref):
    c = c_ref[...]
    cact = c * jax.nn.sigmoid(c)
    o_ref[...] = jnp.sum(cact * w_ref[...], axis=0, keepdims=True) + b_ref[...]


def _modulation(c, ada_w, ada_b):
    D = D_MODEL
    tn = 1024
    n_out = 6 * D
    c_col = c.reshape(D, 1)
    return pl.pallas_call(
        _mod_kernel,
        out_shape=jax.ShapeDtypeStruct((DEPTH, 1, n_out), F32),
        grid=(DEPTH, n_out // tn),
        in_specs=[pl.BlockSpec((D, 1), lambda l, j: (0, 0)),
                  pl.BlockSpec((None, D, tn), lambda l, j: (l, 0, j)),
                  pl.BlockSpec((None, 1, tn), lambda l, j: (l, 0, j))],
        out_specs=pl.BlockSpec((None, 1, tn), lambda l, j: (l, 0, j)),
        compiler_params=_cparams(("arbitrary", "arbitrary")),
        name="adaln_mod",
    )(c_col, ada_w, ada_b.reshape(DEPTH, 1, n_out))


def _norm_mod(x, g, sc, sh):
    y = x * lax.rsqrt(jnp.mean(x * x, axis=-1, keepdims=True) + EPS)
    return (y * g) * (1.0 + sc) + sh


def _norm_proj_kernel(x_ref, g_ref, sc_ref, sh_ref, w_ref, *rest, has_extra, bf16_from):
    if has_extra:
        wx_ref, o_ref, ox_ref, ob_ref, h_ref = rest
    else:
        o_ref, h_ref = rest
    j = pl.program_id(1)

    @pl.when(j == 0)
    def _():
        h = _norm_mod(x_ref[...], g_ref[...], sc_ref[...], sh_ref[...]).astype(BF16)
        h_ref[...] = h
        if has_extra:
            ox_ref[...] = _dot(h, wx_ref[...])

    y = _dot(h_ref[...], w_ref[...])
    o_ref[...] = y
    if has_extra:
        @pl.when(j >= bf16_from)
        def _():
            ob_ref[...] = y.astype(BF16)


def _norm_proj(x, g, sc, sh, w, w_extra=None, bf16_from=None, tm=1024, tn=512):
    T, D = x.shape
    N = w.shape[1]
    has_extra = w_extra is not None
    vec = pl.BlockSpec((1, D), lambda i, j: (0, 0))
    in_specs = [pl.BlockSpec((tm, D), lambda i, j: (i, 0)), vec, vec, vec,
                pl.BlockSpec((D, tn), lambda i, j: (0, j))]
    out_shape = [jax.ShapeDtypeStruct((T, N), F32)]
    out_specs = [pl.BlockSpec((tm, tn), lambda i, j: (i, j))]
    args = [x, g, sc, sh, w]
    if has_extra:
        nx = w_extra.shape[1]
        in_specs.append(pl.BlockSpec((D, nx), lambda i, j: (0, 0)))
        out_shape.append(jax.ShapeDtypeStruct((T, nx), F32))
        out_specs.append(pl.BlockSpec((tm, nx), lambda i, j: (i, 0)))
        args.append(w_extra)
        out_shape.append(jax.ShapeDtypeStruct((T, N - bf16_from * tn), BF16))
        out_specs.append(pl.BlockSpec((tm, tn), lambda i, j: (i, jnp.maximum(j - bf16_from, 0))))
    return pl.pallas_call(
        functools.partial(_norm_proj_kernel, has_extra=has_extra, bf16_from=bf16_from),
        out_shape=out_shape,
        grid=(T // tm, N // tn),
        in_specs=in_specs,
        out_specs=out_specs,
        scratch_shapes=[pltpu.VMEM((tm, D), BF16)],
        compiler_params=_cparams(("arbitrary", "arbitrary")),
        name="norm_proj",
    )(*args)


def _out_proj_kernel(a1_ref, a2_ref, w1_ref, w2_ref, x_ref, g_ref, o_ref):
    y = _dot(a1_ref[...], w1_ref[...]) + _dot(a2_ref[...], w2_ref[...])
    o_ref[...] = x_ref[...] + g_ref[...] * y


def _out_proj(a1, a2, w, x, gate, a2_col=0, tm=1024, tn=1024):
    T, D = x.shape
    K1 = K2 = w.shape[0] // 2
    return pl.pallas_call(
        _out_proj_kernel,
        out_shape=jax.ShapeDtypeStruct((T, D), F32),
        grid=(T // tm, D // tn),
        in_specs=[pl.BlockSpec((tm, K1), lambda i, j: (i, 0)),
                  pl.BlockSpec((tm, K2), lambda i, j: (i, a2_col)),
                  pl.BlockSpec((K1, tn), lambda i, j: (0, j)),
                  pl.BlockSpec((K2, tn), lambda i, j: (1, j)),
                  pl.BlockSpec((tm, tn), lambda i, j: (i, j)),
                  pl.BlockSpec((1, tn), lambda i, j: (0, j))],
        out_specs=pl.BlockSpec((tm, tn), lambda i, j: (i, j)),
        compiler_params=_cparams(("arbitrary", "arbitrary")),
        name="out_proj",
    )(a1, a2, w, w, x, gate)


def _compress_kernel(k2_ref, pe_ref, w1_ref, b1_ref, w2_ref, b2_ref, o_ref):
    half = (CMP_LEN // 2) * HEAD_DIM
    k2 = k2_ref[...]
    a = _dot((k2 + pe_ref[0:1, :]).astype(BF16), w1_ref[0:half, :].astype(BF16))
    b = _dot((k2 + pe_ref[1:2, :]).astype(BF16), w1_ref[half:2 * half, :].astype(BF16))
    nb = k2.shape[0]
    b_up = pltpu.roll(b, nb - 1, axis=0)
    hid = jax.nn.gelu(a + b_up + b1_ref[...])
    o_ref[...] = _dot(hid.astype(BF16), w2_ref[...].astype(BF16)) + b2_ref[...]


def _compress(kv, pe, w1, b1, w2, b2):
    _, G, T, HD = kv.shape
    nb = T // CMP_STRIDE
    row = CMP_STRIDE * HD
    kv2 = kv.reshape(2, G, nb, row)
    pe2 = pe.reshape(2, 2, row)
    return pl.pallas_call(
        _compress_kernel,
        out_shape=jax.ShapeDtypeStruct((2, G, nb, HD), F32),
        grid=(2, G),
        in_specs=[pl.BlockSpec((None, None, nb, row), lambda a, g: (a, g, 0, 0)),
                  pl.BlockSpec((None, 2, row), lambda a, g: (a, 0, 0)),
                  pl.BlockSpec((None, CMP_LEN * HD, HD), lambda a, g: (a, 0, 0)),
                  pl.BlockSpec((None, 1, HD), lambda a, g: (a, 0, 0)),
                  pl.BlockSpec((None, HD, HD), lambda a, g: (a, 0, 0)),
                  pl.BlockSpec((None, 1, HD), lambda a, g: (a, 0, 0))],
        out_specs=pl.BlockSpec((None, None, nb, HD), lambda a, g: (a, g, 0, 0)),
        compiler_params=_cparams(("arbitrary", "arbitrary")),
        name="nsa_compress",
    )(kv2, pe2, w1, b1.reshape(2, 1, HD), w2, b2.reshape(2, 1, HD))


def _bias_from_rel(rel, tbl_ref, head):
    far = tbl_ref[N_BUCKETS - 1, head]
    val = jnp.zeros(rel.shape, F32)
    for b in range(N_BUCKETS - 2, -1, -1):
        val = jnp.where(rel < BUCKET_STARTS[b + 1], (tbl_ref[b, head] - far) * LOG2E, val)
    return val


def _lane_tile4(x):
    return jnp.concatenate([x, x, x, x], axis=1)


def _col_softmax_stats(s):
    m = jnp.maximum(jnp.max(s, axis=0, keepdims=True), M_FLOOR)
    e = jnp.exp2(s - m)
    d = jnp.sum(e, axis=0, keepdims=True)
    return e, d


def _nsa_kernel(tbl_ref, q_ref, gl_ref, kc_ref, vc_ref, ks_ref, vs_ref, kw_ref, vw_ref, o_ref,
                bd_ref, bp_ref, bn_ref, imp_ref, cnt_ref, *, ncp):
    g = pl.program_id(0)
    ci = pl.program_id(1)
    Q = Q_BLOCK
    HD = HEAD_DIM
    H4 = NSA_HPG

    row_i = lax.broadcasted_iota(jnp.int32, (Q, Q), 0)
    lane_i = lax.broadcasted_iota(jnp.int32, (Q, Q), 1)

    @pl.when((g == 0) & (ci == 0))
    def _build_bias_tiles():
        for h in range(NSA_HEADS):
            rel_d = lane_i - row_i
            bd_ref[h] = jnp.where(rel_d >= 0, _bias_from_rel(jnp.maximum(rel_d, 0), tbl_ref, h), NEG)
            bp_ref[h] = _bias_from_rel(lane_i - row_i + Q, tbl_ref, h)
            rel_n = lane_i - CMP_STRIDE * row_i + (CMP_STRIDE * (Q - 8) - (CMP_LEN - 1))
            bn_ref[h] = jnp.where(rel_n >= 0, _bias_from_rel(jnp.maximum(rel_n, 0), tbl_ref, h), NEG)

    def head_tiles(ref):
        return jnp.concatenate([ref[g * H4 + h] for h in range(H4)], axis=1)

    qt = q_ref[...] * (HD ** -0.5 * LOG2E)
    qs = jnp.concatenate([qt[:, h * HD:(h + 1) * HD] for h in range(H4)], axis=0).astype(BF16)

    near0 = 8 * ci - (Q - 8)
    kc_far = kc_ref[pl.ds(Q, ncp), :]
    vc_far = vc_ref[pl.ds(Q, ncp), :]
    near_row = pl.multiple_of(8 * ci + 8, 8)
    kc_near = kc_ref[pl.ds(near_row, Q), :]
    vc_near = vc_ref[pl.ds(near_row, Q), :]
    n_far = lax.broadcasted_iota(jnp.int32, (ncp, 1), 0)
    s_far = jnp.where(n_far < near0, _dot_nt(kc_far.astype(BF16), qs), NEG)
    m_near = lax.broadcasted_iota(jnp.int32, (Q, 1), 0)
    s_near = _dot_nt(kc_near.astype(BF16), qs) + head_tiles(bn_ref)
    s_near = jnp.where(m_near + near0 >= 0, s_near, NEG)
    s_c = jnp.concatenate([s_far, s_near], axis=0)
    e_c, d_c = _col_softmax_stats(s_c)
    p_c = e_c * (1.0 / jnp.where(d_c > 0, d_c, 1.0))
    v_c = jnp.concatenate([vc_far, vc_near], axis=0).astype(BF16)
    o_cmp = _dot_tn(v_c, p_c.astype(BF16))

    psum = p_c[:, 0:Q] + p_c[:, Q:2 * Q] + p_c[:, 2 * Q:3 * Q] + p_c[:, 3 * Q:4 * Q]
    p_hi = psum.astype(BF16)
    p_lo = (psum - p_hi.astype(F32)).astype(BF16)
    jb = lax.broadcasted_iota(jnp.int32, (Q, ncp), 0)
    nb = lax.broadcasted_iota(jnp.int32, (Q, ncp), 1)
    ratio = SLC_LEN // CMP_STRIDE
    span = CMP_LEN // CMP_STRIDE - 1
    cover_far = ((nb >= ratio * jb - span) & (nb <= ratio * jb + ratio - 1)).astype(BF16)
    nn = lane_i + near0
    cover_near = ((nn >= ratio * row_i - span) & (nn <= ratio * row_i + ratio - 1)).astype(BF16)
    cover = jnp.concatenate([cover_far, cover_near], axis=1)
    imp = _dot(cover, p_hi) + _dot(cover, p_lo)
    cur = 2 * ci + (lane_i >= SLC_LEN).astype(jnp.int32)
    imp = jnp.where(row_i == cur, jnp.inf, jnp.where(row_i > cur, -jnp.inf, imp))
    imp_ref[...] = imp

    n_cand = (2 * ci + 2 + RANK_UNROLL - 1) // RANK_UNROLL
    cnt_ref[...] = jnp.zeros((Q, Q), F32)
    for band in range(Q // RANK_BAND):
        lo_row = band * RANK_BAND

        @pl.when(lo_row <= 2 * ci + 1)
        def _(lo_row=lo_row):
            imp_b = imp[lo_row:lo_row + RANK_BAND]
            row_b = row_i[lo_row:lo_row + RANK_BAND]

            def rank_body(it, cnt):
                for u in range(RANK_UNROLL):
                    b = it * RANK_UNROLL + u
                    r = imp_ref[pl.ds(b, 1), :]
                    ahead = (r > imp_b) | ((r == imp_b) & (b < row_b))
                    cnt = cnt + ahead.astype(F32)
                return cnt

            cnt_ref[lo_row:lo_row + RANK_BAND, :] = lax.fori_loop(0, n_cand, rank_body,
                                                                   jnp.zeros((RANK_BAND, Q), F32))

    sel = ((cnt_ref[...] < SLC_TOPN) & (row_i <= cur)).astype(F32)

    sel_q = jnp.transpose(sel)
    near_blk = lane_i >= 2 * ci - 2
    m_far = jnp.where((sel_q > 0) & jnp.logical_not(near_blk), 0.0, NEG).astype(BF16)
    m_near = jnp.where((sel_q > 0) & near_blk, 0.0, NEG).astype(BF16)
    qa_far = jnp.concatenate([qs, jnp.concatenate([m_far] * H4, axis=0)], axis=1)
    qa_near = jnp.concatenate([qs, jnp.concatenate([m_near] * H4, axis=0)], axis=1)

    def slc_rows(kt0, nt):
        r0 = pl.multiple_of((kt0 + 1) * Q, Q)
        return ks_ref[pl.ds(r0, nt * Q), :], vs_ref[pl.ds(r0, nt * Q), :]

    def online_update(streams):
        m_new = [jnp.maximum(c[0], jnp.max(s, axis=0, keepdims=True)) for s, _, c in streams]
        alpha = [jnp.exp2(c[0] - mn) for (_, _, c), mn in zip(streams, m_new)]
        p = [jnp.exp2(s - mn) for (s, _, _), mn in zip(streams, m_new)]
        l_new = [a * c[1] + jnp.sum(pp, axis=0, keepdims=True) for a, pp, (_, _, c) in zip(alpha, p, streams)]
        pv = [_dot_tn(v, pp.astype(BF16)) for pp, (_, v, _) in zip(p, streams)]
        return [(mn, ln, a * c[2] + x) for mn, ln, a, x, (_, _, c) in zip(m_new, l_new, alpha, pv, streams)]

    carry0 = (jnp.full((1, H4 * Q), M_FLOOR, F32), jnp.zeros((1, H4 * Q), F32), jnp.zeros((HD, H4 * Q), F32))
    n_far = jnp.maximum(ci - 1, 0)

    def far_body(it, carries):
        kt0 = it * (2 * SLC_GROUP)
        k_a, v_a = slc_rows(kt0, SLC_GROUP)
        k_b, v_b = slc_rows(kt0 + SLC_GROUP, SLC_GROUP)
        s_a = _dot_nt(k_a, qa_far)
        s_b = _dot_nt(k_b, qa_far)
        return tuple(online_update([(s_a, v_a, carries[0]), (s_b, v_b, carries[1])]))

    (m_a, l_a, acc_a), (m_b, l_b, acc_b) = lax.fori_loop(
        0, (n_far + 2 * SLC_GROUP - 1) // (2 * SLC_GROUP), far_body, (carry0, carry0))
    m_ab = jnp.maximum(m_a, m_b)
    w_a = jnp.exp2(m_a - m_ab)
    w_b = jnp.exp2(m_b - m_ab)
    carry = (m_ab, w_a * l_a + w_b * l_b, w_a * acc_a + w_b * acc_b)
    bp_t = head_tiles(bp_ref)
    bd_t = head_tiles(bd_ref)
    k_n, v_n = slc_rows(ci - 1, 2)
    s_n = _dot_nt(k_n, qa_near) + jnp.concatenate([bp_t, bd_t], axis=0)
    (_, l_s, acc_s), = online_update([(s_n, v_n, carry)])
    o_slc = acc_s * (1.0 / jnp.where(l_s > 0, l_s, 1.0))

    kw = kw_ref[pl.ds(pl.multiple_of(ci * Q, Q), WINDOW + Q), :]
    vw = vw_ref[pl.ds(pl.multiple_of(ci * Q, Q), WINDOW + Q), :]
    s_w = _dot_nt(kw, qs)
    n_wt = WINDOW // Q
    in_window = lane_i < row_i
    pieces = [jnp.concatenate([jnp.where(in_window, s_w[0:Q, h * Q:(h + 1) * Q], NEG) for h in range(H4)], axis=1)]
    for t in range(1, n_wt - 1):
        pieces.append(s_w[t * Q:(t + 1) * Q])
    pieces.append(s_w[(n_wt - 1) * Q:n_wt * Q] + bp_t)
    pieces.append(s_w[n_wt * Q:(n_wt + 1) * Q] + bd_t)
    s_w = jnp.concatenate(pieces, axis=0)
    x_w = lax.broadcasted_iota(jnp.int32, (WINDOW + Q, 1), 0)
    s_w = jnp.where(x_w + ci * Q >= WINDOW, s_w, NEG)
    e_w, d_w = _col_softmax_stats(s_w)
    p_w = e_w * (1.0 / jnp.where(d_w > 0, d_w, 1.0))
    o_win = _dot_tn(vw, p_w.astype(BF16))

    gt = jnp.transpose(jax.nn.sigmoid(gl_ref[...]))
    outs = []
    for h in range(H4):
        base = (g * H4 + h) * 3
        sl = slice(h * Q, (h + 1) * Q)
        gc = gt_row(gt, base)
        gs = gt_row(gt, base + 1)
        gw = gt_row(gt, base + 2)
        o_h = gc * o_cmp[:, sl] + gs * o_slc[:, sl] + gw * o_win[:, sl]
        outs.append(jnp.transpose(o_h))
    o_ref[...] = jnp.concatenate(outs, axis=1).astype(o_ref.dtype)


def gt_row(gt, idx):
    rows = lax.broadcasted_iota(jnp.int32, gt.shape, 0)
    return jnp.sum(jnp.where(rows == idx, gt, 0.0), axis=0, keepdims=True)


def _nsa_attention(rel_bias, proj, proj_gate, kcmp, vcmp, ks, vs, kw, vw):
    T = proj.shape[0]
    G = NSA_KV_HEADS
    Q = Q_BLOCK
    ncp = T // CMP_STRIDE
    nch = T // Q
    assert nch % SLC_GROUP == 0 and T // SLC_LEN <= Q
    kernel = functools.partial(_nsa_kernel, ncp=ncp)
    full = lambda rows, w=HEAD_DIM: pl.BlockSpec((None, rows, w), lambda g, c: (g, 0, 0))
    return pl.pallas_call(
        kernel,
        out_shape=jax.ShapeDtypeStruct((T, NSA_HEADS * HEAD_DIM), BF16),
        grid=(G, nch),
        in_specs=[pl.BlockSpec(memory_space=pltpu.SMEM),
                  pl.BlockSpec((Q, NSA_HPG * HEAD_DIM), lambda g, c: (c, g)),
                  pl.BlockSpec((Q, LANES), lambda g, c: (c, 0)),
                  full(ncp + 2 * Q), full(ncp + 2 * Q),
                  full(T + Q + SLC_TAIL, HEAD_DIM + Q), full(T + Q + SLC_TAIL), full(T + WINDOW), full(T + WINDOW)],
        out_specs=pl.BlockSpec((Q, NSA_HPG * HEAD_DIM), lambda g, c: (c, g)),
        scratch_shapes=[pltpu.VMEM((NSA_HEADS, Q, Q), F32), pltpu.VMEM((NSA_HEADS, Q, Q), F32),
                        pltpu.VMEM((NSA_HEADS, Q, Q), F32), pltpu.VMEM((Q, Q), F32), pltpu.VMEM((Q, Q), F32)],
        compiler_params=_cparams(("arbitrary", "arbitrary")),
        name="nsa_attention",
    )(rel_bias, proj, proj_gate, kcmp, vcmp, ks, vs, kw, vw)


def _softplus(z):
    return jnp.maximum(z, 0.0) + jnp.log1p(jnp.exp(-jnp.abs(z)))


def _lru_kernel(x_ref, y_ref, cw_ref, cb_ref, wa_ref, ba_ref, wi_ref, bi_ref, lam_ref, o_ref,
                xbuf, hc, a_s, b_s, h_s, *, tb):
    i = pl.program_id(0)

    @pl.when(i == 0)
    def _():
        xbuf[0:8, :] = jnp.zeros((8, LRU_WIDTH), F32)
        hc[...] = jnp.zeros((1, LRU_WIDTH), F32)

    xbuf[8:8 + tb, :] = x_ref[...]
    xc = cb_ref[...]
    for w in range(CONV_W):
        xc = xc + cw_ref[w:w + 1, :] * xbuf[8 - (CONV_W - 1) + w:8 - (CONV_W - 1) + w + tb, :]
    xbuf[0:8, :] = xbuf[tb:tb + 8, :]

    xcb = xc.astype(BF16)
    ra, ia = [], []
    for n in range(LRU_BLOCKS):
        xg = xcb[:, n * LRU_BW:(n + 1) * LRU_BW]
        ra.append(_dot(xg, wa_ref[n].astype(BF16)))
        ia.append(_dot(xg, wi_ref[n].astype(BF16)))
    r_gate = jax.nn.sigmoid(jnp.concatenate(ra, axis=1) + ba_ref[...])
    i_gate = jax.nn.sigmoid(jnp.concatenate(ia, axis=1) + bi_ref[...])
    log_a = (-LRU_C * r_gate) * _softplus(-lam_ref[...])
    a = jnp.exp(log_a)
    a_s[...] = a
    b_s[...] = jnp.sqrt(-jnp.tanh(log_a) * (a * a + 1.0)) * (i_gate * xc)

    rows = lax.broadcasted_iota(jnp.int32, (8, LRU_WIDTH), 0)

    def tile(k, h):
        r0 = pl.multiple_of(k * 8, 8)
        A = a_s[pl.ds(r0, 8), :]
        B = b_s[pl.ds(r0, 8), :]
        for sh in (1, 2, 4):
            ok = rows >= sh
            A_p = pltpu.roll(A, sh, axis=0)
            B_p = pltpu.roll(B, sh, axis=0)
            B = jnp.where(ok, A * B_p + B, B)
            A = jnp.where(ok, A * A_p, A)
        H = A * h + B
        h_s[pl.ds(r0, 8), :] = H
        return H[7:8, :]

    hc[...] = lax.fori_loop(0, tb // 8, tile, hc[...])
    o_ref[...] = (jax.nn.gelu(y_ref[...]) * h_s[...]).astype(o_ref.dtype)


def _rglru(proj, y_col, x_col, conv_w, conv_b, wa, ba, wi, bi, lam, tb=256):
    T = proj.shape[0]
    W = LRU_WIDTH
    vec = pl.BlockSpec((1, W), lambda i: (0, 0))
    blk = pl.BlockSpec((LRU_BLOCKS, LRU_BW, LRU_BW), lambda i: (0, 0, 0))
    return pl.pallas_call(
        functools.partial(_lru_kernel, tb=tb),
        out_shape=jax.ShapeDtypeStruct((T, W), BF16),
        grid=(T // tb,),
        in_specs=[pl.BlockSpec((tb, W), lambda i: (i, x_col)),
                  pl.BlockSpec((tb, W), lambda i: (i, y_col)),
                  pl.BlockSpec((CONV_W, W), lambda i: (0, 0)), vec, blk, vec, blk, vec, vec],
        out_specs=pl.BlockSpec((tb, W), lambda i: (i, 0)),
        scratch_shapes=[pltpu.VMEM((tb + 8, W), F32), pltpu.VMEM((1, W), F32),
                        pltpu.VMEM((tb, W), F32), pltpu.VMEM((tb, W), F32), pltpu.VMEM((tb, W), F32)],
        compiler_params=_cparams(("arbitrary",)),
        name="rglru",
    )(proj, proj, conv_w, conv_b.reshape(1, W), wa, ba.reshape(1, W), wi, bi.reshape(1, W), lam.reshape(1, W))


def _split3(x):
    hi = x.astype(BF16)
    r = x - hi.astype(F32)
    mid = r.astype(BF16)
    lo = (r - mid.astype(F32)).astype(BF16)
    return hi, mid, lo


def _hgrn_kernel(q_ref, f_ref, v_ref, g_ref, lbl_ref, ng_ref, o_ref, st_ref, oacc, *, tbh, layer):
    t = pl.program_id(1)
    C = HG_CHUNK
    SB = HG_SUB

    NH = HG_HEADS_PER_STEP
    DK, DV = HG_DK, HG_DV

    @pl.when(t == 0)
    def _():
        st_ref[...] = jnp.zeros((NH, DV, DK), F32)

    lg = lbl_ref[...]
    e = jnp.exp(lg - jnp.max(lg, axis=0, keepdims=True))
    sm = e / jnp.sum(e, axis=0, keepdims=True)
    cum = sm[0:1, :]
    for l in range(1, layer + 1):
        cum = cum + sm[l:l + 1, :]
    lb_all = cum - sm[0:1, :]

    tri = (lax.broadcasted_iota(jnp.int32, (C, C), 0) >= lax.broadcasted_iota(jnp.int32, (C, C), 1)).astype(BF16)
    rows8 = lax.broadcasted_iota(jnp.int32, (8, DK), 0)

    def head_chunk(r0, hh):
        cs = slice(hh * DK, (hh + 1) * DK)
        lb = lb_all[:, cs]
        qz = q_ref[pl.ds(r0, C), cs]
        qq = qz * jax.nn.sigmoid(qz)
        f = lb + (1.0 - lb) * jax.nn.sigmoid(f_ref[pl.ds(r0, C), cs])
        logf = jnp.log2(jnp.maximum(f, 1e-30))
        kk = 1.0 - f
        vv = v_ref[pl.ds(r0, C), cs]
        l_hi, l_mid, l_lo = _split3(logf)
        bcum = _dot(tri, l_hi) + _dot(tri, l_mid) + _dot(tri, l_lo)
        st = st_ref[hh]
        o_parts = []
        vb = vv.astype(BF16)
        for I in range(C // SB):
            lo_r = I * SB
            for a in range(SB // 8):
                t0 = lo_r + 8 * a
                q8 = qq[t0:t0 + 8]
                b8 = bcum[t0:t0 + 8]
                o8 = jnp.zeros((8, DV), F32)
                for s in range(lo_r, t0 + 8):
                    d = b8 - bcum[s:s + 1]
                    if s > t0:
                        d = jnp.where(rows8 >= s - t0, d, NEG)
                    w = q8 * (kk[s:s + 1] * jnp.exp2(d))
                    o8 = o8 + jnp.sum(w, axis=1, keepdims=True) * vv[s:s + 1]
                o_parts.append(o8)
        for I in range(1, C // SB):
            lo_r = I * SB
            ref = bcum[lo_r - 1:lo_r]
            qe = (qq[lo_r:lo_r + SB] * jnp.exp2(bcum[lo_r:lo_r + SB] - ref)).astype(BF16)
            ke = (kk[0:lo_r] * jnp.exp2(ref - bcum[0:lo_r])).astype(BF16)
            att = _dot_nt(qe, ke)
            o_off = _dot(att.astype(BF16), vb[0:lo_r])
            for a in range(SB // 8):
                o_parts[lo_r // 8 + a] = o_parts[lo_r // 8 + a] + o_off[8 * a:8 * a + 8]
        o_intra = jnp.concatenate(o_parts, axis=0)
        o_inter = _dot_nt((qq * jnp.exp2(bcum)).astype(BF16), st.astype(BF16))
        oacc[pl.ds(r0, C), cs] = o_inter + o_intra
        b_last = bcum[C - 1:C]
        kd = (kk * jnp.exp2(b_last - bcum)).astype(BF16)
        st_ref[hh] = st * jnp.exp2(b_last) + _dot_tn(vb, kd)

    def chunk(c, carry):
        r0 = pl.multiple_of(c * C, C)
        for hh in range(NH):
            head_chunk(r0, hh)
        return carry

    lax.fori_loop(0, tbh // C, chunk, 0)
    gz = g_ref[...]
    gate = (gz * jax.nn.sigmoid(gz)) * ng_ref[...]
    for hh in range(NH):
        cs = slice(hh * DV, (hh + 1) * DV)
        o = oacc[:, cs]
        o = o * lax.rsqrt(jnp.mean(o * o, axis=-1, keepdims=True) + EPS)
        o_ref[:, cs] = (o * gate[:, cs]).astype(o_ref.dtype)


def _hgrn2(proj, lb_logits, norm_g, layer, tbh=512):
    T = proj.shape[0]
    NH = HG_HEADS_PER_STEP
    HB = HG_HEADS // NH
    wk = NH * HG_DK
    col = lambda off: pl.BlockSpec((tbh, wk), lambda h, t: (t, off + h))
    return pl.pallas_call(
        functools.partial(_hgrn_kernel, tbh=tbh, layer=layer),
        out_shape=jax.ShapeDtypeStruct((T, HG_VW), BF16),
        grid=(HB, T // tbh),
        in_specs=[col(0), col(HB), col(2 * HB), col(3 * HB),
                  pl.BlockSpec((DEPTH, wk), lambda h, t: (0, h)),
                  pl.BlockSpec((1, wk), lambda h, t: (0, h))],
        out_specs=pl.BlockSpec((tbh, wk), lambda h, t: (t, h)),
        scratch_shapes=[pltpu.VMEM((NH, HG_DV, HG_DK), F32), pltpu.VMEM((tbh, wk), F32)],
        compiler_params=_cparams(("arbitrary", "arbitrary")),
        name="hgrn2",
    )(proj, proj, proj, proj, lb_logits, norm_g.reshape(1, HG_VW))


def _router_kernel(x_ref, g_ref, sc_ref, sh_ref, w_ref, b_ref, h_ref, eid_ref, gate_ref, cnt_ref):
    h = _norm_mod(x_ref[...], g_ref[...], sc_ref[...], sh_ref[...])
    tm, d = h.shape
    hb = h.astype(BF16)
    lo = lax.bitcast_convert_type(hb[:, :d // 2].astype(F32), jnp.uint32) >> 16
    hi = lax.bitcast_convert_type(hb[:, d // 2:].astype(F32), jnp.uint32)
    packed = hi | lo
    pt = d // 2 // LANES
    for c in range(pt):
        h_ref[pl.ds(c, tm, stride=pt), :] = packed[:, c * LANES:(c + 1) * LANES]
    h1, h2, h3 = _split3(h)
    w1, w2, w3 = _split3(w_ref[...])
    a = _dot(h1, jnp.concatenate([w1, w2, w3], axis=1))
    b = _dot(h2, jnp.concatenate([w1, w2], axis=1))
    c = _dot(h3, w1)
    L = LANES
    logits = (a[:, 0:L] + (a[:, L:2 * L] + b[:, 0:L])
              + (b[:, L:2 * L] + a[:, 2 * L:3 * L] + c)) + b_ref[...]
    lane = lax.broadcasted_iota(jnp.int32, logits.shape, 1)
    is_g = lane < N_GROUPS
    glog = jnp.where(is_g, logits, -jnp.inf)
    gmax = jnp.max(glog, axis=-1, keepdims=True)
    grp = jnp.min(jnp.where(glog == gmax, lane, LANES), axis=-1, keepdims=True)
    gsum = jnp.sum(jnp.where(is_g, jnp.exp(glog - gmax), 0.0), axis=-1, keepdims=True)
    p_grp = 1.0 / gsum
    lo = N_GROUPS + EXP_PER_GROUP * grp
    el = jnp.where((lane >= lo) & (lane < lo + EXP_PER_GROUP), logits, -jnp.inf)
    v1 = jnp.max(el, axis=-1, keepdims=True)
    i1 = jnp.min(jnp.where(el == v1, lane, LANES), axis=-1, keepdims=True)
    el2 = jnp.where(lane == i1, -jnp.inf, el)
    v2 = jnp.max(el2, axis=-1, keepdims=True)
    i2 = jnp.min(jnp.where(el2 == v2, lane, LANES), axis=-1, keepdims=True)
    e2 = jnp.exp(v2 - v1)
    den = 1.0 + e2
    g1 = p_grp * (1.0 / den)
    g2 = p_grp * (e2 / den)
    eid_ref[...] = jnp.where(lane == 0, i1 - N_GROUPS, jnp.where(lane == 1, i2 - N_GROUPS, 0))
    gate_ref[...] = jnp.where(lane == 0, g1, jnp.where(lane == 1, g2, 0.0))

    @pl.when(pl.program_id(0) == 0)
    def _():
        cnt_ref[...] = jnp.zeros(cnt_ref.shape, F32)

    picked = ((lane == i1 - N_GROUPS) | (lane == i2 - N_GROUPS)).astype(F32)
    cnt_ref[...] += jnp.sum(picked, axis=0, keepdims=True)


def _dispatch_kernel(eid_ref, base_ref, dest_ref, tok_ref, carry, dstage, dsm, zbuf, sem, *, tb):
    i = pl.program_id(0)

    @pl.when(i == 0)
    def _():
        carry[...] = jnp.zeros(carry.shape, F32)
        zbuf[...] = jnp.zeros(zbuf.shape, jnp.int32)
        z = pltpu.make_async_copy(zbuf, tok_ref, sem)
        z.start()
        z.wait()

    e = eid_ref[...]
    lane = lax.broadcasted_iota(jnp.int32, e.shape, 1)
    oh0 = (lane == e[:, 0:1]).astype(F32)
    oh1 = (lane == e[:, 1:2]).astype(F32)
    both = oh0 + oh1
    before = (lax.broadcasted_iota(jnp.int32, (tb, tb), 1) < lax.broadcasted_iota(jnp.int32, (tb, tb), 0)).astype(BF16)
    prior = _dot(before, both.astype(BF16)) + (carry[...] + base_ref[...])
    d0 = jnp.sum(oh0 * prior, axis=1, keepdims=True)
    d1 = jnp.sum(oh1 * (prior + oh0), axis=1, keepdims=True)
    carry[...] += jnp.sum(both, axis=0, keepdims=True)
    dest = jnp.where(lane == 0, d0, jnp.where(lane == 1, d1, 0.0)).astype(jnp.int32)
    dest_ref[...] = dest

    dstage[...] = jnp.transpose(dest)[0:8, :]
    cp = pltpu.make_async_copy(dstage, dsm, sem)
    cp.start()
    cp.wait()

    def body(n, c):
        for k in range(TOPK_IN_GROUP):
            p = dsm[k, n]
            tok_ref[p >> 7, p & (LANES - 1)] = i * tb + n
        return c

    lax.fori_loop(0, tb, body, 0, unroll=8)


def _dispatch(eid_l, base, n_blk, tb=512):
    N = eid_l.shape[0]
    return pl.pallas_call(
        functools.partial(_dispatch_kernel, tb=tb),
        out_shape=[jax.ShapeDtypeStruct((N, LANES), jnp.int32), jax.ShapeDtypeStruct((n_blk, LANES), jnp.int32)],
        grid=(N // tb,),
        in_specs=[pl.BlockSpec((tb, LANES), lambda i: (i, 0)), pl.BlockSpec((1, LANES), lambda i: (0, 0))],
        out_specs=[pl.BlockSpec((tb, LANES), lambda i: (i, 0)), pl.BlockSpec(memory_space=pltpu.SMEM)],
        scratch_shapes=[pltpu.VMEM((1, LANES), F32), pltpu.VMEM((8, tb), jnp.int32), pltpu.SMEM((8, tb), jnp.int32),
                        pltpu.VMEM((n_blk, LANES), jnp.int32), pltpu.SemaphoreType.DMA(())],
        compiler_params=_cparams(("arbitrary",)),
        name="moe_dispatch",
    )(eid_l, base)


def _router(x, g, sc, sh, w_r, b_r, tm=512):
    T, D = x.shape
    vec = pl.BlockSpec((1, D), lambda i: (0, 0))
    return pl.pallas_call(
        _router_kernel,
        out_shape=[jax.ShapeDtypeStruct((T * PACKED_ROWS, LANES), jnp.uint32), jax.ShapeDtypeStruct((T, LANES), jnp.int32),
                   jax.ShapeDtypeStruct((T, LANES), F32), jax.ShapeDtypeStruct((1, LANES), F32)],
        grid=(T // tm,),
        in_specs=[pl.BlockSpec((tm, D), lambda i: (i, 0)), vec, vec, vec,
                  pl.BlockSpec((D, LANES), lambda i: (0, 0)), pl.BlockSpec((1, LANES), lambda i: (0, 0))],
        out_specs=[pl.BlockSpec((tm * PACKED_ROWS, LANES), lambda i: (i, 0)),
                   pl.BlockSpec((tm, LANES), lambda i: (i, 0)),
                   pl.BlockSpec((tm, LANES), lambda i: (i, 0)),
                   pl.BlockSpec((1, LANES), lambda i: (0, 0))],
        compiler_params=_cparams(("arbitrary",)),
        name="moe_router",
    )(x, g, sc, sh, w_r, b_r)


def _cast_rows(src_ref, slot, dst_ref, rows):
    def body(c, carry):
        r = pl.multiple_of(c * rows, rows)
        dst_ref[pl.ds(r, rows), :] = src_ref[slot, pl.ds(r, rows), :].astype(BF16)
        return carry
    lax.fori_loop(0, dst_ref.shape[0] // rows, body, 0, unroll=4)


def _expert_kernel(rid_ref, re_ref, nu_ref, nv_ref, tok_ref, h_ref, wg_hbm, wu_hbm, wd_hbm, y_ref,
                   wgf, wuf, wdf, wgb, wub, wdb, xbuf, wsem, xsem, *, layer):
    i = pl.program_id(0)
    n_used = nu_ref[0]
    R = MOE_BLOCK

    def w_copies(e, slot):
        return ((pltpu.make_async_copy(wg_hbm.at[layer, e], wgf.at[slot], wsem.at[slot, 0]), 1),
                (pltpu.make_async_copy(wu_hbm.at[layer, e], wuf.at[slot], wsem.at[slot, 1]), 1),
                (pltpu.make_async_copy(wd_hbm.at[layer, e], wdf.at[slot], wsem.at[slot, 2]), 1))

    HT = PACKED_ROWS

    GR = 8

    def groups(blk):
        return (nv_ref[blk] + GR - 1) // GR

    def x_start(blk, slot):
        def body(gi, c):
            for u in range(GR):
                r = gi * GR + u
                tok = tok_ref[blk * R + r]
                pltpu.make_async_copy(h_ref.at[pl.ds(pl.multiple_of(tok * HT, HT), HT)],
                                      xbuf.at[slot, pl.ds(pl.multiple_of(r * HT, HT), HT)], xsem.at[slot]).start()
            return c
        lax.fori_loop(0, groups(blk), body, 0)

    def x_wait(blk, slot):
        def body(gi, c):
            pltpu.make_async_copy(h_ref.at[pl.ds(0, GR * HT)], xbuf.at[slot, pl.ds(0, GR * HT)], xsem.at[slot]).wait()
            return c
        lax.fori_loop(0, groups(blk), body, 0)

    def w_request(run):
        @pl.when(re_ref[run] >= 0)
        def _():
            for c, pri in w_copies(re_ref[run], run % 2):
                c.start(priority=pri)

    @pl.when(i == 0)
    def _():
        w_request(0)
        w_request(1)
        xbuf[...] = jnp.zeros(xbuf.shape, xbuf.dtype)
        x_start(0, 0)

    @pl.when(i < n_used)
    def _():
        slot = i % 2

        @pl.when(i + 1 < n_used)
        def _():
            x_start(i + 1, 1 - slot)

        run = rid_ref[i]

        @pl.when((i == 0) | (run != rid_ref[jnp.maximum(i - 1, 0)]))
        def _():
            ws = run % 2
            for c, _ in w_copies(re_ref[run], ws):
                c.wait()
            _cast_rows(wgf, ws, wgb, 64)
            _cast_rows(wuf, ws, wub, 64)
            _cast_rows(wdf, ws, wdb, 16)
            w_request(run + 2)

        x_wait(i, slot)
        words = [xbuf[slot, pl.ds(c, R, stride=HT), :] for c in range(HT)]
        lo = [lax.bitcast_convert_type(w << 16, F32) for w in words]
        hi = [lax.bitcast_convert_type(w & jnp.uint32(0xFFFF0000), F32) for w in words]
        x = jnp.concatenate(lo + hi, axis=1).astype(BF16)
        hg = _dot(x, wgb[...])
        hu = _dot(x, wub[...])
        hid = (hg * jax.nn.sigmoid(hg)) * hu
        y_ref[...] = _dot(hid.astype(BF16), wdb[...])

    @pl.when(i >= n_used)
    def _():
        y_ref[...] = jnp.zeros(y_ref.shape, y_ref.dtype)


def _experts(h, tok, run_id, run_e, n_used, n_valid, w_gate, w_up, w_down, layer):
    D = D_MODEL
    n_pad = tok.shape[0]
    n_blk = n_pad // MOE_BLOCK
    any_spec = pl.BlockSpec(memory_space=pl.ANY)
    return pl.pallas_call(
        functools.partial(_expert_kernel, layer=layer),
        out_shape=jax.ShapeDtypeStruct((n_pad, D), F32),
        grid_spec=pltpu.PrefetchScalarGridSpec(
            num_scalar_prefetch=5,
            grid=(n_blk,),
            in_specs=[any_spec, any_spec, any_spec, any_spec],
            out_specs=pl.BlockSpec((MOE_BLOCK, D), lambda i, *_: (i, 0)),
            scratch_shapes=[pltpu.VMEM((2, D, D_EXPERT), F32), pltpu.VMEM((2, D, D_EXPERT), F32),
                            pltpu.VMEM((2, D_EXPERT, D), F32),
                            pltpu.VMEM((D, D_EXPERT), BF16), pltpu.VMEM((D, D_EXPERT), BF16),
                            pltpu.VMEM((D_EXPERT, D), BF16),
                            pltpu.VMEM((2, MOE_BLOCK * PACKED_ROWS, LANES), jnp.uint32),
                            pltpu.SemaphoreType.DMA((2, 4)), pltpu.SemaphoreType.DMA((2,))]),
        compiler_params=_cparams(("arbitrary",)),
        name="moe_experts",
    )(run_id, run_e, n_used, n_valid, tok, h, w_gate, w_up, w_down)


def _combine_kernel(slot_ref, y_ref, x_ref, w_ref, g_ref, fg_ref, o_ref, buf, sem, *, rows, final_norm):
    i = pl.program_id(0)

    def copy(r, k):
        return pltpu.make_async_copy(y_ref.at[pl.ds(slot_ref[(i * rows + r) * 2 + k], 1)],
                                     buf.at[k, pl.ds(r, 1)], sem)

    def start(r, c):
        copy(r, 0).start()
        copy(r, 1).start()
        return c

    lax.fori_loop(0, rows, start, 0, unroll=8)
    for k in range(TOPK_IN_GROUP):
        pltpu.make_async_copy(y_ref.at[pl.ds(0, rows)], buf.at[k], sem).wait()
    w = w_ref[...]
    moe = buf[0] * w[:, 0:1] + buf[1] * w[:, 1:2]
    y = x_ref[...] + g_ref[...] * moe
    if final_norm:
        y = (y * lax.rsqrt(jnp.mean(y * y, axis=-1, keepdims=True) + EPS)) * fg_ref[...]
    o_ref[...] = y


def _combine(yb, slots, x, gate_w, g2, final_g, final_norm, rows=256):
    T, D = x.shape
    return pl.pallas_call(
        functools.partial(_combine_kernel, rows=rows, final_norm=final_norm),
        out_shape=jax.ShapeDtypeStruct((T, D), F32),
        grid_spec=pltpu.PrefetchScalarGridSpec(
            num_scalar_prefetch=1,
            grid=(T // rows,),
            in_specs=[pl.BlockSpec(memory_space=pl.ANY),
                      pl.BlockSpec((rows, D), lambda i, s: (i, 0)),
                      pl.BlockSpec((rows, LANES), lambda i, s: (i, 0)),
                      pl.BlockSpec((1, D), lambda i, s: (0, 0)),
                      pl.BlockSpec((1, D), lambda i, s: (0, 0))],
            out_specs=pl.BlockSpec((rows, D), lambda i, s: (i, 0)),
            scratch_shapes=[pltpu.VMEM((2, rows, D), F32), pltpu.SemaphoreType.DMA(())]),
        compiler_params=_cparams(("arbitrary",)),
        name="moe_combine",
    )(slots.reshape(-1), yb, x, gate_w, g2, final_g)


def _hier_moe_residual(x, g, sc, sh, g2, w_grp, b_grp, w_exp, b_exp, w_gate, w_up, w_down, layer, final_g,
                       final_norm):
    N, D = x.shape
    n_route = N_GROUPS + N_EXPERTS
    w_r = jnp.zeros((D, LANES), F32).at[:, :N_GROUPS].set(w_grp).at[:, N_GROUPS:n_route].set(w_exp)
    b_r = jnp.zeros((1, LANES), F32).at[0, :N_GROUPS].set(b_grp).at[0, N_GROUPS:n_route].set(b_exp)
    h, eid_l, gate_l, cnt = _router(x, g, sc, sh, w_r, b_r)

    A = N * TOPK_IN_GROUP
    counts = cnt[0, :N_EXPERTS].astype(jnp.int32)
    padded = (counts + MOE_BLOCK - 1) // MOE_BLOCK * MOE_BLOCK
    pad_end = jnp.cumsum(padded)
    pad_start = pad_end - padded
    n_blk = -(-(A + N_EXPERTS * MOE_BLOCK) // MOE_BLOCK)
    base = jnp.zeros((1, LANES), F32).at[0, :N_EXPERTS].set(pad_start.astype(F32))
    dest_l, tok_tbl = _dispatch(eid_l, base, n_blk)
    buf_tok = tok_tbl.reshape(-1)
    blk_idx = jnp.arange(n_blk, dtype=jnp.int32)
    n_used = pad_end[-1] // MOE_BLOCK
    blk_e = jnp.minimum(jnp.sum((pad_end[None, :] <= (blk_idx * MOE_BLOCK)[:, None]).astype(jnp.int32), axis=1),
                        N_EXPERTS - 1)
    blk_e = jnp.where(blk_idx < n_used, blk_e, blk_e[n_used - 1])
    change = jnp.concatenate([jnp.ones((1,), bool), blk_e[1:] != blk_e[:-1]])
    run_id = (jnp.cumsum(change.astype(jnp.int32)) - 1).astype(jnp.int32)
    runs = jnp.arange(n_blk + 2, dtype=jnp.int32)
    run_e = jnp.max(jnp.where(run_id[None, :] == runs[:, None], blk_e[None, :], -1), axis=1).astype(jnp.int32)
    slots = dest_l[:, :TOPK_IN_GROUP]

    n_valid = jnp.clip(counts[blk_e] - (blk_idx * MOE_BLOCK - pad_start[blk_e]), 0, MOE_BLOCK).astype(jnp.int32)
    yb = _experts(h, buf_tok, run_id, run_e, n_used.reshape(1).astype(jnp.int32), n_valid, w_gate, w_up, w_down, layer)
    return _combine(yb, slots, x, gate_l, g2, final_g, final_norm)


def _even_mixer_residual(x, g, sc, sh, g1, rel_bias, w_in, w_out, cmp_pe, cmp_w1, cmp_b1, cmp_w2, cmp_b2,
                         conv_w, conv_b, lru_wa, lru_ba, lru_wi, lru_bi, lru_lambda):
    T, D = x.shape
    G, HD = NSA_KV_HEADS, HEAD_DIM
    nq = NSA_HEADS * HD
    n_kv = 6 * NSA_KV_W
    n_gate = 3 * NSA_HEADS
    w_main = jnp.concatenate([w_in[:, :nq], w_in[:, nq + n_kv + n_gate:], w_in[:, nq:nq + n_kv]],
                             axis=1).astype(BF16)
    w_gl = jnp.zeros((D, LANES), F32).at[:, :n_gate].set(w_in[:, nq + n_kv:nq + n_kv + n_gate]).astype(BF16)
    kv0 = nq + 2 * LRU_WIDTH
    tn = 768
    proj, proj_gate, kv_b = _norm_proj(x, g, sc, sh, w_main, w_gl, bf16_from=kv0 // tn, tn=tn)

    def kv_heads(j, src=kv_b, c_base=0):
        c0 = c_base + j * NSA_KV_W
        return src[:, c0:c0 + NSA_KV_W].reshape(T, G, HD).transpose(1, 0, 2)

    cmp_in = jnp.stack([kv_heads(0, proj, kv0), kv_heads(1, proj, kv0)], axis=0)
    cmp_out = _compress(cmp_in, cmp_pe, cmp_w1, cmp_b1, cmp_w2, cmp_b2)
    ncp = T // CMP_STRIDE
    valid = (jnp.arange(ncp) < ncp - 1)[None, None, :, None]
    cmp_pad = jnp.pad(jnp.where(valid, cmp_out, 0.0), ((0, 0), (0, 0), (Q_BLOCK, Q_BLOCK), (0, 0)))
    slc_pad = ((0, 0), (Q_BLOCK, SLC_TAIL), (0, 0))
    blk_of_row = jnp.pad(jnp.arange(T, dtype=jnp.int32) // SLC_LEN, (Q_BLOCK, SLC_TAIL), constant_values=Q_BLOCK - 1)
    blk_onehot = (blk_of_row[:, None] == jnp.arange(Q_BLOCK, dtype=jnp.int32)[None, :]).astype(BF16)
    ks = jnp.concatenate([jnp.pad(kv_heads(2), slc_pad), jnp.broadcast_to(blk_onehot, (G,) + blk_onehot.shape)], axis=2)
    vs = jnp.pad(kv_heads(3), slc_pad)
    kw = jnp.pad(kv_heads(4), ((0, 0), (WINDOW, 0), (0, 0)))
    vw = jnp.pad(kv_heads(5), ((0, 0), (WINDOW, 0), (0, 0)))
    nsa_out = _nsa_attention(rel_bias, proj, proj_gate, cmp_pad[0], cmp_pad[1], ks, vs, kw, vw)

    y_col = nq // LRU_WIDTH
    lru_out = _rglru(proj, y_col, y_col + 1, conv_w, conv_b, lru_wa, lru_ba, lru_wi, lru_bi, lru_lambda)
    w_o = w_out.astype(BF16)
    return _out_proj(nsa_out, lru_out, w_o, x, g1)


def _odd_mixer_residual(x, g, sc, sh, g1, lb_logits, w_in, w_out, norm_g, layer):
    proj = _norm_proj(x, g, sc, sh, w_in.astype(BF16), tn=1024)[0]
    o = _hgrn2(proj, lb_logits, norm_g, layer)
    w_o = w_out.astype(BF16)
    return _out_proj(o, o, w_o, x, g1, a2_col=1)


def kernel(x, c, rel_bias, ada_w, ada_b, norm_mix_g, norm_ffn_g, ev_w_in, ev_w_out, cmp_pe, cmp_w1, cmp_b1, cmp_w2, cmp_b2, lru_conv_w, lru_conv_b, lru_wa, lru_ba, lru_wi, lru_bi, lru_lambda, od_w_in, od_w_out, hg_lb_logits, hg_norm_g, moe_w_grp, moe_b_grp, moe_w_exp, moe_b_exp, moe_w_gate, moe_w_up, moe_w_down, final_g):
    B, T, D = x.shape
    assert B == 1 and D == D_MODEL
    xt = x.reshape(T, D)
    mod = _modulation(c, ada_w, ada_b)
    for l in range(DEPTH):
        sh1, sc1, g1, sh2, sc2, g2 = [mod[l, :, k * D:(k + 1) * D] for k in range(6)]
        gm = norm_mix_g[l].reshape(1, D)
        gf = norm_ffn_g[l].reshape(1, D)
        j = l // 2
        if l % 2 == 0:
            xt = _even_mixer_residual(xt, gm, sc1, sh1, g1, rel_bias, ev_w_in[j], ev_w_out[j], cmp_pe[j],
                                      cmp_w1[j], cmp_b1[j], cmp_w2[j], cmp_b2[j], lru_conv_w[j],
                                      lru_conv_b[j], lru_wa[j], lru_ba[j], lru_wi[j], lru_bi[j], lru_lambda[j])
        else:
            xt = _odd_mixer_residual(xt, gm, sc1, sh1, g1, hg_lb_logits, od_w_in[j], od_w_out[j],
                                     hg_norm_g[j], l)
        xt = _hier_moe_residual(xt, gf, sc2, sh2, g2, moe_w_grp[l], moe_b_grp[l], moe_w_exp[l], moe_b_exp[l],
                                moe_w_gate, moe_w_up, moe_w_down, l, final_g.reshape(1, D), l == DEPTH - 1)
    return xt.reshape(B, T, D)
```

```python
import functools
import math

import numpy as np
import jax
import jax.numpy as jnp
from jax import lax
from jax.experimental import pallas as pl
from jax.experimental.pallas import tpu as pltpu

F32 = jnp.float32
BF16 = jnp.bfloat16

D_MODEL = 2048
DEPTH = 2
NSA_HEADS = 8
NSA_KV_HEADS = 2
NSA_HPG = NSA_HEADS // NSA_KV_HEADS
HEAD_DIM = 128
NSA_KV_W = NSA_KV_HEADS * HEAD_DIM
CMP_LEN = 32
CMP_STRIDE = 16
SLC_LEN = 64
SLC_TOPN = 16
WINDOW = 512
Q_BLOCK = 128
LRU_WIDTH = 1024
LRU_BLOCKS = 8
LRU_BW = LRU_WIDTH // LRU_BLOCKS
CONV_W = 4
LRU_C = 8.0
HG_HEADS = 16
HG_DK = 128
HG_DV = 128
HG_CHUNK = 64
HG_SUB = 16
HG_HEADS_PER_STEP = 16
SLC_GROUP = 4
RANK_UNROLL = 4
RANK_BAND = 32
SLC_TAIL = 2 * SLC_GROUP * Q_BLOCK
HG_KW = HG_HEADS * HG_DK
HG_VW = HG_HEADS * HG_DV
N_BUCKETS = 32
MAX_DIST = 128
N_GROUPS = 8
EXP_PER_GROUP = 8
N_EXPERTS = N_GROUPS * EXP_PER_GROUP
TOPK_IN_GROUP = 2
D_EXPERT = 512
MOE_BLOCK = 128
EPS = 1e-6

LANES = 128
PACKED_ROWS = D_MODEL // 2 // LANES
NEG = -1e30
M_FLOOR = -1e20
LOG2E = math.log2(math.e)
VMEM_LIMIT = 56 * 1024 * 1024


def _cparams(sem):
    return pltpu.CompilerParams(dimension_semantics=sem, vmem_limit_bytes=VMEM_LIMIT)


def _dot(a, b):
    return jnp.dot(a, b, preferred_element_type=F32)


def _dot_nt(a, b):
    return lax.dot_general(a, b, (((1,), (1,)), ((), ())), preferred_element_type=F32)


def _dot_tn(a, b):
    return lax.dot_general(a, b, (((0,), (0,)), ((), ())), preferred_element_type=F32)


def _bucket_starts():
    n = np.arange(0, MAX_DIST + 1, dtype=np.int32)
    max_exact = N_BUCKETS // 2
    nf = np.maximum(n, 1).astype(np.float32)
    large = max_exact + (np.log(nf / np.float32(max_exact)) / np.float32(math.log(MAX_DIST / max_exact))
                         * np.float32(N_BUCKETS - max_exact)).astype(np.int32)
    large = np.minimum(large, N_BUCKETS - 1)
    b = np.where(n < max_exact, n, large)
    starts = [int(np.argmax(b >= k)) for k in range(N_BUCKETS)]
    assert all(b[s] == k for k, s in enumerate(starts)) and b[-1] == N_BUCKETS - 1
    return starts


BUCKET_STARTS = _bucket_starts()


def _mod_kernel(c_ref, w_ref, b_ref, o_ref):
    c = c_ref[...]
    cact = c * jax.nn.sigmoid(c)
    o_ref[...] = jnp.sum(cact * w_ref[...], axis=0, keepdims=True) + b_ref[...]


def _modulation(c, ada_w, ada_b):
    D = D_MODEL
    tn = 1024
    n_out = 6 * D
    c_col = c.reshape(D, 1)
    return pl.pallas_call(
        _mod_kernel,
        out_shape=jax.ShapeDtypeStruct((DEPTH, 1, n_out), F32),
        grid=(DEPTH, n_out // tn),
        in_specs=[pl.BlockSpec((D, 1), lambda l, j: (0, 0)),
                  pl.BlockSpec((None, D, tn), lambda l, j: (l, 0, j)),
                  pl.BlockSpec((None, 1, tn), lambda l, j: (l, 0, j))],
        out_specs=pl.BlockSpec((None, 1, tn), lambda l, j: (l, 0, j)),
        compiler_params=_cparams(("arbitrary", "arbitrary")),
        name="adaln_mod",
    )(c_col, ada_w, ada_b.reshape(DEPTH, 1, n_out))


def _norm_mod(x, g, sc, sh):
    y = x * lax.rsqrt(jnp.mean(x * x, axis=-1, keepdims=True) + EPS)
    return (y * g) * (1.0 + sc) + sh


def _norm_proj_kernel(x_ref, g_ref, sc_ref, sh_ref, w_ref, *rest, has_extra, bf16_from):
    if has_extra:
        wx_ref, o_ref, ox_ref, ob_ref, h_ref = rest
    else:
        o_ref, h_ref = rest
    j = pl.program_id(1)

    @pl.when(j == 0)
    def _():
        h = _norm_mod(x_ref[...], g_ref[...], sc_ref[...], sh_ref[...]).astype(BF16)
        h_ref[...] = h
        if has_extra:
            ox_ref[...] = _dot(h, wx_ref[...])

    y = _dot(h_ref[...], w_ref[...])
    o_ref[...] = y
    if has_extra:
        @pl.when(j >= bf16_from)
        def _():
            ob_ref[...] = y.astype(BF16)


def _norm_proj(x, g, sc, sh, w, w_extra=None, bf16_from=None, tm=1024, tn=512):
    T, D = x.shape
    N = w.shape[1]
    has_extra = w_extra is not None
    vec = pl.BlockSpec((1, D), lambda i, j: (0, 0))
    in_specs = [pl.BlockSpec((tm, D), lambda i, j: (i, 0)), vec, vec, vec,
                pl.BlockSpec((D, tn), lambda i, j: (0, j))]
    out_shape = [jax.ShapeDtypeStruct((T, N), F32)]
    out_specs = [pl.BlockSpec((tm, tn), lambda i, j: (i, j))]
    args = [x, g, sc, sh, w]
    if has_extra:
        nx = w_extra.shape[1]
        in_specs.append(pl.BlockSpec((D, nx), lambda i, j: (0, 0)))
        out_shape.append(jax.ShapeDtypeStruct((T, nx), F32))
        out_specs.append(pl.BlockSpec((tm, nx), lambda i, j: (i, 0)))
        args.append(w_extra)
        out_shape.append(jax.ShapeDtypeStruct((T, N - bf16_from * tn), BF16))
        out_specs.append(pl.BlockSpec((tm, tn), lambda i, j: (i, jnp.maximum(j - bf16_from, 0))))
    return pl.pallas_call(
        functools.partial(_norm_proj_kernel, has_extra=has_extra, bf16_from=bf16_from),
        out_shape=out_shape,
        grid=(T // tm, N // tn),
        in_specs=in_specs,
        out_specs=out_specs,
        scratch_shapes=[pltpu.VMEM((tm, D), BF16)],
        compiler_params=_cparams(("arbitrary", "arbitrary")),
        name="norm_proj",
    )(*args)


def _out_proj_kernel(a1_ref, a2_ref, w1_ref, w2_ref, x_ref, g_ref, o_ref):
    y = _dot(a1_ref[...], w1_ref[...]) + _dot(a2_ref[...], w2_ref[...])
    o_ref[...] = x_ref[...] + g_ref[...] * y


def _out_proj(a1, a2, w, x, gate, a2_col=0, tm=1024, tn=1024):
    T, D = x.shape
    K1 = K2 = w.shape[0] // 2
    return pl.pallas_call(
        _out_proj_kernel,
        out_shape=jax.ShapeDtypeStruct((T, D), F32),
        grid=(T // tm, D // tn),
        in_specs=[pl.BlockSpec((tm, K1), lambda i, j: (i, 0)),
                  pl.BlockSpec((tm, K2), lambda i, j: (i, a2_col)),
                  pl.BlockSpec((K1, tn), lambda i, j: (0, j)),
                  pl.BlockSpec((K2, tn), lambda i, j: (1, j)),
                  pl.BlockSpec((tm, tn), lambda i, j: (i, j)),
                  pl.BlockSpec((1, tn), lambda i, j: (0, j))],
        out_specs=pl.BlockSpec((tm, tn), lambda i, j: (i, j)),
        compiler_params=_cparams(("arbitrary", "arbitrary")),
        name="out_proj",
    )(a1, a2, w, w, x, gate)


def _compress_kernel(k2_ref, pe_ref, w1_ref, b1_ref, w2_ref, b2_ref, o_ref):
    half = (CMP_LEN // 2) * HEAD_DIM
    k2 = k2_ref[...]
    a = _dot((k2 + pe_ref[0:1, :]).astype(BF16), w1_ref[0:half, :].astype(BF16))
    b = _dot((k2 + pe_ref[1:2, :]).astype(BF16), w1_ref[half:2 * half, :].astype(BF16))
    nb = k2.shape[0]
    b_up = pltpu.roll(b, nb - 1, axis=0)
    hid = jax.nn.gelu(a + b_up + b1_ref[...])
    o_ref[...] = _dot(hid.astype(BF16), w2_ref[...].astype(BF16)) + b2_ref[...]


def _compress(kv, pe, w1, b1, w2, b2):
    _, G, T, HD = kv.shape
    nb = T // CMP_STRIDE
    row = CMP_STRIDE * HD
    kv2 = kv.reshape(2, G, nb, row)
    pe2 = pe.reshape(2, 2, row)
    return pl.pallas_call(
        _compress_kernel,
        out_shape=jax.ShapeDtypeStruct((2, G, nb, HD), F32),
        grid=(2, G),
        in_specs=[pl.BlockSpec((None, None, nb, row), lambda a, g: (a, g, 0, 0)),
                  pl.BlockSpec((None, 2, row), lambda a, g: (a, 0, 0)),
                  pl.BlockSpec((None, CMP_LEN * HD, HD), lambda a, g: (a, 0, 0)),
                  pl.BlockSpec((None, 1, HD), lambda a, g: (a, 0, 0)),
                  pl.BlockSpec((None, HD, HD), lambda a, g: (a, 0, 0)),
                  pl.BlockSpec((None, 1, HD), lambda a, g: (a, 0, 0))],
        out_specs=pl.BlockSpec((None, None, nb, HD), lambda a, g: (a, g, 0, 0)),
        compiler_params=_cparams(("arbitrary", "arbitrary")),
        name="nsa_compress",
    )(kv2, pe2, w1, b1.reshape(2, 1, HD), w2, b2.reshape(2, 1, HD))


def _bias_from_rel(rel, tbl_ref, head):
    far = tbl_ref[N_BUCKETS - 1, head]
    val = jnp.zeros(rel.shape, F32)
    for b in range(N_BUCKETS - 2, -1, -1):
        val = jnp.where(rel < BUCKET_STARTS[b + 1], (tbl_ref[b, head] - far) * LOG2E, val)
    return val


def _lane_tile4(x):
    return jnp.concatenate([x, x, x, x], axis=1)


def _col_softmax_stats(s):
    m = jnp.maximum(jnp.max(s, axis=0, keepdims=True), M_FLOOR)
    e = jnp.exp2(s - m)
    d = jnp.sum(e, axis=0, keepdims=True)
    return e, d


def _nsa_kernel(tbl_ref, q_ref, gl_ref, kc_ref, vc_ref, ks_ref, vs_ref, kw_ref, vw_ref, o_ref,
                bd_ref, bp_ref, bn_ref, imp_ref, cnt_ref, *, ncp):
    g = pl.program_id(0)
    ci = pl.program_id(1)
    Q = Q_BLOCK
    HD = HEAD_DIM
    H4 = NSA_HPG

    row_i = lax.broadcasted_iota(jnp.int32, (Q, Q), 0)
    lane_i = lax.broadcasted_iota(jnp.int32, (Q, Q), 1)

    @pl.when((g == 0) & (ci == 0))
    def _build_bias_tiles():
        for h in range(NSA_HEADS):
            rel_d = lane_i - row_i
            bd_ref[h] = jnp.where(rel_d >= 0, _bias_from_rel(jnp.maximum(rel_d, 0), tbl_ref, h), NEG)
            bp_ref[h] = _bias_from_rel(lane_i - row_i + Q, tbl_ref, h)
            rel_n = lane_i - CMP_STRIDE * row_i + (CMP_STRIDE * (Q - 8) - (CMP_LEN - 1))
            bn_ref[h] = jnp.where(rel_n >= 0, _bias_from_rel(jnp.maximum(rel_n, 0), tbl_ref, h), NEG)

    def head_tiles(ref):
        return jnp.concatenate([ref[g * H4 + h] for h in range(H4)], axis=1)

    qt = q_ref[...] * (HD ** -0.5 * LOG2E)
    qs = jnp.concatenate([qt[:, h * HD:(h + 1) * HD] for h in range(H4)], axis=0).astype(BF16)

    near0 = 8 * ci - (Q - 8)
    kc_far = kc_ref[pl.ds(Q, ncp), :]
    vc_far = vc_ref[pl.ds(Q, ncp), :]
    near_row = pl.multiple_of(8 * ci + 8, 8)
    kc_near = kc_ref[pl.ds(near_row, Q), :]
    vc_near = vc_ref[pl.ds(near_row, Q), :]
    n_far = lax.broadcasted_iota(jnp.int32, (ncp, 1), 0)
    s_far = jnp.where(n_far < near0, _dot_nt(kc_far.astype(BF16), qs), NEG)
    m_near = lax.broadcasted_iota(jnp.int32, (Q, 1), 0)
    s_near = _dot_nt(kc_near.astype(BF16), qs) + head_tiles(bn_ref)
    s_near = jnp.where(m_near + near0 >= 0, s_near, NEG)
    s_c = jnp.concatenate([s_far, s_near], axis=0)
    e_c, d_c = _col_softmax_stats(s_c)
    p_c = e_c * (1.0 / jnp.where(d_c > 0, d_c, 1.0))
    v_c = jnp.concatenate([vc_far, vc_near], axis=0).astype(BF16)
    o_cmp = _dot_tn(v_c, p_c.astype(BF16))

    psum = p_c[:, 0:Q] + p_c[:, Q:2 * Q] + p_c[:, 2 * Q:3 * Q] + p_c[:, 3 * Q:4 * Q]
    p_hi = psum.astype(BF16)
    p_lo = (psum - p_hi.astype(F32)).astype(BF16)
    jb = lax.broadcasted_iota(jnp.int32, (Q, ncp), 0)
    nb = lax.broadcasted_iota(jnp.int32, (Q, ncp), 1)
    ratio = SLC_LEN // CMP_STRIDE
    span = CMP_LEN // CMP_STRIDE - 1
    cover_far = ((nb >= ratio * jb - span) & (nb <= ratio * jb + ratio - 1)).astype(BF16)
    nn = lane_i + near0
    cover_near = ((nn >= ratio * row_i - span) & (nn <= ratio * row_i + ratio - 1)).astype(BF16)
    cover = jnp.concatenate([cover_far, cover_near], axis=1)
    imp = _dot(cover, p_hi) + _dot(cover, p_lo)
    cur = 2 * ci + (lane_i >= SLC_LEN).astype(jnp.int32)
    imp = jnp.where(row_i == cur, jnp.inf, jnp.where(row_i > cur, -jnp.inf, imp))
    imp_ref[...] = imp

    n_cand = (2 * ci + 2 + RANK_UNROLL - 1) // RANK_UNROLL
    cnt_ref[...] = jnp.zeros((Q, Q), F32)
    for band in range(Q // RANK_BAND):
        lo_row = band * RANK_BAND

        @pl.when(lo_row <= 2 * ci + 1)
        def _(lo_row=lo_row):
            imp_b = imp[lo_row:lo_row + RANK_BAND]
            row_b = row_i[lo_row:lo_row + RANK_BAND]

            def rank_body(it, cnt):
                for u in range(RANK_UNROLL):
                    b = it * RANK_UNROLL + u
                    r = imp_ref[pl.ds(b, 1), :]
                    ahead = (r > imp_b) | ((r == imp_b) & (b < row_b))
                    cnt = cnt + ahead.astype(F32)
                return cnt

            cnt_ref[lo_row:lo_row + RANK_BAND, :] = lax.fori_loop(0, n_cand, rank_body,
                                                                   jnp.zeros((RANK_BAND, Q), F32))

    sel = ((cnt_ref[...] < SLC_TOPN) & (row_i <= cur)).astype(F32)

    sel_q = jnp.transpose(sel)
    near_blk = lane_i >= 2 * ci - 2
    m_far = jnp.where((sel_q > 0) & jnp.logical_not(near_blk), 0.0, NEG).astype(BF16)
    m_near = jnp.where((sel_q > 0) & near_blk, 0.0, NEG).astype(BF16)
    qa_far = jnp.concatenate([qs, jnp.concatenate([m_far] * H4, axis=0)], axis=1)
    qa_near = jnp.concatenate([qs, jnp.concatenate([m_near] * H4, axis=0)], axis=1)

    def slc_rows(kt0, nt):
        r0 = pl.multiple_of((kt0 + 1) * Q, Q)
        return ks_ref[pl.ds(r0, nt * Q), :], vs_ref[pl.ds(r0, nt * Q), :]

    def online_update(streams):
        m_new = [jnp.maximum(c[0], jnp.max(s, axis=0, keepdims=True)) for s, _, c in streams]
        alpha = [jnp.exp2(c[0] - mn) for (_, _, c), mn in zip(streams, m_new)]
        p = [jnp.exp2(s - mn) for (s, _, _), mn in zip(streams, m_new)]
        l_new = [a * c[1] + jnp.sum(pp, axis=0, keepdims=True) for a, pp, (_, _, c) in zip(alpha, p, streams)]
        pv = [_dot_tn(v, pp.astype(BF16)) for pp, (_, v, _) in zip(p, streams)]
        return [(mn, ln, a * c[2] + x) for mn, ln, a, x, (_, _, c) in zip(m_new, l_new, alpha, pv, streams)]

    carry0 = (jnp.full((1, H4 * Q), M_FLOOR, F32), jnp.zeros((1, H4 * Q), F32), jnp.zeros((HD, H4 * Q), F32))
    n_far = jnp.maximum(ci - 1, 0)

    def far_body(it, carries):
        kt0 = it * (2 * SLC_GROUP)
        k_a, v_a = slc_rows(kt0, SLC_GROUP)
        k_b, v_b = slc_rows(kt0 + SLC_GROUP, SLC_GROUP)
        s_a = _dot_nt(k_a, qa_far)
        s_b = _dot_nt(k_b, qa_far)
        return tuple(online_update([(s_a, v_a, carries[0]), (s_b, v_b, carries[1])]))

    (m_a, l_a, acc_a), (m_b, l_b, acc_b) = lax.fori_loop(
        0, (n_far + 2 * SLC_GROUP - 1) // (2 * SLC_GROUP), far_body, (carry0, carry0))
    m_ab = jnp.maximum(m_a, m_b)
    w_a = jnp.exp2(m_a - m_ab)
    w_b = jnp.exp2(m_b - m_ab)
    carry = (m_ab, w_a * l_a + w_b * l_b, w_a * acc_a + w_b * acc_b)
    bp_t = head_tiles(bp_ref)
    bd_t = head_tiles(bd_ref)
    k_n, v_n = slc_rows(ci - 1, 2)
    s_n = _dot_nt(k_n, qa_near) + jnp.concatenate([bp_t, bd_t], axis=0)
    (_, l_s, acc_s), = online_update([(s_n, v_n, carry)])
    o_slc = acc_s * (1.0 / jnp.where(l_s > 0, l_s, 1.0))

    kw = kw_ref[pl.ds(pl.multiple_of(ci * Q, Q), WINDOW + Q), :]
    vw = vw_ref[pl.ds(pl.multiple_of(ci * Q, Q), WINDOW + Q), :]
    s_w = _dot_nt(kw, qs)
    n_wt = WINDOW // Q
    in_window = lane_i < row_i
    pieces = [jnp.concatenate([jnp.where(in_window, s_w[0:Q, h * Q:(h + 1) * Q], NEG) for h in range(H4)], axis=1)]
    for t in range(1, n_wt - 1):
        pieces.append(s_w[t * Q:(t + 1) * Q])
    pieces.append(s_w[(n_wt - 1) * Q:n_wt * Q] + bp_t)
    pieces.append(s_w[n_wt * Q:(n_wt + 1) * Q] + bd_t)
    s_w = jnp.concatenate(pieces, axis=0)
    x_w = lax.broadcasted_iota(jnp.int32, (WINDOW + Q, 1), 0)
    s_w = jnp.where(x_w + ci * Q >= WINDOW, s_w, NEG)
    e_w, d_w = _col_softmax_stats(s_w)
    p_w = e_w * (1.0 / jnp.where(d_w > 0, d_w, 1.0))
    o_win = _dot_tn(vw, p_w.astype(BF16))

    gt = jnp.transpose(jax.nn.sigmoid(gl_ref[...]))
    outs = []
    for h in range(H4):
        base = (g * H4 + h) * 3
        sl = slice(h * Q, (h + 1) * Q)
        gc = gt_row(gt, base)
        gs = gt_row(gt, base + 1)
        gw = gt_row(gt, base + 2)
        o_h = gc * o_cmp[:, sl] + gs * o_slc[:, sl] + gw * o_win[:, sl]
        outs.append(jnp.transpose(o_h))
    o_ref[...] = jnp.concatenate(outs, axis=1).astype(o_ref.dtype)


def gt_row(gt, idx):
    rows = lax.broadcasted_iota(jnp.int32, gt.shape, 0)
    return jnp.sum(jnp.where(rows == idx, gt, 0.0), axis=0, keepdims=True)


def _nsa_attention(rel_bias, proj, proj_gate, kcmp, vcmp, ks, vs, kw, vw):
    T = proj.shape[0]
    G = NSA_KV_HEADS
    Q = Q_BLOCK
    ncp = T // CMP_STRIDE
    nch = T // Q
    assert nch % SLC_GROUP == 0 and T // SLC_LEN <= Q
    kernel = functools.partial(_nsa_kernel, ncp=ncp)
    full = lambda rows, w=HEAD_DIM: pl.BlockSpec((None, rows, w), lambda g, c: (g, 0, 0))
    return pl.pallas_call(
        kernel,
        out_shape=jax.ShapeDtypeStruct((T, NSA_HEADS * HEAD_DIM), BF16),
        grid=(G, nch),
        in_specs=[pl.BlockSpec(memory_space=pltpu.SMEM),
                  pl.BlockSpec((Q, NSA_HPG * HEAD_DIM), lambda g, c: (c, g)),
                  pl.BlockSpec((Q, LANES), lambda g, c: (c, 0)),
                  full(ncp + 2 * Q), full(ncp + 2 * Q),
                  full(T + Q + SLC_TAIL, HEAD_DIM + Q), full(T + Q + SLC_TAIL), full(T + WINDOW), full(T + WINDOW)],
        out_specs=pl.BlockSpec((Q, NSA_HPG * HEAD_DIM), lambda g, c: (c, g)),
        scratch_shapes=[pltpu.VMEM((NSA_HEADS, Q, Q), F32), pltpu.VMEM((NSA_HEADS, Q, Q), F32),
                        pltpu.VMEM((NSA_HEADS, Q, Q), F32), pltpu.VMEM((Q, Q), F32), pltpu.VMEM((Q, Q), F32)],
        compiler_params=_cparams(("arbitrary", "arbitrary")),
        name="nsa_attention",
    )(rel_bias, proj, proj_gate, kcmp, vcmp, ks, vs, kw, vw)


def _softplus(z):
    return jnp.maximum(z, 0.0) + jnp.log1p(jnp.exp(-jnp.abs(z)))


def _lru_kernel(x_ref, y_ref, cw_ref, cb_ref, wa_ref, ba_ref, wi_ref, bi_ref, lam_ref, o_ref,
                xbuf, hc, a_s, b_s, h_s, *, tb):
    i = pl.program_id(0)

    @pl.when(i == 0)
    def _():
        xbuf[0:8, :] = jnp.zeros((8, LRU_WIDTH), F32)
        hc[...] = jnp.zeros((1, LRU_WIDTH), F32)

    xbuf[8:8 + tb, :] = x_ref[...]
    xc = cb_ref[...]
    for w in range(CONV_W):
        xc = xc + cw_ref[w:w + 1, :] * xbuf[8 - (CONV_W - 1) + w:8 - (CONV_W - 1) + w + tb, :]
    xbuf[0:8, :] = xbuf[tb:tb + 8, :]

    xcb = xc.astype(BF16)
    ra, ia = [], []
    for n in range(LRU_BLOCKS):
        xg = xcb[:, n * LRU_BW:(n + 1) * LRU_BW]
        ra.append(_dot(xg, wa_ref[n].astype(BF16)))
        ia.append(_dot(xg, wi_ref[n].astype(BF16)))
    r_gate = jax.nn.sigmoid(jnp.concatenate(ra, axis=1) + ba_ref[...])
    i_gate = jax.nn.sigmoid(jnp.concatenate(ia, axis=1) + bi_ref[...])
    log_a = (-LRU_C * r_gate) * _softplus(-lam_ref[...])
    a = jnp.exp(log_a)
    a_s[...] = a
    b_s[...] = jnp.sqrt(-jnp.tanh(log_a) * (a * a + 1.0)) * (i_gate * xc)

    rows = lax.broadcasted_iota(jnp.int32, (8, LRU_WIDTH), 0)

    def tile(k, h):
        r0 = pl.multiple_of(k * 8, 8)
        A = a_s[pl.ds(r0, 8), :]
        B = b_s[pl.ds(r0, 8), :]
        for sh in (1, 2, 4):
            ok = rows >= sh
            A_p = pltpu.roll(A, sh, axis=0)
            B_p = pltpu.roll(B, sh, axis=0)
            B = jnp.where(ok, A * B_p + B, B)
            A = jnp.where(ok, A * A_p, A)
        H = A * h + B
        h_s[pl.ds(r0, 8), :] = H
        return H[7:8, :]

    hc[...] = lax.fori_loop(0, tb // 8, tile, hc[...])
    o_ref[...] = (jax.nn.gelu(y_ref[...]) * h_s[...]).astype(o_ref.dtype)


def _rglru(proj, y_col, x_col, conv_w, conv_b, wa, ba, wi, bi, lam, tb=256):
    T = proj.shape[0]
    W = LRU_WIDTH
    vec = pl.BlockSpec((1, W), lambda i: (0, 0))
    blk = pl.BlockSpec((LRU_BLOCKS, LRU_BW, LRU_BW), lambda i: (0, 0, 0))
    return pl.pallas_call(
        functools.partial(_lru_kernel, tb=tb),
        out_shape=jax.ShapeDtypeStruct((T, W), BF16),
        grid=(T // tb,),
        in_specs=[pl.BlockSpec((tb, W), lambda i: (i, x_col)),
                  pl.BlockSpec((tb, W), lambda i: (i, y_col)),
                  pl.BlockSpec((CONV_W, W), lambda i: (0, 0)), vec, blk, vec, blk, vec, vec],
        out_specs=pl.BlockSpec((tb, W), lambda i: (i, 0)),
        scratch_shapes=[pltpu.VMEM((tb + 8, W), F32), pltpu.VMEM((1, W), F32),
                        pltpu.VMEM((tb, W), F32), pltpu.VMEM((tb, W), F32), pltpu.VMEM((tb, W), F32)],
        compiler_params=_cparams(("arbitrary",)),
        name="rglru",
    )(proj, proj, conv_w, conv_b.reshape(1, W), wa, ba.reshape(1, W), wi, bi.reshape(1, W), lam.reshape(1, W))


def _split3(x):
    hi = x.astype(BF16)
    r = x - hi.astype(F32)
    mid = r.astype(BF16)
    lo = (r - mid.astype(F32)).astype(BF16)
    return hi, mid, lo


def _hgrn_kernel(q_ref, f_ref, v_ref, g_ref, lbl_ref, ng_ref, o_ref, st_ref, oacc, *, tbh, layer):
    t = pl.program_id(1)
    C = HG_CHUNK
    SB = HG_SUB

    NH = HG_HEADS_PER_STEP
    DK, DV = HG_DK, HG_DV

    @pl.when(t == 0)
    def _():
        st_ref[...] = jnp.zeros((NH, DV, DK), F32)

    lg = lbl_ref[...]
    e = jnp.exp(lg - jnp.max(lg, axis=0, keepdims=True))
    sm = e / jnp.sum(e, axis=0, keepdims=True)
    cum = sm[0:1, :]
    for l in range(1, layer + 1):
        cum = cum + sm[l:l + 1, :]
    lb_all = cum - sm[0:1, :]

    tri = (lax.broadcasted_iota(jnp.int32, (C, C), 0) >= lax.broadcasted_iota(jnp.int32, (C, C), 1)).astype(BF16)
    rows8 = lax.broadcasted_iota(jnp.int32, (8, DK), 0)

    def head_chunk(r0, hh):
        cs = slice(hh * DK, (hh + 1) * DK)
        lb = lb_all[:, cs]
        qz = q_ref[pl.ds(r0, C), cs]
        qq = qz * jax.nn.sigmoid(qz)
        f = lb + (1.0 - lb) * jax.nn.sigmoid(f_ref[pl.ds(r0, C), cs])
        logf = jnp.log2(jnp.maximum(f, 1e-30))
        kk = 1.0 - f
        vv = v_ref[pl.ds(r0, C), cs]
        l_hi, l_mid, l_lo = _split3(logf)
        bcum = _dot(tri, l_hi) + _dot(tri, l_mid) + _dot(tri, l_lo)
        st = st_ref[hh]
        o_parts = []
        vb = vv.astype(BF16)
        for I in range(C // SB):
            lo_r = I * SB
            for a in range(SB // 8):
                t0 = lo_r + 8 * a
                q8 = qq[t0:t0 + 8]
                b8 = bcum[t0:t0 + 8]
                o8 = jnp.zeros((8, DV), F32)
                for s in range(lo_r, t0 + 8):
                    d = b8 - bcum[s:s + 1]
                    if s > t0:
                        d = jnp.where(rows8 >= s - t0, d, NEG)
                    w = q8 * (kk[s:s + 1] * jnp.exp2(d))
                    o8 = o8 + jnp.sum(w, axis=1, keepdims=True) * vv[s:s + 1]
                o_parts.append(o8)
        for I in range(1, C // SB):
            lo_r = I * SB
            ref = bcum[lo_r - 1:lo_r]
            qe = (qq[lo_r:lo_r + SB] * jnp.exp2(bcum[lo_r:lo_r + SB] - ref)).astype(BF16)
            ke = (kk[0:lo_r] * jnp.exp2(ref - bcum[0:lo_r])).astype(BF16)
            att = _dot_nt(qe, ke)
            o_off = _dot(att.astype(BF16), vb[0:lo_r])
            for a in range(SB // 8):
                o_parts[lo_r // 8 + a] = o_parts[lo_r // 8 + a] + o_off[8 * a:8 * a + 8]
        o_intra = jnp.concatenate(o_parts, axis=0)
        o_inter = _dot_nt((qq * jnp.exp2(bcum)).astype(BF16), st.astype(BF16))
        oacc[pl.ds(r0, C), cs] = o_inter + o_intra
        b_last = bcum[C - 1:C]
        kd = (kk * jnp.exp2(b_last - bcum)).astype(BF16)
        st_ref[hh] = st * jnp.exp2(b_last) + _dot_tn(vb, kd)

    def chunk(c, carry):
        r0 = pl.multiple_of(c * C, C)
        for hh in range(NH):
            head_chunk(r0, hh)
        return carry

    lax.fori_loop(0, tbh // C, chunk, 0)
    gz = g_ref[...]
    gate = (gz * jax.nn.sigmoid(gz)) * ng_ref[...]
    for hh in range(NH):
        cs = slice(hh * DV, (hh + 1) * DV)
        o = oacc[:, cs]
        o = o * lax.rsqrt(jnp.mean(o * o, axis=-1, keepdims=True) + EPS)
        o_ref[:, cs] = (o * gate[:, cs]).astype(o_ref.dtype)


def _hgrn2(proj, lb_logits, norm_g, layer, tbh=512):
    T = proj.shape[0]
    NH = HG_HEADS_PER_STEP
    HB = HG_HEADS // NH
    wk = NH * HG_DK
    col = lambda off: pl.BlockSpec((tbh, wk), lambda h, t: (t, off + h))
    return pl.pallas_call(
        functools.partial(_hgrn_kernel, tbh=tbh, layer=layer),
        out_shape=jax.ShapeDtypeStruct((T, HG_VW), BF16),
        grid=(HB, T // tbh),
        in_specs=[col(0), col(HB), col(2 * HB), col(3 * HB),
                  pl.BlockSpec((DEPTH, wk), lambda h, t: (0, h)),
                  pl.BlockSpec((1, wk), lambda h, t: (0, h))],
        out_specs=pl.BlockSpec((tbh, wk), lambda h, t: (t, h)),
        scratch_shapes=[pltpu.VMEM((NH, HG_DV, HG_DK), F32), pltpu.VMEM((tbh, wk), F32)],
        compiler_params=_cparams(("arbitrary", "arbitrary")),
        name="hgrn2",
    )(proj, proj, proj, proj, lb_logits, norm_g.reshape(1, HG_VW))


def _router_kernel(x_ref, g_ref, sc_ref, sh_ref, w_ref, b_ref, h_ref, eid_ref, gate_ref, cnt_ref):
    h = _norm_mod(x_ref[...], g_ref[...], sc_ref[...], sh_ref[...])
    tm, d = h.shape
    hb = h.astype(BF16)
    lo = lax.bitcast_convert_type(hb[:, :d // 2].astype(F32), jnp.uint32) >> 16
    hi = lax.bitcast_convert_type(hb[:, d // 2:].astype(F32), jnp.uint32)
    packed = hi | lo
    pt = d // 2 // LANES
    for c in range(pt):
        h_ref[pl.ds(c, tm, stride=pt), :] = packed[:, c * LANES:(c + 1) * LANES]
    h1, h2, h3 = _split3(h)
    w1, w2, w3 = _split3(w_ref[...])
    a = _dot(h1, jnp.concatenate([w1, w2, w3], axis=1))
    b = _dot(h2, jnp.concatenate([w1, w2], axis=1))
    c = _dot(h3, w1)
    L = LANES
    logits = (a[:, 0:L] + (a[:, L:2 * L] + b[:, 0:L])
              + (b[:, L:2 * L] + a[:, 2 * L:3 * L] + c)) + b_ref[...]
    lane = lax.broadcasted_iota(jnp.int32, logits.shape, 1)
    is_g = lane < N_GROUPS
    glog = jnp.where(is_g, logits, -jnp.inf)
    gmax = jnp.max(glog, axis=-1, keepdims=True)
    grp = jnp.min(jnp.where(glog == gmax, lane, LANES), axis=-1, keepdims=True)
    gsum = jnp.sum(jnp.where(is_g, jnp.exp(glog - gmax), 0.0), axis=-1, keepdims=True)
    p_grp = 1.0 / gsum
    lo = N_GROUPS + EXP_PER_GROUP * grp
    el = jnp.where((lane >= lo) & (lane < lo + EXP_PER_GROUP), logits, -jnp.inf)
    v1 = jnp.max(el, axis=-1, keepdims=True)
    i1 = jnp.min(jnp.where(el == v1, lane, LANES), axis=-1, keepdims=True)
    el2 = jnp.where(lane == i1, -jnp.inf, el)
    v2 = jnp.max(el2, axis=-1, keepdims=True)
    i2 = jnp.min(jnp.where(el2 == v2, lane, LANES), axis=-1, keepdims=True)
    e2 = jnp.exp(v2 - v1)
    den = 1.0 + e2
    g1 = p_grp * (1.0 / den)
    g2 = p_grp * (e2 / den)
    eid_ref[...] = jnp.where(lane == 0, i1 - N_GROUPS, jnp.where(lane == 1, i2 - N_GROUPS, 0))
    gate_ref[...] = jnp.where(lane == 0, g1, jnp.where(lane == 1, g2, 0.0))

    @pl.when(pl.program_id(0) == 0)
    def _():
        cnt_ref[...] = jnp.zeros(cnt_ref.shape, F32)

    picked = ((lane == i1 - N_GROUPS) | (lane == i2 - N_GROUPS)).astype(F32)
    cnt_ref[...] += jnp.sum(picked, axis=0, keepdims=True)


def _dispatch_kernel(eid_ref, base_ref, dest_ref, tok_ref, carry, dstage, dsm, sem, *, tb):
    i = pl.program_id(0)

    @pl.when(i == 0)
    def _():
        carry[...] = jnp.zeros(carry.shape, F32)

        def zero(j, c):
            tok_ref[j] = 0
            return c
        lax.fori_loop(0, tok_ref.shape[0], zero, 0, unroll=8)

    e = eid_ref[...]
    lane = lax.broadcasted_iota(jnp.int32, e.shape, 1)
    oh0 = (lane == e[:, 0:1]).astype(F32)
    oh1 = (lane == e[:, 1:2]).astype(F32)
    both = oh0 + oh1
    before = (lax.broadcasted_iota(jnp.int32, (tb, tb), 1) < lax.broadcasted_iota(jnp.int32, (tb, tb), 0)).astype(BF16)
    prior = _dot(before, both.astype(BF16)) + (carry[...] + base_ref[...])
    d0 = jnp.sum(oh0 * prior, axis=1, keepdims=True)
    d1 = jnp.sum(oh1 * (prior + oh0), axis=1, keepdims=True)
    carry[...] += jnp.sum(both, axis=0, keepdims=True)
    dest = jnp.where(lane == 0, d0, jnp.where(lane == 1, d1, 0.0)).astype(jnp.int32)
    dest_ref[...] = dest

    dstage[...] = jnp.transpose(dest)[0:8, :]
    cp = pltpu.make_async_copy(dstage, dsm, sem)
    cp.start()
    cp.wait()

    def body(n, c):
        for k in range(TOPK_IN_GROUP):
            tok_ref[dsm[k, n]] = i * tb + n
        return c

    lax.fori_loop(0, tb, body, 0, unroll=8)


def _dispatch(eid_l, base, n_blk, tb=512):
    N = eid_l.shape[0]
    return pl.pallas_call(
        functools.partial(_dispatch_kernel, tb=tb),
        out_shape=[jax.ShapeDtypeStruct((N, LANES), jnp.int32), jax.ShapeDtypeStruct((n_blk * MOE_BLOCK,), jnp.int32)],
        grid=(N // tb,),
        in_specs=[pl.BlockSpec((tb, LANES), lambda i: (i, 0)), pl.BlockSpec((1, LANES), lambda i: (0, 0))],
        out_specs=[pl.BlockSpec((tb, LANES), lambda i: (i, 0)), pl.BlockSpec(memory_space=pltpu.SMEM)],
        scratch_shapes=[pltpu.VMEM((1, LANES), F32), pltpu.VMEM((8, tb), jnp.int32), pltpu.SMEM((8, tb), jnp.int32),
                        pltpu.SemaphoreType.DMA(())],
        compiler_params=_cparams(("arbitrary",)),
        name="moe_dispatch",
    )(eid_l, base)


def _router(x, g, sc, sh, w_r, b_r, tm=512):
    T, D = x.shape
    vec = pl.BlockSpec((1, D), lambda i: (0, 0))
    return pl.pallas_call(
        _router_kernel,
        out_shape=[jax.ShapeDtypeStruct((T * PACKED_ROWS, LANES), jnp.uint32), jax.ShapeDtypeStruct((T, LANES), jnp.int32),
                   jax.ShapeDtypeStruct((T, LANES), F32), jax.ShapeDtypeStruct((1, LANES), F32)],
        grid=(T // tm,),
        in_specs=[pl.BlockSpec((tm, D), lambda i: (i, 0)), vec, vec, vec,
                  pl.BlockSpec((D, LANES), lambda i: (0, 0)), pl.BlockSpec((1, LANES), lambda i: (0, 0))],
        out_specs=[pl.BlockSpec((tm * PACKED_ROWS, LANES), lambda i: (i, 0)),
                   pl.BlockSpec((tm, LANES), lambda i: (i, 0)),
                   pl.BlockSpec((tm, LANES), lambda i: (i, 0)),
                   pl.BlockSpec((1, LANES), lambda i: (0, 0))],
        compiler_params=_cparams(("arbitrary",)),
        name="moe_router",
    )(x, g, sc, sh, w_r, b_r)


def _cast_rows(src_ref, slot, dst_ref, rows):
    def body(c, carry):
        r = pl.multiple_of(c * rows, rows)
        dst_ref[pl.ds(r, rows), :] = src_ref[slot, pl.ds(r, rows), :].astype(BF16)
        return carry
    lax.fori_loop(0, dst_ref.shape[0] // rows, body, 0, unroll=4)


def _expert_kernel(rid_ref, re_ref, nu_ref, nv_ref, tok_ref, h_ref, wg_hbm, wu_hbm, wd_hbm, y_ref,
                   wgf, wuf, wdf, wgb, wub, wdb, xbuf, wsem, xsem, *, layer):
    i = pl.program_id(0)
    n_used = nu_ref[0]
    R = MOE_BLOCK

    def w_copies(e, slot):
        return ((pltpu.make_async_copy(wg_hbm.at[layer, e], wgf.at[slot], wsem.at[slot, 0]), 1),
                (pltpu.make_async_copy(wu_hbm.at[layer, e], wuf.at[slot], wsem.at[slot, 1]), 1),
                (pltpu.make_async_copy(wd_hbm.at[layer, e], wdf.at[slot], wsem.at[slot, 2]), 1))

    HT = PACKED_ROWS

    GR = 8

    def groups(blk):
        return (nv_ref[blk] + GR - 1) // GR

    def x_start(blk, slot):
        def body(gi, c):
            for u in range(GR):
                r = gi * GR + u
                tok = tok_ref[blk * R + r]
                pltpu.make_async_copy(h_ref.at[pl.ds(pl.multiple_of(tok * HT, HT), HT)],
                                      xbuf.at[slot, pl.ds(pl.multiple_of(r * HT, HT), HT)], xsem.at[slot]).start()
            return c
        lax.fori_loop(0, groups(blk), body, 0)

    def x_wait(blk, slot):
        def body(gi, c):
            pltpu.make_async_copy(h_ref.at[pl.ds(0, GR * HT)], xbuf.at[slot, pl.ds(0, GR * HT)], xsem.at[slot]).wait()
            return c
        lax.fori_loop(0, groups(blk), body, 0)

    def w_request(run):
        @pl.when(re_ref[run] >= 0)
        def _():
            for c, pri in w_copies(re_ref[run], run % 2):
                c.start(priority=pri)

    @pl.when(i == 0)
    def _():
        w_request(0)
        w_request(1)
        xbuf[...] = jnp.zeros(xbuf.shape, xbuf.dtype)
        x_start(0, 0)

    @pl.when(i < n_used)
    def _():
        slot = i % 2

        @pl.when(i + 1 < n_used)
        def _():
            x_start(i + 1, 1 - slot)

        run = rid_ref[i]

        @pl.when((i == 0) | (run != rid_ref[jnp.maximum(i - 1, 0)]))
        def _():
            ws = run % 2
            for c, _ in w_copies(re_ref[run], ws):
                c.wait()
            _cast_rows(wgf, ws, wgb, 64)
            _cast_rows(wuf, ws, wub, 64)
            _cast_rows(wdf, ws, wdb, 16)
            w_request(run + 2)

        x_wait(i, slot)
        words = [xbuf[slot, pl.ds(c, R, stride=HT), :] for c in range(HT)]
        lo = [lax.bitcast_convert_type(w << 16, F32) for w in words]
        hi = [lax.bitcast_convert_type(w & jnp.uint32(0xFFFF0000), F32) for w in words]
        x = jnp.concatenate(lo + hi, axis=1).astype(BF16)
        hg = _dot(x, wgb[...])
        hu = _dot(x, wub[...])
        hid = (hg * jax.nn.sigmoid(hg)) * hu
        y_ref[...] = _dot(hid.astype(BF16), wdb[...])

    @pl.when(i >= n_used)
    def _():
        y_ref[...] = jnp.zeros(y_ref.shape, y_ref.dtype)


def _experts(h, tok, run_id, run_e, n_used, n_valid, w_gate, w_up, w_down, layer):
    D = D_MODEL
    n_pad = tok.shape[0]
    n_blk = n_pad // MOE_BLOCK
    any_spec = pl.BlockSpec(memory_space=pl.ANY)
    return pl.pallas_call(
        functools.partial(_expert_kernel, layer=layer),
        out_shape=jax.ShapeDtypeStruct((n_pad, D), F32),
        grid_spec=pltpu.PrefetchScalarGridSpec(
            num_scalar_prefetch=5,
            grid=(n_blk,),
            in_specs=[any_spec, any_spec, any_spec, any_spec],
            out_specs=pl.BlockSpec((MOE_BLOCK, D), lambda i, *_: (i, 0)),
            scratch_shapes=[pltpu.VMEM((2, D, D_EXPERT), F32), pltpu.VMEM((2, D, D_EXPERT), F32),
                            pltpu.VMEM((2, D_EXPERT, D), F32),
                            pltpu.VMEM((D, D_EXPERT), BF16), pltpu.VMEM((D, D_EXPERT), BF16),
                            pltpu.VMEM((D_EXPERT, D), BF16),
                            pltpu.VMEM((2, MOE_BLOCK * PACKED_ROWS, LANES), jnp.uint32),
                            pltpu.SemaphoreType.DMA((2, 4)), pltpu.SemaphoreType.DMA((2,))]),
        compiler_params=_cparams(("arbitrary",)),
        name="moe_experts",
    )(run_id, run_e, n_used, n_valid, tok, h, w_gate, w_up, w_down)


def _combine_kernel(slot_ref, y_ref, x_ref, w_ref, g_ref, fg_ref, o_ref, buf, sem, *, rows, final_norm):
    i = pl.program_id(0)

    def copy(r, k):
        return pltpu.make_async_copy(y_ref.at[pl.ds(slot_ref[(i * rows + r) * 2 + k], 1)],
                                     buf.at[k, pl.ds(r, 1)], sem)

    def start(r, c):
        copy(r, 0).start()
        copy(r, 1).start()
        return c

    lax.fori_loop(0, rows, start, 0, unroll=8)
    for k in range(TOPK_IN_GROUP):
        pltpu.make_async_copy(y_ref.at[pl.ds(0, rows)], buf.at[k], sem).wait()
    w = w_ref[...]
    moe = buf[0] * w[:, 0:1] + buf[1] * w[:, 1:2]
    y = x_ref[...] + g_ref[...] * moe
    if final_norm:
        y = (y * lax.rsqrt(jnp.mean(y * y, axis=-1, keepdims=True) + EPS)) * fg_ref[...]
    o_ref[...] = y


def _combine(yb, slots, x, gate_w, g2, final_g, final_norm, rows=256):
    T, D = x.shape
    return pl.pallas_call(
        functools.partial(_combine_kernel, rows=rows, final_norm=final_norm),
        out_shape=jax.ShapeDtypeStruct((T, D), F32),
        grid_spec=pltpu.PrefetchScalarGridSpec(
            num_scalar_prefetch=1,
            grid=(T // rows,),
            in_specs=[pl.BlockSpec(memory_space=pl.ANY),
                      pl.BlockSpec((rows, D), lambda i, s: (i, 0)),
                      pl.BlockSpec((rows, LANES), lambda i, s: (i, 0)),
                      pl.BlockSpec((1, D), lambda i, s: (0, 0)),
                      pl.BlockSpec((1, D), lambda i, s: (0, 0))],
            out_specs=pl.BlockSpec((rows, D), lambda i, s: (i, 0)),
            scratch_shapes=[pltpu.VMEM((2, rows, D), F32), pltpu.SemaphoreType.DMA(())]),
        compiler_params=_cparams(("arbitrary",)),
        name="moe_combine",
    )(slots.reshape(-1), yb, x, gate_w, g2, final_g)


def _hier_moe_residual(x, g, sc, sh, g2, w_grp, b_grp, w_exp, b_exp, w_gate, w_up, w_down, layer, final_g,
                       final_norm):
    N, D = x.shape
    n_route = N_GROUPS + N_EXPERTS
    w_r = jnp.zeros((D, LANES), F32).at[:, :N_GROUPS].set(w_grp).at[:, N_GROUPS:n_route].set(w_exp)
    b_r = jnp.zeros((1, LANES), F32).at[0, :N_GROUPS].set(b_grp).at[0, N_GROUPS:n_route].set(b_exp)
    h, eid_l, gate_l, cnt = _router(x, g, sc, sh, w_r, b_r)

    A = N * TOPK_IN_GROUP
    counts = cnt[0, :N_EXPERTS].astype(jnp.int32)
    padded = (counts + MOE_BLOCK - 1) // MOE_BLOCK * MOE_BLOCK
    pad_end = jnp.cumsum(padded)
    pad_start = pad_end - padded
    n_blk = -(-(A + N_EXPERTS * MOE_BLOCK) // MOE_BLOCK)
    base = jnp.zeros((1, LANES), F32).at[0, :N_EXPERTS].set(pad_start.astype(F32))
    dest_l, tok_tbl = _dispatch(eid_l, base, n_blk)
    buf_tok = tok_tbl
    blk_idx = jnp.arange(n_blk, dtype=jnp.int32)
    n_used = pad_end[-1] // MOE_BLOCK
    blk_e = jnp.minimum(jnp.sum((pad_end[None, :] <= (blk_idx * MOE_BLOCK)[:, None]).astype(jnp.int32), axis=1),
                        N_EXPERTS - 1)
    blk_e = jnp.where(blk_idx < n_used, blk_e, blk_e[n_used - 1])
    change = jnp.concatenate([jnp.ones((1,), bool), blk_e[1:] != blk_e[:-1]])
    run_id = (jnp.cumsum(change.astype(jnp.int32)) - 1).astype(jnp.int32)
    runs = jnp.arange(n_blk + 2, dtype=jnp.int32)
    run_e = jnp.max(jnp.where(run_id[None, :] == runs[:, None], blk_e[None, :], -1), axis=1).astype(jnp.int32)
    slots = dest_l[:, :TOPK_IN_GROUP]

    n_valid = jnp.clip(counts[blk_e] - (blk_idx * MOE_BLOCK - pad_start[blk_e]), 0, MOE_BLOCK).astype(jnp.int32)
    yb = _experts(h, buf_tok, run_id, run_e, n_used.reshape(1).astype(jnp.int32), n_valid, w_gate, w_up, w_down, layer)
    return _combine(yb, slots, x, gate_l, g2, final_g, final_norm)


def _even_mixer_residual(x, g, sc, sh, g1, rel_bias, w_in, w_out, cmp_pe, cmp_w1, cmp_b1, cmp_w2, cmp_b2,
                         conv_w, conv_b, lru_wa, lru_ba, lru_wi, lru_bi, lru_lambda):
    T, D = x.shape
    G, HD = NSA_KV_HEADS, HEAD_DIM
    nq = NSA_HEADS * HD
    n_kv = 6 * NSA_KV_W
    n_gate = 3 * NSA_HEADS
    w_main = jnp.concatenate([w_in[:, :nq], w_in[:, nq + n_kv + n_gate:], w_in[:, nq:nq + n_kv]],
                             axis=1).astype(BF16)
    w_gl = jnp.zeros((D, LANES), F32).at[:, :n_gate].set(w_in[:, nq + n_kv:nq + n_kv + n_gate]).astype(BF16)
    kv0 = nq + 2 * LRU_WIDTH
    tn = 768
    proj, proj_gate, kv_b = _norm_proj(x, g, sc, sh, w_main, w_gl, bf16_from=kv0 // tn, tn=tn)

    def kv_heads(j, src=kv_b, c_base=0):
        c0 = c_base + j * NSA_KV_W
        return src[:, c0:c0 + NSA_KV_W].reshape(T, G, HD).transpose(1, 0, 2)

    cmp_in = jnp.stack([kv_heads(0, proj, kv0), kv_heads(1, proj, kv0)], axis=0)
    cmp_out = _compress(cmp_in, cmp_pe, cmp_w1, cmp_b1, cmp_w2, cmp_b2)
    ncp = T // CMP_STRIDE
    valid = (jnp.arange(ncp) < ncp - 1)[None, None, :, None]
    cmp_pad = jnp.pad(jnp.where(valid, cmp_out, 0.0), ((0, 0), (0, 0), (Q_BLOCK, Q_BLOCK), (0, 0)))
    slc_pad = ((0, 0), (Q_BLOCK, SLC_TAIL), (0, 0))
    blk_of_row = jnp.pad(jnp.arange(T, dtype=jnp.int32) // SLC_LEN, (Q_BLOCK, SLC_TAIL), constant_values=Q_BLOCK - 1)
    blk_onehot = (blk_of_row[:, None] == jnp.arange(Q_BLOCK, dtype=jnp.int32)[None, :]).astype(BF16)
    ks = jnp.concatenate([jnp.pad(kv_heads(2), slc_pad), jnp.broadcast_to(blk_onehot, (G,) + blk_onehot.shape)], axis=2)
    vs = jnp.pad(kv_heads(3), slc_pad)
    kw = jnp.pad(kv_heads(4), ((0, 0), (WINDOW, 0), (0, 0)))
    vw = jnp.pad(kv_heads(5), ((0, 0), (WINDOW, 0), (0, 0)))
    nsa_out = _nsa_attention(rel_bias, proj, proj_gate, cmp_pad[0], cmp_pad[1], ks, vs, kw, vw)

    y_col = nq // LRU_WIDTH
    lru_out = _rglru(proj, y_col, y_col + 1, conv_w, conv_b, lru_wa, lru_ba, lru_wi, lru_bi, lru_lambda)
    w_o = w_out.astype(BF16)
    return _out_proj(nsa_out, lru_out, w_o, x, g1)


def _odd_mixer_residual(x, g, sc, sh, g1, lb_logits, w_in, w_out, norm_g, layer):
    proj = _norm_proj(x, g, sc, sh, w_in.astype(BF16), tn=1024)[0]
    o = _hgrn2(proj, lb_logits, norm_g, layer)
    w_o = w_out.astype(BF16)
    return _out_proj(o, o, w_o, x, g1, a2_col=1)


def kernel(x, c, rel_bias, ada_w, ada_b, norm_mix_g, norm_ffn_g, ev_w_in, ev_w_out, cmp_pe, cmp_w1, cmp_b1, cmp_w2, cmp_b2, lru_conv_w, lru_conv_b, lru_wa, lru_ba, lru_wi, lru_bi, lru_lambda, od_w_in, od_w_out, hg_lb_logits, hg_norm_g, moe_w_grp, moe_b_grp, moe_w_exp, moe_b_exp, moe_w_gate, moe_w_up, moe_w_down, final_g):
    B, T, D = x.shape
    assert B == 1 and D == D_MODEL
    xt = x.reshape(T, D)
    mod = _modulation(c, ada_w, ada_b)
    for l in range(DEPTH):
        sh1, sc1, g1, sh2, sc2, g2 = [mod[l, :, k * D:(k + 1) * D] for k in range(6)]
        gm = norm_mix_g[l].reshape(1, D)
        gf = norm_ffn_g[l].reshape(1, D)
        j = l // 2
        if l % 2 == 0:
            xt = _even_mixer_residual(xt, gm, sc1, sh1, g1, rel_bias, ev_w_in[j], ev_w_out[j], cmp_pe[j],
                                      cmp_w1[j], cmp_b1[j], cmp_w2[j], cmp_b2[j], lru_conv_w[j],
                                      lru_conv_b[j], lru_wa[j], lru_ba[j], lru_wi[j], lru_bi[j], lru_lambda[j])
        else:
            xt = _odd_mixer_residual(xt, gm, sc1, sh1, g1, hg_lb_logits, od_w_in[j], od_w_out[j],
                                     hg_norm_g[j], l)
        xt = _hier_moe_residual(xt, gf, sc2, sh2, g2, moe_w_grp[l], moe_b_grp[l], moe_w_exp[l], moe_b_exp[l],
                                moe_w_gate, moe_w_up, moe_w_down, l, final_g.reshape(1, D), l == DEPTH - 1)
    return xt.reshape(B, T, D)
```

```python
import functools
import math

import numpy as np
import jax
import jax.numpy as jnp
from jax import lax
from jax.experimental import pallas as pl
from jax.experimental.pallas import tpu as pltpu

F32 = jnp.float32
BF16 = jnp.bfloat16

D_MODEL = 2048
DEPTH = 2
NSA_HEADS = 8
NSA_KV_HEADS = 2
NSA_HPG = NSA_HEADS // NSA_KV_HEADS
HEAD_DIM = 128
NSA_KV_W = NSA_KV_HEADS * HEAD_DIM
CMP_LEN = 32
CMP_STRIDE = 16
SLC_LEN = 64
SLC_TOPN = 16
WINDOW = 512
Q_BLOCK = 128
LRU_WIDTH = 1024
LRU_BLOCKS = 8
LRU_BW = LRU_WIDTH // LRU_BLOCKS
CONV_W = 4
LRU_C = 8.0
HG_HEADS = 16
HG_DK = 128
HG_DV = 128
HG_CHUNK = 64
HG_SUB = 16
HG_HEADS_PER_STEP = 16
SLC_GROUP = 4
RANK_UNROLL = 4
RANK_BAND = 32
SLC_TAIL = 2 * SLC_GROUP * Q_BLOCK
HG_KW = HG_HEADS * HG_DK
HG_VW = HG_HEADS * HG_DV
N_BUCKETS = 32
MAX_DIST = 128
N_GROUPS = 8
EXP_PER_GROUP = 8
N_EXPERTS = N_GROUPS * EXP_PER_GROUP
TOPK_IN_GROUP = 2
D_EXPERT = 512
MOE_BLOCK = 128
EPS = 1e-6

LANES = 128
PACKED_ROWS = D_MODEL // 2 // LANES
NEG = -1e30
M_FLOOR = -1e20
LOG2E = math.log2(math.e)
VMEM_LIMIT = 56 * 1024 * 1024


def _cparams(sem):
    return pltpu.CompilerParams(dimension_semantics=sem, vmem_limit_bytes=VMEM_LIMIT)


def _dot(a, b):
    return jnp.dot(a, b, preferred_element_type=F32)


def _dot_nt(a, b):
    return lax.dot_general(a, b, (((1,), (1,)), ((), ())), preferred_element_type=F32)


def _dot_tn(a, b):
    return lax.dot_general(a, b, (((0,), (0,)), ((), ())), preferred_element_type=F32)


def _bucket_starts():
    n = np.arange(0, MAX_DIST + 1, dtype=np.int32)
    max_exact = N_BUCKETS // 2
    nf = np.maximum(n, 1).astype(np.float32)
    large = max_exact + (np.log(nf / np.float32(max_exact)) / np.float32(math.log(MAX_DIST / max_exact))
                         * np.float32(N_BUCKETS - max_exact)).astype(np.int32)
    large = np.minimum(large, N_BUCKETS - 1)
    b = np.where(n < max_exact, n, large)
    starts = [int(np.argmax(b >= k)) for k in range(N_BUCKETS)]
    assert all(b[s] == k for k, s in enumerate(starts)) and b[-1] == N_BUCKETS - 1
    return starts


BUCKET_STARTS = _bucket_starts()


def _mod_kernel(c_ref, w_ref, b_ref, o_ref):
    c = c_ref[...]
    cact = c * jax.nn.sigmoid(c)
    o_ref[...] = jnp.sum(cact * w_ref[...], axis=0, keepdims=True) + b_ref[...]


def _modulation(c, ada_w, ada_b):
    D = D_MODEL
    tn = 1024
    n_out = 6 * D
    c_col = c.reshape(D, 1)
    return pl.pallas_call(
        _mod_kernel,
        out_shape=jax.ShapeDtypeStruct((DEPTH, 1, n_out), F32),
        grid=(DEPTH, n_out // tn),
        in_specs=[pl.BlockSpec((D, 1), lambda l, j: (0, 0)),
                  pl.BlockSpec((None, D, tn), lambda l, j: (l, 0, j)),
                  pl.BlockSpec((None, 1, tn), lambda l, j: (l, 0, j))],
        out_specs=pl.BlockSpec((None, 1, tn), lambda l, j: (l, 0, j)),
        compiler_params=_cparams(("arbitrary", "arbitrary")),
        name="adaln_mod",
    )(c_col, ada_w, ada_b.reshape(DEPTH, 1, n_out))


def _norm_mod(x, g, sc, sh):
    y = x * lax.rsqrt(jnp.mean(x * x, axis=-1, keepdims=True) + EPS)
    return (y * g) * (1.0 + sc) + sh


def _norm_proj_kernel(x_ref, g_ref, sc_ref, sh_ref, w_ref, *rest, has_extra, bf16_from):
    if has_extra:
        wx_ref, o_ref, ox_ref, ob_ref, h_ref = rest
    else:
        o_ref, h_ref = rest
    j = pl.program_id(1)

    @pl.when(j == 0)
    def _():
        h = _norm_mod(x_ref[...], g_ref[...], sc_ref[...], sh_ref[...]).astype(BF16)
        h_ref[...] = h
        if has_extra:
            ox_ref[...] = _dot(h, wx_ref[...])

    y = _dot(h_ref[...], w_ref[...])
    o_ref[...] = y
    if has_extra:
        @pl.when(j >= bf16_from)
        def _():
            ob_ref[...] = y.astype(BF16)


def _norm_proj(x, g, sc, sh, w, w_extra=None, bf16_from=None, tm=1024, tn=512):
    T, D = x.shape
    N = w.shape[1]
    has_extra = w_extra is not None
    vec = pl.BlockSpec((1, D), lambda i, j: (0, 0))
    in_specs = [pl.BlockSpec((tm, D), lambda i, j: (i, 0)), vec, vec, vec,
                pl.BlockSpec((D, tn), lambda i, j: (0, j))]
    out_shape = [jax.ShapeDtypeStruct((T, N), F32)]
    out_specs = [pl.BlockSpec((tm, tn), lambda i, j: (i, j))]
    args = [x, g, sc, sh, w]
    if has_extra:
        nx = w_extra.shape[1]
        in_specs.append(pl.BlockSpec((D, nx), lambda i, j: (0, 0)))
        out_shape.append(jax.ShapeDtypeStruct((T, nx), F32))
        out_specs.append(pl.BlockSpec((tm, nx), lambda i, j: (i, 0)))
        args.append(w_extra)
        out_shape.append(jax.ShapeDtypeStruct((T, N - bf16_from * tn), BF16))
        out_specs.append(pl.BlockSpec((tm, tn), lambda i, j: (i, jnp.maximum(j - bf16_from, 0))))
    return pl.pallas_call(
        functools.partial(_norm_proj_kernel, has_extra=has_extra, bf16_from=bf16_from),
        out_shape=out_shape,
        grid=(T // tm, N // tn),
        in_specs=in_specs,
        out_specs=out_specs,
        scratch_shapes=[pltpu.VMEM((tm, D), BF16)],
        compiler_params=_cparams(("arbitrary", "arbitrary")),
        name="norm_proj",
    )(*args)


def _out_proj_kernel(a1_ref, a2_ref, w1_ref, w2_ref, x_ref, g_ref, o_ref):
    y = _dot(a1_ref[...], w1_ref[...]) + _dot(a2_ref[...], w2_ref[...])
    o_ref[...] = x_ref[...] + g_ref[...] * y


def _out_proj(a1, a2, w, x, gate, a2_col=0, tm=1024, tn=1024):
    T, D = x.shape
    K1 = K2 = w.shape[0] // 2
    return pl.pallas_call(
        _out_proj_kernel,
        out_shape=jax.ShapeDtypeStruct((T, D), F32),
        grid=(T // tm, D // tn),
        in_specs=[pl.BlockSpec((tm, K1), lambda i, j: (i, 0)),
                  pl.BlockSpec((tm, K2), lambda i, j: (i, a2_col)),
                  pl.BlockSpec((K1, tn), lambda i, j: (0, j)),
                  pl.BlockSpec((K2, tn), lambda i, j: (1, j)),
                  pl.BlockSpec((tm, tn), lambda i, j: (i, j)),
                  pl.BlockSpec((1, tn), lambda i, j: (0, j))],
        out_specs=pl.BlockSpec((tm, tn), lambda i, j: (i, j)),
        compiler_params=_cparams(("arbitrary", "arbitrary")),
        name="out_proj",
    )(a1, a2, w, w, x, gate)


def _compress_kernel(k2_ref, pe_ref, w1_ref, b1_ref, w2_ref, b2_ref, o_ref):
    half = (CMP_LEN // 2) * HEAD_DIM
    k2 = k2_ref[...]
    a = _dot((k2 + pe_ref[0:1, :]).astype(BF16), w1_ref[0:half, :].astype(BF16))
    b = _dot((k2 + pe_ref[1:2, :]).astype(BF16), w1_ref[half:2 * half, :].astype(BF16))
    nb = k2.shape[0]
    b_up = pltpu.roll(b, nb - 1, axis=0)
    hid = jax.nn.gelu(a + b_up + b1_ref[...])
    o_ref[...] = _dot(hid.astype(BF16), w2_ref[...].astype(BF16)) + b2_ref[...]


def _compress(kv, pe, w1, b1, w2, b2):
    _, G, T, HD = kv.shape
    nb = T // CMP_STRIDE
    row = CMP_STRIDE * HD
    kv2 = kv.reshape(2, G, nb, row)
    pe2 = pe.reshape(2, 2, row)
    return pl.pallas_call(
        _compress_kernel,
        out_shape=jax.ShapeDtypeStruct((2, G, nb, HD), F32),
        grid=(2, G),
        in_specs=[pl.BlockSpec((None, None, nb, row), lambda a, g: (a, g, 0, 0)),
                  pl.BlockSpec((None, 2, row), lambda a, g: (a, 0, 0)),
                  pl.BlockSpec((None, CMP_LEN * HD, HD), lambda a, g: (a, 0, 0)),
                  pl.BlockSpec((None, 1, HD), lambda a, g: (a, 0, 0)),
                  pl.BlockSpec((None, HD, HD), lambda a, g: (a, 0, 0)),
                  pl.BlockSpec((None, 1, HD), lambda a, g: (a, 0, 0))],
        out_specs=pl.BlockSpec((None, None, nb, HD), lambda a, g: (a, g, 0, 0)),
        compiler_params=_cparams(("arbitrary", "arbitrary")),
        name="nsa_compress",
    )(kv2, pe2, w1, b1.reshape(2, 1, HD), w2, b2.reshape(2, 1, HD))


def _bias_from_rel(rel, tbl_ref, head):
    far = tbl_ref[N_BUCKETS - 1, head]
    val = jnp.zeros(rel.shape, F32)
    for b in range(N_BUCKETS - 2, -1, -1):
        val = jnp.where(rel < BUCKET_STARTS[b + 1], (tbl_ref[b, head] - far) * LOG2E, val)
    return val


def _lane_tile4(x):
    return jnp.concatenate([x, x, x, x], axis=1)


def _col_softmax_stats(s):
    m = jnp.maximum(jnp.max(s, axis=0, keepdims=True), M_FLOOR)
    e = jnp.exp2(s - m)
    d = jnp.sum(e, axis=0, keepdims=True)
    return e, d


def _nsa_kernel(tbl_ref, q_ref, gl_ref, kc_ref, vc_ref, ks_ref, vs_ref, kw_ref, vw_ref, o_ref,
                bd_ref, bp_ref, bn_ref, imp_ref, cnt_ref, *, ncp):
    g = pl.program_id(0)
    ci = pl.program_id(1)
    Q = Q_BLOCK
    HD = HEAD_DIM
    H4 = NSA_HPG

    row_i = lax.broadcasted_iota(jnp.int32, (Q, Q), 0)
    lane_i = lax.broadcasted_iota(jnp.int32, (Q, Q), 1)

    @pl.when((g == 0) & (ci == 0))
    def _build_bias_tiles():
        for h in range(NSA_HEADS):
            rel_d = lane_i - row_i
            bd_ref[h] = jnp.where(rel_d >= 0, _bias_from_rel(jnp.maximum(rel_d, 0), tbl_ref, h), NEG)
            bp_ref[h] = _bias_from_rel(lane_i - row_i + Q, tbl_ref, h)
            rel_n = lane_i - CMP_STRIDE * row_i + (CMP_STRIDE * (Q - 8) - (CMP_LEN - 1))
            bn_ref[h] = jnp.where(rel_n >= 0, _bias_from_rel(jnp.maximum(rel_n, 0), tbl_ref, h), NEG)

    def head_tiles(ref):
        return jnp.concatenate([ref[g * H4 + h] for h in range(H4)], axis=1)

    qt = q_ref[...] * (HD ** -0.5 * LOG2E)
    qs = jnp.concatenate([qt[:, h * HD:(h + 1) * HD] for h in range(H4)], axis=0).astype(BF16)

    near0 = 8 * ci - (Q - 8)
    kc_far = kc_ref[pl.ds(Q, ncp), :]
    vc_far = vc_ref[pl.ds(Q, ncp), :]
    near_row = pl.multiple_of(8 * ci + 8, 8)
    kc_near = kc_ref[pl.ds(near_row, Q), :]
    vc_near = vc_ref[pl.ds(near_row, Q), :]
    n_far = lax.broadcasted_iota(jnp.int32, (ncp, 1), 0)
    s_far = jnp.where(n_far < near0, _dot_nt(kc_far.astype(BF16), qs), NEG)
    m_near = lax.broadcasted_iota(jnp.int32, (Q, 1), 0)
    s_near = _dot_nt(kc_near.astype(BF16), qs) + head_tiles(bn_ref)
    s_near = jnp.where(m_near + near0 >= 0, s_near, NEG)
    s_c = jnp.concatenate([s_far, s_near], axis=0)
    e_c, d_c = _col_softmax_stats(s_c)
    p_c = e_c * (1.0 / jnp.where(d_c > 0, d_c, 1.0))
    v_c = jnp.concatenate([vc_far, vc_near], axis=0).astype(BF16)
    o_cmp = _dot_tn(v_c, p_c.astype(BF16))

    psum = p_c[:, 0:Q] + p_c[:, Q:2 * Q] + p_c[:, 2 * Q:3 * Q] + p_c[:, 3 * Q:4 * Q]
    p_hi = psum.astype(BF16)
    p_lo = (psum - p_hi.astype(F32)).astype(BF16)
    jb = lax.broadcasted_iota(jnp.int32, (Q, ncp), 0)
    nb = lax.broadcasted_iota(jnp.int32, (Q, ncp), 1)
    ratio = SLC_LEN // CMP_STRIDE
    span = CMP_LEN // CMP_STRIDE - 1
    cover_far = ((nb >= ratio * jb - span) & (nb <= ratio * jb + ratio - 1)).astype(BF16)
    nn = lane_i + near0
    cover_near = ((nn >= ratio * row_i - span) & (nn <= ratio * row_i + ratio - 1)).astype(BF16)
    cover = jnp.concatenate([cover_far, cover_near], axis=1)
    imp = _dot(cover, p_hi) + _dot(cover, p_lo)
    cur = 2 * ci + (lane_i >= SLC_LEN).astype(jnp.int32)
    imp = jnp.where(row_i == cur, jnp.inf, jnp.where(row_i > cur, -jnp.inf, imp))
    imp_ref[...] = imp

    n_cand = (2 * ci + 2 + RANK_UNROLL - 1) // RANK_UNROLL
    cnt_ref[...] = jnp.zeros((Q, Q), F32)
    for band in range(Q // RANK_BAND):
        lo_row = band * RANK_BAND

        @pl.when(lo_row <= 2 * ci + 1)
        def _(lo_row=lo_row):
            imp_b = imp[lo_row:lo_row + RANK_BAND]
            row_b = row_i[lo_row:lo_row + RANK_BAND]

            def rank_body(it, cnt):
                for u in range(RANK_UNROLL):
                    b = it * RANK_UNROLL + u
                    r = imp_ref[pl.ds(b, 1), :]
                    ahead = (r > imp_b) | ((r == imp_b) & (b < row_b))
                    cnt = cnt + ahead.astype(F32)
                return cnt

            cnt_ref[lo_row:lo_row + RANK_BAND, :] = lax.fori_loop(0, n_cand, rank_body,
                                                                   jnp.zeros((RANK_BAND, Q), F32))

    sel = ((cnt_ref[...] < SLC_TOPN) & (row_i <= cur)).astype(F32)

    sel_q = jnp.transpose(sel)
    near_blk = lane_i >= 2 * ci - 2
    m_far = jnp.where((sel_q > 0) & jnp.logical_not(near_blk), 0.0, NEG).astype(BF16)
    m_near = jnp.where((sel_q > 0) & near_blk, 0.0, NEG).astype(BF16)
    qa_far = jnp.concatenate([qs, jnp.concatenate([m_far] * H4, axis=0)], axis=1)
    qa_near = jnp.concatenate([qs, jnp.concatenate([m_near] * H4, axis=0)], axis=1)

    def slc_rows(kt0, nt):
        r0 = pl.multiple_of((kt0 + 1) * Q, Q)
        return ks_ref[pl.ds(r0, nt * Q), :], vs_ref[pl.ds(r0, nt * Q), :]

    def online_update(streams):
        m_new = [jnp.maximum(c[0], jnp.max(s, axis=0, keepdims=True)) for s, _, c in streams]
        alpha = [jnp.exp2(c[0] - mn) for (_, _, c), mn in zip(streams, m_new)]
        p = [jnp.exp2(s - mn) for (s, _, _), mn in zip(streams, m_new)]
        l_new = [a * c[1] + jnp.sum(pp, axis=0, keepdims=True) for a, pp, (_, _, c) in zip(alpha, p, streams)]
        pv = [_dot_tn(v, pp.astype(BF16)) for pp, (_, v, _) in zip(p, streams)]
        return [(mn, ln, a * c[2] + x) for mn, ln, a, x, (_, _, c) in zip(m_new, l_new, alpha, pv, streams)]

    carry0 = (jnp.full((1, H4 * Q), M_FLOOR, F32), jnp.zeros((1, H4 * Q), F32), jnp.zeros((HD, H4 * Q), F32))
    n_far = jnp.maximum(ci - 1, 0)

    def far_body(it, carries):
        kt0 = it * (2 * SLC_GROUP)
        k_a, v_a = slc_rows(kt0, SLC_GROUP)
        k_b, v_b = slc_rows(kt0 + SLC_GROUP, SLC_GROUP)
        s_a = _dot_nt(k_a, qa_far)
        s_b = _dot_nt(k_b, qa_far)
        return tuple(online_update([(s_a, v_a, carries[0]), (s_b, v_b, carries[1])]))

    (m_a, l_a, acc_a), (m_b, l_b, acc_b) = lax.fori_loop(
        0, (n_far + 2 * SLC_GROUP - 1) // (2 * SLC_GROUP), far_body, (carry0, carry0))
    m_ab = jnp.maximum(m_a, m_b)
    w_a = jnp.exp2(m_a - m_ab)
    w_b = jnp.exp2(m_b - m_ab)
    carry = (m_ab, w_a * l_a + w_b * l_b, w_a * acc_a + w_b * acc_b)
    bp_t = head_tiles(bp_ref)
    bd_t = head_tiles(bd_ref)
    k_n, v_n = slc_rows(ci - 1, 2)
    s_n = _dot_nt(k_n, qa_near) + jnp.concatenate([bp_t, bd_t], axis=0)
    (_, l_s, acc_s), = online_update([(s_n, v_n, carry)])
    o_slc = acc_s * (1.0 / jnp.where(l_s > 0, l_s, 1.0))

    kw = kw_ref[pl.ds(pl.multiple_of(ci * Q, Q), WINDOW + Q), :]
    vw = vw_ref[pl.ds(pl.multiple_of(ci * Q, Q), WINDOW + Q), :]
    s_w = _dot_nt(kw, qs)
    n_wt = WINDOW // Q
    in_window = lane_i < row_i
    pieces = [jnp.concatenate([jnp.where(in_window, s_w[0:Q, h * Q:(h + 1) * Q], NEG) for h in range(H4)], axis=1)]
    for t in range(1, n_wt - 1):
        pieces.append(s_w[t * Q:(t + 1) * Q])
    pieces.append(s_w[(n_wt - 1) * Q:n_wt * Q] + bp_t)
    pieces.append(s_w[n_wt * Q:(n_wt + 1) * Q] + bd_t)
    s_w = jnp.concatenate(pieces, axis=0)
    x_w = lax.broadcasted_iota(jnp.int32, (WINDOW + Q, 1), 0)
    s_w = jnp.where(x_w + ci * Q >= WINDOW, s_w, NEG)
    e_w, d_w = _col_softmax_stats(s_w)
    p_w = e_w * (1.0 / jnp.where(d_w > 0, d_w, 1.0))
    o_win = _dot_tn(vw, p_w.astype(BF16))

    gt = jnp.transpose(jax.nn.sigmoid(gl_ref[...]))
    outs = []
    for h in range(H4):
        base = (g * H4 + h) * 3
        sl = slice(h * Q, (h + 1) * Q)
        gc = gt_row(gt, base)
        gs = gt_row(gt, base + 1)
        gw = gt_row(gt, base + 2)
        o_h = gc * o_cmp[:, sl] + gs * o_slc[:, sl] + gw * o_win[:, sl]
        outs.append(jnp.transpose(o_h))
    o_ref[...] = jnp.concatenate(outs, axis=1).astype(o_ref.dtype)


def gt_row(gt, idx):
    rows = lax.broadcasted_iota(jnp.int32, gt.shape, 0)
    return jnp.sum(jnp.where(rows == idx, gt, 0.0), axis=0, keepdims=True)


def _nsa_attention(rel_bias, proj, proj_gate, kcmp, vcmp, ks, vs, kw, vw):
    T = proj.shape[0]
    G = NSA_KV_HEADS
    Q = Q_BLOCK
    ncp = T // CMP_STRIDE
    nch = T // Q
    assert nch % SLC_GROUP == 0 and T // SLC_LEN <= Q
    kernel = functools.partial(_nsa_kernel, ncp=ncp)
    full = lambda rows, w=HEAD_DIM: pl.BlockSpec((None, rows, w), lambda g, c: (g, 0, 0))
    return pl.pallas_call(
        kernel,
        out_shape=jax.ShapeDtypeStruct((T, NSA_HEADS * HEAD_DIM), BF16),
        grid=(G, nch),
        in_specs=[pl.BlockSpec(memory_space=pltpu.SMEM),
                  pl.BlockSpec((Q, NSA_HPG * HEAD_DIM), lambda g, c: (c, g)),
                  pl.BlockSpec((Q, LANES), lambda g, c: (c, 0)),
                  full(ncp + 2 * Q), full(ncp + 2 * Q),
                  full(T + Q + SLC_TAIL, HEAD_DIM + Q), full(T + Q + SLC_TAIL), full(T + WINDOW), full(T + WINDOW)],
        out_specs=pl.BlockSpec((Q, NSA_HPG * HEAD_DIM), lambda g, c: (c, g)),
        scratch_shapes=[pltpu.VMEM((NSA_HEADS, Q, Q), F32), pltpu.VMEM((NSA_HEADS, Q, Q), F32),
                        pltpu.VMEM((NSA_HEADS, Q, Q), F32), pltpu.VMEM((Q, Q), F32), pltpu.VMEM((Q, Q), F32)],
        compiler_params=_cparams(("arbitrary", "arbitrary")),
        name="nsa_attention",
    )(rel_bias, proj, proj_gate, kcmp, vcmp, ks, vs, kw, vw)


def _softplus(z):
    return jnp.maximum(z, 0.0) + jnp.log1p(jnp.exp(-jnp.abs(z)))


def _lru_kernel(x_ref, y_ref, cw_ref, cb_ref, wa_ref, ba_ref, wi_ref, bi_ref, lam_ref, o_ref,
                xbuf, hc, a_s, b_s, h_s, *, tb):
    i = pl.program_id(0)

    @pl.when(i == 0)
    def _():
        xbuf[...] = jnp.zeros((8, LRU_WIDTH), F32)
        hc[...] = jnp.zeros((1, LRU_WIDTH), F32)

    x = x_ref[...]
    prev = xbuf[...]
    rows8 = lax.broadcasted_iota(jnp.int32, (8, LRU_WIDTH), 0)
    xc = cb_ref[...]
    for w in range(CONV_W):
        sh = CONV_W - 1 - w
        if sh == 0:
            xs = x
        else:
            xr = pltpu.roll(x, sh, axis=0)
            head = jnp.where(rows8 < sh, pltpu.roll(prev, sh, axis=0), xr[0:8])
            xs = jnp.concatenate([head, xr[8:]], axis=0)
        xc = xc + cw_ref[w:w + 1, :] * xs
    xbuf[...] = x[tb - 8:tb]

    xcb = xc.astype(BF16)
    ra, ia = [], []
    for n in range(LRU_BLOCKS):
        xg = xcb[:, n * LRU_BW:(n + 1) * LRU_BW]
        ra.append(_dot(xg, wa_ref[n].astype(BF16)))
        ia.append(_dot(xg, wi_ref[n].astype(BF16)))
    r_gate = jax.nn.sigmoid(jnp.concatenate(ra, axis=1) + ba_ref[...])
    i_gate = jax.nn.sigmoid(jnp.concatenate(ia, axis=1) + bi_ref[...])
    log_a = (-LRU_C * r_gate) * _softplus(-lam_ref[...])
    a = jnp.exp(log_a)
    a_s[...] = a
    b_s[...] = jnp.sqrt(-jnp.tanh(log_a) * (a * a + 1.0)) * (i_gate * xc)

    rows = lax.broadcasted_iota(jnp.int32, (8, LRU_WIDTH), 0)

    def tile(k, h):
        r0 = pl.multiple_of(k * 8, 8)
        A = a_s[pl.ds(r0, 8), :]
        B = b_s[pl.ds(r0, 8), :]
        for sh in (1, 2, 4):
            ok = rows >= sh
            A_p = pltpu.roll(A, sh, axis=0)
            B_p = pltpu.roll(B, sh, axis=0)
            B = jnp.where(ok, A * B_p + B, B)
            A = jnp.where(ok, A * A_p, A)
        H = A * h + B
        h_s[pl.ds(r0, 8), :] = H
        return H[7:8, :]

    hc[...] = lax.fori_loop(0, tb // 8, tile, hc[...])
    o_ref[...] = (jax.nn.gelu(y_ref[...]) * h_s[...]).astype(o_ref.dtype)


def _rglru(proj, y_col, x_col, conv_w, conv_b, wa, ba, wi, bi, lam, tb=256):
    T = proj.shape[0]
    W = LRU_WIDTH
    vec = pl.BlockSpec((1, W), lambda i: (0, 0))
    blk = pl.BlockSpec((LRU_BLOCKS, LRU_BW, LRU_BW), lambda i: (0, 0, 0))
    return pl.pallas_call(
        functools.partial(_lru_kernel, tb=tb),
        out_shape=jax.ShapeDtypeStruct((T, W), BF16),
        grid=(T // tb,),
        in_specs=[pl.BlockSpec((tb, W), lambda i: (i, x_col)),
                  pl.BlockSpec((tb, W), lambda i: (i, y_col)),
                  pl.BlockSpec((CONV_W, W), lambda i: (0, 0)), vec, blk, vec, blk, vec, vec],
        out_specs=pl.BlockSpec((tb, W), lambda i: (i, 0)),
        scratch_shapes=[pltpu.VMEM((8, W), F32), pltpu.VMEM((1, W), F32),
                        pltpu.VMEM((tb, W), F32), pltpu.VMEM((tb, W), F32), pltpu.VMEM((tb, W), F32)],
        compiler_params=_cparams(("arbitrary",)),
        name="rglru",
    )(proj, proj, conv_w, conv_b.reshape(1, W), wa, ba.reshape(1, W), wi, bi.reshape(1, W), lam.reshape(1, W))


def _split3(x):
    hi = x.astype(BF16)
    r = x - hi.astype(F32)
    mid = r.astype(BF16)
    lo = (r - mid.astype(F32)).astype(BF16)
    return hi, mid, lo


def _hgrn_kernel(q_ref, f_ref, v_ref, g_ref, lbl_ref, ng_ref, o_ref, st_ref, oacc, *, tbh, layer):
    t = pl.program_id(1)
    C = HG_CHUNK
    SB = HG_SUB

    NH = HG_HEADS_PER_STEP
    DK, DV = HG_DK, HG_DV

    @pl.when(t == 0)
    def _():
        st_ref[...] = jnp.zeros((NH, DV, DK), F32)

    lg = lbl_ref[...]
    e = jnp.exp(lg - jnp.max(lg, axis=0, keepdims=True))
    sm = e / jnp.sum(e, axis=0, keepdims=True)
    cum = sm[0:1, :]
    for l in range(1, layer + 1):
        cum = cum + sm[l:l + 1, :]
    lb_all = cum - sm[0:1, :]

    tri = (lax.broadcasted_iota(jnp.int32, (C, C), 0) >= lax.broadcasted_iota(jnp.int32, (C, C), 1)).astype(BF16)
    rows8 = lax.broadcasted_iota(jnp.int32, (8, DK), 0)

    def head_chunk(r0, hh):
        cs = slice(hh * DK, (hh + 1) * DK)
        lb = lb_all[:, cs]
        qz = q_ref[pl.ds(r0, C), cs]
        qq = qz * jax.nn.sigmoid(qz)
        f = lb + (1.0 - lb) * jax.nn.sigmoid(f_ref[pl.ds(r0, C), cs])
        logf = jnp.log2(jnp.maximum(f, 1e-30))
        kk = 1.0 - f
        vv = v_ref[pl.ds(r0, C), cs]
        l_hi, l_mid, l_lo = _split3(logf)
        bcum = _dot(tri, l_hi) + _dot(tri, l_mid) + _dot(tri, l_lo)
        st = st_ref[hh]
        o_parts = []
        vb = vv.astype(BF16)
        for I in range(C // SB):
            lo_r = I * SB
            for a in range(SB // 8):
                t0 = lo_r + 8 * a
                q8 = qq[t0:t0 + 8]
                b8 = bcum[t0:t0 + 8]
                o8 = jnp.zeros((8, DV), F32)
                for s in range(lo_r, t0 + 8):
                    d = b8 - bcum[s:s + 1]
                    if s > t0:
                        d = jnp.where(rows8 >= s - t0, d, NEG)
                    w = q8 * (kk[s:s + 1] * jnp.exp2(d))
                    o8 = o8 + jnp.sum(w, axis=1, keepdims=True) * vv[s:s + 1]
                o_parts.append(o8)
        for I in range(1, C // SB):
            lo_r = I * SB
            ref = bcum[lo_r - 1:lo_r]
            qe = (qq[lo_r:lo_r + SB] * jnp.exp2(bcum[lo_r:lo_r + SB] - ref)).astype(BF16)
            ke = (kk[0:lo_r] * jnp.exp2(ref - bcum[0:lo_r])).astype(BF16)
            att = _dot_nt(qe, ke)
            o_off = _dot(att.astype(BF16), vb[0:lo_r])
            for a in range(SB // 8):
                o_parts[lo_r // 8 + a] = o_parts[lo_r // 8 + a] + o_off[8 * a:8 * a + 8]
        o_intra = jnp.concatenate(o_parts, axis=0)
        o_inter = _dot_nt((qq * jnp.exp2(bcum)).astype(BF16), st.astype(BF16))
        oacc[pl.ds(r0, C), cs] = o_inter + o_intra
        b_last = bcum[C - 1:C]
        kd = (kk * jnp.exp2(b_last - bcum)).astype(BF16)
        st_ref[hh] = st * jnp.exp2(b_last) + _dot_tn(vb, kd)

    def chunk(c, carry):
        r0 = pl.multiple_of(c * C, C)
        for hh in range(NH):
            head_chunk(r0, hh)
        return carry

    lax.fori_loop(0, tbh // C, chunk, 0)
    gz = g_ref[...]
    gate = (gz * jax.nn.sigmoid(gz)) * ng_ref[...]
    for hh in range(NH):
        cs = slice(hh * DV, (hh + 1) * DV)
        o = oacc[:, cs]
        o = o * lax.rsqrt(jnp.mean(o * o, axis=-1, keepdims=True) + EPS)
        o_ref[:, cs] = (o * gate[:, cs]).astype(o_ref.dtype)


def _hgrn2(proj, lb_logits, norm_g, layer, tbh=512):
    T = proj.shape[0]
    NH = HG_HEADS_PER_STEP
    HB = HG_HEADS // NH
    wk = NH * HG_DK
    col = lambda off: pl.BlockSpec((tbh, wk), lambda h, t: (t, off + h))
    return pl.pallas_call(
        functools.partial(_hgrn_kernel, tbh=tbh, layer=layer),
        out_shape=jax.ShapeDtypeStruct((T, HG_VW), BF16),
        grid=(HB, T // tbh),
        in_specs=[col(0), col(HB), col(2 * HB), col(3 * HB),
                  pl.BlockSpec((DEPTH, wk), lambda h, t: (0, h)),
                  pl.BlockSpec((1, wk), lambda h, t: (0, h))],
        out_specs=pl.BlockSpec((tbh, wk), lambda h, t: (t, h)),
        scratch_shapes=[pltpu.VMEM((NH, HG_DV, HG_DK), F32), pltpu.VMEM((tbh, wk), F32)],
        compiler_params=_cparams(("arbitrary", "arbitrary")),
        name="hgrn2",
    )(proj, proj, proj, proj, lb_logits, norm_g.reshape(1, HG_VW))


def _router_kernel(x_ref, g_ref, sc_ref, sh_ref, w_ref, b_ref, h_ref, eid_ref, gate_ref, cnt_ref):
    h = _norm_mod(x_ref[...], g_ref[...], sc_ref[...], sh_ref[...])
    tm, d = h.shape
    hb = h.astype(BF16)
    lo = lax.bitcast_convert_type(hb[:, :d // 2].astype(F32), jnp.uint32) >> 16
    hi = lax.bitcast_convert_type(hb[:, d // 2:].astype(F32), jnp.uint32)
    packed = hi | lo
    pt = d // 2 // LANES
    for c in range(pt):
        h_ref[pl.ds(c, tm, stride=pt), :] = packed[:, c * LANES:(c + 1) * LANES]
    h1, h2, h3 = _split3(h)
    w1, w2, w3 = _split3(w_ref[...])
    a = _dot(h1, jnp.concatenate([w1, w2, w3], axis=1))
    b = _dot(h2, jnp.concatenate([w1, w2], axis=1))
    c = _dot(h3, w1)
    L = LANES
    logits = (a[:, 0:L] + (a[:, L:2 * L] + b[:, 0:L])
              + (b[:, L:2 * L] + a[:, 2 * L:3 * L] + c)) + b_ref[...]
    lane = lax.broadcasted_iota(jnp.int32, logits.shape, 1)
    is_g = lane < N_GROUPS
    glog = jnp.where(is_g, logits, -jnp.inf)
    gmax = jnp.max(glog, axis=-1, keepdims=True)
    grp = jnp.min(jnp.where(glog == gmax, lane, LANES), axis=-1, keepdims=True)
    gsum = jnp.sum(jnp.where(is_g, jnp.exp(glog - gmax), 0.0), axis=-1, keepdims=True)
    p_grp = 1.0 / gsum
    lo = N_GROUPS + EXP_PER_GROUP * grp
    el = jnp.where((lane >= lo) & (lane < lo + EXP_PER_GROUP), logits, -jnp.inf)
    v1 = jnp.max(el, axis=-1, keepdims=True)
    i1 = jnp.min(jnp.where(el == v1, lane, LANES), axis=-1, keepdims=True)
    el2 = jnp.where(lane == i1, -jnp.inf, el)
    v2 = jnp.max(el2, axis=-1, keepdims=True)
    i2 = jnp.min(jnp.where(el2 == v2, lane, LANES), axis=-1, keepdims=True)
    e2 = jnp.exp(v2 - v1)
    den = 1.0 + e2
    g1 = p_grp * (1.0 / den)
    g2 = p_grp * (e2 / den)
    eid_ref[...] = jnp.where(lane == 0, i1 - N_GROUPS, jnp.where(lane == 1, i2 - N_GROUPS, 0))
    gate_ref[...] = jnp.where(lane == 0, g1, jnp.where(lane == 1, g2, 0.0))

    @pl.when(pl.program_id(0) == 0)
    def _():
        cnt_ref[...] = jnp.zeros(cnt_ref.shape, F32)

    picked = ((lane == i1 - N_GROUPS) | (lane == i2 - N_GROUPS)).astype(F32)
    cnt_ref[...] += jnp.sum(picked, axis=0, keepdims=True)


def _dispatch_kernel(eid_ref, base_ref, dest_ref, tok_ref, carry, dstage, dsm, sem, *, tb):
    i = pl.program_id(0)

    @pl.when(i == 0)
    def _():
        carry[...] = jnp.zeros(carry.shape, F32)

        def zero(j, c):
            tok_ref[j] = 0
            return c
        lax.fori_loop(0, tok_ref.shape[0], zero, 0, unroll=8)

    e = eid_ref[...]
    lane = lax.broadcasted_iota(jnp.int32, e.shape, 1)
    oh0 = (lane == e[:, 0:1]).astype(F32)
    oh1 = (lane == e[:, 1:2]).astype(F32)
    both = oh0 + oh1
    before = (lax.broadcasted_iota(jnp.int32, (tb, tb), 1) < lax.broadcasted_iota(jnp.int32, (tb, tb), 0)).astype(BF16)
    prior = _dot(before, both.astype(BF16)) + (carry[...] + base_ref[...])
    d0 = jnp.sum(oh0 * prior, axis=1, keepdims=True)
    d1 = jnp.sum(oh1 * (prior + oh0), axis=1, keepdims=True)
    carry[...] += jnp.sum(both, axis=0, keepdims=True)
    dest = jnp.where(lane == 0, d0, jnp.where(lane == 1, d1, 0.0)).astype(jnp.int32)
    dest_ref[...] = dest

    dstage[...] = jnp.transpose(dest)[0:8, :]
    cp = pltpu.make_async_copy(dstage, dsm, sem)
    cp.start()
    cp.wait()

    def body(n, c):
        for k in range(TOPK_IN_GROUP):
            tok_ref[dsm[k, n]] = i * tb + n
        return c

    lax.fori_loop(0, tb, body, 0, unroll=8)


def _dispatch(eid_l, base, n_blk, tb=512):
    N = eid_l.shape[0]
    return pl.pallas_call(
        functools.partial(_dispatch_kernel, tb=tb),
        out_shape=[jax.ShapeDtypeStruct((N, LANES), jnp.int32), jax.ShapeDtypeStruct((n_blk * MOE_BLOCK,), jnp.int32)],
        grid=(N // tb,),
        in_specs=[pl.BlockSpec((tb, LANES), lambda i: (i, 0)), pl.BlockSpec((1, LANES), lambda i: (0, 0))],
        out_specs=[pl.BlockSpec((tb, LANES), lambda i: (i, 0)), pl.BlockSpec(memory_space=pltpu.SMEM)],
        scratch_shapes=[pltpu.VMEM((1, LANES), F32), pltpu.VMEM((8, tb), jnp.int32), pltpu.SMEM((8, tb), jnp.int32),
                        pltpu.SemaphoreType.DMA(())],
        compiler_params=_cparams(("arbitrary",)),
        name="moe_dispatch",
    )(eid_l, base)


def _router(x, g, sc, sh, w_r, b_r, tm=512):
    T, D = x.shape
    vec = pl.BlockSpec((1, D), lambda i: (0, 0))
    return pl.pallas_call(
        _router_kernel,
        out_shape=[jax.ShapeDtypeStruct((T * PACKED_ROWS, LANES), jnp.uint32), jax.ShapeDtypeStruct((T, LANES), jnp.int32),
                   jax.ShapeDtypeStruct((T, LANES), F32), jax.ShapeDtypeStruct((1, LANES), F32)],
        grid=(T // tm,),
        in_specs=[pl.BlockSpec((tm, D), lambda i: (i, 0)), vec, vec, vec,
                  pl.BlockSpec((D, LANES), lambda i: (0, 0)), pl.BlockSpec((1, LANES), lambda i: (0, 0))],
        out_specs=[pl.BlockSpec((tm * PACKED_ROWS, LANES), lambda i: (i, 0)),
                   pl.BlockSpec((tm, LANES), lambda i: (i, 0)),
                   pl.BlockSpec((tm, LANES), lambda i: (i, 0)),
                   pl.BlockSpec((1, LANES), lambda i: (0, 0))],
        compiler_params=_cparams(("arbitrary",)),
        name="moe_router",
    )(x, g, sc, sh, w_r, b_r)


def _cast_rows(src_ref, slot, dst_ref, rows):
    def body(c, carry):
        r = pl.multiple_of(c * rows, rows)
        dst_ref[pl.ds(r, rows), :] = src_ref[slot, pl.ds(r, rows), :].astype(BF16)
        return carry
    lax.fori_loop(0, dst_ref.shape[0] // rows, body, 0, unroll=4)


def _expert_kernel(rid_ref, re_ref, nu_ref, nv_ref, tok_ref, h_ref, wg_hbm, wu_hbm, wd_hbm, y_ref,
                   wgf, wuf, wdf, wgb, wub, wdb, xbuf, wsem, xsem, *, layer):
    i = pl.program_id(0)
    n_used = nu_ref[0]
    R = MOE_BLOCK

    def w_copies(e, slot):
        return ((pltpu.make_async_copy(wg_hbm.at[layer, e], wgf.at[slot], wsem.at[slot, 0]), 1),
                (pltpu.make_async_copy(wu_hbm.at[layer, e], wuf.at[slot], wsem.at[slot, 1]), 1),
                (pltpu.make_async_copy(wd_hbm.at[layer, e], wdf.at[slot], wsem.at[slot, 2]), 1))

    HT = PACKED_ROWS

    GR = 8

    def groups(blk):
        return (nv_ref[blk] + GR - 1) // GR

    def x_start(blk, slot):
        def body(gi, c):
            for u in range(GR):
                r = gi * GR + u
                tok = tok_ref[blk * R + r]
                pltpu.make_async_copy(h_ref.at[pl.ds(pl.multiple_of(tok * HT, HT), HT)],
                                      xbuf.at[slot, pl.ds(pl.multiple_of(r * HT, HT), HT)], xsem.at[slot]).start()
            return c
        lax.fori_loop(0, groups(blk), body, 0)

    def x_wait(blk, slot):
        def body(gi, c):
            pltpu.make_async_copy(h_ref.at[pl.ds(0, GR * HT)], xbuf.at[slot, pl.ds(0, GR * HT)], xsem.at[slot]).wait()
            return c
        lax.fori_loop(0, groups(blk), body, 0)

    def w_request(run):
        @pl.when(re_ref[run] >= 0)
        def _():
            for c, pri in w_copies(re_ref[run], run % 2):
                c.start(priority=pri)

    @pl.when(i == 0)
    def _():
        w_request(0)
        w_request(1)
        xbuf[...] = jnp.zeros(xbuf.shape, xbuf.dtype)
        x_start(0, 0)

    @pl.when(i < n_used)
    def _():
        slot = i % 2

        @pl.when(i + 1 < n_used)
        def _():
            x_start(i + 1, 1 - slot)

        run = rid_ref[i]

        @pl.when((i == 0) | (run != rid_ref[jnp.maximum(i - 1, 0)]))
        def _():
            ws = run % 2
            for c, _ in w_copies(re_ref[run], ws):
                c.wait()
            _cast_rows(wgf, ws, wgb, 64)
            _cast_rows(wuf, ws, wub, 64)
            _cast_rows(wdf, ws, wdb, 16)
            w_request(run + 2)

        x_wait(i, slot)
        words = [xbuf[slot, pl.ds(c, R, stride=HT), :] for c in range(HT)]
        lo = [lax.bitcast_convert_type(w << 16, F32) for w in words]
        hi = [lax.bitcast_convert_type(w & jnp.uint32(0xFFFF0000), F32) for w in words]
        x = jnp.concatenate(lo + hi, axis=1).astype(BF16)
        hg = _dot(x, wgb[...])
        hu = _dot(x, wub[...])
        hid = (hg * jax.nn.sigmoid(hg)) * hu
        y_ref[...] = _dot(hid.astype(BF16), wdb[...])

    @pl.when(i >= n_used)
    def _():
        y_ref[...] = jnp.zeros(y_ref.shape, y_ref.dtype)


def _experts(h, tok, run_id, run_e, n_used, n_valid, w_gate, w_up, w_down, layer):
    D = D_MODEL
    n_pad = tok.shape[0]
    n_blk = n_pad // MOE_BLOCK
    any_spec = pl.BlockSpec(memory_space=pl.ANY)
    return pl.pallas_call(
        functools.partial(_expert_kernel, layer=layer),
        out_shape=jax.ShapeDtypeStruct((n_pad, D), F32),
        grid_spec=pltpu.PrefetchScalarGridSpec(
            num_scalar_prefetch=5,
            grid=(n_blk,),
            in_specs=[any_spec, any_spec, any_spec, any_spec],
            out_specs=pl.BlockSpec((MOE_BLOCK, D), lambda i, *_: (i, 0)),
            scratch_shapes=[pltpu.VMEM((2, D, D_EXPERT), F32), pltpu.VMEM((2, D, D_EXPERT), F32),
                            pltpu.VMEM((2, D_EXPERT, D), F32),
                            pltpu.VMEM((D, D_EXPERT), BF16), pltpu.VMEM((D, D_EXPERT), BF16),
                            pltpu.VMEM((D_EXPERT, D), BF16),
                            pltpu.VMEM((2, MOE_BLOCK * PACKED_ROWS, LANES), jnp.uint32),
                            pltpu.SemaphoreType.DMA((2, 4)), pltpu.SemaphoreType.DMA((2,))]),
        compiler_params=_cparams(("arbitrary",)),
        name="moe_experts",
    )(run_id, run_e, n_used, n_valid, tok, h, w_gate, w_up, w_down)


def _combine_kernel(slot_ref, y_ref, x_ref, w_ref, g_ref, fg_ref, o_ref, buf, sem, *, rows, final_norm):
    i = pl.program_id(0)

    def copy(r, k):
        return pltpu.make_async_copy(y_ref.at[pl.ds(slot_ref[(i * rows + r) * 2 + k], 1)],
                                     buf.at[k, pl.ds(r, 1)], sem)

    def start(r, c):
        copy(r, 0).start()
        copy(r, 1).start()
        return c

    lax.fori_loop(0, rows, start, 0, unroll=8)
    for k in range(TOPK_IN_GROUP):
        pltpu.make_async_copy(y_ref.at[pl.ds(0, rows)], buf.at[k], sem).wait()
    w = w_ref[...]
    moe = buf[0] * w[:, 0:1] + buf[1] * w[:, 1:2]
    y = x_ref[...] + g_ref[...] * moe
    if final_norm:
        y = (y * lax.rsqrt(jnp.mean(y * y, axis=-1, keepdims=True) + EPS)) * fg_ref[...]
    o_ref[...] = y


def _combine(yb, slots, x, gate_w, g2, final_g, final_norm, rows=256):
    T, D = x.shape
    return pl.pallas_call(
        functools.partial(_combine_kernel, rows=rows, final_norm=final_norm),
        out_shape=jax.ShapeDtypeStruct((T, D), F32),
        grid_spec=pltpu.PrefetchScalarGridSpec(
            num_scalar_prefetch=1,
            grid=(T // rows,),
            in_specs=[pl.BlockSpec(memory_space=pl.ANY),
                      pl.BlockSpec((rows, D), lambda i, s: (i, 0)),
                      pl.BlockSpec((rows, LANES), lambda i, s: (i, 0)),
                      pl.BlockSpec((1, D), lambda i, s: (0, 0)),
                      pl.BlockSpec((1, D), lambda i, s: (0, 0))],
            out_specs=pl.BlockSpec((rows, D), lambda i, s: (i, 0)),
            scratch_shapes=[pltpu.VMEM((2, rows, D), F32), pltpu.SemaphoreType.DMA(())]),
        compiler_params=_cparams(("arbitrary",)),
        name="moe_combine",
    )(slots.reshape(-1), yb, x, gate_w, g2, final_g)


def _hier_moe_residual(x, g, sc, sh, g2, w_grp, b_grp, w_exp, b_exp, w_gate, w_up, w_down, layer, final_g,
                       final_norm):
    N, D = x.shape
    n_route = N_GROUPS + N_EXPERTS
    w_r = jnp.zeros((D, LANES), F32).at[:, :N_GROUPS].set(w_grp).at[:, N_GROUPS:n_route].set(w_exp)
    b_r = jnp.zeros((1, LANES), F32).at[0, :N_GROUPS].set(b_grp).at[0, N_GROUPS:n_route].set(b_exp)
    h, eid_l, gate_l, cnt = _router(x, g, sc, sh, w_r, b_r)

    A = N * TOPK_IN_GROUP
    counts = cnt[0, :N_EXPERTS].astype(jnp.int32)
    padded = (counts + MOE_BLOCK - 1) // MOE_BLOCK * MOE_BLOCK
    pad_end = jnp.cumsum(padded)
    pad_start = pad_end - padded
    n_blk = -(-(A + N_EXPERTS * MOE_BLOCK) // MOE_BLOCK)
    base = jnp.zeros((1, LANES), F32).at[0, :N_EXPERTS].set(pad_start.astype(F32))
    dest_l, tok_tbl = _dispatch(eid_l, base, n_blk)
    buf_tok = tok_tbl
    blk_idx = jnp.arange(n_blk, dtype=jnp.int32)
    n_used = pad_end[-1] // MOE_BLOCK
    blk_e = jnp.minimum(jnp.sum((pad_end[None, :] <= (blk_idx * MOE_BLOCK)[:, None]).astype(jnp.int32), axis=1),
                        N_EXPERTS - 1)
    blk_e = jnp.where(blk_idx < n_used, blk_e, blk_e[n_used - 1])
    change = jnp.concatenate([jnp.ones((1,), bool), blk_e[1:] != blk_e[:-1]])
    run_id = (jnp.cumsum(change.astype(jnp.int32)) - 1).astype(jnp.int32)
    runs = jnp.arange(n_blk + 2, dtype=jnp.int32)
    run_e = jnp.max(jnp.where(run_id[None, :] == runs[:, None], blk_e[None, :], -1), axis=1).astype(jnp.int32)
    slots = dest_l[:, :TOPK_IN_GROUP]

    n_valid = jnp.clip(counts[blk_e] - (blk_idx * MOE_BLOCK - pad_start[blk_e]), 0, MOE_BLOCK).astype(jnp.int32)
    yb = _experts(h, buf_tok, run_id, run_e, n_used.reshape(1).astype(jnp.int32), n_valid, w_gate, w_up, w_down, layer)
    return _combine(yb, slots, x, gate_l, g2, final_g, final_norm)


def _even_mixer_residual(x, g, sc, sh, g1, rel_bias, w_in, w_out, cmp_pe, cmp_w1, cmp_b1, cmp_w2, cmp_b2,
                         conv_w, conv_b, lru_wa, lru_ba, lru_wi, lru_bi, lru_lambda):
    T, D = x.shape
    G, HD = NSA_KV_HEADS, HEAD_DIM
    nq = NSA_HEADS * HD
    n_kv = 6 * NSA_KV_W
    n_gate = 3 * NSA_HEADS
    w_main = jnp.concatenate([w_in[:, :nq], w_in[:, nq + n_kv + n_gate:], w_in[:, nq:nq + n_kv]],
                             axis=1).astype(BF16)
    w_gl = jnp.zeros((D, LANES), F32).at[:, :n_gate].set(w_in[:, nq + n_kv:nq + n_kv + n_gate]).astype(BF16)
    kv0 = nq + 2 * LRU_WIDTH
    tn = 768
    proj, proj_gate, kv_b = _norm_proj(x, g, sc, sh, w_main, w_gl, bf16_from=kv0 // tn, tn=tn)

    def kv_heads(j, src=kv_b, c_base=0):
        c0 = c_base + j * NSA_KV_W
        return src[:, c0:c0 + NSA_KV_W].reshape(T, G, HD).transpose(1, 0, 2)

    cmp_in = jnp.stack([kv_heads(0, proj, kv0), kv_heads(1, proj, kv0)], axis=0)
    cmp_out = _compress(cmp_in, cmp_pe, cmp_w1, cmp_b1, cmp_w2, cmp_b2)
    ncp = T // CMP_STRIDE
    valid = (jnp.arange(ncp) < ncp - 1)[None, None, :, None]
    cmp_pad = jnp.pad(jnp.where(valid, cmp_out, 0.0), ((0, 0), (0, 0), (Q_BLOCK, Q_BLOCK), (0, 0)))
    slc_pad = ((0, 0), (Q_BLOCK, SLC_TAIL), (0, 0))
    blk_of_row = jnp.pad(jnp.arange(T, dtype=jnp.int32) // SLC_LEN, (Q_BLOCK, SLC_TAIL), constant_values=Q_BLOCK - 1)
    blk_onehot = (blk_of_row[:, None] == jnp.arange(Q_BLOCK, dtype=jnp.int32)[None, :]).astype(BF16)
    ks = jnp.concatenate([jnp.pad(kv_heads(2), slc_pad), jnp.broadcast_to(blk_onehot, (G,) + blk_onehot.shape)], axis=2)
    vs = jnp.pad(kv_heads(3), slc_pad)
    kw = jnp.pad(kv_heads(4), ((0, 0), (WINDOW, 0), (0, 0)))
    vw = jnp.pad(kv_heads(5), ((0, 0), (WINDOW, 0), (0, 0)))
    nsa_out = _nsa_attention(rel_bias, proj, proj_gate, cmp_pad[0], cmp_pad[1], ks, vs, kw, vw)

    y_col = nq // LRU_WIDTH
    lru_out = _rglru(proj, y_col, y_col + 1, conv_w, conv_b, lru_wa, lru_ba, lru_wi, lru_bi, lru_lambda)
    w_o = w_out.astype(BF16)
    return _out_proj(nsa_out, lru_out, w_o, x, g1)


def _odd_mixer_residual(x, g, sc, sh, g1, lb_logits, w_in, w_out, norm_g, layer):
    proj = _norm_proj(x, g, sc, sh, w_in.astype(BF16), tn=1024)[0]
    o = _hgrn2(proj, lb_logits, norm_g, layer)
    w_o = w_out.astype(BF16)
    return _out_proj(o, o, w_o, x, g1, a2_col=1)


def kernel(x, c, rel_bias, ada_w, ada_b, norm_mix_g, norm_ffn_g, ev_w_in, ev_w_out, cmp_pe, cmp_w1, cmp_b1, cmp_w2, cmp_b2, lru_conv_w, lru_conv_b, lru_wa, lru_ba, lru_wi, lru_bi, lru_lambda, od_w_in, od_w_out, hg_lb_logits, hg_norm_g, moe_w_grp, moe_b_grp, moe_w_exp, moe_b_exp, moe_w_gate, moe_w_up, moe_w_down, final_g):
    B, T, D = x.shape
    assert B == 1 and D == D_MODEL
    xt = x.reshape(T, D)
    mod = _modulation(c, ada_w, ada_b)
    for l in range(DEPTH):
        sh1, sc1, g1, sh2, sc2, g2 = [mod[l, :, k * D:(k + 1) * D] for k in range(6)]
        gm = norm_mix_g[l].reshape(1, D)
        gf = norm_ffn_g[l].reshape(1, D)
        j = l // 2
        if l % 2 == 0:
            xt = _even_mixer_residual(xt, gm, sc1, sh1, g1, rel_bias, ev_w_in[j], ev_w_out[j], cmp_pe[j],
                                      cmp_w1[j], cmp_b1[j], cmp_w2[j], cmp_b2[j], lru_conv_w[j],
                                      lru_conv_b[j], lru_wa[j], lru_ba[j], lru_wi[j], lru_bi[j], lru_lambda[j])
        else:
            xt = _odd_mixer_residual(xt, gm, sc1, sh1, g1, hg_lb_logits, od_w_in[j], od_w_out[j],
                                     hg_norm_g[j], l)
        xt = _hier_moe_residual(xt, gf, sc2, sh2, g2, moe_w_grp[l], moe_b_grp[l], moe_w_exp[l], moe_b_exp[l],
                                moe_w_gate, moe_w_up, moe_w_down, l, final_g.reshape(1, D), l == DEPTH - 1)
    return xt.reshape(B, T, D)
```

```python
import functools
import math

import numpy as np
import jax
import jax.numpy as jnp
from jax import lax
from jax.experimental import pallas as pl
from jax.experimental.pallas import tpu as pltpu

F32 = jnp.float32
BF16 = jnp.bfloat16

D_MODEL = 2048
DEPTH = 2
NSA_HEADS = 8
NSA_KV_HEADS = 2
NSA_HPG = NSA_HEADS // NSA_KV_HEADS
HEAD_DIM = 128
NSA_KV_W = NSA_KV_HEADS * HEAD_DIM
CMP_LEN = 32
CMP_STRIDE = 16
SLC_LEN = 64
SLC_TOPN = 16
WINDOW = 512
Q_BLOCK = 128
LRU_WIDTH = 1024
LRU_BLOCKS = 8
LRU_BW = LRU_WIDTH // LRU_BLOCKS
CONV_W = 4
LRU_C = 8.0
HG_HEADS = 16
HG_DK = 128
HG_DV = 128
HG_CHUNK = 64
HG_SUB = 16
HG_HEADS_PER_STEP = 16
SLC_GROUP = 4
RANK_UNROLL = 4
RANK_BAND = 32
SLC_TAIL = 2 * SLC_GROUP * Q_BLOCK
HG_KW = HG_HEADS * HG_DK
HG_VW = HG_HEADS * HG_DV
N_BUCKETS = 32
MAX_DIST = 128
N_GROUPS = 8
EXP_PER_GROUP = 8
N_EXPERTS = N_GROUPS * EXP_PER_GROUP
TOPK_IN_GROUP = 2
D_EXPERT = 512
MOE_BLOCK = 128
EPS = 1e-6

LANES = 128
PACKED_ROWS = D_MODEL // 2 // LANES
NEG = -1e30
M_FLOOR = -1e20
LOG2E = math.log2(math.e)
VMEM_LIMIT = 56 * 1024 * 1024


def _cparams(sem):
    return pltpu.CompilerParams(dimension_semantics=sem, vmem_limit_bytes=VMEM_LIMIT)


def _dot(a, b):
    return jnp.dot(a, b, preferred_element_type=F32)


def _dot_nt(a, b):
    return lax.dot_general(a, b, (((1,), (1,)), ((), ())), preferred_element_type=F32)


def _dot_tn(a, b):
    return lax.dot_general(a, b, (((0,), (0,)), ((), ())), preferred_element_type=F32)


def _bucket_starts():
    n = np.arange(0, MAX_DIST + 1, dtype=np.int32)
    max_exact = N_BUCKETS // 2
    nf = np.maximum(n, 1).astype(np.float32)
    large = max_exact + (np.log(nf / np.float32(max_exact)) / np.float32(math.log(MAX_DIST / max_exact))
                         * np.float32(N_BUCKETS - max_exact)).astype(np.int32)
    large = np.minimum(large, N_BUCKETS - 1)
    b = np.where(n < max_exact, n, large)
    starts = [int(np.argmax(b >= k)) for k in range(N_BUCKETS)]
    assert all(b[s] == k for k, s in enumerate(starts)) and b[-1] == N_BUCKETS - 1
    return starts


BUCKET_STARTS = _bucket_starts()


def _mod_kernel(c_ref, w_ref, b_ref, o_ref):
    c = c_ref[...]
    cact = c * jax.nn.sigmoid(c)
    o_ref[...] = jnp.sum(cact * w_ref[...], axis=0, keepdims=True) + b_ref[...]


def _modulation(c, ada_w, ada_b):
    D = D_MODEL
    tn = 1024
    n_out = 6 * D
    c_col = c.reshape(D, 1)
    return pl.pallas_call(
        _mod_kernel,
        out_shape=jax.ShapeDtypeStruct((DEPTH, 1, n_out), F32),
        grid=(DEPTH, n_out // tn),
        in_specs=[pl.BlockSpec((D, 1), lambda l, j: (0, 0)),
                  pl.BlockSpec((None, D, tn), lambda l, j: (l, 0, j)),
                  pl.BlockSpec((None, 1, tn), lambda l, j: (l, 0, j))],
        out_specs=pl.BlockSpec((None, 1, tn), lambda l, j: (l, 0, j)),
        compiler_params=_cparams(("arbitrary", "arbitrary")),
        name="adaln_mod",
    )(c_col, ada_w, ada_b.reshape(DEPTH, 1, n_out))


def _norm_mod(x, g, sc, sh):
    y = x * lax.rsqrt(jnp.mean(x * x, axis=-1, keepdims=True) + EPS)
    return (y * g) * (1.0 + sc) + sh


def _norm_proj_kernel(x_ref, g_ref, sc_ref, sh_ref, w_ref, *rest, has_extra, bf16_from):
    if has_extra:
        wx_ref, o_ref, ox_ref, ob_ref, h_ref = rest
    else:
        o_ref, h_ref = rest
    j = pl.program_id(1)

    @pl.when(j == 0)
    def _():
        h = _norm_mod(x_ref[...], g_ref[...], sc_ref[...], sh_ref[...]).astype(BF16)
        h_ref[...] = h
        if has_extra:
            ox_ref[...] = _dot(h, wx_ref[...])

    y = _dot(h_ref[...], w_ref[...])
    o_ref[...] = y
    if has_extra:
        @pl.when(j >= bf16_from)
        def _():
            ob_ref[...] = y.astype(BF16)


def _norm_proj(x, g, sc, sh, w, w_extra=None, bf16_from=None, tm=1024, tn=512):
    T, D = x.shape
    N = w.shape[1]
    has_extra = w_extra is not None
    vec = pl.BlockSpec((1, D), lambda i, j: (0, 0))
    in_specs = [pl.BlockSpec((tm, D), lambda i, j: (i, 0)), vec, vec, vec,
                pl.BlockSpec((D, tn), lambda i, j: (0, j))]
    out_shape = [jax.ShapeDtypeStruct((T, N), F32)]
    out_specs = [pl.BlockSpec((tm, tn), lambda i, j: (i, j))]
    args = [x, g, sc, sh, w]
    if has_extra:
        nx = w_extra.shape[1]
        in_specs.append(pl.BlockSpec((D, nx), lambda i, j: (0, 0)))
        out_shape.append(jax.ShapeDtypeStruct((T, nx), F32))
        out_specs.append(pl.BlockSpec((tm, nx), lambda i, j: (i, 0)))
        args.append(w_extra)
        out_shape.append(jax.ShapeDtypeStruct((T, N - bf16_from * tn), BF16))
        out_specs.append(pl.BlockSpec((tm, tn), lambda i, j: (i, jnp.maximum(j - bf16_from, 0))))
    return pl.pallas_call(
        functools.partial(_norm_proj_kernel, has_extra=has_extra, bf16_from=bf16_from),
        out_shape=out_shape,
        grid=(T // tm, N // tn),
        in_specs=in_specs,
        out_specs=out_specs,
        scratch_shapes=[pltpu.VMEM((tm, D), BF16)],
        compiler_params=_cparams(("arbitrary", "arbitrary")),
        name="norm_proj",
    )(*args)


def _out_proj_kernel(a1_ref, a2_ref, w1_ref, w2_ref, x_ref, g_ref, o_ref):
    y = _dot(a1_ref[...], w1_ref[...]) + _dot(a2_ref[...], w2_ref[...])
    o_ref[...] = x_ref[...] + g_ref[...] * y


def _out_proj(a1, a2, w, x, gate, a2_col=0, tm=1024, tn=1024):
    T, D = x.shape
    K1 = K2 = w.shape[0] // 2
    return pl.pallas_call(
        _out_proj_kernel,
        out_shape=jax.ShapeDtypeStruct((T, D), F32),
        grid=(T // tm, D // tn),
        in_specs=[pl.BlockSpec((tm, K1), lambda i, j: (i, 0)),
                  pl.BlockSpec((tm, K2), lambda i, j: (i, a2_col)),
                  pl.BlockSpec((K1, tn), lambda i, j: (0, j)),
                  pl.BlockSpec((K2, tn), lambda i, j: (1, j)),
                  pl.BlockSpec((tm, tn), lambda i, j: (i, j)),
                  pl.BlockSpec((1, tn), lambda i, j: (0, j))],
        out_specs=pl.BlockSpec((tm, tn), lambda i, j: (i, j)),
        compiler_params=_cparams(("arbitrary", "arbitrary")),
        name="out_proj",
    )(a1, a2, w, w, x, gate)


def _compress_kernel(k2_ref, pe_ref, w1_ref, b1_ref, w2_ref, b2_ref, o_ref):
    half = (CMP_LEN // 2) * HEAD_DIM
    k2 = k2_ref[...]
    a = _dot((k2 + pe_ref[0:1, :]).astype(BF16), w1_ref[0:half, :].astype(BF16))
    b = _dot((k2 + pe_ref[1:2, :]).astype(BF16), w1_ref[half:2 * half, :].astype(BF16))
    nb = k2.shape[0]
    b_up = pltpu.roll(b, nb - 1, axis=0)
    hid = jax.nn.gelu(a + b_up + b1_ref[...])
    o_ref[...] = _dot(hid.astype(BF16), w2_ref[...].astype(BF16)) + b2_ref[...]


def _compress(kv, pe, w1, b1, w2, b2):
    _, G, T, HD = kv.shape
    nb = T // CMP_STRIDE
    row = CMP_STRIDE * HD
    kv2 = kv.reshape(2, G, nb, row)
    pe2 = pe.reshape(2, 2, row)
    return pl.pallas_call(
        _compress_kernel,
        out_shape=jax.ShapeDtypeStruct((2, G, nb, HD), F32),
        grid=(2, G),
        in_specs=[pl.BlockSpec((None, None, nb, row), lambda a, g: (a, g, 0, 0)),
                  pl.BlockSpec((None, 2, row), lambda a, g: (a, 0, 0)),
                  pl.BlockSpec((None, CMP_LEN * HD, HD), lambda a, g: (a, 0, 0)),
                  pl.BlockSpec((None, 1, HD), lambda a, g: (a, 0, 0)),
                  pl.BlockSpec((None, HD, HD), lambda a, g: (a, 0, 0)),
                  pl.BlockSpec((None, 1, HD), lambda a, g: (a, 0, 0))],
        out_specs=pl.BlockSpec((None, None, nb, HD), lambda a, g: (a, g, 0, 0)),
        compiler_params=_cparams(("arbitrary", "arbitrary")),
        name="nsa_compress",
    )(kv2, pe2, w1, b1.reshape(2, 1, HD), w2, b2.reshape(2, 1, HD))


def _bias_from_rel(rel, tbl_ref, head):
    far = tbl_ref[N_BUCKETS - 1, head]
    val = jnp.zeros(rel.shape, F32)
    for b in range(N_BUCKETS - 2, -1, -1):
        val = jnp.where(rel < BUCKET_STARTS[b + 1], (tbl_ref[b, head] - far) * LOG2E, val)
    return val


def _lane_tile4(x):
    return jnp.concatenate([x, x, x, x], axis=1)


def _col_softmax_stats(s):
    m = jnp.maximum(jnp.max(s, axis=0, keepdims=True), M_FLOOR)
    e = jnp.exp2(s - m)
    d = jnp.sum(e, axis=0, keepdims=True)
    return e, d


def _nsa_kernel(tbl_ref, q_ref, gl_ref, kc_ref, vc_ref, ks_ref, vs_ref, kw_ref, vw_ref, o_ref,
                bd_ref, bp_ref, bn_ref, imp_ref, cnt_ref, *, ncp):
    g = pl.program_id(0)
    ci = pl.program_id(1)
    Q = Q_BLOCK
    HD = HEAD_DIM
    H4 = NSA_HPG

    row_i = lax.broadcasted_iota(jnp.int32, (Q, Q), 0)
    lane_i = lax.broadcasted_iota(jnp.int32, (Q, Q), 1)

    @pl.when((g == 0) & (ci == 0))
    def _build_bias_tiles():
        for h in range(NSA_HEADS):
            rel_d = lane_i - row_i
            bd_ref[h] = jnp.where(rel_d >= 0, _bias_from_rel(jnp.maximum(rel_d, 0), tbl_ref, h), NEG)
            bp_ref[h] = _bias_from_rel(lane_i - row_i + Q, tbl_ref, h)
            rel_n = lane_i - CMP_STRIDE * row_i + (CMP_STRIDE * (Q - 8) - (CMP_LEN - 1))
            bn_ref[h] = jnp.where(rel_n >= 0, _bias_from_rel(jnp.maximum(rel_n, 0), tbl_ref, h), NEG)

    def head_tiles(ref):
        return jnp.concatenate([ref[g * H4 + h] for h in range(H4)], axis=1)

    qt = q_ref[...] * (HD ** -0.5 * LOG2E)
    qs = jnp.concatenate([qt[:, h * HD:(h + 1) * HD] for h in range(H4)], axis=0).astype(BF16)

    near0 = 8 * ci - (Q - 8)
    kc_far = kc_ref[pl.ds(Q, ncp), :]
    vc_far = vc_ref[pl.ds(Q, ncp), :]
    near_row = pl.multiple_of(8 * ci + 8, 8)
    kc_near = kc_ref[pl.ds(near_row, Q), :]
    vc_near = vc_ref[pl.ds(near_row, Q), :]
    n_far = lax.broadcasted_iota(jnp.int32, (ncp, 1), 0)
    s_far = jnp.where(n_far < near0, _dot_nt(kc_far.astype(BF16), qs), NEG)
    m_near = lax.broadcasted_iota(jnp.int32, (Q, 1), 0)
    s_near = _dot_nt(kc_near.astype(BF16), qs) + head_tiles(bn_ref)
    s_near = jnp.where(m_near + near0 >= 0, s_near, NEG)
    s_c = jnp.concatenate([s_far, s_near], axis=0)
    e_c, d_c = _col_softmax_stats(s_c)
    p_c = e_c * (1.0 / jnp.where(d_c > 0, d_c, 1.0))
    v_c = jnp.concatenate([vc_far, vc_near], axis=0).astype(BF16)
    o_cmp = _dot_tn(v_c, p_c.astype(BF16))

    psum = p_c[:, 0:Q] + p_c[:, Q:2 * Q] + p_c[:, 2 * Q:3 * Q] + p_c[:, 3 * Q:4 * Q]
    p_hi = psum.astype(BF16)
    p_lo = (psum - p_hi.astype(F32)).astype(BF16)
    jb = lax.broadcasted_iota(jnp.int32, (Q, ncp), 0)
    nb = lax.broadcasted_iota(jnp.int32, (Q, ncp), 1)
    ratio = SLC_LEN // CMP_STRIDE
    span = CMP_LEN // CMP_STRIDE - 1
    cover_far = ((nb >= ratio * jb - span) & (nb <= ratio * jb + ratio - 1)).astype(BF16)
    nn = lane_i + near0
    cover_near = ((nn >= ratio * row_i - span) & (nn <= ratio * row_i + ratio - 1)).astype(BF16)
    cover = jnp.concatenate([cover_far, cover_near], axis=1)
    imp = _dot(cover, p_hi) + _dot(cover, p_lo)
    cur = 2 * ci + (lane_i >= SLC_LEN).astype(jnp.int32)
    imp = jnp.where(row_i == cur, jnp.inf, jnp.where(row_i > cur, -jnp.inf, imp))
    imp_ref[...] = imp

    n_cand = (2 * ci + 2 + RANK_UNROLL - 1) // RANK_UNROLL
    cnt_ref[...] = jnp.zeros((Q, Q), F32)
    for band in range(Q // RANK_BAND):
        lo_row = band * RANK_BAND

        @pl.when(lo_row <= 2 * ci + 1)
        def _(lo_row=lo_row):
            imp_b = imp[lo_row:lo_row + RANK_BAND]
            row_b = row_i[lo_row:lo_row + RANK_BAND]

            def rank_body(it, cnt):
                for u in range(RANK_UNROLL):
                    b = it * RANK_UNROLL + u
                    r = imp_ref[pl.ds(b, 1), :]
                    ahead = (r > imp_b) | ((r == imp_b) & (b < row_b))
                    cnt = cnt + ahead.astype(F32)
                return cnt

            cnt_ref[lo_row:lo_row + RANK_BAND, :] = lax.fori_loop(0, n_cand, rank_body,
                                                                   jnp.zeros((RANK_BAND, Q), F32))

    sel = ((cnt_ref[...] < SLC_TOPN) & (row_i <= cur)).astype(F32)

    sel_q = jnp.transpose(sel)
    near_blk = lane_i >= 2 * ci - 2
    m_far = jnp.where((sel_q > 0) & jnp.logical_not(near_blk), 0.0, NEG).astype(BF16)
    m_near = jnp.where((sel_q > 0) & near_blk, 0.0, NEG).astype(BF16)
    qa_far = jnp.concatenate([qs, jnp.concatenate([m_far] * H4, axis=0)], axis=1)
    qa_near = jnp.concatenate([qs, jnp.concatenate([m_near] * H4, axis=0)], axis=1)

    def slc_rows(kt0, nt):
        r0 = pl.multiple_of((kt0 + 1) * Q, Q)
        return ks_ref[pl.ds(r0, nt * Q), :], vs_ref[pl.ds(r0, nt * Q), :]

    def online_update(streams):
        m_new = [jnp.maximum(c[0], jnp.max(s, axis=0, keepdims=True)) for s, _, c in streams]
        alpha = [jnp.exp2(c[0] - mn) for (_, _, c), mn in zip(streams, m_new)]
        p = [jnp.exp2(s - mn) for (s, _, _), mn in zip(streams, m_new)]
        l_new = [a * c[1] + jnp.sum(pp, axis=0, keepdims=True) for a, pp, (_, _, c) in zip(alpha, p, streams)]
        pv = [_dot_tn(v, pp.astype(BF16)) for pp, (_, v, _) in zip(p, streams)]
        return [(mn, ln, a * c[2] + x) for mn, ln, a, x, (_, _, c) in zip(m_new, l_new, alpha, pv, streams)]

    carry0 = (jnp.full((1, H4 * Q), M_FLOOR, F32), jnp.zeros((1, H4 * Q), F32), jnp.zeros((HD, H4 * Q), F32))
    n_far = jnp.maximum(ci - 1, 0)

    def far_body(it, carries):
        kt0 = it * (2 * SLC_GROUP)
        k_a, v_a = slc_rows(kt0, SLC_GROUP)
        k_b, v_b = slc_rows(kt0 + SLC_GROUP, SLC_GROUP)
        s_a = _dot_nt(k_a, qa_far)
        s_b = _dot_nt(k_b, qa_far)
        return tuple(online_update([(s_a, v_a, carries[0]), (s_b, v_b, carries[1])]))

    (m_a, l_a, acc_a), (m_b, l_b, acc_b) = lax.fori_loop(
        0, (n_far + 2 * SLC_GROUP - 1) // (2 * SLC_GROUP), far_body, (carry0, carry0))
    m_ab = jnp.maximum(m_a, m_b)
    w_a = jnp.exp2(m_a - m_ab)
    w_b = jnp.exp2(m_b - m_ab)
    carry = (m_ab, w_a * l_a + w_b * l_b, w_a * acc_a + w_b * acc_b)
    bp_t = head_tiles(bp_ref)
    bd_t = head_tiles(bd_ref)
    k_n, v_n = slc_rows(ci - 1, 2)
    s_n = _dot_nt(k_n, qa_near) + jnp.concatenate([bp_t, bd_t], axis=0)
    (_, l_s, acc_s), = online_update([(s_n, v_n, carry)])
    o_slc = acc_s * (1.0 / jnp.where(l_s > 0, l_s, 1.0))

    kw = kw_ref[pl.ds(pl.multiple_of(ci * Q, Q), WINDOW + Q), :]
    vw = vw_ref[pl.ds(pl.multiple_of(ci * Q, Q), WINDOW + Q), :]
    s_w = _dot_nt(kw, qs)
    n_wt = WINDOW // Q
    in_window = lane_i < row_i
    pieces = [jnp.concatenate([jnp.where(in_window, s_w[0:Q, h * Q:(h + 1) * Q], NEG) for h in range(H4)], axis=1)]
    for t in range(1, n_wt - 1):
        pieces.append(s_w[t * Q:(t + 1) * Q])
    pieces.append(s_w[(n_wt - 1) * Q:n_wt * Q] + bp_t)
    pieces.append(s_w[n_wt * Q:(n_wt + 1) * Q] + bd_t)
    s_w = jnp.concatenate(pieces, axis=0)
    x_w = lax.broadcasted_iota(jnp.int32, (WINDOW + Q, 1), 0)
    s_w = jnp.where(x_w + ci * Q >= WINDOW, s_w, NEG)
    e_w, d_w = _col_softmax_stats(s_w)
    p_w = e_w * (1.0 / jnp.where(d_w > 0, d_w, 1.0))
    o_win = _dot_tn(vw, p_w.astype(BF16))

    gt = jnp.transpose(jax.nn.sigmoid(gl_ref[...]))
    outs = []
    for h in range(H4):
        base = (g * H4 + h) * 3
        sl = slice(h * Q, (h + 1) * Q)
        gc = gt_row(gt, base)
        gs = gt_row(gt, base + 1)
        gw = gt_row(gt, base + 2)
        o_h = gc * o_cmp[:, sl] + gs * o_slc[:, sl] + gw * o_win[:, sl]
        outs.append(jnp.transpose(o_h))
    o_ref[...] = jnp.concatenate(outs, axis=1).astype(o_ref.dtype)


def gt_row(gt, idx):
    rows = lax.broadcasted_iota(jnp.int32, gt.shape, 0)
    return jnp.sum(jnp.where(rows == idx, gt, 0.0), axis=0, keepdims=True)


def _nsa_attention(rel_bias, proj, proj_gate, kcmp, vcmp, ks, vs, kw, vw):
    T = proj.shape[0]
    G = NSA_KV_HEADS
    Q = Q_BLOCK
    ncp = T // CMP_STRIDE
    nch = T // Q
    assert nch % SLC_GROUP == 0 and T // SLC_LEN <= Q
    kernel = functools.partial(_nsa_kernel, ncp=ncp)
    full = lambda rows, w=HEAD_DIM: pl.BlockSpec((None, rows, w), lambda g, c: (g, 0, 0))
    return pl.pallas_call(
        kernel,
        out_shape=jax.ShapeDtypeStruct((T, NSA_HEADS * HEAD_DIM), BF16),
        grid=(G, nch),
        in_specs=[pl.BlockSpec(memory_space=pltpu.SMEM),
                  pl.BlockSpec((Q, NSA_HPG * HEAD_DIM), lambda g, c: (c, g)),
                  pl.BlockSpec((Q, LANES), lambda g, c: (c, 0)),
                  full(ncp + 2 * Q), full(ncp + 2 * Q),
                  full(T + Q + SLC_TAIL, HEAD_DIM + Q), full(T + Q + SLC_TAIL), full(T + WINDOW), full(T + WINDOW)],
        out_specs=pl.BlockSpec((Q, NSA_HPG * HEAD_DIM), lambda g, c: (c, g)),
        scratch_shapes=[pltpu.VMEM((NSA_HEADS, Q, Q), F32), pltpu.VMEM((NSA_HEADS, Q, Q), F32),
                        pltpu.VMEM((NSA_HEADS, Q, Q), F32), pltpu.VMEM((Q, Q), F32), pltpu.VMEM((Q, Q), F32)],
        compiler_params=_cparams(("arbitrary", "arbitrary")),
        name="nsa_attention",
    )(rel_bias, proj, proj_gate, kcmp, vcmp, ks, vs, kw, vw)


def _softplus(z):
    return jnp.maximum(z, 0.0) + jnp.log1p(jnp.exp(-jnp.abs(z)))


def _lru_kernel(x_ref, y_ref, cw_ref, cb_ref, wa_ref, ba_ref, wi_ref, bi_ref, lam_ref, o_ref,
                xbuf, hc, a_s, b_s, h_s, *, tb):
    i = pl.program_id(0)

    @pl.when(i == 0)
    def _():
        xbuf[...] = jnp.zeros((8, LRU_WIDTH), F32)
        hc[...] = jnp.zeros((1, LRU_WIDTH), F32)

    x = x_ref[...]
    prev = xbuf[...]
    rows8 = lax.broadcasted_iota(jnp.int32, (8, LRU_WIDTH), 0)
    xc = cb_ref[...]
    for w in range(CONV_W):
        sh = CONV_W - 1 - w
        if sh == 0:
            xs = x
        else:
            xr = pltpu.roll(x, sh, axis=0)
            head = jnp.where(rows8 < sh, pltpu.roll(prev, sh, axis=0), xr[0:8])
            xs = jnp.concatenate([head, xr[8:]], axis=0)
        xc = xc + cw_ref[w:w + 1, :] * xs
    xbuf[...] = x[tb - 8:tb]

    xcb = xc.astype(BF16)
    ra, ia = [], []
    for n in range(LRU_BLOCKS):
        xg = xcb[:, n * LRU_BW:(n + 1) * LRU_BW]
        ra.append(_dot(xg, wa_ref[n].astype(BF16)))
        ia.append(_dot(xg, wi_ref[n].astype(BF16)))
    r_gate = jax.nn.sigmoid(jnp.concatenate(ra, axis=1) + ba_ref[...])
    i_gate = jax.nn.sigmoid(jnp.concatenate(ia, axis=1) + bi_ref[...])
    log_a = (-LRU_C * r_gate) * _softplus(-lam_ref[...])
    a = jnp.exp(log_a)
    a_s[...] = a
    b_s[...] = jnp.sqrt(-jnp.tanh(log_a) * (a * a + 1.0)) * (i_gate * xc)

    rows = lax.broadcasted_iota(jnp.int32, (8, LRU_WIDTH), 0)

    def tile(k, h):
        r0 = pl.multiple_of(k * 8, 8)
        A = a_s[pl.ds(r0, 8), :]
        B = b_s[pl.ds(r0, 8), :]
        for sh in (1, 2, 4):
            ok = rows >= sh
            A_p = pltpu.roll(A, sh, axis=0)
            B_p = pltpu.roll(B, sh, axis=0)
            B = jnp.where(ok, A * B_p + B, B)
            A = jnp.where(ok, A * A_p, A)
        H = A * h + B
        h_s[pl.ds(r0, 8), :] = H
        return H[7:8, :]

    hc[...] = lax.fori_loop(0, tb // 8, tile, hc[...])
    o_ref[...] = (jax.nn.gelu(y_ref[...]) * h_s[...]).astype(o_ref.dtype)


def _rglru(proj, y_col, x_col, conv_w, conv_b, wa, ba, wi, bi, lam, tb=256):
    T = proj.shape[0]
    W = LRU_WIDTH
    vec = pl.BlockSpec((1, W), lambda i: (0, 0))
    blk = pl.BlockSpec((LRU_BLOCKS, LRU_BW, LRU_BW), lambda i: (0, 0, 0))
    return pl.pallas_call(
        functools.partial(_lru_kernel, tb=tb),
        out_shape=jax.ShapeDtypeStruct((T, W), BF16),
        grid=(T // tb,),
        in_specs=[pl.BlockSpec((tb, W), lambda i: (i, x_col)),
                  pl.BlockSpec((tb, W), lambda i: (i, y_col)),
                  pl.BlockSpec((CONV_W, W), lambda i: (0, 0)), vec, blk, vec, blk, vec, vec],
        out_specs=pl.BlockSpec((tb, W), lambda i: (i, 0)),
        scratch_shapes=[pltpu.VMEM((8, W), F32), pltpu.VMEM((1, W), F32),
                        pltpu.VMEM((tb, W), F32), pltpu.VMEM((tb, W), F32), pltpu.VMEM((tb, W), F32)],
        compiler_params=_cparams(("arbitrary",)),
        name="rglru",
    )(proj, proj, conv_w, conv_b.reshape(1, W), wa, ba.reshape(1, W), wi, bi.reshape(1, W), lam.reshape(1, W))


def _split3(x):
    hi = x.astype(BF16)
    r = x - hi.astype(F32)
    mid = r.astype(BF16)
    lo = (r - mid.astype(F32)).astype(BF16)
    return hi, mid, lo


def _hgrn_kernel(q_ref, f_ref, v_ref, g_ref, lbl_ref, ng_ref, o_ref, st_ref, oacc, *, tbh, layer):
    t = pl.program_id(1)
    C = HG_CHUNK
    SB = HG_SUB

    NH = HG_HEADS_PER_STEP
    DK, DV = HG_DK, HG_DV

    @pl.when(t == 0)
    def _():
        st_ref[...] = jnp.zeros((NH, DV, DK), F32)

    lg = lbl_ref[...]
    e = jnp.exp(lg - jnp.max(lg, axis=0, keepdims=True))
    sm = e / jnp.sum(e, axis=0, keepdims=True)
    cum = sm[0:1, :]
    for l in range(1, layer + 1):
        cum = cum + sm[l:l + 1, :]
    lb_all = cum - sm[0:1, :]

    tri = (lax.broadcasted_iota(jnp.int32, (C, C), 0) >= lax.broadcasted_iota(jnp.int32, (C, C), 1)).astype(BF16)
    rows8 = lax.broadcasted_iota(jnp.int32, (8, DK), 0)

    def head_chunk(r0, hh):
        cs = slice(hh * DK, (hh + 1) * DK)
        lb = lb_all[:, cs]
        qz = q_ref[pl.ds(r0, C), cs]
        qq = qz * jax.nn.sigmoid(qz)
        f = lb + (1.0 - lb) * jax.nn.sigmoid(f_ref[pl.ds(r0, C), cs])
        logf = jnp.log2(jnp.maximum(f, 1e-30))
        kk = 1.0 - f
        vv = v_ref[pl.ds(r0, C), cs]
        l_hi, l_mid, l_lo = _split3(logf)
        bcum = _dot(tri, l_hi) + _dot(tri, l_mid) + _dot(tri, l_lo)
        st = st_ref[hh]
        o_parts = []
        vb = vv.astype(BF16)
        for I in range(C // SB):
            lo_r = I * SB
            for a in range(SB // 8):
                t0 = lo_r + 8 * a
                q8 = qq[t0:t0 + 8]
                b8 = bcum[t0:t0 + 8]
                o8 = jnp.zeros((8, DV), F32)
                for s in range(lo_r, t0 + 8):
                    d = b8 - bcum[s:s + 1]
                    if s > t0:
                        d = jnp.where(rows8 >= s - t0, d, NEG)
                    w = q8 * (kk[s:s + 1] * jnp.exp2(d))
                    o8 = o8 + jnp.sum(w, axis=1, keepdims=True) * vv[s:s + 1]
                o_parts.append(o8)
        for I in range(1, C // SB):
            lo_r = I * SB
            ref = bcum[lo_r - 1:lo_r]
            qe = (qq[lo_r:lo_r + SB] * jnp.exp2(bcum[lo_r:lo_r + SB] - ref)).astype(BF16)
            ke = (kk[0:lo_r] * jnp.exp2(ref - bcum[0:lo_r])).astype(BF16)
            att = _dot_nt(qe, ke)
            o_off = _dot(att.astype(BF16), vb[0:lo_r])
            for a in range(SB // 8):
                o_parts[lo_r // 8 + a] = o_parts[lo_r // 8 + a] + o_off[8 * a:8 * a + 8]
        o_intra = jnp.concatenate(o_parts, axis=0)
        o_inter = _dot_nt((qq * jnp.exp2(bcum)).astype(BF16), st.astype(BF16))
        oacc[pl.ds(r0, C), cs] = o_inter + o_intra
        b_last = bcum[C - 1:C]
        kd = (kk * jnp.exp2(b_last - bcum)).astype(BF16)
        st_ref[hh] = st * jnp.exp2(b_last) + _dot_tn(vb, kd)

    def chunk(c, carry):
        r0 = pl.multiple_of(c * C, C)
        for hh in range(NH):
            head_chunk(r0, hh)
        return carry

    lax.fori_loop(0, tbh // C, chunk, 0)
    gz = g_ref[...]
    gate = (gz * jax.nn.sigmoid(gz)) * ng_ref[...]
    for hh in range(NH):
        cs = slice(hh * DV, (hh + 1) * DV)
        o = oacc[:, cs]
        o = o * lax.rsqrt(jnp.mean(o * o, axis=-1, keepdims=True) + EPS)
        o_ref[:, cs] = (o * gate[:, cs]).astype(o_ref.dtype)


def _hgrn2(proj, lb_logits, norm_g, layer, tbh=512):
    T = proj.shape[0]
    NH = HG_HEADS_PER_STEP
    HB = HG_HEADS // NH
    wk = NH * HG_DK
    col = lambda off: pl.BlockSpec((tbh, wk), lambda h, t: (t, off + h))
    return pl.pallas_call(
        functools.partial(_hgrn_kernel, tbh=tbh, layer=layer),
        out_shape=jax.ShapeDtypeStruct((T, HG_VW), BF16),
        grid=(HB, T // tbh),
        in_specs=[col(0), col(HB), col(2 * HB), col(3 * HB),
                  pl.BlockSpec((DEPTH, wk), lambda h, t: (0, h)),
                  pl.BlockSpec((1, wk), lambda h, t: (0, h))],
        out_specs=pl.BlockSpec((tbh, wk), lambda h, t: (t, h)),
        scratch_shapes=[pltpu.VMEM((NH, HG_DV, HG_DK), F32), pltpu.VMEM((tbh, wk), F32)],
        compiler_params=_cparams(("arbitrary", "arbitrary")),
        name="hgrn2",
    )(proj, proj, proj, proj, lb_logits, norm_g.reshape(1, HG_VW))


def _router_kernel(x_ref, g_ref, sc_ref, sh_ref, w_ref, b_ref, h_ref, eid_ref, gate_ref, cnt_ref):
    h = _norm_mod(x_ref[...], g_ref[...], sc_ref[...], sh_ref[...])
    tm, d = h.shape
    hb = h.astype(BF16)
    lo = lax.bitcast_convert_type(hb[:, :d // 2].astype(F32), jnp.uint32) >> 16
    hi = lax.bitcast_convert_type(hb[:, d // 2:].astype(F32), jnp.uint32)
    packed = hi | lo
    pt = d // 2 // LANES
    for c in range(pt):
        h_ref[pl.ds(c, tm, stride=pt), :] = packed[:, c * LANES:(c + 1) * LANES]
    h1, h2, h3 = _split3(h)
    w1, w2, w3 = _split3(w_ref[...])
    a = _dot(h1, jnp.concatenate([w1, w2, w3], axis=1))
    b = _dot(h2, jnp.concatenate([w1, w2], axis=1))
    c = _dot(h3, w1)
    L = LANES
    logits = (a[:, 0:L] + (a[:, L:2 * L] + b[:, 0:L])
              + (b[:, L:2 * L] + a[:, 2 * L:3 * L] + c)) + b_ref[...]
    lane = lax.broadcasted_iota(jnp.int32, logits.shape, 1)
    is_g = lane < N_GROUPS
    glog = jnp.where(is_g, logits, -jnp.inf)
    gmax = jnp.max(glog, axis=-1, keepdims=True)
    grp = jnp.min(jnp.where(glog == gmax, lane, LANES), axis=-1, keepdims=True)
    gsum = jnp.sum(jnp.where(is_g, jnp.exp(glog - gmax), 0.0), axis=-1, keepdims=True)
    p_grp = 1.0 / gsum
    lo = N_GROUPS + EXP_PER_GROUP * grp
    el = jnp.where((lane >= lo) & (lane < lo + EXP_PER_GROUP), logits, -jnp.inf)
    v1 = jnp.max(el, axis=-1, keepdims=True)
    i1 = jnp.min(jnp.where(el == v1, lane, LANES), axis=-1, keepdims=True)
    el2 = jnp.where(lane == i1, -jnp.inf, el)
    v2 = jnp.max(el2, axis=-1, keepdims=True)
    i2 = jnp.min(jnp.where(el2 == v2, lane, LANES), axis=-1, keepdims=True)
    e2 = jnp.exp(v2 - v1)
    den = 1.0 + e2
    g1 = p_grp * (1.0 / den)
    g2 = p_grp * (e2 / den)
    eid_ref[...] = jnp.where(lane == 0, i1 - N_GROUPS, jnp.where(lane == 1, i2 - N_GROUPS, 0))
    gate_ref[...] = jnp.where(lane == 0, g1, jnp.where(lane == 1, g2, 0.0))

    @pl.when(pl.program_id(0) == 0)
    def _():
        cnt_ref[...] = jnp.zeros(cnt_ref.shape, F32)

    picked = ((lane == i1 - N_GROUPS) | (lane == i2 - N_GROUPS)).astype(F32)
    cnt_ref[...] += jnp.sum(picked, axis=0, keepdims=True)


def _dispatch_kernel(eid_ref, base_ref, dest_ref, tok_ref, carry, dstage, dsm, sem, *, tb):
    i = pl.program_id(0)

    @pl.when(i == 0)
    def _():
        carry[...] = jnp.zeros(carry.shape, F32)

        def zero(j, c):
            tok_ref[j] = 0
            return c
        lax.fori_loop(0, tok_ref.shape[0], zero, 0, unroll=8)

    e = eid_ref[...]
    lane = lax.broadcasted_iota(jnp.int32, e.shape, 1)
    oh0 = (lane == e[:, 0:1]).astype(F32)
    oh1 = (lane == e[:, 1:2]).astype(F32)
    both = oh0 + oh1
    before = (lax.broadcasted_iota(jnp.int32, (tb, tb), 1) < lax.broadcasted_iota(jnp.int32, (tb, tb), 0)).astype(BF16)
    prior = _dot(before, both.astype(BF16)) + (carry[...] + base_ref[...])
    d0 = jnp.sum(oh0 * prior, axis=1, keepdims=True)
    d1 = jnp.sum(oh1 * (prior + oh0), axis=1, keepdims=True)
    carry[...] += jnp.sum(both, axis=0, keepdims=True)
    dest = jnp.where(lane == 0, d0, jnp.where(lane == 1, d1, 0.0)).astype(jnp.int32)
    dest_ref[...] = dest

    dstage[...] = jnp.transpose(dest)[0:8, :]
    cp = pltpu.make_async_copy(dstage, dsm, sem)
    cp.start()
    cp.wait()

    def body(n, c):
        for k in range(TOPK_IN_GROUP):
            tok_ref[dsm[k, n]] = i * tb + n
        return c

    lax.fori_loop(0, tb, body, 0, unroll=8)


def _dispatch(eid_l, base, n_blk, tb=512):
    N = eid_l.shape[0]
    return pl.pallas_call(
        functools.partial(_dispatch_kernel, tb=tb),
        out_shape=[jax.ShapeDtypeStruct((N, LANES), jnp.int32), jax.ShapeDtypeStruct((n_blk * MOE_BLOCK,), jnp.int32)],
        grid=(N // tb,),
        in_specs=[pl.BlockSpec((tb, LANES), lambda i: (i, 0)), pl.BlockSpec((1, LANES), lambda i: (0, 0))],
        out_specs=[pl.BlockSpec((tb, LANES), lambda i: (i, 0)), pl.BlockSpec(memory_space=pltpu.SMEM)],
        scratch_shapes=[pltpu.VMEM((1, LANES), F32), pltpu.VMEM((8, tb), jnp.int32), pltpu.SMEM((8, tb), jnp.int32),
                        pltpu.SemaphoreType.DMA(())],
        compiler_params=_cparams(("arbitrary",)),
        name="moe_dispatch",
    )(eid_l, base)


def _router(x, g, sc, sh, w_r, b_r, tm=512):
    T, D = x.shape
    vec = pl.BlockSpec((1, D), lambda i: (0, 0))
    return pl.pallas_call(
        _router_kernel,
        out_shape=[jax.ShapeDtypeStruct((T * PACKED_ROWS, LANES), jnp.uint32), jax.ShapeDtypeStruct((T, LANES), jnp.int32),
                   jax.ShapeDtypeStruct((T, LANES), F32), jax.ShapeDtypeStruct((1, LANES), F32)],
        grid=(T // tm,),
        in_specs=[pl.BlockSpec((tm, D), lambda i: (i, 0)), vec, vec, vec,
                  pl.BlockSpec((D, LANES), lambda i: (0, 0)), pl.BlockSpec((1, LANES), lambda i: (0, 0))],
        out_specs=[pl.BlockSpec((tm * PACKED_ROWS, LANES), lambda i: (i, 0)),
                   pl.BlockSpec((tm, LANES), lambda i: (i, 0)),
                   pl.BlockSpec((tm, LANES), lambda i: (i, 0)),
                   pl.BlockSpec((1, LANES), lambda i: (0, 0))],
        compiler_params=_cparams(("arbitrary",)),
        name="moe_router",
    )(x, g, sc, sh, w_r, b_r)


def _cast_rows(src_ref, slot, dst_ref, rows):
    def body(c, carry):
        r = pl.multiple_of(c * rows, rows)
        dst_ref[pl.ds(r, rows), :] = src_ref[slot, pl.ds(r, rows), :].astype(BF16)
        return carry
    lax.fori_loop(0, dst_ref.shape[0] // rows, body, 0, unroll=4)


def _expert_kernel(rid_ref, re_ref, nu_ref, nv_ref, tok_ref, h_ref, wg_hbm, wu_hbm, wd_hbm, y_ref,
                   wgf, wuf, wdf, wgb, wub, wdb, xbuf, wsem, xsem, *, layer):
    i = pl.program_id(0)
    n_used = nu_ref[0]
    R = MOE_BLOCK

    def w_copies(e, slot):
        return ((pltpu.make_async_copy(wg_hbm.at[layer, e], wgf.at[slot], wsem.at[slot, 0]), 1),
                (pltpu.make_async_copy(wu_hbm.at[layer, e], wuf.at[slot], wsem.at[slot, 1]), 1),
                (pltpu.make_async_copy(wd_hbm.at[layer, e], wdf.at[slot], wsem.at[slot, 2]), 1))

    HT = PACKED_ROWS

    GR = 8

    def groups(blk):
        return (nv_ref[blk] + GR - 1) // GR

    def x_start(blk, slot):
        def body(gi, c):
            for u in range(GR):
                r = gi * GR + u
                tok = tok_ref[blk * R + r]
                pltpu.make_async_copy(h_ref.at[pl.ds(pl.multiple_of(tok * HT, HT), HT)],
                                      xbuf.at[slot, pl.ds(pl.multiple_of(r * HT, HT), HT)], xsem.at[slot]).start()
            return c
        lax.fori_loop(0, groups(blk), body, 0)

    def x_wait(blk, slot):
        def body(gi, c):
            pltpu.make_async_copy(h_ref.at[pl.ds(0, GR * HT)], xbuf.at[slot, pl.ds(0, GR * HT)], xsem.at[slot]).wait()
            return c
        lax.fori_loop(0, groups(blk), body, 0)

    def w_request(run):
        @pl.when(re_ref[run] >= 0)
        def _():
            for c, pri in w_copies(re_ref[run], run % 2):
                c.start(priority=pri)

    @pl.when(i == 0)
    def _():
        w_request(0)
        w_request(1)
        xbuf[...] = jnp.zeros(xbuf.shape, xbuf.dtype)
        x_start(0, 0)

    @pl.when(i < n_used)
    def _():
        slot = i % 2

        @pl.when(i + 1 < n_used)
        def _():
            x_start(i + 1, 1 - slot)

        run = rid_ref[i]

        @pl.when((i == 0) | (run != rid_ref[jnp.maximum(i - 1, 0)]))
        def _():
            ws = run % 2
            for c, _ in w_copies(re_ref[run], ws):
                c.wait()
            _cast_rows(wgf, ws, wgb, 64)
            _cast_rows(wuf, ws, wub, 64)
            _cast_rows(wdf, ws, wdb, 16)
            w_request(run + 2)

        x_wait(i, slot)
        words = [xbuf[slot, pl.ds(c, R, stride=HT), :] for c in range(HT)]
        lo = [lax.bitcast_convert_type(w << 16, F32) for w in words]
        hi = [lax.bitcast_convert_type(w & jnp.uint32(0xFFFF0000), F32) for w in words]
        x = jnp.concatenate(lo + hi, axis=1).astype(BF16)
        hg = _dot(x, wgb[...])
        hu = _dot(x, wub[...])
        hid = (hg * jax.nn.sigmoid(hg)) * hu
        y_ref[...] = _dot(hid.astype(BF16), wdb[...])

    @pl.when(i >= n_used)
    def _():
        y_ref[...] = jnp.zeros(y_ref.shape, y_ref.dtype)


def _experts(h, tok, run_id, run_e, n_used, n_valid, w_gate, w_up, w_down, layer):
    D = D_MODEL
    n_pad = tok.shape[0]
    n_blk = n_pad // MOE_BLOCK
    any_spec = pl.BlockSpec(memory_space=pl.ANY)
    return pl.pallas_call(
        functools.partial(_expert_kernel, layer=layer),
        out_shape=jax.ShapeDtypeStruct((n_pad, D), F32),
        grid_spec=pltpu.PrefetchScalarGridSpec(
            num_scalar_prefetch=5,
            grid=(n_blk,),
            in_specs=[any_spec, any_spec, any_spec, any_spec],
            out_specs=pl.BlockSpec((MOE_BLOCK, D), lambda i, *_: (i, 0)),
            scratch_shapes=[pltpu.VMEM((2, D, D_EXPERT), F32), pltpu.VMEM((2, D, D_EXPERT), F32),
                            pltpu.VMEM((2, D_EXPERT, D), F32),
                            pltpu.VMEM((D, D_EXPERT), BF16), pltpu.VMEM((D, D_EXPERT), BF16),
                            pltpu.VMEM((D_EXPERT, D), BF16),
                            pltpu.VMEM((2, MOE_BLOCK * PACKED_ROWS, LANES), jnp.uint32),
                            pltpu.SemaphoreType.DMA((2, 4)), pltpu.SemaphoreType.DMA((2,))]),
        compiler_params=_cparams(("arbitrary",)),
        name="moe_experts",
    )(run_id, run_e, n_used, n_valid, tok, h, w_gate, w_up, w_down)


def _combine_kernel(slot_ref, y_ref, x_ref, w_ref, g_ref, fg_ref, o_ref, buf, sem, *, rows, final_norm):
    i = pl.program_id(0)
    n_steps = pl.num_programs(0)

    def start_block(blk, bs):
        def start(r, c):
            for k in range(TOPK_IN_GROUP):
                pltpu.make_async_copy(y_ref.at[pl.ds(slot_ref[(blk * rows + r) * 2 + k], 1)],
                                      buf.at[bs, k, pl.ds(r, 1)], sem.at[bs]).start()
            return c
        lax.fori_loop(0, rows, start, 0, unroll=8)

    @pl.when(i == 0)
    def _():
        start_block(0, 0)

    bs = i % 2

    @pl.when(i + 1 < n_steps)
    def _():
        start_block(i + 1, 1 - bs)

    for k in range(TOPK_IN_GROUP):
        pltpu.make_async_copy(y_ref.at[pl.ds(0, rows)], buf.at[bs, k], sem.at[bs]).wait()
    w = w_ref[...]
    moe = buf[bs, 0] * w[:, 0:1] + buf[bs, 1] * w[:, 1:2]
    y = x_ref[...] + g_ref[...] * moe
    if final_norm:
        y = (y * lax.rsqrt(jnp.mean(y * y, axis=-1, keepdims=True) + EPS)) * fg_ref[...]
    o_ref[...] = y


def _combine(yb, slots, x, gate_w, g2, final_g, final_norm, rows=256):
    T, D = x.shape
    return pl.pallas_call(
        functools.partial(_combine_kernel, rows=rows, final_norm=final_norm),
        out_shape=jax.ShapeDtypeStruct((T, D), F32),
        grid_spec=pltpu.PrefetchScalarGridSpec(
            num_scalar_prefetch=1,
            grid=(T // rows,),
            in_specs=[pl.BlockSpec(memory_space=pl.ANY),
                      pl.BlockSpec((rows, D), lambda i, s: (i, 0)),
                      pl.BlockSpec((rows, LANES), lambda i, s: (i, 0)),
                      pl.BlockSpec((1, D), lambda i, s: (0, 0)),
                      pl.BlockSpec((1, D), lambda i, s: (0, 0))],
            out_specs=pl.BlockSpec((rows, D), lambda i, s: (i, 0)),
            scratch_shapes=[pltpu.VMEM((2, TOPK_IN_GROUP, rows, D), F32), pltpu.SemaphoreType.DMA((2,))]),
        compiler_params=_cparams(("arbitrary",)),
        name="moe_combine",
    )(slots.reshape(-1), yb, x, gate_w, g2, final_g)


def _hier_moe_residual(x, g, sc, sh, g2, w_grp, b_grp, w_exp, b_exp, w_gate, w_up, w_down, layer, final_g,
                       final_norm):
    N, D = x.shape
    n_route = N_GROUPS + N_EXPERTS
    w_r = jnp.zeros((D, LANES), F32).at[:, :N_GROUPS].set(w_grp).at[:, N_GROUPS:n_route].set(w_exp)
    b_r = jnp.zeros((1, LANES), F32).at[0, :N_GROUPS].set(b_grp).at[0, N_GROUPS:n_route].set(b_exp)
    h, eid_l, gate_l, cnt = _router(x, g, sc, sh, w_r, b_r)

    A = N * TOPK_IN_GROUP
    counts = cnt[0, :N_EXPERTS].astype(jnp.int32)
    padded = (counts + MOE_BLOCK - 1) // MOE_BLOCK * MOE_BLOCK
    pad_end = jnp.cumsum(padded)
    pad_start = pad_end - padded
    n_blk = -(-(A + N_EXPERTS * MOE_BLOCK) // MOE_BLOCK)
    base = jnp.zeros((1, LANES), F32).at[0, :N_EXPERTS].set(pad_start.astype(F32))
    dest_l, tok_tbl = _dispatch(eid_l, base, n_blk)
    buf_tok = tok_tbl
    blk_idx = jnp.arange(n_blk, dtype=jnp.int32)
    n_used = pad_end[-1] // MOE_BLOCK
    blk_e = jnp.minimum(jnp.sum((pad_end[None, :] <= (blk_idx * MOE_BLOCK)[:, None]).astype(jnp.int32), axis=1),
                        N_EXPERTS - 1)
    blk_e = jnp.where(blk_idx < n_used, blk_e, blk_e[n_used - 1])
    change = jnp.concatenate([jnp.ones((1,), bool), blk_e[1:] != blk_e[:-1]])
    run_id = (jnp.cumsum(change.astype(jnp.int32)) - 1).astype(jnp.int32)
    runs = jnp.arange(n_blk + 2, dtype=jnp.int32)
    run_e = jnp.max(jnp.where(run_id[None, :] == runs[:, None], blk_e[None, :], -1), axis=1).astype(jnp.int32)
    slots = dest_l[:, :TOPK_IN_GROUP]

    n_valid = jnp.clip(counts[blk_e] - (blk_idx * MOE_BLOCK - pad_start[blk_e]), 0, MOE_BLOCK).astype(jnp.int32)
    yb = _experts(h, buf_tok, run_id, run_e, n_used.reshape(1).astype(jnp.int32), n_valid, w_gate, w_up, w_down, layer)
    return _combine(yb, slots, x, gate_l, g2, final_g, final_norm)


def _even_mixer_residual(x, g, sc, sh, g1, rel_bias, w_in, w_out, cmp_pe, cmp_w1, cmp_b1, cmp_w2, cmp_b2,
                         conv_w, conv_b, lru_wa, lru_ba, lru_wi, lru_bi, lru_lambda):
    T, D = x.shape
    G, HD = NSA_KV_HEADS, HEAD_DIM
    nq = NSA_HEADS * HD
    n_kv = 6 * NSA_KV_W
    n_gate = 3 * NSA_HEADS
    w_main = jnp.concatenate([w_in[:, :nq], w_in[:, nq + n_kv + n_gate:], w_in[:, nq:nq + n_kv]],
                             axis=1).astype(BF16)
    w_gl = jnp.zeros((D, LANES), F32).at[:, :n_gate].set(w_in[:, nq + n_kv:nq + n_kv + n_gate]).astype(BF16)
    kv0 = nq + 2 * LRU_WIDTH
    tn = 768
    proj, proj_gate, kv_b = _norm_proj(x, g, sc, sh, w_main, w_gl, bf16_from=kv0 // tn, tn=tn)

    def kv_heads(j, src=kv_b, c_base=0):
        c0 = c_base + j * NSA_KV_W
        return src[:, c0:c0 + NSA_KV_W].reshape(T, G, HD).transpose(1, 0, 2)

    cmp_in = jnp.stack([kv_heads(0, proj, kv0), kv_heads(1, proj, kv0)], axis=0)
    cmp_out = _compress(cmp_in, cmp_pe, cmp_w1, cmp_b1, cmp_w2, cmp_b2)
    ncp = T // CMP_STRIDE
    valid = (jnp.arange(ncp) < ncp - 1)[None, None, :, None]
    cmp_pad = jnp.pad(jnp.where(valid, cmp_out, 0.0), ((0, 0), (0, 0), (Q_BLOCK, Q_BLOCK), (0, 0)))
    slc_pad = ((0, 0), (Q_BLOCK, SLC_TAIL), (0, 0))
    blk_of_row = jnp.pad(jnp.arange(T, dtype=jnp.int32) // SLC_LEN, (Q_BLOCK, SLC_TAIL), constant_values=Q_BLOCK - 1)
    blk_onehot = (blk_of_row[:, None] == jnp.arange(Q_BLOCK, dtype=jnp.int32)[None, :]).astype(BF16)
    ks = jnp.concatenate([jnp.pad(kv_heads(2), slc_pad), jnp.broadcast_to(blk_onehot, (G,) + blk_onehot.shape)], axis=2)
    vs = jnp.pad(kv_heads(3), slc_pad)
    kw = jnp.pad(kv_heads(4), ((0, 0), (WINDOW, 0), (0, 0)))
    vw = jnp.pad(kv_heads(5), ((0, 0), (WINDOW, 0), (0, 0)))
    nsa_out = _nsa_attention(rel_bias, proj, proj_gate, cmp_pad[0], cmp_pad[1], ks, vs, kw, vw)

    y_col = nq // LRU_WIDTH
    lru_out = _rglru(proj, y_col, y_col + 1, conv_w, conv_b, lru_wa, lru_ba, lru_wi, lru_bi, lru_lambda)
    w_o = w_out.astype(BF16)
    return _out_proj(nsa_out, lru_out, w_o, x, g1)


def _odd_mixer_residual(x, g, sc, sh, g1, lb_logits, w_in, w_out, norm_g, layer):
    proj = _norm_proj(x, g, sc, sh, w_in.astype(BF16), tn=1024)[0]
    o = _hgrn2(proj, lb_logits, norm_g, layer)
    w_o = w_out.astype(BF16)
    return _out_proj(o, o, w_o, x, g1, a2_col=1)


def kernel(x, c, rel_bias, ada_w, ada_b, norm_mix_g, norm_ffn_g, ev_w_in, ev_w_out, cmp_pe, cmp_w1, cmp_b1, cmp_w2, cmp_b2, lru_conv_w, lru_conv_b, lru_wa, lru_ba, lru_wi, lru_bi, lru_lambda, od_w_in, od_w_out, hg_lb_logits, hg_norm_g, moe_w_grp, moe_b_grp, moe_w_exp, moe_b_exp, moe_w_gate, moe_w_up, moe_w_down, final_g):
    B, T, D = x.shape
    assert B == 1 and D == D_MODEL
    xt = x.reshape(T, D)
    mod = _modulation(c, ada_w, ada_b)
    for l in range(DEPTH):
        sh1, sc1, g1, sh2, sc2, g2 = [mod[l, :, k * D:(k + 1) * D] for k in range(6)]
        gm = norm_mix_g[l].reshape(1, D)
        gf = norm_ffn_g[l].reshape(1, D)
        j = l // 2
        if l % 2 == 0:
            xt = _even_mixer_residual(xt, gm, sc1, sh1, g1, rel_bias, ev_w_in[j], ev_w_out[j], cmp_pe[j],
                                      cmp_w1[j], cmp_b1[j], cmp_w2[j], cmp_b2[j], lru_conv_w[j],
                                      lru_conv_b[j], lru_wa[j], lru_ba[j], lru_wi[j], lru_bi[j], lru_lambda[j])
        else:
            xt = _odd_mixer_residual(xt, gm, sc1, sh1, g1, hg_lb_logits, od_w_in[j], od_w_out[j],
                                     hg_norm_g[j], l)
        xt = _hier_moe_residual(xt, gf, sc2, sh2, g2, moe_w_grp[l], moe_b_grp[l], moe_w_exp[l], moe_b_exp[l],
                                moe_w_gate, moe_w_up, moe_w_down, l, final_g.reshape(1, D), l == DEPTH - 1)
    return xt.reshape(B, T, D)
```

```python
import functools
import math

import numpy as np
import jax
import jax.numpy as jnp
from jax import lax
from jax.experimental import pallas as pl
from jax.experimental.pallas import tpu as pltpu

F32 = jnp.float32
BF16 = jnp.bfloat16

D_MODEL = 2048
DEPTH = 2
NSA_HEADS = 8
NSA_KV_HEADS = 2
NSA_HPG = NSA_HEADS // NSA_KV_HEADS
HEAD_DIM = 128
NSA_KV_W = NSA_KV_HEADS * HEAD_DIM
CMP_LEN = 32
CMP_STRIDE = 16
SLC_LEN = 64
SLC_TOPN = 16
WINDOW = 512
Q_BLOCK = 128
LRU_WIDTH = 1024
LRU_BLOCKS = 8
LRU_BW = LRU_WIDTH // LRU_BLOCKS
CONV_W = 4
LRU_C = 8.0
HG_HEADS = 16
HG_DK = 128
HG_DV = 128
HG_CHUNK = 64
HG_SUB = 16
HG_HEADS_PER_STEP = 16
SLC_GROUP = 4
RANK_UNROLL = 4
RANK_BAND = 32
SLC_TAIL = 2 * SLC_GROUP * Q_BLOCK
HG_KW = HG_HEADS * HG_DK
HG_VW = HG_HEADS * HG_DV
N_BUCKETS = 32
MAX_DIST = 128
N_GROUPS = 8
EXP_PER_GROUP = 8
N_EXPERTS = N_GROUPS * EXP_PER_GROUP
TOPK_IN_GROUP = 2
D_EXPERT = 512
MOE_BLOCK = 128
EPS = 1e-6

LANES = 128
PACKED_ROWS = D_MODEL // 2 // LANES
NEG = -1e30
M_FLOOR = -1e20
LOG2E = math.log2(math.e)
VMEM_LIMIT = 56 * 1024 * 1024


def _cparams(sem):
    return pltpu.CompilerParams(dimension_semantics=sem, vmem_limit_bytes=VMEM_LIMIT)


def _dot(a, b):
    return jnp.dot(a, b, preferred_element_type=F32)


def _dot_nt(a, b):
    return lax.dot_general(a, b, (((1,), (1,)), ((), ())), preferred_element_type=F32)


def _dot_tn(a, b):
    return lax.dot_general(a, b, (((0,), (0,)), ((), ())), preferred_element_type=F32)


def _bucket_starts():
    n = np.arange(0, MAX_DIST + 1, dtype=np.int32)
    max_exact = N_BUCKETS // 2
    nf = np.maximum(n, 1).astype(np.float32)
    large = max_exact + (np.log(nf / np.float32(max_exact)) / np.float32(math.log(MAX_DIST / max_exact))
                         * np.float32(N_BUCKETS - max_exact)).astype(np.int32)
    large = np.minimum(large, N_BUCKETS - 1)
    b = np.where(n < max_exact, n, large)
    starts = [int(np.argmax(b >= k)) for k in range(N_BUCKETS)]
    assert all(b[s] == k for k, s in enumerate(starts)) and b[-1] == N_BUCKETS - 1
    return starts


BUCKET_STARTS = _bucket_starts()


def _mod_kernel(c_ref, w_ref, b_ref, o_ref):
    c = c_ref[...]
    cact = c * jax.nn.sigmoid(c)
    o_ref[...] = jnp.sum(cact * w_ref[...], axis=0, keepdims=True) + b_ref[...]


def _modulation(c, ada_w, ada_b):
    D = D_MODEL
    tn = 1024
    n_out = 6 * D
    c_col = c.reshape(D, 1)
    return pl.pallas_call(
        _mod_kernel,
        out_shape=jax.ShapeDtypeStruct((DEPTH, 1, n_out), F32),
        grid=(DEPTH, n_out // tn),
        in_specs=[pl.BlockSpec((D, 1), lambda l, j: (0, 0)),
                  pl.BlockSpec((None, D, tn), lambda l, j: (l, 0, j)),
                  pl.BlockSpec((None, 1, tn), lambda l, j: (l, 0, j))],
        out_specs=pl.BlockSpec((None, 1, tn), lambda l, j: (l, 0, j)),
        compiler_params=_cparams(("arbitrary", "arbitrary")),
        name="adaln_mod",
    )(c_col, ada_w, ada_b.reshape(DEPTH, 1, n_out))


def _norm_mod(x, g, sc, sh):
    y = x * lax.rsqrt(jnp.mean(x * x, axis=-1, keepdims=True) + EPS)
    return (y * g) * (1.0 + sc) + sh


def _norm_proj_kernel(x_ref, g_ref, sc_ref, sh_ref, w_ref, *rest, has_extra, bf16_from):
    if has_extra:
        wx_ref, o_ref, ox_ref, ob_ref, h_ref = rest
    else:
        o_ref, h_ref = rest
    j = pl.program_id(1)

    @pl.when(j == 0)
    def _():
        h = _norm_mod(x_ref[...], g_ref[...], sc_ref[...], sh_ref[...]).astype(BF16)
        h_ref[...] = h
        if has_extra:
            ox_ref[...] = _dot(h, wx_ref[...])

    y = _dot(h_ref[...], w_ref[...])
    o_ref[...] = y
    if has_extra:
        @pl.when(j >= bf16_from)
        def _():
            ob_ref[...] = y.astype(BF16)


def _norm_proj(x, g, sc, sh, w, w_extra=None, bf16_from=None, tm=1024, tn=512):
    T, D = x.shape
    N = w.shape[1]
    has_extra = w_extra is not None
    vec = pl.BlockSpec((1, D), lambda i, j: (0, 0))
    in_specs = [pl.BlockSpec((tm, D), lambda i, j: (i, 0)), vec, vec, vec,
                pl.BlockSpec((D, tn), lambda i, j: (0, j))]
    out_shape = [jax.ShapeDtypeStruct((T, N), F32)]
    out_specs = [pl.BlockSpec((tm, tn), lambda i, j: (i, j))]
    args = [x, g, sc, sh, w]
    if has_extra:
        nx = w_extra.shape[1]
        in_specs.append(pl.BlockSpec((D, nx), lambda i, j: (0, 0)))
        out_shape.append(jax.ShapeDtypeStruct((T, nx), F32))
        out_specs.append(pl.BlockSpec((tm, nx), lambda i, j: (i, 0)))
        args.append(w_extra)
        out_shape.append(jax.ShapeDtypeStruct((T, N - bf16_from * tn), BF16))
        out_specs.append(pl.BlockSpec((tm, tn), lambda i, j: (i, jnp.maximum(j - bf16_from, 0))))
    return pl.pallas_call(
        functools.partial(_norm_proj_kernel, has_extra=has_extra, bf16_from=bf16_from),
        out_shape=out_shape,
        grid=(T // tm, N // tn),
        in_specs=in_specs,
        out_specs=out_specs,
        scratch_shapes=[pltpu.VMEM((tm, D), BF16)],
        compiler_params=_cparams(("arbitrary", "arbitrary")),
        name="norm_proj",
    )(*args)


def _out_proj_kernel(a1_ref, a2_ref, w1_ref, w2_ref, x_ref, g_ref, o_ref):
    y = _dot(a1_ref[...], w1_ref[...]) + _dot(a2_ref[...], w2_ref[...])
    o_ref[...] = x_ref[...] + g_ref[...] * y


def _out_proj(a1, a2, w, x, gate, a2_col=0, tm=1024, tn=1024):
    T, D = x.shape
    K1 = K2 = w.shape[0] // 2
    return pl.pallas_call(
        _out_proj_kernel,
        out_shape=jax.ShapeDtypeStruct((T, D), F32),
        grid=(T // tm, D // tn),
        in_specs=[pl.BlockSpec((tm, K1), lambda i, j: (i, 0)),
                  pl.BlockSpec((tm, K2), lambda i, j: (i, a2_col)),
                  pl.BlockSpec((K1, tn), lambda i, j: (0, j)),
                  pl.BlockSpec((K2, tn), lambda i, j: (1, j)),
                  pl.BlockSpec((tm, tn), lambda i, j: (i, j)),
                  pl.BlockSpec((1, tn), lambda i, j: (0, j))],
        out_specs=pl.BlockSpec((tm, tn), lambda i, j: (i, j)),
        compiler_params=_cparams(("arbitrary", "arbitrary")),
        name="out_proj",
    )(a1, a2, w, w, x, gate)


def _compress_kernel(k2_ref, pe_ref, w1_ref, b1_ref, w2_ref, b2_ref, o_ref):
    half = (CMP_LEN // 2) * HEAD_DIM
    k2 = k2_ref[...]
    a = _dot((k2 + pe_ref[0:1, :]).astype(BF16), w1_ref[0:half, :].astype(BF16))
    b = _dot((k2 + pe_ref[1:2, :]).astype(BF16), w1_ref[half:2 * half, :].astype(BF16))
    nb = k2.shape[0]
    b_up = pltpu.roll(b, nb - 1, axis=0)
    hid = jax.nn.gelu(a + b_up + b1_ref[...])
    o_ref[...] = _dot(hid.astype(BF16), w2_ref[...].astype(BF16)) + b2_ref[...]


def _compress(kv, pe, w1, b1, w2, b2):
    _, G, T, HD = kv.shape
    nb = T // CMP_STRIDE
    row = CMP_STRIDE * HD
    kv2 = kv.reshape(2, G, nb, row)
    pe2 = pe.reshape(2, 2, row)
    return pl.pallas_call(
        _compress_kernel,
        out_shape=jax.ShapeDtypeStruct((2, G, nb, HD), F32),
        grid=(2, G),
        in_specs=[pl.BlockSpec((None, None, nb, row), lambda a, g: (a, g, 0, 0)),
                  pl.BlockSpec((None, 2, row), lambda a, g: (a, 0, 0)),
                  pl.BlockSpec((None, CMP_LEN * HD, HD), lambda a, g: (a, 0, 0)),
                  pl.BlockSpec((None, 1, HD), lambda a, g: (a, 0, 0)),
                  pl.BlockSpec((None, HD, HD), lambda a, g: (a, 0, 0)),
                  pl.BlockSpec((None, 1, HD), lambda a, g: (a, 0, 0))],
        out_specs=pl.BlockSpec((None, None, nb, HD), lambda a, g: (a, g, 0, 0)),
        compiler_params=_cparams(("arbitrary", "arbitrary")),
        name="nsa_compress",
    )(kv2, pe2, w1, b1.reshape(2, 1, HD), w2, b2.reshape(2, 1, HD))


def _bias_from_rel(rel, tbl_ref, head):
    far = tbl_ref[N_BUCKETS - 1, head]
    val = jnp.zeros(rel.shape, F32)
    for b in range(N_BUCKETS - 2, -1, -1):
        val = jnp.where(rel < BUCKET_STARTS[b + 1], (tbl_ref[b, head] - far) * LOG2E, val)
    return val


def _lane_tile4(x):
    return jnp.concatenate([x, x, x, x], axis=1)


def _col_softmax_stats(s):
    m = jnp.maximum(jnp.max(s, axis=0, keepdims=True), M_FLOOR)
    e = jnp.exp2(s - m)
    d = jnp.sum(e, axis=0, keepdims=True)
    return e, d


def _nsa_kernel(tbl_ref, q_ref, gl_ref, kc_ref, vc_ref, ks_ref, vs_ref, kw_ref, vw_ref, o_ref,
                bd_ref, bp_ref, bn_ref, imp_ref, cnt_ref, *, ncp):
    g = pl.program_id(0)
    ci = pl.program_id(1)
    Q = Q_BLOCK
    HD = HEAD_DIM
    H4 = NSA_HPG

    row_i = lax.broadcasted_iota(jnp.int32, (Q, Q), 0)
    lane_i = lax.broadcasted_iota(jnp.int32, (Q, Q), 1)

    @pl.when((g == 0) & (ci == 0))
    def _build_bias_tiles():
        for h in range(NSA_HEADS):
            rel_d = lane_i - row_i
            bd_ref[h] = jnp.where(rel_d >= 0, _bias_from_rel(jnp.maximum(rel_d, 0), tbl_ref, h), NEG)
            bp_ref[h] = _bias_from_rel(lane_i - row_i + Q, tbl_ref, h)
            rel_n = lane_i - CMP_STRIDE * row_i + (CMP_STRIDE * (Q - 8) - (CMP_LEN - 1))
            bn_ref[h] = jnp.where(rel_n >= 0, _bias_from_rel(jnp.maximum(rel_n, 0), tbl_ref, h), NEG)

    def head_tiles(ref):
        return jnp.concatenate([ref[g * H4 + h] for h in range(H4)], axis=1)

    qt = q_ref[...] * (HD ** -0.5 * LOG2E)
    qs = jnp.concatenate([qt[:, h * HD:(h + 1) * HD] for h in range(H4)], axis=0).astype(BF16)

    near0 = 8 * ci - (Q - 8)
    kc_far = kc_ref[pl.ds(Q, ncp), :]
    vc_far = vc_ref[pl.ds(Q, ncp), :]
    near_row = pl.multiple_of(8 * ci + 8, 8)
    kc_near = kc_ref[pl.ds(near_row, Q), :]
    vc_near = vc_ref[pl.ds(near_row, Q), :]
    n_far = lax.broadcasted_iota(jnp.int32, (ncp, 1), 0)
    s_far = jnp.where(n_far < near0, _dot_nt(kc_far.astype(BF16), qs), NEG)
    m_near = lax.broadcasted_iota(jnp.int32, (Q, 1), 0)
    s_near = _dot_nt(kc_near.astype(BF16), qs) + head_tiles(bn_ref)
    s_near = jnp.where(m_near + near0 >= 0, s_near, NEG)
    s_c = jnp.concatenate([s_far, s_near], axis=0)
    e_c, d_c = _col_softmax_stats(s_c)
    p_c = e_c * (1.0 / jnp.where(d_c > 0, d_c, 1.0))
    v_c = jnp.concatenate([vc_far, vc_near], axis=0).astype(BF16)
    o_cmp = _dot_tn(v_c, p_c.astype(BF16))

    psum = p_c[:, 0:Q] + p_c[:, Q:2 * Q] + p_c[:, 2 * Q:3 * Q] + p_c[:, 3 * Q:4 * Q]
    p_hi = psum.astype(BF16)
    p_lo = (psum - p_hi.astype(F32)).astype(BF16)
    jb = lax.broadcasted_iota(jnp.int32, (Q, ncp), 0)
    nb = lax.broadcasted_iota(jnp.int32, (Q, ncp), 1)
    ratio = SLC_LEN // CMP_STRIDE
    span = CMP_LEN // CMP_STRIDE - 1
    cover_far = ((nb >= ratio * jb - span) & (nb <= ratio * jb + ratio - 1)).astype(BF16)
    nn = lane_i + near0
    cover_near = ((nn >= ratio * row_i - span) & (nn <= ratio * row_i + ratio - 1)).astype(BF16)
    cover = jnp.concatenate([cover_far, cover_near], axis=1)
    imp = _dot(cover, p_hi) + _dot(cover, p_lo)
    cur = 2 * ci + (lane_i >= SLC_LEN).astype(jnp.int32)
    imp = jnp.where(row_i == cur, jnp.inf, jnp.where(row_i > cur, -jnp.inf, imp))
    imp_ref[...] = imp

    n_cand = (2 * ci + 2 + RANK_UNROLL - 1) // RANK_UNROLL
    cnt_ref[...] = jnp.zeros((Q, Q), F32)
    for band in range(Q // RANK_BAND):
        lo_row = band * RANK_BAND

        @pl.when(lo_row <= 2 * ci + 1)
        def _(lo_row=lo_row):
            imp_b = imp[lo_row:lo_row + RANK_BAND]
            row_b = row_i[lo_row:lo_row + RANK_BAND]

            def rank_body(it, cnt):
                for u in range(RANK_UNROLL):
                    b = it * RANK_UNROLL + u
                    r = imp_ref[pl.ds(b, 1), :]
                    ahead = (r > imp_b) | ((r == imp_b) & (b < row_b))
                    cnt = cnt + ahead.astype(F32)
                return cnt

            cnt_ref[lo_row:lo_row + RANK_BAND, :] = lax.fori_loop(0, n_cand, rank_body,
                                                                   jnp.zeros((RANK_BAND, Q), F32))

    sel = ((cnt_ref[...] < SLC_TOPN) & (row_i <= cur)).astype(F32)

    sel_q = jnp.transpose(sel)
    near_blk = lane_i >= 2 * ci - 2
    m_far = jnp.where((sel_q > 0) & jnp.logical_not(near_blk), 0.0, NEG).astype(BF16)
    m_near = jnp.where((sel_q > 0) & near_blk, 0.0, NEG).astype(BF16)
    qa_far = jnp.concatenate([qs, jnp.concatenate([m_far] * H4, axis=0)], axis=1)
    qa_near = jnp.concatenate([qs, jnp.concatenate([m_near] * H4, axis=0)], axis=1)

    def slc_rows(kt0, nt):
        r0 = pl.multiple_of((kt0 + 1) * Q, Q)
        return ks_ref[pl.ds(r0, nt * Q), :], vs_ref[pl.ds(r0, nt * Q), :]

    def online_update(streams):
        m_new = [jnp.maximum(c[0], jnp.max(s, axis=0, keepdims=True)) for s, _, c in streams]
        alpha = [jnp.exp2(c[0] - mn) for (_, _, c), mn in zip(streams, m_new)]
        p = [jnp.exp2(s - mn) for (s, _, _), mn in zip(streams, m_new)]
        l_new = [a * c[1] + jnp.sum(pp, axis=0, keepdims=True) for a, pp, (_, _, c) in zip(alpha, p, streams)]
        pv = [_dot_tn(v, pp.astype(BF16)) for pp, (_, v, _) in zip(p, streams)]
        return [(mn, ln, a * c[2] + x) for mn, ln, a, x, (_, _, c) in zip(m_new, l_new, alpha, pv, streams)]

    carry0 = (jnp.full((1, H4 * Q), M_FLOOR, F32), jnp.zeros((1, H4 * Q), F32), jnp.zeros((HD, H4 * Q), F32))
    n_far = jnp.maximum(ci - 1, 0)

    def far_body(it, carries):
        kt0 = it * (2 * SLC_GROUP)
        k_a, v_a = slc_rows(kt0, SLC_GROUP)
        k_b, v_b = slc_rows(kt0 + SLC_GROUP, SLC_GROUP)
        s_a = _dot_nt(k_a, qa_far)
        s_b = _dot_nt(k_b, qa_far)
        return tuple(online_update([(s_a, v_a, carries[0]), (s_b, v_b, carries[1])]))

    (m_a, l_a, acc_a), (m_b, l_b, acc_b) = lax.fori_loop(
        0, (n_far + 2 * SLC_GROUP - 1) // (2 * SLC_GROUP), far_body, (carry0, carry0))
    m_ab = jnp.maximum(m_a, m_b)
    w_a = jnp.exp2(m_a - m_ab)
    w_b = jnp.exp2(m_b - m_ab)
    carry = (m_ab, w_a * l_a + w_b * l_b, w_a * acc_a + w_b * acc_b)
    bp_t = head_tiles(bp_ref)
    bd_t = head_tiles(bd_ref)
    k_n, v_n = slc_rows(ci - 1, 2)
    s_n = _dot_nt(k_n, qa_near) + jnp.concatenate([bp_t, bd_t], axis=0)
    (_, l_s, acc_s), = online_update([(s_n, v_n, carry)])
    o_slc = acc_s * (1.0 / jnp.where(l_s > 0, l_s, 1.0))

    kw = kw_ref[pl.ds(pl.multiple_of(ci * Q, Q), WINDOW + Q), :]
    vw = vw_ref[pl.ds(pl.multiple_of(ci * Q, Q), WINDOW + Q), :]
    s_w = _dot_nt(kw, qs)
    n_wt = WINDOW // Q
    in_window = lane_i < row_i
    pieces = [jnp.concatenate([jnp.where(in_window, s_w[0:Q, h * Q:(h + 1) * Q], NEG) for h in range(H4)], axis=1)]
    for t in range(1, n_wt - 1):
        pieces.append(s_w[t * Q:(t + 1) * Q])
    pieces.append(s_w[(n_wt - 1) * Q:n_wt * Q] + bp_t)
    pieces.append(s_w[n_wt * Q:(n_wt + 1) * Q] + bd_t)
    s_w = jnp.concatenate(pieces, axis=0)
    x_w = lax.broadcasted_iota(jnp.int32, (WINDOW + Q, 1), 0)
    s_w = jnp.where(x_w + ci * Q >= WINDOW, s_w, NEG)
    e_w, d_w = _col_softmax_stats(s_w)
    p_w = e_w * (1.0 / jnp.where(d_w > 0, d_w, 1.0))
    o_win = _dot_tn(vw, p_w.astype(BF16))

    gt = jnp.transpose(jax.nn.sigmoid(gl_ref[...]))
    outs = []
    for h in range(H4):
        base = (g * H4 + h) * 3
        sl = slice(h * Q, (h + 1) * Q)
        gc = gt_row(gt, base)
        gs = gt_row(gt, base + 1)
        gw = gt_row(gt, base + 2)
        o_h = gc * o_cmp[:, sl] + gs * o_slc[:, sl] + gw * o_win[:, sl]
        outs.append(jnp.transpose(o_h))
    o_ref[...] = jnp.concatenate(outs, axis=1).astype(o_ref.dtype)


def gt_row(gt, idx):
    rows = lax.broadcasted_iota(jnp.int32, gt.shape, 0)
    return jnp.sum(jnp.where(rows == idx, gt, 0.0), axis=0, keepdims=True)


def _nsa_attention(rel_bias, proj, proj_gate, kcmp, vcmp, ks, vs, kw, vw):
    T = proj.shape[0]
    G = NSA_KV_HEADS
    Q = Q_BLOCK
    ncp = T // CMP_STRIDE
    nch = T // Q
    assert nch % SLC_GROUP == 0 and T // SLC_LEN <= Q
    kernel = functools.partial(_nsa_kernel, ncp=ncp)
    full = lambda rows, w=HEAD_DIM: pl.BlockSpec((None, rows, w), lambda g, c: (g, 0, 0))
    return pl.pallas_call(
        kernel,
        out_shape=jax.ShapeDtypeStruct((T, NSA_HEADS * HEAD_DIM), BF16),
        grid=(G, nch),
        in_specs=[pl.BlockSpec(memory_space=pltpu.SMEM),
                  pl.BlockSpec((Q, NSA_HPG * HEAD_DIM), lambda g, c: (c, g)),
                  pl.BlockSpec((Q, LANES), lambda g, c: (c, 0)),
                  full(ncp + 2 * Q), full(ncp + 2 * Q),
                  full(T + Q + SLC_TAIL, HEAD_DIM + Q), full(T + Q + SLC_TAIL), full(T + WINDOW), full(T + WINDOW)],
        out_specs=pl.BlockSpec((Q, NSA_HPG * HEAD_DIM), lambda g, c: (c, g)),
        scratch_shapes=[pltpu.VMEM((NSA_HEADS, Q, Q), F32), pltpu.VMEM((NSA_HEADS, Q, Q), F32),
                        pltpu.VMEM((NSA_HEADS, Q, Q), F32), pltpu.VMEM((Q, Q), F32), pltpu.VMEM((Q, Q), F32)],
        compiler_params=_cparams(("arbitrary", "arbitrary")),
        name="nsa_attention",
    )(rel_bias, proj, proj_gate, kcmp, vcmp, ks, vs, kw, vw)


def _softplus(z):
    return jnp.maximum(z, 0.0) + jnp.log1p(jnp.exp(-jnp.abs(z)))


def _lru_kernel(x_ref, y_ref, cw_ref, cb_ref, wa_ref, ba_ref, wi_ref, bi_ref, lam_ref, o_ref,
                xbuf, hc, a_s, b_s, h_s, *, tb):
    i = pl.program_id(0)

    @pl.when(i == 0)
    def _():
        xbuf[...] = jnp.zeros((8, LRU_WIDTH), F32)
        hc[...] = jnp.zeros((1, LRU_WIDTH), F32)

    x = x_ref[...]
    prev = xbuf[...]
    rows8 = lax.broadcasted_iota(jnp.int32, (8, LRU_WIDTH), 0)
    xc = cb_ref[...]
    for w in range(CONV_W):
        sh = CONV_W - 1 - w
        if sh == 0:
            xs = x
        else:
            xr = pltpu.roll(x, sh, axis=0)
            head = jnp.where(rows8 < sh, pltpu.roll(prev, sh, axis=0), xr[0:8])
            xs = jnp.concatenate([head, xr[8:]], axis=0)
        xc = xc + cw_ref[w:w + 1, :] * xs
    xbuf[...] = x[tb - 8:tb]

    xcb = xc.astype(BF16)
    ra, ia = [], []
    for n in range(LRU_BLOCKS):
        xg = xcb[:, n * LRU_BW:(n + 1) * LRU_BW]
        ra.append(_dot(xg, wa_ref[n].astype(BF16)))
        ia.append(_dot(xg, wi_ref[n].astype(BF16)))
    r_gate = jax.nn.sigmoid(jnp.concatenate(ra, axis=1) + ba_ref[...])
    i_gate = jax.nn.sigmoid(jnp.concatenate(ia, axis=1) + bi_ref[...])
    log_a = (-LRU_C * r_gate) * _softplus(-lam_ref[...])
    a = jnp.exp(log_a)
    a_s[...] = a
    b_s[...] = jnp.sqrt(-jnp.tanh(log_a) * (a * a + 1.0)) * (i_gate * xc)

    rows = lax.broadcasted_iota(jnp.int32, (8, LRU_WIDTH), 0)

    def tile(k, h):
        r0 = pl.multiple_of(k * 8, 8)
        A = a_s[pl.ds(r0, 8), :]
        B = b_s[pl.ds(r0, 8), :]
        for sh in (1, 2, 4):
            ok = rows >= sh
            A_p = pltpu.roll(A, sh, axis=0)
            B_p = pltpu.roll(B, sh, axis=0)
            B = jnp.where(ok, A * B_p + B, B)
            A = jnp.where(ok, A * A_p, A)
        H = A * h + B
        h_s[pl.ds(r0, 8), :] = H
        return H[7:8, :]

    hc[...] = lax.fori_loop(0, tb // 8, tile, hc[...])
    o_ref[...] = (jax.nn.gelu(y_ref[...]) * h_s[...]).astype(o_ref.dtype)


def _rglru(proj, y_col, x_col, conv_w, conv_b, wa, ba, wi, bi, lam, tb=256):
    T = proj.shape[0]
    W = LRU_WIDTH
    vec = pl.BlockSpec((1, W), lambda i: (0, 0))
    blk = pl.BlockSpec((LRU_BLOCKS, LRU_BW, LRU_BW), lambda i: (0, 0, 0))
    return pl.pallas_call(
        functools.partial(_lru_kernel, tb=tb),
        out_shape=jax.ShapeDtypeStruct((T, W), BF16),
        grid=(T // tb,),
        in_specs=[pl.BlockSpec((tb, W), lambda i: (i, x_col)),
                  pl.BlockSpec((tb, W), lambda i: (i, y_col)),
                  pl.BlockSpec((CONV_W, W), lambda i: (0, 0)), vec, blk, vec, blk, vec, vec],
        out_specs=pl.BlockSpec((tb, W), lambda i: (i, 0)),
        scratch_shapes=[pltpu.VMEM((8, W), F32), pltpu.VMEM((1, W), F32),
                        pltpu.VMEM((tb, W), F32), pltpu.VMEM((tb, W), F32), pltpu.VMEM((tb, W), F32)],
        compiler_params=_cparams(("arbitrary",)),
        name="rglru",
    )(proj, proj, conv_w, conv_b.reshape(1, W), wa, ba.reshape(1, W), wi, bi.reshape(1, W), lam.reshape(1, W))


def _split3(x):
    hi = x.astype(BF16)
    r = x - hi.astype(F32)
    mid = r.astype(BF16)
    lo = (r - mid.astype(F32)).astype(BF16)
    return hi, mid, lo


def _hgrn_kernel(q_ref, f_ref, v_ref, g_ref, lbl_ref, ng_ref, o_ref, st_ref, oacc, *, tbh, layer):
    t = pl.program_id(1)
    C = HG_CHUNK
    SB = HG_SUB

    NH = HG_HEADS_PER_STEP
    DK, DV = HG_DK, HG_DV

    @pl.when(t == 0)
    def _():
        st_ref[...] = jnp.zeros((NH, DV, DK), F32)

    lg = lbl_ref[...]
    e = jnp.exp(lg - jnp.max(lg, axis=0, keepdims=True))
    sm = e / jnp.sum(e, axis=0, keepdims=True)
    cum = sm[0:1, :]
    for l in range(1, layer + 1):
        cum = cum + sm[l:l + 1, :]
    lb_all = cum - sm[0:1, :]

    tri = (lax.broadcasted_iota(jnp.int32, (C, C), 0) >= lax.broadcasted_iota(jnp.int32, (C, C), 1)).astype(BF16)
    rows8 = lax.broadcasted_iota(jnp.int32, (8, DK), 0)

    def head_chunk(r0, hh):
        cs = slice(hh * DK, (hh + 1) * DK)
        lb = lb_all[:, cs]
        qz = q_ref[pl.ds(r0, C), cs]
        qq = qz * jax.nn.sigmoid(qz)
        f = lb + (1.0 - lb) * jax.nn.sigmoid(f_ref[pl.ds(r0, C), cs])
        logf = jnp.log2(jnp.maximum(f, 1e-30))
        kk = 1.0 - f
        vv = v_ref[pl.ds(r0, C), cs]
        l_hi, l_mid, l_lo = _split3(logf)
        bcum = _dot(tri, l_hi) + _dot(tri, l_mid) + _dot(tri, l_lo)
        st = st_ref[hh]
        o_parts = []
        vb = vv.astype(BF16)
        for I in range(C // SB):
            lo_r = I * SB
            for a in range(SB // 8):
                t0 = lo_r + 8 * a
                q8 = qq[t0:t0 + 8]
                b8 = bcum[t0:t0 + 8]
                o8 = jnp.zeros((8, DV), F32)
                for s in range(lo_r, t0 + 8):
                    d = b8 - bcum[s:s + 1]
                    if s > t0:
                        d = jnp.where(rows8 >= s - t0, d, NEG)
                    w = q8 * (kk[s:s + 1] * jnp.exp2(d))
                    o8 = o8 + jnp.sum(w, axis=1, keepdims=True) * vv[s:s + 1]
                o_parts.append(o8)
        for I in range(1, C // SB):
            lo_r = I * SB
            ref = bcum[lo_r - 1:lo_r]
            qe = (qq[lo_r:lo_r + SB] * jnp.exp2(bcum[lo_r:lo_r + SB] - ref)).astype(BF16)
            ke = (kk[0:lo_r] * jnp.exp2(ref - bcum[0:lo_r])).astype(BF16)
            att = _dot_nt(qe, ke)
            o_off = _dot(att.astype(BF16), vb[0:lo_r])
            for a in range(SB // 8):
                o_parts[lo_r // 8 + a] = o_parts[lo_r // 8 + a] + o_off[8 * a:8 * a + 8]
        o_intra = jnp.concatenate(o_parts, axis=0)
        o_inter = _dot_nt((qq * jnp.exp2(bcum)).astype(BF16), st.astype(BF16))
        oacc[pl.ds(r0, C), cs] = o_inter + o_intra
        b_last = bcum[C - 1:C]
        kd = (kk * jnp.exp2(b_last - bcum)).astype(BF16)
        st_ref[hh] = st * jnp.exp2(b_last) + _dot_tn(vb, kd)

    def chunk(c, carry):
        r0 = pl.multiple_of(c * C, C)
        for hh in range(NH):
            head_chunk(r0, hh)
        return carry

    lax.fori_loop(0, tbh // C, chunk, 0)
    gz = g_ref[...]
    gate = (gz * jax.nn.sigmoid(gz)) * ng_ref[...]
    for hh in range(NH):
        cs = slice(hh * DV, (hh + 1) * DV)
        o = oacc[:, cs]
        o = o * lax.rsqrt(jnp.mean(o * o, axis=-1, keepdims=True) + EPS)
        o_ref[:, cs] = (o * gate[:, cs]).astype(o_ref.dtype)


def _hgrn2(proj, lb_logits, norm_g, layer, tbh=512):
    T = proj.shape[0]
    NH = HG_HEADS_PER_STEP
    HB = HG_HEADS // NH
    wk = NH * HG_DK
    col = lambda off: pl.BlockSpec((tbh, wk), lambda h, t: (t, off + h))
    return pl.pallas_call(
        functools.partial(_hgrn_kernel, tbh=tbh, layer=layer),
        out_shape=jax.ShapeDtypeStruct((T, HG_VW), BF16),
        grid=(HB, T // tbh),
        in_specs=[col(0), col(HB), col(2 * HB), col(3 * HB),
                  pl.BlockSpec((DEPTH, wk), lambda h, t: (0, h)),
                  pl.BlockSpec((1, wk), lambda h, t: (0, h))],
        out_specs=pl.BlockSpec((tbh, wk), lambda h, t: (t, h)),
        scratch_shapes=[pltpu.VMEM((NH, HG_DV, HG_DK), F32), pltpu.VMEM((tbh, wk), F32)],
        compiler_params=_cparams(("arbitrary", "arbitrary")),
        name="hgrn2",
    )(proj, proj, proj, proj, lb_logits, norm_g.reshape(1, HG_VW))


def _router_kernel(x_ref, g_ref, sc_ref, sh_ref, w_ref, b_ref, h_ref, eid_ref, gate_ref, cnt_ref):
    h = _norm_mod(x_ref[...], g_ref[...], sc_ref[...], sh_ref[...])
    tm, d = h.shape
    hb = h.astype(BF16)
    lo = lax.bitcast_convert_type(hb[:, :d // 2].astype(F32), jnp.uint32) >> 16
    hi = lax.bitcast_convert_type(hb[:, d // 2:].astype(F32), jnp.uint32)
    packed = hi | lo
    pt = d // 2 // LANES
    for c in range(pt):
        h_ref[pl.ds(c, tm, stride=pt), :] = packed[:, c * LANES:(c + 1) * LANES]
    h1, h2, h3 = _split3(h)
    w1, w2, w3 = _split3(w_ref[...])
    a = _dot(h1, jnp.concatenate([w1, w2, w3], axis=1))
    b = _dot(h2, jnp.concatenate([w1, w2], axis=1))
    c = _dot(h3, w1)
    L = LANES
    logits = (a[:, 0:L] + (a[:, L:2 * L] + b[:, 0:L])
              + (b[:, L:2 * L] + a[:, 2 * L:3 * L] + c)) + b_ref[...]
    lane = lax.broadcasted_iota(jnp.int32, logits.shape, 1)
    is_g = lane < N_GROUPS
    glog = jnp.where(is_g, logits, -jnp.inf)
    gmax = jnp.max(glog, axis=-1, keepdims=True)
    grp = jnp.min(jnp.where(glog == gmax, lane, LANES), axis=-1, keepdims=True)
    gsum = jnp.sum(jnp.where(is_g, jnp.exp(glog - gmax), 0.0), axis=-1, keepdims=True)
    p_grp = 1.0 / gsum
    lo = N_GROUPS + EXP_PER_GROUP * grp
    el = jnp.where((lane >= lo) & (lane < lo + EXP_PER_GROUP), logits, -jnp.inf)
    v1 = jnp.max(el, axis=-1, keepdims=True)
    i1 = jnp.min(jnp.where(el == v1, lane, LANES), axis=-1, keepdims=True)
    el2 = jnp.where(lane == i1, -jnp.inf, el)
    v2 = jnp.max(el2, axis=-1, keepdims=True)
    i2 = jnp.min(jnp.where(el2 == v2, lane, LANES), axis=-1, keepdims=True)
    e2 = jnp.exp(v2 - v1)
    den = 1.0 + e2
    g1 = p_grp * (1.0 / den)
    g2 = p_grp * (e2 / den)
    eid_ref[...] = jnp.where(lane == 0, i1 - N_GROUPS, jnp.where(lane == 1, i2 - N_GROUPS, 0))
    gate_ref[...] = jnp.where(lane == 0, g1, jnp.where(lane == 1, g2, 0.0))

    @pl.when(pl.program_id(0) == 0)
    def _():
        cnt_ref[...] = jnp.zeros(cnt_ref.shape, F32)

    picked = ((lane == i1 - N_GROUPS) | (lane == i2 - N_GROUPS)).astype(F32)
    cnt_ref[...] += jnp.sum(picked, axis=0, keepdims=True)


def _dispatch_kernel(eid_ref, base_ref, dest_ref, tok_ref, carry, dstage, dsm, sem, *, tb):
    i = pl.program_id(0)

    @pl.when(i == 0)
    def _():
        carry[...] = jnp.zeros(carry.shape, F32)

        def zero(j, c):
            tok_ref[j] = 0
            return c
        lax.fori_loop(0, tok_ref.shape[0], zero, 0, unroll=8)

    e = eid_ref[...]
    lane = lax.broadcasted_iota(jnp.int32, e.shape, 1)
    oh0 = (lane == e[:, 0:1]).astype(F32)
    oh1 = (lane == e[:, 1:2]).astype(F32)
    both = oh0 + oh1
    before = (lax.broadcasted_iota(jnp.int32, (tb, tb), 1) < lax.broadcasted_iota(jnp.int32, (tb, tb), 0)).astype(BF16)
    prior = _dot(before, both.astype(BF16)) + (carry[...] + base_ref[...])
    d0 = jnp.sum(oh0 * prior, axis=1, keepdims=True)
    d1 = jnp.sum(oh1 * (prior + oh0), axis=1, keepdims=True)
    carry[...] += jnp.sum(both, axis=0, keepdims=True)
    dest = jnp.where(lane == 0, d0, jnp.where(lane == 1, d1, 0.0)).astype(jnp.int32)
    dest_ref[...] = dest

    dstage[...] = jnp.transpose(dest)[0:8, :]
    cp = pltpu.make_async_copy(dstage, dsm, sem)
    cp.start()
    cp.wait()

    def body(n, c):
        for k in range(TOPK_IN_GROUP):
            tok_ref[dsm[k, n]] = i * tb + n
        return c

    lax.fori_loop(0, tb, body, 0, unroll=8)


def _dispatch(eid_l, base, n_blk, tb=512):
    N = eid_l.shape[0]
    return pl.pallas_call(
        functools.partial(_dispatch_kernel, tb=tb),
        out_shape=[jax.ShapeDtypeStruct((N, LANES), jnp.int32), jax.ShapeDtypeStruct((n_blk * MOE_BLOCK,), jnp.int32)],
        grid=(N // tb,),
        in_specs=[pl.BlockSpec((tb, LANES), lambda i: (i, 0)), pl.BlockSpec((1, LANES), lambda i: (0, 0))],
        out_specs=[pl.BlockSpec((tb, LANES), lambda i: (i, 0)), pl.BlockSpec(memory_space=pltpu.SMEM)],
        scratch_shapes=[pltpu.VMEM((1, LANES), F32), pltpu.VMEM((8, tb), jnp.int32), pltpu.SMEM((8, tb), jnp.int32),
                        pltpu.SemaphoreType.DMA(())],
        compiler_params=_cparams(("arbitrary",)),
        name="moe_dispatch",
    )(eid_l, base)


def _router(x, g, sc, sh, w_r, b_r, tm=512):
    T, D = x.shape
    vec = pl.BlockSpec((1, D), lambda i: (0, 0))
    return pl.pallas_call(
        _router_kernel,
        out_shape=[jax.ShapeDtypeStruct((T * PACKED_ROWS, LANES), jnp.uint32), jax.ShapeDtypeStruct((T, LANES), jnp.int32),
                   jax.ShapeDtypeStruct((T, LANES), F32), jax.ShapeDtypeStruct((1, LANES), F32)],
        grid=(T // tm,),
        in_specs=[pl.BlockSpec((tm, D), lambda i: (i, 0)), vec, vec, vec,
                  pl.BlockSpec((D, LANES), lambda i: (0, 0)), pl.BlockSpec((1, LANES), lambda i: (0, 0))],
        out_specs=[pl.BlockSpec((tm * PACKED_ROWS, LANES), lambda i: (i, 0)),
                   pl.BlockSpec((tm, LANES), lambda i: (i, 0)),
                   pl.BlockSpec((tm, LANES), lambda i: (i, 0)),
                   pl.BlockSpec((1, LANES), lambda i: (0, 0))],
        compiler_params=_cparams(("arbitrary",)),
        name="moe_router",
    )(x, g, sc, sh, w_r, b_r)


def _cast_rows(src_ref, slot, dst_ref, rows):
    def body(c, carry):
        r = pl.multiple_of(c * rows, rows)
        dst_ref[pl.ds(r, rows), :] = src_ref[slot, pl.ds(r, rows), :].astype(BF16)
        return carry
    lax.fori_loop(0, dst_ref.shape[0] // rows, body, 0, unroll=4)


def _expert_kernel(rid_ref, re_ref, nu_ref, nv_ref, tok_ref, h_ref, wg_hbm, wu_hbm, wd_hbm, y_ref,
                   wgf, wuf, wdf, wgb, wub, wdb, xbuf, wsem, xsem, *, layer):
    i = pl.program_id(0)
    n_used = nu_ref[0]
    R = MOE_BLOCK

    def w_copies(e, slot):
        return ((pltpu.make_async_copy(wg_hbm.at[layer, e], wgf.at[slot], wsem.at[slot, 0]), 1),
                (pltpu.make_async_copy(wu_hbm.at[layer, e], wuf.at[slot], wsem.at[slot, 1]), 1),
                (pltpu.make_async_copy(wd_hbm.at[layer, e], wdf.at[slot], wsem.at[slot, 2]), 1))

    HT = PACKED_ROWS

    GR = 8

    def groups(blk):
        return (nv_ref[blk] + GR - 1) // GR

    def x_start(blk, slot):
        def body(gi, c):
            for u in range(GR):
                r = gi * GR + u
                tok = tok_ref[blk * R + r]
                pltpu.make_async_copy(h_ref.at[pl.ds(pl.multiple_of(tok * HT, HT), HT)],
                                      xbuf.at[slot, pl.ds(pl.multiple_of(r * HT, HT), HT)], xsem.at[slot]).start()
            return c
        lax.fori_loop(0, groups(blk), body, 0)

    def x_wait(blk, slot):
        def body(gi, c):
            pltpu.make_async_copy(h_ref.at[pl.ds(0, GR * HT)], xbuf.at[slot, pl.ds(0, GR * HT)], xsem.at[slot]).wait()
            return c
        lax.fori_loop(0, groups(blk), body, 0)

    def w_request(run):
        @pl.when(re_ref[run] >= 0)
        def _():
            for c, pri in w_copies(re_ref[run], run % 2):
                c.start(priority=pri)

    @pl.when(i == 0)
    def _():
        w_request(0)
        w_request(1)
        xbuf[...] = jnp.zeros(xbuf.shape, xbuf.dtype)
        x_start(0, 0)

    @pl.when(i < n_used)
    def _():
        slot = i % 2

        @pl.when(i + 1 < n_used)
        def _():
            x_start(i + 1, 1 - slot)

        run = rid_ref[i]

        @pl.when((i == 0) | (run != rid_ref[jnp.maximum(i - 1, 0)]))
        def _():
            ws = run % 2
            for c, _ in w_copies(re_ref[run], ws):
                c.wait()
            _cast_rows(wgf, ws, wgb, 64)
            _cast_rows(wuf, ws, wub, 64)
            _cast_rows(wdf, ws, wdb, 16)
            w_request(run + 2)

        x_wait(i, slot)
        words = [xbuf[slot, pl.ds(c, R, stride=HT), :] for c in range(HT)]
        lo = [lax.bitcast_convert_type(w << 16, F32) for w in words]
        hi = [lax.bitcast_convert_type(w & jnp.uint32(0xFFFF0000), F32) for w in words]
        x = jnp.concatenate(lo + hi, axis=1).astype(BF16)
        hg = _dot(x, wgb[...])
        hu = _dot(x, wub[...])
        hid = (hg * jax.nn.sigmoid(hg)) * hu
        y_ref[...] = _dot(hid.astype(BF16), wdb[...])

    @pl.when(i >= n_used)
    def _():
        y_ref[...] = jnp.zeros(y_ref.shape, y_ref.dtype)


def _experts(h, tok, run_id, run_e, n_used, n_valid, w_gate, w_up, w_down, layer):
    D = D_MODEL
    n_pad = tok.shape[0]
    n_blk = n_pad // MOE_BLOCK
    any_spec = pl.BlockSpec(memory_space=pl.ANY)
    return pl.pallas_call(
        functools.partial(_expert_kernel, layer=layer),
        out_shape=jax.ShapeDtypeStruct((n_pad, D), F32),
        grid_spec=pltpu.PrefetchScalarGridSpec(
            num_scalar_prefetch=5,
            grid=(n_blk,),
            in_specs=[any_spec, any_spec, any_spec, any_spec],
            out_specs=pl.BlockSpec((MOE_BLOCK, D), lambda i, *_: (i, 0)),
            scratch_shapes=[pltpu.VMEM((2, D, D_EXPERT), F32), pltpu.VMEM((2, D, D_EXPERT), F32),
                            pltpu.VMEM((2, D_EXPERT, D), F32),
                            pltpu.VMEM((D, D_EXPERT), BF16), pltpu.VMEM((D, D_EXPERT), BF16),
                            pltpu.VMEM((D_EXPERT, D), BF16),
                            pltpu.VMEM((2, MOE_BLOCK * PACKED_ROWS, LANES), jnp.uint32),
                            pltpu.SemaphoreType.DMA((2, 4)), pltpu.SemaphoreType.DMA((2,))]),
        compiler_params=_cparams(("arbitrary",)),
        name="moe_experts",
    )(run_id, run_e, n_used, n_valid, tok, h, w_gate, w_up, w_down)


def _combine_kernel(slot_ref, y_ref, x_ref, w_ref, g_ref, fg_ref, o_ref, buf, sem, *, rows, final_norm):
    i = pl.program_id(0)
    n_steps = pl.num_programs(0)

    def start_block(blk, bs):
        def start(r, c):
            for k in range(TOPK_IN_GROUP):
                pltpu.make_async_copy(y_ref.at[pl.ds(slot_ref[(blk * rows + r) * 2 + k], 1)],
                                      buf.at[bs, k, pl.ds(r, 1)], sem.at[bs]).start()
            return c
        lax.fori_loop(0, rows, start, 0, unroll=8)

    @pl.when(i == 0)
    def _():
        start_block(0, 0)

    bs = i % 2

    @pl.when(i + 1 < n_steps)
    def _():
        start_block(i + 1, 1 - bs)

    for k in range(TOPK_IN_GROUP):
        pltpu.make_async_copy(y_ref.at[pl.ds(0, rows)], buf.at[bs, k], sem.at[bs]).wait()
    w = w_ref[...]
    moe = buf[bs, 0] * w[:, 0:1] + buf[bs, 1] * w[:, 1:2]
    y = x_ref[...] + g_ref[...] * moe
    if final_norm:
        y = (y * lax.rsqrt(jnp.mean(y * y, axis=-1, keepdims=True) + EPS)) * fg_ref[...]
    o_ref[...] = y


def _combine(yb, slots, x, gate_w, g2, final_g, final_norm, rows=512):
    T, D = x.shape
    return pl.pallas_call(
        functools.partial(_combine_kernel, rows=rows, final_norm=final_norm),
        out_shape=jax.ShapeDtypeStruct((T, D), F32),
        grid_spec=pltpu.PrefetchScalarGridSpec(
            num_scalar_prefetch=1,
            grid=(T // rows,),
            in_specs=[pl.BlockSpec(memory_space=pl.ANY),
                      pl.BlockSpec((rows, D), lambda i, s: (i, 0)),
                      pl.BlockSpec((rows, LANES), lambda i, s: (i, 0)),
                      pl.BlockSpec((1, D), lambda i, s: (0, 0)),
                      pl.BlockSpec((1, D), lambda i, s: (0, 0))],
            out_specs=pl.BlockSpec((rows, D), lambda i, s: (i, 0)),
            scratch_shapes=[pltpu.VMEM((2, TOPK_IN_GROUP, rows, D), F32), pltpu.SemaphoreType.DMA((2,))]),
        compiler_params=_cparams(("arbitrary",)),
        name="moe_combine",
    )(slots.reshape(-1), yb, x, gate_w, g2, final_g)


def _hier_moe_residual(x, g, sc, sh, g2, w_grp, b_grp, w_exp, b_exp, w_gate, w_up, w_down, layer, final_g,
                       final_norm):
    N, D = x.shape
    n_route = N_GROUPS + N_EXPERTS
    w_r = jnp.zeros((D, LANES), F32).at[:, :N_GROUPS].set(w_grp).at[:, N_GROUPS:n_route].set(w_exp)
    b_r = jnp.zeros((1, LANES), F32).at[0, :N_GROUPS].set(b_grp).at[0, N_GROUPS:n_route].set(b_exp)
    h, eid_l, gate_l, cnt = _router(x, g, sc, sh, w_r, b_r)

    A = N * TOPK_IN_GROUP
    counts = cnt[0, :N_EXPERTS].astype(jnp.int32)
    padded = (counts + MOE_BLOCK - 1) // MOE_BLOCK * MOE_BLOCK
    pad_end = jnp.cumsum(padded)
    pad_start = pad_end - padded
    n_blk = -(-(A + N_EXPERTS * MOE_BLOCK) // MOE_BLOCK)
    base = jnp.zeros((1, LANES), F32).at[0, :N_EXPERTS].set(pad_start.astype(F32))
    dest_l, tok_tbl = _dispatch(eid_l, base, n_blk)
    buf_tok = tok_tbl
    blk_idx = jnp.arange(n_blk, dtype=jnp.int32)
    n_used = pad_end[-1] // MOE_BLOCK
    blk_e = jnp.minimum(jnp.sum((pad_end[None, :] <= (blk_idx * MOE_BLOCK)[:, None]).astype(jnp.int32), axis=1),
                        N_EXPERTS - 1)
    blk_e = jnp.where(blk_idx < n_used, blk_e, blk_e[n_used - 1])
    change = jnp.concatenate([jnp.ones((1,), bool), blk_e[1:] != blk_e[:-1]])
    run_id = (jnp.cumsum(change.astype(jnp.int32)) - 1).astype(jnp.int32)
    runs = jnp.arange(n_blk + 2, dtype=jnp.int32)
    run_e = jnp.max(jnp.where(run_id[None, :] == runs[:, None], blk_e[None, :], -1), axis=1).astype(jnp.int32)
    slots = dest_l[:, :TOPK_IN_GROUP]

    n_valid = jnp.clip(counts[blk_e] - (blk_idx * MOE_BLOCK - pad_start[blk_e]), 0, MOE_BLOCK).astype(jnp.int32)
    yb = _experts(h, buf_tok, run_id, run_e, n_used.reshape(1).astype(jnp.int32), n_valid, w_gate, w_up, w_down, layer)
    return _combine(yb, slots, x, gate_l, g2, final_g, final_norm)


def _even_mixer_residual(x, g, sc, sh, g1, rel_bias, w_in, w_out, cmp_pe, cmp_w1, cmp_b1, cmp_w2, cmp_b2,
                         conv_w, conv_b, lru_wa, lru_ba, lru_wi, lru_bi, lru_lambda):
    T, D = x.shape
    G, HD = NSA_KV_HEADS, HEAD_DIM
    nq = NSA_HEADS * HD
    n_kv = 6 * NSA_KV_W
    n_gate = 3 * NSA_HEADS
    w_main = jnp.concatenate([w_in[:, :nq], w_in[:, nq + n_kv + n_gate:], w_in[:, nq:nq + n_kv]],
                             axis=1).astype(BF16)
    w_gl = jnp.zeros((D, LANES), F32).at[:, :n_gate].set(w_in[:, nq + n_kv:nq + n_kv + n_gate]).astype(BF16)
    kv0 = nq + 2 * LRU_WIDTH
    tn = 768
    proj, proj_gate, kv_b = _norm_proj(x, g, sc, sh, w_main, w_gl, bf16_from=kv0 // tn, tn=tn)

    def kv_heads(j, src=kv_b, c_base=0):
        c0 = c_base + j * NSA_KV_W
        return src[:, c0:c0 + NSA_KV_W].reshape(T, G, HD).transpose(1, 0, 2)

    cmp_in = jnp.stack([kv_heads(0, proj, kv0), kv_heads(1, proj, kv0)], axis=0)
    cmp_out = _compress(cmp_in, cmp_pe, cmp_w1, cmp_b1, cmp_w2, cmp_b2)
    ncp = T // CMP_STRIDE
    valid = (jnp.arange(ncp) < ncp - 1)[None, None, :, None]
    cmp_pad = jnp.pad(jnp.where(valid, cmp_out, 0.0), ((0, 0), (0, 0), (Q_BLOCK, Q_BLOCK), (0, 0)))
    slc_pad = ((0, 0), (Q_BLOCK, SLC_TAIL), (0, 0))
    blk_of_row = jnp.pad(jnp.arange(T, dtype=jnp.int32) // SLC_LEN, (Q_BLOCK, SLC_TAIL), constant_values=Q_BLOCK - 1)
    blk_onehot = (blk_of_row[:, None] == jnp.arange(Q_BLOCK, dtype=jnp.int32)[None, :]).astype(BF16)
    ks = jnp.concatenate([jnp.pad(kv_heads(2), slc_pad), jnp.broadcast_to(blk_onehot, (G,) + blk_onehot.shape)], axis=2)
    vs = jnp.pad(kv_heads(3), slc_pad)
    kw = jnp.pad(kv_heads(4), ((0, 0), (WINDOW, 0), (0, 0)))
    vw = jnp.pad(kv_heads(5), ((0, 0), (WINDOW, 0), (0, 0)))
    nsa_out = _nsa_attention(rel_bias, proj, proj_gate, cmp_pad[0], cmp_pad[1], ks, vs, kw, vw)

    y_col = nq // LRU_WIDTH
    lru_out = _rglru(proj, y_col, y_col + 1, conv_w, conv_b, lru_wa, lru_ba, lru_wi, lru_bi, lru_lambda)
    w_o = w_out.astype(BF16)
    return _out_proj(nsa_out, lru_out, w_o, x, g1)


def _odd_mixer_residual(x, g, sc, sh, g1, lb_logits, w_in, w_out, norm_g, layer):
    proj = _norm_proj(x, g, sc, sh, w_in.astype(BF16), tn=1024)[0]
    o = _hgrn2(proj, lb_logits, norm_g, layer)
    w_o = w_out.astype(BF16)
    return _out_proj(o, o, w_o, x, g1, a2_col=1)


def kernel(x, c, rel_bias, ada_w, ada_b, norm_mix_g, norm_ffn_g, ev_w_in, ev_w_out, cmp_pe, cmp_w1, cmp_b1, cmp_w2, cmp_b2, lru_conv_w, lru_conv_b, lru_wa, lru_ba, lru_wi, lru_bi, lru_lambda, od_w_in, od_w_out, hg_lb_logits, hg_norm_g, moe_w_grp, moe_b_grp, moe_w_exp, moe_b_exp, moe_w_gate, moe_w_up, moe_w_down, final_g):
    B, T, D = x.shape
    assert B == 1 and D == D_MODEL
    xt = x.reshape(T, D)
    mod = _modulation(c, ada_w, ada_b)
    for l in range(DEPTH):
        sh1, sc1, g1, sh2, sc2, g2 = [mod[l, :, k * D:(k + 1) * D] for k in range(6)]
        gm = norm_mix_g[l].reshape(1, D)
        gf = norm_ffn_g[l].reshape(1, D)
        j = l // 2
        if l % 2 == 0:
            xt = _even_mixer_residual(xt, gm, sc1, sh1, g1, rel_bias, ev_w_in[j], ev_w_out[j], cmp_pe[j],
                                      cmp_w1[j], cmp_b1[j], cmp_w2[j], cmp_b2[j], lru_conv_w[j],
                                      lru_conv_b[j], lru_wa[j], lru_ba[j], lru_wi[j], lru_bi[j], lru_lambda[j])
        else:
            xt = _odd_mixer_residual(xt, gm, sc1, sh1, g1, hg_lb_logits, od_w_in[j], od_w_out[j],
                                     hg_norm_g[j], l)
        xt = _hier_moe_residual(xt, gf, sc2, sh2, g2, moe_w_grp[l], moe_b_grp[l], moe_w_exp[l], moe_b_exp[l],
                                moe_w_gate, moe_w_up, moe_w_down, l, final_g.reshape(1, D), l == DEPTH - 1)
    return xt.reshape(B, T, D)
```
